```python
import math
import jax, jax.numpy as jnp
from jax import lax
import numpy as np

D_MODEL = 2048
BATCH = 4
SEQ = 4096
DEPTH = 1

GRID_W = 64
CTX_LEN = 256
EPS = 1e-6
HEAD_DIM = 128
N_Q_HEADS = 8
N_KV_HEADS = 2
GROUP = N_Q_HEADS // N_KV_HEADS
ATT_WIDTH = N_Q_HEADS * HEAD_DIM
KV_WIDTH = N_KV_HEADS * HEAD_DIM
ROPE_THETA = 10000.0
ROPE_FREQS = HEAD_DIM // 4
Q_BLOCK = 128
REC_WIDTH = D_MODEL - ATT_WIDTH
REC_BLOCKS = 8
REC_BLOCK_DIM = REC_WIDTH // REC_BLOCKS
CONV_W = 4
CONV_LEFT = 2
LRU_C = 8.0
IN_WIDTH = ATT_WIDTH + 2 * KV_WIDTH + 2 * REC_WIDTH
SPLITS = (ATT_WIDTH, ATT_WIDTH + KV_WIDTH, ATT_WIDTH + 2 * KV_WIDTH, ATT_WIDTH + 2 * KV_WIDTH + REC_WIDTH)
N_EXPERTS = 32
TOP_K = 4
D_FF = D_MODEL
SWIGLU_LIMIT = 7.0
SWIGLU_ALPHA = 1.702
MOE_BLOCK = 256

kernel_name = "hybrid_gqa_rglru_moe_dit_layer"


def rmsnorm(x, g):
    xf = x.astype(jnp.float32)
    y = xf * lax.rsqrt(jnp.mean(xf * xf, axis=-1, keepdims=True) + EPS)
    return y.astype(x.dtype) * g


def modulate(x, shift, scale):
    return x * (1 + scale) + shift


def axial_rope(seq):
    rows = seq // GRID_W
    row = jnp.repeat(jnp.arange(rows), GRID_W).astype(jnp.float32)
    col = jnp.tile(jnp.arange(GRID_W), rows).astype(jnp.float32)
    inv = ROPE_THETA ** (-jnp.arange(ROPE_FREQS, dtype=jnp.float32) / ROPE_FREQS)
    ang = jnp.stack([row[:, None] * inv, col[:, None] * inv], axis=1)
    return jnp.cos(ang), jnp.sin(ang)


def apply_rope(x, cos, sin):
    xs = x.reshape(*x.shape[:-1], 2, 2, ROPE_FREQS)
    x1, x2 = xs[..., 0, :], xs[..., 1, :]
    c = cos[:, None].astype(x.dtype)
    s = sin[:, None].astype(x.dtype)
    return jnp.stack([x1 * c - x2 * s, x2 * c + x1 * s], axis=-2).reshape(x.shape)


def attend(q, k, v):
    s = jnp.einsum('bqkgd,bnkd->bkgqn', q, k).astype(jnp.float32) * (HEAD_DIM ** -0.5)
    p = jax.nn.softmax(s, axis=-1).astype(v.dtype)
    return jnp.einsum('bkgqn,bnkd->bqkgd', p, v)


def short_conv(u, w, b):
    L = u.shape[1]
    up = jnp.pad(u, ((0, 0), (CONV_LEFT, CONV_W - 1 - CONV_LEFT), (0, 0)))
    out = b
    for j in range(CONV_W):
        out = out + up[:, j:j + L] * w[j]
    return out


def lru_coeffs(u, w_a, b_a, w_x, b_x, lam):
    Bsz, L, _ = u.shape
    ub = u.reshape(Bsz, L, REC_BLOCKS, REC_BLOCK_DIM)
    ga = jnp.einsum('blnd,rnde->rblne', ub, w_a).reshape(2, Bsz, L, REC_WIDTH) + b_a[:, None, None, :]
    gx = jnp.einsum('blnd,rnde->rblne', ub, w_x).reshape(2, Bsz, L, REC_WIDTH) + b_x[:, None, None, :]
    r = jax.nn.sigmoid(ga.astype(jnp.float32))
    i = jax.nn.sigmoid(gx.astype(jnp.float32))
    log_a = -LRU_C * r * jax.nn.softplus(-lam.astype(jnp.float32))[:, None, None, :]
    a = jnp.exp(log_a)
    mult = jnp.sqrt(-jnp.expm1(2.0 * log_a))
    return a, mult * i * u.astype(jnp.float32)[None]


def linear_scan(a, b, h0):
    b = b.at[:, 0].add(a[:, 0] * h0)

    def comb(left, right):
        a_l, b_l = left
        a_r, b_r = right
        return a_r * a_l, a_r * b_l + b_r

    _, h = lax.associative_scan(comb, (a, b), axis=1)
    return h


def bidir_scan(a, b, h0_f, h0_b):
    h_f = linear_scan(a[0], b[0], h0_f)
    h_b = jnp.flip(linear_scan(jnp.flip(a[1], 1), jnp.flip(b[1], 1), h0_b), 1)
    return h_f, h_b


def moe(h, w_router, b_router, w_gu, b_gu, w_down, b_down):
    T, D = h.shape
    logits = (h @ w_router + b_router).astype(jnp.float32)
    top_val, top_idx = lax.top_k(logits, TOP_K)
    gate = jax.nn.softmax(top_val, axis=-1)
    A = T * TOP_K
    e_flat = top_idx.reshape(A)
    tok_flat = (jnp.arange(A, dtype=jnp.int32) // TOP_K).astype(jnp.int32)
    w_flat = gate.reshape(A)
    order = jnp.argsort(e_flat)
    e_sorted, tok_sorted, w_sorted = e_flat[order], tok_flat[order], w_flat[order]
    counts = jnp.zeros((N_EXPERTS,), jnp.int32).at[e_flat].add(1)
    starts = jnp.cumsum(counts) - counts
    padded = (counts + MOE_BLOCK - 1) // MOE_BLOCK * MOE_BLOCK
    pad_starts = jnp.cumsum(padded) - padded
    pad_ends = pad_starts + padded
    dest = pad_starts[e_sorted] + (jnp.arange(A, dtype=jnp.int32) - starts[e_sorted])
    n_blocks = (A + N_EXPERTS * (MOE_BLOCK - 1) + MOE_BLOCK - 1) // MOE_BLOCK
    P = n_blocks * MOE_BLOCK
    slot_tok = jnp.full((P,), T, jnp.int32).at[dest].set(tok_sorted)
    slot_w = jnp.zeros((P,), h.dtype).at[dest].set(w_sorted.astype(h.dtype))
    block_start = jnp.arange(n_blocks, dtype=jnp.int32) * MOE_BLOCK
    block_e = jnp.minimum(jnp.searchsorted(pad_ends, block_start, side='right'), N_EXPERTS - 1)
    h_pad = jnp.concatenate([h, jnp.zeros((1, D), h.dtype)], axis=0)

    def expert_block(args):
        e, toks, w = args
        xb = h_pad[toks]
        gu = xb @ w_gu[e] + b_gu[e]
        g_, up = gu[:, 0::2], gu[:, 1::2]
        g_ = jnp.minimum(g_, SWIGLU_LIMIT)
        up = jnp.clip(up, -SWIGLU_LIMIT, SWIGLU_LIMIT)
        act = (up + 1) * (g_ * jax.nn.sigmoid(SWIGLU_ALPHA * g_))
        return (act @ w_down[e] + b_down[e]) * w[:, None]

    outs = lax.map(expert_block, (block_e, slot_tok.reshape(n_blocks, MOE_BLOCK), slot_w.reshape(n_blocks, MOE_BLOCK)))
    y = jnp.zeros((T + 1, D), h.dtype).at[slot_tok].add(outs.reshape(P, D))
    return y[:T]


def setup_inputs(seed: int = 0) -> dict:
    key = jax.random.key(seed)
    ks = jax.random.split(key, 32)
    f32 = jnp.float32
    D = D_MODEL

    def nrm(k, shape, scale):
        return jax.random.normal(k, shape, f32) * scale

    u = jax.random.uniform(ks[16], (DEPTH, 2, REC_WIDTH), f32, 0.9, 0.999)
    return {
        "x": nrm(ks[0], (BATCH, SEQ, D), 1.0),
        "c": nrm(ks[1], (BATCH, D), 1.0),
        "ctx": nrm(ks[2], (BATCH, CTX_LEN, D), 1.0),
        "c_ctx": nrm(ks[3], (D,), 1.0),
        "w_mod": nrm(ks[4], (DEPTH, D, 6 * D), 0.5 * D ** -0.5),
        "b_mod": nrm(ks[5], (DEPTH, 6 * D), 0.02),
        "g_norm1": 1.0 + nrm(ks[6], (DEPTH, D), 0.02),
        "w_in": nrm(ks[7], (DEPTH, D, IN_WIDTH), D ** -0.5),
        "g_q": 1.0 + nrm(ks[8], (DEPTH, HEAD_DIM), 0.02),
        "g_k": 1.0 + nrm(ks[9], (DEPTH, HEAD_DIM), 0.02),
        "conv_w": nrm(ks[10], (DEPTH, CONV_W, REC_WIDTH), CONV_W ** -0.5),
        "conv_b": nrm(ks[11], (DEPTH, REC_WIDTH), 0.02),
        "w_gate_a": nrm(ks[12], (DEPTH, 2, REC_BLOCKS, REC_BLOCK_DIM, REC_BLOCK_DIM), REC_BLOCK_DIM ** -0.5),
        "b_gate_a": nrm(ks[13], (DEPTH, 2, REC_WIDTH), 0.02),
        "w_gate_x": nrm(ks[14], (DEPTH, 2, REC_BLOCKS, REC_BLOCK_DIM, REC_BLOCK_DIM), REC_BLOCK_DIM ** -0.5),
        "b_gate_x": nrm(ks[15], (DEPTH, 2, REC_WIDTH), 0.02),
        "lru_lambda": jnp.log(u) - jnp.log1p(-u),
        "g_att_out": 1.0 + nrm(ks[17], (DEPTH, ATT_WIDTH), 0.02),
        "g_rec_out": 1.0 + nrm(ks[18], (DEPTH, REC_WIDTH), 0.02),
        "w_out": nrm(ks[19], (DEPTH, D, D), D ** -0.5),
        "g_norm2": 1.0 + nrm(ks[20], (DEPTH, D), 0.02),
        "w_router": nrm(ks[21], (DEPTH, D, N_EXPERTS), D ** -0.5),
        "b_router": nrm(ks[22], (DEPTH, N_EXPERTS), 0.01),
        "w_gate_up": nrm(ks[23], (DEPTH, N_EXPERTS, D, 2 * D_FF), D ** -0.5),
        "b_gate_up": nrm(ks[24], (DEPTH, N_EXPERTS, 2 * D_FF), 0.02),
        "w_down": nrm(ks[25], (DEPTH, N_EXPERTS, D_FF, D), D_FF ** -0.5),
        "b_down": nrm(ks[26], (DEPTH, N_EXPERTS, D), 0.02),
    }


def reference(x, c, ctx, c_ctx, w_mod, b_mod, g_norm1, w_in, g_q, g_k, conv_w, conv_b,
              w_gate_a, b_gate_a, w_gate_x, b_gate_x, lru_lambda, g_att_out, g_rec_out, w_out,
              g_norm2, w_router, b_router, w_gate_up, b_gate_up, w_down, b_down):
    B, S, D = x.shape
    C = ctx.shape[1]
    cos, sin = axial_rope(S)
    n_qb = S // Q_BLOCK
    cx = ctx
    for l in range(DEPTH):
        update_ctx = l < DEPTH - 1
        mod = jax.nn.silu(c) @ w_mod[l] + b_mod[l]
        mod_c = jax.nn.silu(c_ctx) @ w_mod[l] + b_mod[l]
        sh1, sc1, gt1, sh2, sc2, gt2 = jnp.split(mod[:, None, :], 6, axis=-1)
        csh1, csc1, cgt1, csh2, csc2, cgt2 = jnp.split(mod_c, 6)

        h = modulate(rmsnorm(x, g_norm1[l]), sh1, sc1)
        hc = modulate(rmsnorm(cx, g_norm1[l]), csh1, csc1)
        q, k, v, xr, yr = jnp.split(h @ w_in[l], SPLITS, axis=-1)
        qc, kc, vc, xrc, yrc = jnp.split(hc @ w_in[l], SPLITS, axis=-1)

        q = apply_rope(rmsnorm(q.reshape(B, S, N_Q_HEADS, HEAD_DIM), g_q[l]), cos, sin)
        k = apply_rope(rmsnorm(k.reshape(B, S, N_KV_HEADS, HEAD_DIM), g_k[l]), cos, sin)
        v = v.reshape(B, S, N_KV_HEADS, HEAD_DIM)
        kc = rmsnorm(kc.reshape(B, C, N_KV_HEADS, HEAD_DIM), g_k[l])
        vc = vc.reshape(B, C, N_KV_HEADS, HEAD_DIM)
        k_all = jnp.concatenate([kc, k], axis=1)
        v_all = jnp.concatenate([vc, v], axis=1)
        qb = q.reshape(B, n_qb, Q_BLOCK, N_KV_HEADS, GROUP, HEAD_DIM).swapaxes(0, 1)
        att = lax.map(lambda qblk: attend(qblk, k_all, v_all), qb)
        att = att.swapaxes(0, 1).reshape(B, S, ATT_WIDTH)

        ac, bc = lru_coeffs(short_conv(xrc, conv_w[l], conv_b[l]), w_gate_a[l], b_gate_a[l],
                            w_gate_x[l], b_gate_x[l], lru_lambda[l])
        zeros = jnp.zeros((B, REC_WIDTH), jnp.float32)
        hcf, hcb = bidir_scan(ac, bc, zeros, zeros)
        a, bb = lru_coeffs(short_conv(xr, conv_w[l], conv_b[l]), w_gate_a[l], b_gate_a[l],
                           w_gate_x[l], b_gate_x[l], lru_lambda[l])
        hf, hb = bidir_scan(a, bb, hcf[:, -1], hcb[:, 0])
        rec = (hf + hb).astype(x.dtype) * jax.nn.gelu(yr, approximate=True)

        mix = jnp.concatenate([rmsnorm(att, g_att_out[l]), rmsnorm(rec, g_rec_out[l])], axis=-1) @ w_out[l]
        x = x + gt1 * mix
        h2 = modulate(rmsnorm(x, g_norm2[l]), sh2, sc2)
        x = x + gt2 * moe(h2.reshape(B * S, D), w_router[l], b_router[l], w_gate_up[l], b_gate_up[l],
                          w_down[l], b_down[l]).reshape(B, S, D)

        if update_ctx:
            qc = rmsnorm(qc.reshape(B, C, N_Q_HEADS, HEAD_DIM), g_q[l]).reshape(B, C, N_KV_HEADS, GROUP, HEAD_DIM)
            attc = attend(qc, kc, vc).reshape(B, C, ATT_WIDTH)
            recc = (hcf + hcb).astype(cx.dtype) * jax.nn.gelu(yrc, approximate=True)
            mixc = jnp.concatenate([rmsnorm(attc, g_att_out[l]), rmsnorm(recc, g_rec_out[l])], axis=-1) @ w_out[l]
            cx = cx + cgt1 * mixc
            h2c = modulate(rmsnorm(cx, g_norm2[l]), csh2, csc2)
            cx = cx + cgt2 * moe(h2c.reshape(B * C, D), w_router[l], b_router[l], w_gate_up[l], b_gate_up[l],
                                 w_down[l], b_down[l]).reshape(B, C, D)
    return x
```

```python
import functools
import math

import jax
import jax.numpy as jnp
from jax import lax
from jax.experimental import pallas as pl
from jax.experimental.pallas import tpu as pltpu

F32 = jnp.float32
BF16 = jnp.bfloat16
I32 = jnp.int32

EPS = 1e-6
GRID_W = 64
HEAD_DIM = 128
N_Q_HEADS = 8
N_KV_HEADS = 2
GROUP = N_Q_HEADS // N_KV_HEADS
ATT_WIDTH = N_Q_HEADS * HEAD_DIM
KV_WIDTH = N_KV_HEADS * HEAD_DIM
ROPE_THETA = 10000.0
ROPE_FREQS = HEAD_DIM // 4
REC_BLOCKS = 8
CONV_W = 4
CONV_LEFT = 2
LRU_C = 8.0
N_EXPERTS = 32
TOP_K = 4
SWIGLU_LIMIT = 7.0
SWIGLU_ALPHA = 1.702
MOE_BLOCK = 256

V7X_VMEM_BYTES = 64 * 1024 * 1024
SUBLANES = 8
LANES = 128
SCAN_SEGMENTS = SUBLANES
CONV_PAD = SUBLANES

HIGHEST = lax.Precision.HIGHEST


def _vmem_limit(nbytes):
    return int(min(V7X_VMEM_BYTES - 4 * 1024 * 1024, max(nbytes, 16 * 1024 * 1024)))


def _rms(x, g):
    return x * lax.rsqrt(jnp.mean(x * x, axis=-1, keepdims=True) + EPS) * g


def _mod_kernel(c_ref, w_ref, b_ref, o_ref):
    c = c_ref[...]
    a = c * jax.nn.sigmoid(c)
    o_ref[...] = jnp.dot(a, w_ref[...], preferred_element_type=F32, precision=HIGHEST) + b_ref[...]


def _mod(c8, w_mod, b_mod):
    d, n = w_mod.shape
    tn = 1024
    return pl.pallas_call(
        _mod_kernel,
        grid=(n // tn,),
        in_specs=[pl.BlockSpec((SUBLANES, d), lambda j: (0, 0)),
                  pl.BlockSpec((d, tn), lambda j: (0, j)),
                  pl.BlockSpec((1, tn), lambda j: (0, j))],
        out_specs=pl.BlockSpec((SUBLANES, tn), lambda j: (0, j)),
        out_shape=jax.ShapeDtypeStruct((SUBLANES, n), F32),
        compiler_params=pltpu.CompilerParams(dimension_semantics=("arbitrary",),
                                             vmem_limit_bytes=_vmem_limit(3 * d * tn * 4)),
        name="mod",
    )(c8, w_mod, b_mod.reshape(1, n))


def _qk_norm_rope(y, g, rope):
    yn = _rms(y, g)
    if rope is None:
        return yn
    cosf, sneg, spos = rope
    return yn * cosf + pltpu.roll(yn, HEAD_DIM - ROPE_FREQS, 1) * sneg + pltpu.roll(yn, ROPE_FREQS, 1) * spos


def _inproj_kernel(*refs, d, latent, ctx_row):
    if latent:
        (x_ref, mods_ref, g1_ref, w_ref, rope_ref, gq_ref, gk_ref,
         q_ref, k_ref, v_ref, xr_ref, yr_ref) = refs
        row = pl.program_id(0)
    else:
        x_ref, mods_ref, g1_ref, w_ref, gk_ref, k_ref, v_ref, xr_ref = refs
        row = ctx_row
    sh = mods_ref[pl.ds(row, 1), 0:d]
    sc = mods_ref[pl.ds(row, 1), d:2 * d]
    h = _rms(x_ref[0], g1_ref[...]) * (1.0 + sc) + sh
    hb = h.astype(BF16)

    def proj(lo, hi):
        return jnp.dot(hb, w_ref[:, lo:hi], preferred_element_type=F32)

    o_k = ATT_WIDTH
    o_v = o_k + KV_WIDTH
    o_xr = o_v + KV_WIDTH
    rec_w = d - ATT_WIDTH
    o_yr = o_xr + rec_w
    rope = None
    if latent:
        rp = rope_ref[...]
        rope = (rp[:, 0:HEAD_DIM], rp[:, HEAD_DIM:2 * HEAD_DIM], rp[:, 2 * HEAD_DIM:3 * HEAD_DIM])
        q = proj(0, ATT_WIDTH)
        for hd in range(N_Q_HEADS):
            sl = slice(hd * HEAD_DIM, (hd + 1) * HEAD_DIM)
            q_ref[0, :, sl] = _qk_norm_rope(q[:, sl], gq_ref[...], rope).astype(BF16)
    k = proj(o_k, o_v)
    for hd in range(N_KV_HEADS):
        sl = slice(hd * HEAD_DIM, (hd + 1) * HEAD_DIM)
        k_ref[0, :, sl] = _qk_norm_rope(k[:, sl], gk_ref[...], rope).astype(BF16)
    v_ref[0] = proj(o_v, o_xr).astype(BF16)
    xr_ref[0] = proj(o_xr, o_yr)
    if latent:
        yr_ref[0] = proj(o_yr, o_yr + rec_w)


def _inproj(x, mods, g1, w_in_bf, rope_tab, g_q, g_k, *, latent, ctx_row, tm):
    b, s, d = x.shape
    n = w_in_bf.shape[1]
    rec_w = d - ATT_WIDTH
    grid = (b, s // tm)
    row_spec = lambda w: pl.BlockSpec((1, tm, w), lambda bi, i: (bi, i, 0))
    full2 = lambda a: pl.BlockSpec(a.shape, lambda bi, i: (0, 0))
    in_specs = [row_spec(d), full2(mods), pl.BlockSpec((1, d), lambda bi, i: (0, 0)),
                pl.BlockSpec((d, n), lambda bi, i: (0, 0), pipeline_mode=pl.Buffered(1))]
    args = [x, mods, g1.reshape(1, d), w_in_bf]
    out_specs, out_shape = [], []
    if latent:
        in_specs += [pl.BlockSpec((tm, 3 * HEAD_DIM), lambda bi, i: (i, 0)),
                     pl.BlockSpec((1, HEAD_DIM), lambda bi, i: (0, 0))]
        args += [rope_tab, g_q.reshape(1, HEAD_DIM)]
        out_specs.append(row_spec(ATT_WIDTH))
        out_shape.append(jax.ShapeDtypeStruct((b, s, ATT_WIDTH), BF16))
    in_specs.append(pl.BlockSpec((1, HEAD_DIM), lambda bi, i: (0, 0)))
    args.append(g_k.reshape(1, HEAD_DIM))
    out_specs += [row_spec(KV_WIDTH), row_spec(KV_WIDTH), row_spec(rec_w)]
    out_shape += [jax.ShapeDtypeStruct((b, s, KV_WIDTH), BF16), jax.ShapeDtypeStruct((b, s, KV_WIDTH), BF16),
                  jax.ShapeDtypeStruct((b, s, rec_w), F32)]
    if latent:
        out_specs.append(row_spec(rec_w))
        out_shape.append(jax.ShapeDtypeStruct((b, s, rec_w), F32))
    vmem = d * n * 2 + 2 * tm * d * 4 + 2 * tm * n * 4 + 3 * tm * d * 4 + 2 * tm * n * 4
    return pl.pallas_call(
        functools.partial(_inproj_kernel, d=d, latent=latent, ctx_row=ctx_row),
        grid=grid, in_specs=in_specs, out_specs=out_specs, out_shape=out_shape,
        compiler_params=pltpu.CompilerParams(dimension_semantics=("arbitrary", "arbitrary"),
                                             vmem_limit_bytes=_vmem_limit(vmem)),
        name="inproj_latent" if latent else "inproj_ctx",
    )(*args)


def _attn_kernel(q_ref, k_ref, v_ref, o_ref):
    k = k_ref[0]
    v = v_ref[0]
    scale = HEAD_DIM ** -0.5
    for g in range(GROUP):
        sl = slice(g * HEAD_DIM, (g + 1) * HEAD_DIM)
        s = lax.dot_general(q_ref[0, :, sl], k, (((1,), (1,)), ((), ())), preferred_element_type=F32)
        m = jnp.max(s, axis=-1, keepdims=True)
        p = jnp.exp((s - m) * scale)
        l = jnp.sum(p, axis=-1, keepdims=True)
        o = jnp.dot(p.astype(BF16), v, preferred_element_type=F32)
        o_ref[0, :, sl] = (o / l).astype(BF16)


def _attention(q, k_all, v_all, *, tq):
    b, s, _ = q.shape
    lk = k_all.shape[1]
    gw = GROUP * HEAD_DIM
    vmem = 4 * lk * HEAD_DIM * 2 * 2 + 4 * tq * gw * 2 + 4 * tq * lk * 4
    return pl.pallas_call(
        _attn_kernel,
        grid=(b, N_KV_HEADS, s // tq),
        in_specs=[pl.BlockSpec((1, tq, gw), lambda bi, h, i: (bi, i, h)),
                  pl.BlockSpec((1, lk, HEAD_DIM), lambda bi, h, i: (bi, 0, h)),
                  pl.BlockSpec((1, lk, HEAD_DIM), lambda bi, h, i: (bi, 0, h))],
        out_specs=pl.BlockSpec((1, tq, gw), lambda bi, h, i: (bi, i, h)),
        out_shape=jax.ShapeDtypeStruct((b, s, ATT_WIDTH), BF16),
        compiler_params=pltpu.CompilerParams(dimension_semantics=("arbitrary",) * 3,
                                             vmem_limit_bytes=_vmem_limit(vmem)),
        name="attention",
    )(q, k_all, v_all)


def _gelu_tanh(x):
    return 0.5 * x * (1.0 + jnp.tanh(math.sqrt(2.0 / math.pi) * (x + 0.044715 * x * x * x)))


def _rglru_kernel(xr_ref, xc_ref, yr_ref, cw_ref, cb_ref, wg_ref, bg_ref, lam_ref, o_ref,
                  xp, xpc, af, bf, ab, bb, caf, cbf, cab, cbb, *, s, c):
    nseg = SCAN_SEGMENTS
    seg = s // nseg
    cseg = c // nseg
    zeros_pad = jnp.zeros((CONV_PAD, LANES), F32)
    xp[0:CONV_PAD, :] = zeros_pad
    xp[CONV_PAD + s:2 * CONV_PAD + s, :] = zeros_pad
    xp[CONV_PAD:CONV_PAD + s, :] = xr_ref[0]
    xpc[0:CONV_PAD, :] = zeros_pad
    xpc[CONV_PAD + c:2 * CONV_PAD + c, :] = zeros_pad
    xpc[CONV_PAD:CONV_PAD + c, :] = xc_ref[0]

    cw = cw_ref[...]
    cb = cb_ref[...]
    wg = wg_ref[0]
    bg = bg_ref[...]
    sp = jax.nn.softplus(-lam_ref[...])

    def coeffs(src, lo, n):
        u = cb
        for j in range(CONV_W):
            u = u + src[CONV_PAD + lo + j - CONV_LEFT:CONV_PAD + lo + j - CONV_LEFT + n, :] * cw[j:j + 1, :]
        g = jnp.dot(u.astype(BF16), wg, preferred_element_type=F32) + bg
        out = []
        for r in range(2):
            ga = g[:, (2 * r) * LANES:(2 * r + 1) * LANES]
            gx = g[:, (2 * r + 1) * LANES:(2 * r + 2) * LANES]
            log_a = (-LRU_C) * jax.nn.sigmoid(ga) * sp[r:r + 1, :]
            a = jnp.exp(log_a)
            mult = jnp.sqrt(-jnp.tanh(log_a) * (1.0 + a * a))
            out.append((a, mult * jax.nn.sigmoid(gx) * u))
        return out

    (a0, b0), (a1, b1) = coeffs(xpc, 0, c)
    for q in range(nseg):
        rows = slice(q * cseg, (q + 1) * cseg)
        dst = pl.ds(q, cseg, stride=nseg)
        caf[dst, :] = a0[rows]
        cbf[dst, :] = b0[rows]
        cab[dst, :] = a1[rows]
        cbb[dst, :] = b1[rows]
    for q in range(nseg):
        (a0, b0), (a1, b1) = coeffs(xp, q * seg, seg)
        dst = pl.ds(q, seg, stride=nseg)
        af[dst, :] = a0
        bf[dst, :] = b0
        ab[dst, :] = a1
        bb[dst, :] = b1

    def scan(a_f, b_f, a_b, b_b, n, store):
        def body(j, carry):
            hf, pf, hb, pb = carry
            rf = pl.multiple_of(j * nseg, nseg)
            rb = pl.multiple_of((n - 1 - j) * nseg, nseg)
            av = a_f[pl.ds(rf, nseg), :]
            hf = av * hf + b_f[pl.ds(rf, nseg), :]
            pf = av * pf
            aw = a_b[pl.ds(rb, nseg), :]
            hb = aw * hb + b_b[pl.ds(rb, nseg), :]
            pb = aw * pb
            if store:
                a_f[pl.ds(rf, nseg), :] = pf
                b_f[pl.ds(rf, nseg), :] = hf
                a_b[pl.ds(rb, nseg), :] = pb
                b_b[pl.ds(rb, nseg), :] = hb
            return hf, pf, hb, pb
        z = jnp.zeros((nseg, LANES), F32)
        o = jnp.ones((nseg, LANES), F32)
        return lax.fori_loop(0, n, body, (z, o, z, o), unroll=8)

    def chain(h_end, p_end, h0, reverse):
        order = range(nseg - 1, -1, -1) if reverse else range(nseg)
        enter = [None] * nseg
        cur = h0
        for q in order:
            enter[q] = cur
            cur = h_end[q:q + 1, :] + p_end[q:q + 1, :] * cur
        return enter, cur

    zero_row = jnp.zeros((1, LANES), F32)
    hf, pf, hb, pb = scan(caf, cbf, cab, cbb, cseg, False)
    _, h0f = chain(hf, pf, zero_row, False)
    _, h0b = chain(hb, pb, zero_row, True)
    hf, pf, hb, pb = scan(af, bf, ab, bb, seg, True)
    enter_f, _ = chain(hf, pf, h0f, False)
    enter_b, _ = chain(hb, pb, h0b, True)
    for q in range(nseg):
        src = pl.ds(q, seg, stride=nseg)
        h = bf[src, :] + af[src, :] * enter_f[q] + bb[src, :] + ab[src, :] * enter_b[q]
        rows = slice(q * seg, (q + 1) * seg)
        o_ref[0, rows, :] = (h * _gelu_tanh(yr_ref[0, rows, :])).astype(BF16)


def _rglru(xr, xrc, yr, conv_w, conv_b, w_gates, b_gates, lam):
    b, s, w = xr.shape
    c = xrc.shape[1]
    nb = w // LANES
    slab = lambda n: pl.BlockSpec((1, n, LANES), lambda bi, j: (bi, 0, j))
    scr = lambda n: pltpu.VMEM((n, LANES), F32)
    vmem = (3 * 2 + 5) * s * LANES * 4 + 8 * s * LANES * 4
    return pl.pallas_call(
        functools.partial(_rglru_kernel, s=s, c=c),
        grid=(b, nb),
        in_specs=[slab(s), slab(c), slab(s),
                  pl.BlockSpec((CONV_W, LANES), lambda bi, j: (0, j)),
                  pl.BlockSpec((1, LANES), lambda bi, j: (0, j)),
                  pl.BlockSpec((1, LANES, 4 * LANES), lambda bi, j: (j, 0, 0)),
                  pl.BlockSpec((1, 4 * LANES), lambda bi, j: (0, j)),
                  pl.BlockSpec((2, LANES), lambda bi, j: (0, j))],
        out_specs=slab(s),
        out_shape=jax.ShapeDtypeStruct((b, s, w), BF16),
        scratch_shapes=[scr(s + 2 * CONV_PAD), scr(c + 2 * CONV_PAD),
                        scr(s), scr(s), scr(s), scr(s), scr(c), scr(c), scr(c), scr(c)],
        compiler_params=pltpu.CompilerParams(dimension_semantics=("arbitrary", "arbitrary"),
                                             vmem_limit_bytes=_vmem_limit(vmem)),
        name="rglru",
    )(xr, xrc, yr, conv_w, conv_b, w_gates, b_gates, lam)


def _pack_bf16_pair(lo, hi):
    lo_bits = lax.bitcast_convert_type(lo.astype(BF16).astype(F32), I32)
    hi_bits = lax.bitcast_convert_type(hi.astype(BF16).astype(F32), I32)
    return lax.shift_right_logical(lo_bits, 16) | (hi_bits & jnp.int32(-65536))


def _unpack_bf16_pair(w):
    lo = lax.bitcast_convert_type(lax.shift_left(w, 16), F32).astype(BF16)
    hi = lax.bitcast_convert_type(w & jnp.int32(-65536), F32).astype(BF16)
    return lo, hi


def _merge_kernel(att_ref, rec_ref, x_ref, mods_ref, ga_ref, gr_ref, wo_ref, g2_ref, wr_ref, br_ref,
                  x1_ref, h2_ref, idx_ref, gate_ref, mask_ref, *, d):
    row = pl.program_id(0)
    gt1 = mods_ref[pl.ds(row, 1), 2 * d:3 * d]
    sh2 = mods_ref[pl.ds(row, 1), 3 * d:4 * d]
    sc2 = mods_ref[pl.ds(row, 1), 4 * d:5 * d]
    an = _rms(att_ref[0].astype(F32), ga_ref[...]).astype(BF16)
    rn = _rms(rec_ref[0].astype(F32), gr_ref[...]).astype(BF16)
    mix = (jnp.dot(an, wo_ref[0:ATT_WIDTH, :], preferred_element_type=F32)
           + jnp.dot(rn, wo_ref[ATT_WIDTH:d, :], preferred_element_type=F32))
    x1 = x_ref[0] + gt1 * mix
    x1_ref[0] = x1
    h2 = _rms(x1, g2_ref[...]) * (1.0 + sc2) + sh2
    half = d // 2
    h2_ref[...] = _pack_bf16_pair(h2[:, 0:half], h2[:, half:d])
    logits = jnp.dot(h2, wr_ref[...], preferred_element_type=F32, precision=HIGHEST) + br_ref[...]
    tm = logits.shape[0]
    lane = lax.broadcasted_iota(I32, (tm, N_EXPERTS), 1).astype(F32)
    col = lax.broadcasted_iota(I32, (tm, TOP_K), 1)
    idx = jnp.zeros((tm, TOP_K), F32)
    ex = jnp.zeros((tm, TOP_K), F32)
    mask = jnp.zeros((tm, N_EXPERTS), F32)
    rest = logits
    top = None
    for k in range(TOP_K):
        m = jnp.max(rest, axis=-1, keepdims=True)
        first = jnp.min(jnp.where(rest == m, lane, float(N_EXPERTS)), axis=-1, keepdims=True)
        sel = lane == first
        if k == 0:
            top = m
        idx = jnp.where(col == k, first, idx)
        ex = jnp.where(col == k, jnp.exp(m - top), ex)
        mask = jnp.where(sel, 1.0, mask)
        rest = jnp.where(sel, -jnp.inf, rest)
    idx_ref[...] = idx.astype(I32)
    gate_ref[...] = ex / jnp.sum(ex, axis=-1, keepdims=True)
    mask_ref[...] = mask


def _merge(att, rec, x, mods, g_att, g_rec, w_out_bf, g2, w_router, b_router, *, tm):
    b, s, d = x.shape
    t = b * s
    nt = s // tm
    rec_w = d - ATT_WIDTH
    row3 = lambda w: pl.BlockSpec((1, tm, w), lambda bi, i: (bi, i, 0))
    tok2 = lambda w: pl.BlockSpec((tm, w), lambda bi, i: (bi * nt + i, 0))
    const = lambda shape, **kw: pl.BlockSpec(shape, lambda bi, i: (0,) * len(shape), **kw)
    vmem = d * d * 2 + 2 * tm * (ATT_WIDTH + rec_w) * 2 + 4 * tm * d * 4 + tm * d * 4 + 8 * tm * d * 4
    return pl.pallas_call(
        functools.partial(_merge_kernel, d=d),
        grid=(b, nt),
        in_specs=[row3(ATT_WIDTH), row3(rec_w), row3(d), const(mods.shape),
                  const((1, ATT_WIDTH)), const((1, rec_w)),
                  const((d, d), pipeline_mode=pl.Buffered(1)), const((1, d)),
                  const((d, N_EXPERTS)), const((1, N_EXPERTS))],
        out_specs=[row3(d), tok2(d // 2), tok2(TOP_K), tok2(TOP_K), tok2(N_EXPERTS)],
        out_shape=[jax.ShapeDtypeStruct((b, s, d), F32), jax.ShapeDtypeStruct((t, d // 2), I32),
                   jax.ShapeDtypeStruct((t, TOP_K), I32), jax.ShapeDtypeStruct((t, TOP_K), F32),
                   jax.ShapeDtypeStruct((t, N_EXPERTS), F32)],
        compiler_params=pltpu.CompilerParams(dimension_semantics=("arbitrary", "arbitrary"),
                                             vmem_limit_bytes=_vmem_limit(vmem)),
        name="merge",
    )(att, rec, x, mods, g_att.reshape(1, -1), g_rec.reshape(1, -1), w_out_bf, g2.reshape(1, d),
      w_router, b_router.reshape(1, N_EXPERTS))


def _rank_kernel(mask_ref, idx_ref, rank_ref, cnt_ref, carry):
    @pl.when(pl.program_id(0) == 0)
    def _():
        carry[...] = jnp.zeros_like(carry)

    m = mask_ref[...]
    tb = m.shape[0]
    r = lax.broadcasted_iota(I32, (tb, tb), 0)
    cidx = lax.broadcasted_iota(I32, (tb, tb), 1)
    tri = jnp.where(cidx < r, 1.0, 0.0).astype(BF16)
    before = jnp.dot(tri, m.astype(BF16), preferred_element_type=F32) + carry[...]
    lane = lax.broadcasted_iota(I32, (tb, N_EXPERTS), 1)
    col = lax.broadcasted_iota(I32, (tb, TOP_K), 1)
    idx = idx_ref[...]
    rank = jnp.zeros((tb, TOP_K), F32)
    for k in range(TOP_K):
        pick = jnp.sum(jnp.where(lane == idx[:, k:k + 1], before, 0.0), axis=-1, keepdims=True)
        rank = jnp.where(col == k, pick, rank)
    rank_ref[...] = rank.astype(I32)
    carry[...] = carry[...] + jnp.sum(m, axis=0, keepdims=True)
    cnt_ref[...] = carry[...].astype(I32)


def _rank(mask, idx, *, tb):
    t = mask.shape[0]
    return pl.pallas_call(
        _rank_kernel,
        grid=(t // tb,),
        in_specs=[pl.BlockSpec((tb, N_EXPERTS), lambda i: (i, 0)), pl.BlockSpec((tb, TOP_K), lambda i: (i, 0))],
        out_specs=[pl.BlockSpec((tb, TOP_K), lambda i: (i, 0)), pl.BlockSpec((1, N_EXPERTS), lambda i: (0, 0))],
        out_shape=[jax.ShapeDtypeStruct((t, TOP_K), I32), jax.ShapeDtypeStruct((1, N_EXPERTS), I32)],
        scratch_shapes=[pltpu.VMEM((1, N_EXPERTS), F32)],
        compiler_params=pltpu.CompilerParams(dimension_semantics=("arbitrary",)),
        name="rank",
    )(mask, idx)


def _dispatch_kernel(dest_ref, h2_ref, xs_in_ref, xs_ref, sem, *, tb):
    del xs_in_ref
    base = pl.program_id(0) * tb

    def row_copy(tok, k):
        return pltpu.make_async_copy(h2_ref.at[pl.ds(tok, 1)],
                                     xs_ref.at[pl.ds(dest_ref[tok * TOP_K + k], 1)], sem)

    def body(i, _):
        for k in range(TOP_K):
            row_copy(base + i, k).start()
        return 0

    lax.fori_loop(0, tb, body, 0)
    pltpu.make_async_copy(h2_ref.at[pl.ds(0, tb * TOP_K)], xs_ref.at[pl.ds(0, tb * TOP_K)], sem).wait()


def _dispatch(dest_flat, h2p, xs_init, *, tb):
    t, wd = h2p.shape
    p = xs_init.shape[0]
    return pl.pallas_call(
        functools.partial(_dispatch_kernel, tb=tb),
        grid_spec=pltpu.PrefetchScalarGridSpec(
            num_scalar_prefetch=1, grid=(t // tb,),
            in_specs=[pl.BlockSpec(memory_space=pl.ANY), pl.BlockSpec(memory_space=pl.ANY)],
            out_specs=pl.BlockSpec(memory_space=pl.ANY),
            scratch_shapes=[pltpu.SemaphoreType.DMA(())]),
        out_shape=jax.ShapeDtypeStruct((p, wd), I32),
        input_output_aliases={2: 0},
        compiler_params=pltpu.CompilerParams(dimension_semantics=("arbitrary",), has_side_effects=True),
        name="dispatch",
    )(dest_flat, h2p, xs_init)


def _gu_kernel(be_ref, xs_ref, wg_ref, wu_ref, bg_ref, bu_ref, act_ref):
    del be_ref
    lo, hi = _unpack_bf16_pair(xs_ref[...])
    xb = jnp.concatenate([lo, hi], axis=1)
    g = jnp.dot(xb, wg_ref[0], preferred_element_type=F32) + bg_ref[0]
    u = jnp.dot(xb, wu_ref[0], preferred_element_type=F32) + bu_ref[0]
    g = jnp.minimum(g, SWIGLU_LIMIT)
    u = jnp.clip(u, -SWIGLU_LIMIT, SWIGLU_LIMIT)
    act_ref[...] = ((u + 1.0) * (g * jax.nn.sigmoid(SWIGLU_ALPHA * g))).astype(BF16)


def _moe_gu(block_e, xs, w_gu, b_gu, *, tf):
    p, half = xs.shape
    d = 2 * half
    e, _, f2 = w_gu.shape
    f = f2 // 2
    nj = f // tf
    nblk = p // MOE_BLOCK
    vmem = 2 * 2 * d * tf * 2 + 2 * MOE_BLOCK * half * 4 + 2 * MOE_BLOCK * tf * 2 + 6 * MOE_BLOCK * tf * 4 + MOE_BLOCK * d * 8
    return pl.pallas_call(
        _gu_kernel,
        grid_spec=pltpu.PrefetchScalarGridSpec(
            num_scalar_prefetch=1, grid=(nj, nblk),
            in_specs=[pl.BlockSpec((MOE_BLOCK, half), lambda j, s, be: (s, 0)),
                      pl.BlockSpec((1, d, tf), lambda j, s, be: (be[s], 0, j)),
                      pl.BlockSpec((1, d, tf), lambda j, s, be: (be[s], 0, nj + j)),
                      pl.BlockSpec((1, 1, tf), lambda j, s, be: (be[s], 0, j)),
                      pl.BlockSpec((1, 1, tf), lambda j, s, be: (be[s], 0, nj + j))],
            out_specs=pl.BlockSpec((MOE_BLOCK, tf), lambda j, s, be: (s, j))),
        out_shape=jax.ShapeDtypeStruct((p, f), BF16),
        compiler_params=pltpu.CompilerParams(dimension_semantics=("arbitrary", "arbitrary"),
                                             vmem_limit_bytes=_vmem_limit(vmem)),
        name="moe_gu",
    )(block_e, xs, w_gu, w_gu, b_gu, b_gu)


def _down_kernel(be_ref, act_ref, w_ref, b_ref, y_ref):
    del be_ref
    y_ref[...] = jnp.dot(act_ref[...], w_ref[0], preferred_element_type=F32) + b_ref[0]


def _moe_down(block_e, act, w_down, b_down):
    p, f = act.shape
    e, _, d = w_down.shape
    nblk = p // MOE_BLOCK
    vmem = 2 * f * d * 2 + 2 * MOE_BLOCK * f * 2 + 3 * MOE_BLOCK * d * 4
    return pl.pallas_call(
        _down_kernel,
        grid_spec=pltpu.PrefetchScalarGridSpec(
            num_scalar_prefetch=1, grid=(nblk,),
            in_specs=[pl.BlockSpec((MOE_BLOCK, f), lambda s, be: (s, 0)),
                      pl.BlockSpec((1, f, d), lambda s, be: (be[s], 0, 0)),
                      pl.BlockSpec((1, 1, d), lambda s, be: (be[s], 0, 0))],
            out_specs=pl.BlockSpec((MOE_BLOCK, d), lambda s, be: (s, 0))),
        out_shape=jax.ShapeDtypeStruct((p, d), F32),
        compiler_params=pltpu.CompilerParams(dimension_semantics=("arbitrary",),
                                             vmem_limit_bytes=_vmem_limit(vmem)),
        name="moe_down",
    )(block_e, act, w_down, b_down)


def _combine_kernel(dest_ref, ys_ref, x1_ref, gate_ref, mods_ref, o_ref, buf, sem, *, tb, d, nt):
    base = (pl.program_id(0) * nt + pl.program_id(1)) * tb

    def row_copy(i, k):
        return pltpu.make_async_copy(ys_ref.at[pl.ds(dest_ref[(base + i) * TOP_K + k], 1)],
                                     buf.at[k, pl.ds(i, 1)], sem)

    def body(i, _):
        for k in range(TOP_K):
            row_copy(i, k).start()
        return 0

    lax.fori_loop(0, tb, body, 0)
    for k in range(TOP_K):
        pltpu.make_async_copy(ys_ref.at[pl.ds(0, tb)], buf.at[k], sem).wait()
    gt2 = mods_ref[pl.ds(pl.program_id(0), 1), 5 * d:6 * d]
    gate = gate_ref[...]
    y = buf[0] * gate[:, 0:1]
    for k in range(1, TOP_K):
        y = y + buf[k] * gate[:, k:k + 1]
    o_ref[0] = x1_ref[0] + gt2 * y


def _combine(dest_flat, ys, x1, gate, mods, *, tb):
    b, s, d = x1.shape
    nt = s // tb
    vmem = TOP_K * tb * d * 4 + 4 * tb * d * 4 + 4 * tb * d * 4
    return pl.pallas_call(
        functools.partial(_combine_kernel, tb=tb, d=d, nt=nt),
        grid_spec=pltpu.PrefetchScalarGridSpec(
            num_scalar_prefetch=1, grid=(b, nt),
            in_specs=[pl.BlockSpec(memory_space=pl.ANY),
                      pl.BlockSpec((1, tb, d), lambda bi, i, dr: (bi, i, 0)),
                      pl.BlockSpec((tb, TOP_K), lambda bi, i, dr: (bi * nt + i, 0)),
                      pl.BlockSpec(mods.shape, lambda bi, i, dr: (0, 0))],
            out_specs=pl.BlockSpec((1, tb, d), lambda bi, i, dr: (bi, i, 0)),
            scratch_shapes=[pltpu.VMEM((TOP_K, tb, d), F32), pltpu.SemaphoreType.DMA(())]),
        out_shape=jax.ShapeDtypeStruct((b, s, d), F32),
        compiler_params=pltpu.CompilerParams(dimension_semantics=("arbitrary", "arbitrary"),
                                             vmem_limit_bytes=_vmem_limit(vmem)),
        name="combine",
    )(dest_flat, ys, x1, gate, mods)


def _rope_table(s):
    rows = s // GRID_W
    row = jnp.repeat(jnp.arange(rows), GRID_W).astype(F32)
    col = jnp.tile(jnp.arange(GRID_W), rows).astype(F32)
    inv = ROPE_THETA ** (-jnp.arange(ROPE_FREQS, dtype=F32) / ROPE_FREQS)
    ang_r = row[:, None] * inv
    ang_c = col[:, None] * inv
    z = jnp.zeros_like(ang_r)
    cosf = jnp.concatenate([jnp.cos(ang_r)] * 2 + [jnp.cos(ang_c)] * 2, axis=1)
    sneg = jnp.concatenate([-jnp.sin(ang_r), z, -jnp.sin(ang_c), z], axis=1)
    spos = jnp.concatenate([z, jnp.sin(ang_r), z, jnp.sin(ang_c)], axis=1)
    return jnp.concatenate([cosf, sneg, spos], axis=1)


def kernel(x, c, ctx, c_ctx, w_mod, b_mod, g_norm1, w_in, g_q, g_k, conv_w, conv_b, w_gate_a, b_gate_a,
           w_gate_x, b_gate_x, lru_lambda, g_att_out, g_rec_out, w_out, g_norm2, w_router, b_router,
           w_gate_up, b_gate_up, w_down, b_down):
    b, s, d = x.shape
    cl = ctx.shape[1]
    t = b * s
    assert w_mod.shape[0] == 1, "single-layer kernel"
    assert b + 1 <= SUBLANES and d - ATT_WIDTH == REC_BLOCKS * LANES
    assert s % (SCAN_SEGMENTS * SUBLANES) == 0 and cl % (SCAN_SEGMENTS * SUBLANES) == 0

    ctx_row = b
    c8 = jnp.zeros((SUBLANES, d), F32).at[:b].set(c).at[ctx_row].set(c_ctx)
    mods = _mod(c8, w_mod[0], b_mod[0])

    w_in_bf = w_in[0].astype(BF16)
    tm = min(512, s)
    q, k, v, xr, yr = _inproj(x, mods, g_norm1[0], w_in_bf, _rope_table(s), g_q[0], g_k[0],
                              latent=True, ctx_row=ctx_row, tm=tm)
    kc, vc, xrc = _inproj(ctx, mods, g_norm1[0], w_in_bf, None, g_q[0], g_k[0],
                          latent=False, ctx_row=ctx_row, tm=cl)

    att = _attention(q, jnp.concatenate([kc, k], axis=1), jnp.concatenate([vc, v], axis=1), tq=min(256, s))

    w_gates = jnp.concatenate([w_gate_a[0, 0], w_gate_x[0, 0], w_gate_a[0, 1], w_gate_x[0, 1]], axis=-1).astype(BF16)
    rw = d - ATT_WIDTH
    bias = lambda bb: bb.reshape(REC_BLOCKS, LANES)
    b_gates = jnp.concatenate([bias(b_gate_a[0, 0]), bias(b_gate_x[0, 0]), bias(b_gate_a[0, 1]), bias(b_gate_x[0, 1])],
                              axis=-1).reshape(1, 4 * rw)
    rec = _rglru(xr, xrc, yr, conv_w[0], conv_b[0].reshape(1, rw), w_gates, b_gates, lru_lambda[0])

    x1, h2p, top_idx, gate, mask = _merge(att, rec, x, mods, g_att_out[0], g_rec_out[0], w_out[0].astype(BF16),
                                          g_norm2[0], w_router[0], b_router[0], tm=tm)

    rank, counts = _rank(mask, top_idx, tb=MOE_BLOCK)
    counts = counts[0]
    padded = (counts + MOE_BLOCK - 1) // MOE_BLOCK * MOE_BLOCK
    pad_ends = jnp.cumsum(padded)
    pad_starts = pad_ends - padded
    dest = (pad_starts[top_idx] + rank).reshape(t * TOP_K)
    n_blocks = (t * TOP_K + N_EXPERTS * (MOE_BLOCK - 1) + MOE_BLOCK - 1) // MOE_BLOCK
    p = n_blocks * MOE_BLOCK
    block_start = jnp.arange(n_blocks, dtype=I32) * MOE_BLOCK
    block_e = jnp.minimum(jnp.searchsorted(pad_ends, block_start, side='right'), N_EXPERTS - 1).astype(I32)

    xs = _dispatch(dest, h2p, jnp.zeros((p, d // 2), I32), tb=MOE_BLOCK)

    f = w_down.shape[2]
    w_gu = jnp.concatenate([w_gate_up[0][..., 0::2], w_gate_up[0][..., 1::2]], axis=-1).astype(BF16)
    b_gu = jnp.concatenate([b_gate_up[0][..., 0::2], b_gate_up[0][..., 1::2]], axis=-1).reshape(N_EXPERTS, 1, 2 * f)
    act = _moe_gu(block_e, xs, w_gu, b_gu, tf=1024)
    ys = _moe_down(block_e, act, w_down[0].astype(BF16), b_down[0].reshape(N_EXPERTS, 1, d))

    return _combine(dest, ys, x1, gate, mods, tb=128)
```

```python
import functools
import math

import jax
import jax.numpy as jnp
from jax import lax
from jax.experimental import pallas as pl
from jax.experimental.pallas import tpu as pltpu

F32 = jnp.float32
BF16 = jnp.bfloat16
I32 = jnp.int32

EPS = 1e-6
GRID_W = 64
HEAD_DIM = 128
N_Q_HEADS = 8
N_KV_HEADS = 2
GROUP = N_Q_HEADS // N_KV_HEADS
ATT_WIDTH = N_Q_HEADS * HEAD_DIM
KV_WIDTH = N_KV_HEADS * HEAD_DIM
ROPE_THETA = 10000.0
ROPE_FREQS = HEAD_DIM // 4
REC_BLOCKS = 8
CONV_W = 4
CONV_LEFT = 2
LRU_C = 8.0
N_EXPERTS = 32
TOP_K = 4
SWIGLU_LIMIT = 7.0
SWIGLU_ALPHA = 1.702
MOE_BLOCK = 256

V7X_VMEM_BYTES = 64 * 1024 * 1024
SUBLANES = 8
LANES = 128
SCAN_SEGMENTS = SUBLANES
CONV_PAD = SUBLANES

HIGHEST = lax.Precision.HIGHEST


def _vmem_limit(nbytes):
    return int(min(V7X_VMEM_BYTES - 4 * 1024 * 1024, max(nbytes, 16 * 1024 * 1024)))


def _rms(x, g):
    return x * lax.rsqrt(jnp.mean(x * x, axis=-1, keepdims=True) + EPS) * g


def _mod_kernel(c_ref, w_ref, b_ref, o_ref):
    c = c_ref[...]
    a = c * jax.nn.sigmoid(c)
    o_ref[...] = jnp.dot(a, w_ref[...], preferred_element_type=F32, precision=HIGHEST) + b_ref[...]


def _mod(c8, w_mod, b_mod):
    d, n = w_mod.shape
    tn = 1024
    return pl.pallas_call(
        _mod_kernel,
        grid=(n // tn,),
        in_specs=[pl.BlockSpec((SUBLANES, d), lambda j: (0, 0)),
                  pl.BlockSpec((d, tn), lambda j: (0, j)),
                  pl.BlockSpec((1, tn), lambda j: (0, j))],
        out_specs=pl.BlockSpec((SUBLANES, tn), lambda j: (0, j)),
        out_shape=jax.ShapeDtypeStruct((SUBLANES, n), F32),
        compiler_params=pltpu.CompilerParams(dimension_semantics=("arbitrary",),
                                             vmem_limit_bytes=_vmem_limit(3 * d * tn * 4)),
        name="mod",
    )(c8, w_mod, b_mod.reshape(1, n))


def _qk_norm_rope(y, g, rope):
    yn = _rms(y, g)
    if rope is None:
        return yn
    cosf, sneg, spos = rope
    return yn * cosf + pltpu.roll(yn, HEAD_DIM - ROPE_FREQS, 1) * sneg + pltpu.roll(yn, ROPE_FREQS, 1) * spos


def _inproj_kernel(*refs, d, latent, ctx_row):
    if latent:
        (x_ref, mods_ref, g1_ref, w_ref, rope_ref, gq_ref, gk_ref,
         q_ref, k_ref, v_ref, xr_ref, yr_ref) = refs
        row = pl.program_id(0)
    else:
        x_ref, mods_ref, g1_ref, w_ref, gk_ref, k_ref, v_ref, xr_ref = refs
        row = ctx_row
    sh = mods_ref[pl.ds(row, 1), 0:d]
    sc = mods_ref[pl.ds(row, 1), d:2 * d]
    h = _rms(x_ref[0], g1_ref[...]) * (1.0 + sc) + sh
    hb = h.astype(BF16)

    def proj(lo, hi):
        return jnp.dot(hb, w_ref[:, lo:hi], preferred_element_type=F32)

    o_k = ATT_WIDTH
    o_v = o_k + KV_WIDTH
    o_xr = o_v + KV_WIDTH
    rec_w = d - ATT_WIDTH
    o_yr = o_xr + rec_w
    rope = None
    if latent:
        rp = rope_ref[...]
        rope = (rp[:, 0:HEAD_DIM], rp[:, HEAD_DIM:2 * HEAD_DIM], rp[:, 2 * HEAD_DIM:3 * HEAD_DIM])
        q = proj(0, ATT_WIDTH)
        for hd in range(N_Q_HEADS):
            sl = slice(hd * HEAD_DIM, (hd + 1) * HEAD_DIM)
            q_ref[0, :, sl] = _qk_norm_rope(q[:, sl], gq_ref[...], rope).astype(BF16)
    k = proj(o_k, o_v)
    for hd in range(N_KV_HEADS):
        sl = slice(hd * HEAD_DIM, (hd + 1) * HEAD_DIM)
        k_ref[0, :, sl] = _qk_norm_rope(k[:, sl], gk_ref[...], rope).astype(BF16)
    v_ref[0] = proj(o_v, o_xr).astype(BF16)
    xr_ref[0] = proj(o_xr, o_yr)
    if latent:
        yr_ref[0] = proj(o_yr, o_yr + rec_w)


def _inproj(x, mods, g1, w_in_bf, rope_tab, g_q, g_k, *, latent, ctx_row, tm):
    b, s, d = x.shape
    n = w_in_bf.shape[1]
    rec_w = d - ATT_WIDTH
    grid = (b, s // tm)
    row_spec = lambda w: pl.BlockSpec((1, tm, w), lambda bi, i: (bi, i, 0))
    full2 = lambda a: pl.BlockSpec(a.shape, lambda bi, i: (0, 0))
    in_specs = [row_spec(d), full2(mods), pl.BlockSpec((1, d), lambda bi, i: (0, 0)),
                pl.BlockSpec((d, n), lambda bi, i: (0, 0), pipeline_mode=pl.Buffered(1))]
    args = [x, mods, g1.reshape(1, d), w_in_bf]
    out_specs, out_shape = [], []
    if latent:
        in_specs += [pl.BlockSpec((tm, 3 * HEAD_DIM), lambda bi, i: (i, 0)),
                     pl.BlockSpec((1, HEAD_DIM), lambda bi, i: (0, 0))]
        args += [rope_tab, g_q.reshape(1, HEAD_DIM)]
        out_specs.append(row_spec(ATT_WIDTH))
        out_shape.append(jax.ShapeDtypeStruct((b, s, ATT_WIDTH), BF16))
    in_specs.append(pl.BlockSpec((1, HEAD_DIM), lambda bi, i: (0, 0)))
    args.append(g_k.reshape(1, HEAD_DIM))
    out_specs += [row_spec(KV_WIDTH), row_spec(KV_WIDTH), row_spec(rec_w)]
    out_shape += [jax.ShapeDtypeStruct((b, s, KV_WIDTH), BF16), jax.ShapeDtypeStruct((b, s, KV_WIDTH), BF16),
                  jax.ShapeDtypeStruct((b, s, rec_w), F32)]
    if latent:
        out_specs.append(row_spec(rec_w))
        out_shape.append(jax.ShapeDtypeStruct((b, s, rec_w), F32))
    vmem = d * n * 2 + 2 * tm * d * 4 + 2 * tm * n * 4 + 3 * tm * d * 4 + 2 * tm * n * 4
    return pl.pallas_call(
        functools.partial(_inproj_kernel, d=d, latent=latent, ctx_row=ctx_row),
        grid=grid, in_specs=in_specs, out_specs=out_specs, out_shape=out_shape,
        compiler_params=pltpu.CompilerParams(dimension_semantics=("arbitrary", "arbitrary"),
                                             vmem_limit_bytes=_vmem_limit(vmem)),
        name="inproj_latent" if latent else "inproj_ctx",
    )(*args)


def _attn_kernel(q_ref, k_ref, v_ref, o_ref):
    k = k_ref[0]
    v = v_ref[0]
    scale = HEAD_DIM ** -0.5
    for g in range(GROUP):
        sl = slice(g * HEAD_DIM, (g + 1) * HEAD_DIM)
        s = lax.dot_general(q_ref[0, :, sl], k, (((1,), (1,)), ((), ())), preferred_element_type=F32)
        m = jnp.max(s, axis=-1, keepdims=True)
        p = jnp.exp((s - m) * scale)
        l = jnp.sum(p, axis=-1, keepdims=True)
        o = jnp.dot(p.astype(BF16), v, preferred_element_type=F32)
        o_ref[0, :, sl] = (o / l).astype(BF16)


def _attention(q, k_all, v_all, *, tq):
    b, s, _ = q.shape
    lk = k_all.shape[1]
    gw = GROUP * HEAD_DIM
    vmem = 4 * lk * HEAD_DIM * 2 * 2 + 4 * tq * gw * 2 + 4 * tq * lk * 4
    return pl.pallas_call(
        _attn_kernel,
        grid=(b, N_KV_HEADS, s // tq),
        in_specs=[pl.BlockSpec((1, tq, gw), lambda bi, h, i: (bi, i, h)),
                  pl.BlockSpec((1, lk, HEAD_DIM), lambda bi, h, i: (bi, 0, h)),
                  pl.BlockSpec((1, lk, HEAD_DIM), lambda bi, h, i: (bi, 0, h))],
        out_specs=pl.BlockSpec((1, tq, gw), lambda bi, h, i: (bi, i, h)),
        out_shape=jax.ShapeDtypeStruct((b, s, ATT_WIDTH), BF16),
        compiler_params=pltpu.CompilerParams(dimension_semantics=("arbitrary",) * 3,
                                             vmem_limit_bytes=_vmem_limit(vmem)),
        name="attention",
    )(q, k_all, v_all)


def _gelu_tanh(x):
    return 0.5 * x * (1.0 + jnp.tanh(math.sqrt(2.0 / math.pi) * (x + 0.044715 * x * x * x)))


def _rglru_kernel(xr_ref, xc_ref, yr_ref, cw_ref, cb_ref, wg_ref, bg_ref, lam_ref, o_ref,
                  xp, xpc, af, bf, ab, bb, caf, cbf, cab, cbb, *, s, c):
    nseg = SCAN_SEGMENTS
    seg = s // nseg
    cseg = c // nseg
    zeros_pad = jnp.zeros((CONV_PAD, LANES), F32)
    xp[0:CONV_PAD, :] = zeros_pad
    xp[CONV_PAD + s:2 * CONV_PAD + s, :] = zeros_pad
    xp[CONV_PAD:CONV_PAD + s, :] = xr_ref[0]
    xpc[0:CONV_PAD, :] = zeros_pad
    xpc[CONV_PAD + c:2 * CONV_PAD + c, :] = zeros_pad
    xpc[CONV_PAD:CONV_PAD + c, :] = xc_ref[0]

    cw = cw_ref[...]
    cb = cb_ref[...]
    wg = wg_ref[0]
    bg = bg_ref[...]
    sp = jax.nn.softplus(-lam_ref[...])

    def coeffs(src, lo, n):
        u = cb
        for j in range(CONV_W):
            u = u + src[CONV_PAD + lo + j - CONV_LEFT:CONV_PAD + lo + j - CONV_LEFT + n, :] * cw[j:j + 1, :]
        g = jnp.dot(u.astype(BF16), wg, preferred_element_type=F32) + bg
        out = []
        for r in range(2):
            ga = g[:, (2 * r) * LANES:(2 * r + 1) * LANES]
            gx = g[:, (2 * r + 1) * LANES:(2 * r + 2) * LANES]
            log_a = (-LRU_C) * jax.nn.sigmoid(ga) * sp[r:r + 1, :]
            a = jnp.exp(log_a)
            mult = jnp.sqrt(-jnp.tanh(log_a) * (1.0 + a * a))
            out.append((a, mult * jax.nn.sigmoid(gx) * u))
        return out

    (a0, b0), (a1, b1) = coeffs(xpc, 0, c)
    for q in range(nseg):
        rows = slice(q * cseg, (q + 1) * cseg)
        dst = pl.ds(q, cseg, stride=nseg)
        caf[dst, :] = a0[rows]
        cbf[dst, :] = b0[rows]
        cab[dst, :] = a1[rows]
        cbb[dst, :] = b1[rows]
    for q in range(nseg):
        (a0, b0), (a1, b1) = coeffs(xp, q * seg, seg)
        dst = pl.ds(q, seg, stride=nseg)
        af[dst, :] = a0
        bf[dst, :] = b0
        ab[dst, :] = a1
        bb[dst, :] = b1

    def scan(a_f, b_f, a_b, b_b, n, store):
        def body(j, carry):
            hf, pf, hb, pb = carry
            rf = pl.multiple_of(j * nseg, nseg)
            rb = pl.multiple_of((n - 1 - j) * nseg, nseg)
            av = a_f[pl.ds(rf, nseg), :]
            hf = av * hf + b_f[pl.ds(rf, nseg), :]
            pf = av * pf
            aw = a_b[pl.ds(rb, nseg), :]
            hb = aw * hb + b_b[pl.ds(rb, nseg), :]
            pb = aw * pb
            if store:
                a_f[pl.ds(rf, nseg), :] = pf
                b_f[pl.ds(rf, nseg), :] = hf
                a_b[pl.ds(rb, nseg), :] = pb
                b_b[pl.ds(rb, nseg), :] = hb
            return hf, pf, hb, pb
        z = jnp.zeros((nseg, LANES), F32)
        o = jnp.ones((nseg, LANES), F32)
        return lax.fori_loop(0, n, body, (z, o, z, o), unroll=8)

    def chain(h_end, p_end, h0, reverse):
        order = range(nseg - 1, -1, -1) if reverse else range(nseg)
        enter = [None] * nseg
        cur = h0
        for q in order:
            enter[q] = cur
            cur = h_end[q:q + 1, :] + p_end[q:q + 1, :] * cur
        return enter, cur

    zero_row = jnp.zeros((1, LANES), F32)
    hf, pf, hb, pb = scan(caf, cbf, cab, cbb, cseg, False)
    _, h0f = chain(hf, pf, zero_row, False)
    _, h0b = chain(hb, pb, zero_row, True)
    hf, pf, hb, pb = scan(af, bf, ab, bb, seg, True)
    enter_f, _ = chain(hf, pf, h0f, False)
    enter_b, _ = chain(hb, pb, h0b, True)
    for q in range(nseg):
        src = pl.ds(q, seg, stride=nseg)
        h = bf[src, :] + af[src, :] * enter_f[q] + bb[src, :] + ab[src, :] * enter_b[q]
        rows = slice(q * seg, (q + 1) * seg)
        o_ref[0, rows, :] = (h * _gelu_tanh(yr_ref[0, rows, :])).astype(BF16)


def _rglru(xr, xrc, yr, conv_w, conv_b, w_gates, b_gates, lam):
    b, s, w = xr.shape
    c = xrc.shape[1]
    nb = w // LANES
    slab = lambda n: pl.BlockSpec((1, n, LANES), lambda bi, j: (bi, 0, j))
    scr = lambda n: pltpu.VMEM((n, LANES), F32)
    vmem = (3 * 2 + 5) * s * LANES * 4 + 8 * s * LANES * 4
    return pl.pallas_call(
        functools.partial(_rglru_kernel, s=s, c=c),
        grid=(b, nb),
        in_specs=[slab(s), slab(c), slab(s),
                  pl.BlockSpec((CONV_W, LANES), lambda bi, j: (0, j)),
                  pl.BlockSpec((1, LANES), lambda bi, j: (0, j)),
                  pl.BlockSpec((1, LANES, 4 * LANES), lambda bi, j: (j, 0, 0)),
                  pl.BlockSpec((1, 4 * LANES), lambda bi, j: (0, j)),
                  pl.BlockSpec((2, LANES), lambda bi, j: (0, j))],
        out_specs=slab(s),
        out_shape=jax.ShapeDtypeStruct((b, s, w), BF16),
        scratch_shapes=[scr(s + 2 * CONV_PAD), scr(c + 2 * CONV_PAD),
                        scr(s), scr(s), scr(s), scr(s), scr(c), scr(c), scr(c), scr(c)],
        compiler_params=pltpu.CompilerParams(dimension_semantics=("arbitrary", "arbitrary"),
                                             vmem_limit_bytes=_vmem_limit(vmem)),
        name="rglru",
    )(xr, xrc, yr, conv_w, conv_b, w_gates, b_gates, lam)


def _pack_bf16_pair(lo, hi):
    lo_bits = lax.bitcast_convert_type(lo.astype(BF16).astype(F32), I32)
    hi_bits = lax.bitcast_convert_type(hi.astype(BF16).astype(F32), I32)
    return lax.shift_right_logical(lo_bits, 16) | (hi_bits & jnp.int32(-65536))


def _unpack_bf16_pair(w):
    lo = lax.bitcast_convert_type(lax.shift_left(w, 16), F32).astype(BF16)
    hi = lax.bitcast_convert_type(w & jnp.int32(-65536), F32).astype(BF16)
    return lo, hi


def _merge_kernel(att_ref, rec_ref, x_ref, mods_ref, ga_ref, gr_ref, wo_ref, g2_ref, wr_ref, br_ref,
                  x1_ref, h2_ref, idx_ref, gate_ref, mask_ref, *, d):
    row = pl.program_id(0)
    gt1 = mods_ref[pl.ds(row, 1), 2 * d:3 * d]
    sh2 = mods_ref[pl.ds(row, 1), 3 * d:4 * d]
    sc2 = mods_ref[pl.ds(row, 1), 4 * d:5 * d]
    an = _rms(att_ref[0].astype(F32), ga_ref[...]).astype(BF16)
    rn = _rms(rec_ref[0].astype(F32), gr_ref[...]).astype(BF16)
    mix = (jnp.dot(an, wo_ref[0:ATT_WIDTH, :], preferred_element_type=F32)
           + jnp.dot(rn, wo_ref[ATT_WIDTH:d, :], preferred_element_type=F32))
    x1 = x_ref[0] + gt1 * mix
    x1_ref[0] = x1
    h2 = _rms(x1, g2_ref[...]) * (1.0 + sc2) + sh2
    half = d // 2
    h2_ref[...] = _pack_bf16_pair(h2[:, 0:half], h2[:, half:d])
    logits = jnp.dot(h2, wr_ref[...], preferred_element_type=F32, precision=HIGHEST) + br_ref[...]
    tm = logits.shape[0]
    lane = lax.broadcasted_iota(I32, (tm, N_EXPERTS), 1).astype(F32)
    col = lax.broadcasted_iota(I32, (tm, TOP_K), 1)
    idx = jnp.zeros((tm, TOP_K), F32)
    ex = jnp.zeros((tm, TOP_K), F32)
    mask = jnp.zeros((tm, N_EXPERTS), F32)
    rest = logits
    top = None
    for k in range(TOP_K):
        m = jnp.max(rest, axis=-1, keepdims=True)
        first = jnp.min(jnp.where(rest == m, lane, float(N_EXPERTS)), axis=-1, keepdims=True)
        sel = lane == first
        if k == 0:
            top = m
        idx = jnp.where(col == k, first, idx)
        ex = jnp.where(col == k, jnp.exp(m - top), ex)
        mask = jnp.where(sel, 1.0, mask)
        rest = jnp.where(sel, -jnp.inf, rest)
    idx_ref[...] = idx.astype(I32)
    gate_ref[...] = ex / jnp.sum(ex, axis=-1, keepdims=True)
    mask_ref[...] = mask


def _merge(att, rec, x, mods, g_att, g_rec, w_out_bf, g2, w_router, b_router, *, tm):
    b, s, d = x.shape
    t = b * s
    nt = s // tm
    rec_w = d - ATT_WIDTH
    row3 = lambda w: pl.BlockSpec((1, tm, w), lambda bi, i: (bi, i, 0))
    tok2 = lambda w: pl.BlockSpec((tm, w), lambda bi, i: (bi * nt + i, 0))
    const = lambda shape, **kw: pl.BlockSpec(shape, lambda bi, i: (0,) * len(shape), **kw)
    vmem = d * d * 2 + 2 * tm * (ATT_WIDTH + rec_w) * 2 + 4 * tm * d * 4 + tm * d * 4 + 8 * tm * d * 4
    return pl.pallas_call(
        functools.partial(_merge_kernel, d=d),
        grid=(b, nt),
        in_specs=[row3(ATT_WIDTH), row3(rec_w), row3(d), const(mods.shape),
                  const((1, ATT_WIDTH)), const((1, rec_w)),
                  const((d, d), pipeline_mode=pl.Buffered(1)), const((1, d)),
                  const((d, N_EXPERTS)), const((1, N_EXPERTS))],
        out_specs=[row3(d), tok2(d // 2), tok2(TOP_K), tok2(TOP_K), tok2(N_EXPERTS)],
        out_shape=[jax.ShapeDtypeStruct((b, s, d), F32), jax.ShapeDtypeStruct((t, d // 2), I32),
                   jax.ShapeDtypeStruct((t, TOP_K), I32), jax.ShapeDtypeStruct((t, TOP_K), F32),
                   jax.ShapeDtypeStruct((t, N_EXPERTS), F32)],
        compiler_params=pltpu.CompilerParams(dimension_semantics=("arbitrary", "arbitrary"),
                                             vmem_limit_bytes=_vmem_limit(vmem)),
        name="merge",
    )(att, rec, x, mods, g_att.reshape(1, -1), g_rec.reshape(1, -1), w_out_bf, g2.reshape(1, d),
      w_router, b_router.reshape(1, N_EXPERTS))


def _rank_kernel(mask_ref, idx_ref, rank_ref, cnt_ref, carry):
    @pl.when(pl.program_id(0) == 0)
    def _():
        carry[...] = jnp.zeros_like(carry)

    m = mask_ref[...]
    tb = m.shape[0]
    r = lax.broadcasted_iota(I32, (tb, tb), 0)
    cidx = lax.broadcasted_iota(I32, (tb, tb), 1)
    tri = jnp.where(cidx < r, 1.0, 0.0).astype(BF16)
    before = jnp.dot(tri, m.astype(BF16), preferred_element_type=F32) + carry[...]
    lane = lax.broadcasted_iota(I32, (tb, N_EXPERTS), 1)
    col = lax.broadcasted_iota(I32, (tb, TOP_K), 1)
    idx = idx_ref[...]
    rank = jnp.zeros((tb, TOP_K), F32)
    for k in range(TOP_K):
        pick = jnp.sum(jnp.where(lane == idx[:, k:k + 1], before, 0.0), axis=-1, keepdims=True)
        rank = jnp.where(col == k, pick, rank)
    rank_ref[...] = rank.astype(I32)
    carry[...] = carry[...] + jnp.sum(m, axis=0, keepdims=True)
    cnt_ref[...] = carry[...].astype(I32)


def _rank(mask, idx, *, tb):
    t = mask.shape[0]
    return pl.pallas_call(
        _rank_kernel,
        grid=(t // tb,),
        in_specs=[pl.BlockSpec((tb, N_EXPERTS), lambda i: (i, 0)), pl.BlockSpec((tb, TOP_K), lambda i: (i, 0))],
        out_specs=[pl.BlockSpec((tb, TOP_K), lambda i: (i, 0)), pl.BlockSpec((1, N_EXPERTS), lambda i: (0, 0))],
        out_shape=[jax.ShapeDtypeStruct((t, TOP_K), I32), jax.ShapeDtypeStruct((1, N_EXPERTS), I32)],
        scratch_shapes=[pltpu.VMEM((1, N_EXPERTS), F32)],
        compiler_params=pltpu.CompilerParams(dimension_semantics=("arbitrary",)),
        name="rank",
    )(mask, idx)


def _dispatch_kernel(dest_ref, h2_ref, xs_in_ref, xs_ref, sem, *, tb):
    del xs_in_ref
    base = pl.program_id(0) * tb

    def row_copy(i, k):
        return pltpu.make_async_copy(h2_ref.at[pl.ds(i, 1)],
                                     xs_ref.at[pl.ds(dest_ref[(base + i) * TOP_K + k], 1)], sem)

    def body(i, _):
        for k in range(TOP_K):
            row_copy(i, k).start()
        return 0

    lax.fori_loop(0, tb, body, 0, unroll=4)
    for k in range(TOP_K):
        pltpu.make_async_copy(h2_ref, xs_ref.at[pl.ds(0, tb)], sem).wait()


def _dispatch(dest_flat, h2p, xs_init, *, tb):
    t, wd = h2p.shape
    p = xs_init.shape[0]
    return pl.pallas_call(
        functools.partial(_dispatch_kernel, tb=tb),
        grid_spec=pltpu.PrefetchScalarGridSpec(
            num_scalar_prefetch=1, grid=(t // tb,),
            in_specs=[pl.BlockSpec((tb, wd), lambda i, dr: (i, 0)), pl.BlockSpec(memory_space=pl.ANY)],
            out_specs=pl.BlockSpec(memory_space=pl.ANY),
            scratch_shapes=[pltpu.SemaphoreType.DMA(())]),
        out_shape=jax.ShapeDtypeStruct((p, wd), I32),
        input_output_aliases={2: 0},
        compiler_params=pltpu.CompilerParams(dimension_semantics=("arbitrary",), has_side_effects=True),
        name="dispatch",
    )(dest_flat, h2p, xs_init)


PAIR_CHUNK = 512


def _expert_changed(be_ref, nb_ref):
    s = pl.program_id(1)
    return (s == 0) | ((s < nb_ref[0]) & (be_ref[s] != be_ref[jnp.maximum(s - 1, 0)]))


def _gu_kernel(be_ref, nb_ref, xs_ref, w_ref, b_ref, act_ref, wbf):
    @pl.when(_expert_changed(be_ref, nb_ref))
    def _():
        wbf[...] = w_ref[0].astype(BF16)

    @pl.when(pl.program_id(1) < nb_ref[0])
    def _():
        lo, hi = _unpack_bf16_pair(xs_ref[...])
        xb = jnp.concatenate([lo, hi], axis=1)
        gu = jnp.dot(xb, wbf[...], preferred_element_type=F32) + b_ref[0]
        tn = gu.shape[1]
        g = jnp.minimum(gu, SWIGLU_LIMIT)
        up1 = jnp.clip(gu, -SWIGLU_LIMIT, SWIGLU_LIMIT) + 1.0
        paired = (pltpu.roll(up1, tn - 1, 1) * (g * jax.nn.sigmoid(SWIGLU_ALPHA * g))).astype(BF16)
        r = lax.broadcasted_iota(I32, (PAIR_CHUNK, PAIR_CHUNK // 2), 0)
        c = lax.broadcasted_iota(I32, (PAIR_CHUNK, PAIR_CHUNK // 2), 1)
        sel = jnp.where(r == 2 * c, 1.0, 0.0).astype(BF16)
        for ch in range(tn // PAIR_CHUNK):
            act_ref[:, ch * (PAIR_CHUNK // 2):(ch + 1) * (PAIR_CHUNK // 2)] = jnp.dot(
                paired[:, ch * PAIR_CHUNK:(ch + 1) * PAIR_CHUNK], sel, preferred_element_type=F32).astype(BF16)


def _moe_gu(block_e, n_valid, xs, w_gu, b_gu, *, tn):
    p, half = xs.shape
    d = 2 * half
    f2 = w_gu.shape[2]
    nj = f2 // tn
    nblk = p // MOE_BLOCK
    blk = lambda s, nb: jnp.minimum(s, nb[0] - 1)
    vmem = 2 * d * tn * 4 + d * tn * 2 + 2 * MOE_BLOCK * half * 4 + 2 * MOE_BLOCK * tn + 8 * MOE_BLOCK * tn * 4
    return pl.pallas_call(
        _gu_kernel,
        grid_spec=pltpu.PrefetchScalarGridSpec(
            num_scalar_prefetch=2, grid=(nj, nblk),
            in_specs=[pl.BlockSpec((MOE_BLOCK, half), lambda j, s, be, nb: (blk(s, nb), 0)),
                      pl.BlockSpec((1, d, tn), lambda j, s, be, nb: (be[blk(s, nb)], 0, j)),
                      pl.BlockSpec((1, 1, tn), lambda j, s, be, nb: (be[blk(s, nb)], 0, j))],
            out_specs=pl.BlockSpec((MOE_BLOCK, tn // 2), lambda j, s, be, nb: (blk(s, nb), j)),
            scratch_shapes=[pltpu.VMEM((d, tn), BF16)]),
        out_shape=jax.ShapeDtypeStruct((p, f2 // 2), BF16),
        compiler_params=pltpu.CompilerParams(dimension_semantics=("arbitrary", "arbitrary"),
                                             vmem_limit_bytes=_vmem_limit(vmem)),
        name="moe_gu",
    )(block_e, n_valid, xs, w_gu, b_gu)


def _down_kernel(be_ref, nb_ref, act_ref, w_ref, b_ref, y_ref, wbf):
    @pl.when(_expert_changed(be_ref, nb_ref))
    def _():
        wbf[...] = w_ref[0].astype(BF16)

    @pl.when(pl.program_id(1) < nb_ref[0])
    def _():
        y_ref[...] = jnp.dot(act_ref[...], wbf[...], preferred_element_type=F32) + b_ref[0]


def _moe_down(block_e, n_valid, act, w_down, b_down, *, tn):
    p, f = act.shape
    d = w_down.shape[2]
    nblk = p // MOE_BLOCK
    blk = lambda s, nb: jnp.minimum(s, nb[0] - 1)
    vmem = 2 * f * tn * 4 + f * tn * 2 + 2 * MOE_BLOCK * f * 2 + 4 * MOE_BLOCK * tn * 4
    return pl.pallas_call(
        _down_kernel,
        grid_spec=pltpu.PrefetchScalarGridSpec(
            num_scalar_prefetch=2, grid=(d // tn, nblk),
            in_specs=[pl.BlockSpec((MOE_BLOCK, f), lambda j, s, be, nb: (blk(s, nb), 0)),
                      pl.BlockSpec((1, f, tn), lambda j, s, be, nb: (be[blk(s, nb)], 0, j)),
                      pl.BlockSpec((1, 1, tn), lambda j, s, be, nb: (be[blk(s, nb)], 0, j))],
            out_specs=pl.BlockSpec((MOE_BLOCK, tn), lambda j, s, be, nb: (blk(s, nb), j)),
            scratch_shapes=[pltpu.VMEM((f, tn), BF16)]),
        out_shape=jax.ShapeDtypeStruct((p, d), F32),
        compiler_params=pltpu.CompilerParams(dimension_semantics=("arbitrary", "arbitrary"),
                                             vmem_limit_bytes=_vmem_limit(vmem)),
        name="moe_down",
    )(block_e, n_valid, act, w_down, b_down)


def _combine_kernel(dest_ref, ys_ref, x1_ref, gate_ref, mods_ref, o_ref, buf, sem, *, tb, d, nt):
    base = (pl.program_id(0) * nt + pl.program_id(1)) * tb

    def row_copy(i, k):
        return pltpu.make_async_copy(ys_ref.at[pl.ds(dest_ref[(base + i) * TOP_K + k], 1)],
                                     buf.at[k, pl.ds(i, 1)], sem)

    def body(i, _):
        for k in range(TOP_K):
            row_copy(i, k).start()
        return 0

    lax.fori_loop(0, tb, body, 0)
    for k in range(TOP_K):
        pltpu.make_async_copy(ys_ref.at[pl.ds(0, tb)], buf.at[k], sem).wait()
    gt2 = mods_ref[pl.ds(pl.program_id(0), 1), 5 * d:6 * d]
    gate = gate_ref[...]
    y = buf[0] * gate[:, 0:1]
    for k in range(1, TOP_K):
        y = y + buf[k] * gate[:, k:k + 1]
    o_ref[0] = x1_ref[0] + gt2 * y


def _combine(dest_flat, ys, x1, gate, mods, *, tb):
    b, s, d = x1.shape
    nt = s // tb
    vmem = TOP_K * tb * d * 4 + 4 * tb * d * 4 + 4 * tb * d * 4
    return pl.pallas_call(
        functools.partial(_combine_kernel, tb=tb, d=d, nt=nt),
        grid_spec=pltpu.PrefetchScalarGridSpec(
            num_scalar_prefetch=1, grid=(b, nt),
            in_specs=[pl.BlockSpec(memory_space=pl.ANY),
                      pl.BlockSpec((1, tb, d), lambda bi, i, dr: (bi, i, 0)),
                      pl.BlockSpec((tb, TOP_K), lambda bi, i, dr: (bi * nt + i, 0)),
                      pl.BlockSpec(mods.shape, lambda bi, i, dr: (0, 0))],
            out_specs=pl.BlockSpec((1, tb, d), lambda bi, i, dr: (bi, i, 0)),
            scratch_shapes=[pltpu.VMEM((TOP_K, tb, d), F32), pltpu.SemaphoreType.DMA(())]),
        out_shape=jax.ShapeDtypeStruct((b, s, d), F32),
        compiler_params=pltpu.CompilerParams(dimension_semantics=("arbitrary", "arbitrary"),
                                             vmem_limit_bytes=_vmem_limit(vmem)),
        name="combine",
    )(dest_flat, ys, x1, gate, mods)


def _rope_table(s):
    rows = s // GRID_W
    row = jnp.repeat(jnp.arange(rows), GRID_W).astype(F32)
    col = jnp.tile(jnp.arange(GRID_W), rows).astype(F32)
    inv = ROPE_THETA ** (-jnp.arange(ROPE_FREQS, dtype=F32) / ROPE_FREQS)
    ang_r = row[:, None] * inv
    ang_c = col[:, None] * inv
    z = jnp.zeros_like(ang_r)
    cosf = jnp.concatenate([jnp.cos(ang_r)] * 2 + [jnp.cos(ang_c)] * 2, axis=1)
    sneg = jnp.concatenate([-jnp.sin(ang_r), z, -jnp.sin(ang_c), z], axis=1)
    spos = jnp.concatenate([z, jnp.sin(ang_r), z, jnp.sin(ang_c)], axis=1)
    return jnp.concatenate([cosf, sneg, spos], axis=1)


def kernel(x, c, ctx, c_ctx, w_mod, b_mod, g_norm1, w_in, g_q, g_k, conv_w, conv_b, w_gate_a, b_gate_a,
           w_gate_x, b_gate_x, lru_lambda, g_att_out, g_rec_out, w_out, g_norm2, w_router, b_router,
           w_gate_up, b_gate_up, w_down, b_down):
    b, s, d = x.shape
    cl = ctx.shape[1]
    t = b * s
    assert w_mod.shape[0] == 1, "single-layer kernel"
    assert b + 1 <= SUBLANES and d - ATT_WIDTH == REC_BLOCKS * LANES
    assert s % (SCAN_SEGMENTS * SUBLANES) == 0 and cl % (SCAN_SEGMENTS * SUBLANES) == 0

    ctx_row = b
    c8 = jnp.zeros((SUBLANES, d), F32).at[:b].set(c).at[ctx_row].set(c_ctx)
    mods = _mod(c8, w_mod[0], b_mod[0])

    w_in_bf = w_in[0].astype(BF16)
    tm = min(512, s)
    q, k, v, xr, yr = _inproj(x, mods, g_norm1[0], w_in_bf, _rope_table(s), g_q[0], g_k[0],
                              latent=True, ctx_row=ctx_row, tm=tm)
    kc, vc, xrc = _inproj(ctx, mods, g_norm1[0], w_in_bf, None, g_q[0], g_k[0],
                          latent=False, ctx_row=ctx_row, tm=cl)

    att = _attention(q, jnp.concatenate([kc, k], axis=1), jnp.concatenate([vc, v], axis=1), tq=min(256, s))

    w_gates = jnp.concatenate([w_gate_a[0, 0], w_gate_x[0, 0], w_gate_a[0, 1], w_gate_x[0, 1]], axis=-1).astype(BF16)
    rw = d - ATT_WIDTH
    bias = lambda bb: bb.reshape(REC_BLOCKS, LANES)
    b_gates = jnp.concatenate([bias(b_gate_a[0, 0]), bias(b_gate_x[0, 0]), bias(b_gate_a[0, 1]), bias(b_gate_x[0, 1])],
                              axis=-1).reshape(1, 4 * rw)
    rec = _rglru(xr, xrc, yr, conv_w[0], conv_b[0].reshape(1, rw), w_gates, b_gates, lru_lambda[0])

    x1, h2p, top_idx, gate, mask = _merge(att, rec, x, mods, g_att_out[0], g_rec_out[0], w_out[0].astype(BF16),
                                          g_norm2[0], w_router[0], b_router[0], tm=tm)

    rank, counts = _rank(mask, top_idx, tb=MOE_BLOCK)
    counts = counts[0]
    padded = (counts + MOE_BLOCK - 1) // MOE_BLOCK * MOE_BLOCK
    pad_ends = jnp.cumsum(padded)
    pad_starts = pad_ends - padded
    dest = (pad_starts[top_idx] + rank).reshape(t * TOP_K)
    n_blocks = (t * TOP_K + N_EXPERTS * (MOE_BLOCK - 1) + MOE_BLOCK - 1) // MOE_BLOCK
    p = n_blocks * MOE_BLOCK
    block_start = jnp.arange(n_blocks, dtype=I32) * MOE_BLOCK
    block_e = jnp.minimum(jnp.sum((pad_ends[None, :] <= block_start[:, None]).astype(I32), axis=1), N_EXPERTS - 1)
    n_valid = (pad_ends[N_EXPERTS - 1] // MOE_BLOCK).reshape(1).astype(I32)

    xs = _dispatch(dest, h2p, jnp.zeros((p, d // 2), I32), tb=MOE_BLOCK)

    f2 = w_gate_up.shape[3]
    act = _moe_gu(block_e, n_valid, xs, w_gate_up[0], b_gate_up[0].reshape(N_EXPERTS, 1, f2), tn=2048)
    ys = _moe_down(block_e, n_valid, act, w_down[0], b_down[0].reshape(N_EXPERTS, 1, d), tn=1024)

    return _combine(dest, ys, x1, gate, mods, tb=128)
```

```python
import functools
import math

import jax
import jax.numpy as jnp
from jax import lax
from jax.experimental import pallas as pl
from jax.experimental.pallas import tpu as pltpu

F32 = jnp.float32
BF16 = jnp.bfloat16
I32 = jnp.int32

EPS = 1e-6
GRID_W = 64
HEAD_DIM = 128
N_Q_HEADS = 8
N_KV_HEADS = 2
GROUP = N_Q_HEADS // N_KV_HEADS
ATT_WIDTH = N_Q_HEADS * HEAD_DIM
KV_WIDTH = N_KV_HEADS * HEAD_DIM
ROPE_THETA = 10000.0
ROPE_FREQS = HEAD_DIM // 4
REC_BLOCKS = 8
CONV_W = 4
CONV_LEFT = 2
LRU_C = 8.0
N_EXPERTS = 32
TOP_K = 4
SWIGLU_LIMIT = 7.0
SWIGLU_ALPHA = 1.702
MOE_BLOCK = 256

V7X_VMEM_BYTES = 64 * 1024 * 1024
SUBLANES = 8
LANES = 128
SCAN_SEGMENTS = SUBLANES
CONV_PAD = SUBLANES

HIGHEST = lax.Precision.HIGHEST


def _vmem_limit(nbytes):
    return int(min(V7X_VMEM_BYTES - 4 * 1024 * 1024, max(nbytes, 16 * 1024 * 1024)))


def _rms(x, g):
    return x * lax.rsqrt(jnp.mean(x * x, axis=-1, keepdims=True) + EPS) * g


def _mod_kernel(c_ref, w_ref, b_ref, o_ref):
    c = c_ref[...]
    a = c * jax.nn.sigmoid(c)
    o_ref[...] = jnp.dot(a, w_ref[...], preferred_element_type=F32, precision=HIGHEST) + b_ref[...]


def _mod(c8, w_mod, b_mod):
    d, n = w_mod.shape
    tn = 1024
    return pl.pallas_call(
        _mod_kernel,
        grid=(n // tn,),
        in_specs=[pl.BlockSpec((SUBLANES, d), lambda j: (0, 0)),
                  pl.BlockSpec((d, tn), lambda j: (0, j)),
                  pl.BlockSpec((1, tn), lambda j: (0, j))],
        out_specs=pl.BlockSpec((SUBLANES, tn), lambda j: (0, j)),
        out_shape=jax.ShapeDtypeStruct((SUBLANES, n), F32),
        compiler_params=pltpu.CompilerParams(dimension_semantics=("arbitrary",),
                                             vmem_limit_bytes=_vmem_limit(3 * d * tn * 4)),
        name="mod",
    )(c8, w_mod, b_mod.reshape(1, n))


def _qk_norm_rope(y, g, rope):
    yn = _rms(y, g)
    if rope is None:
        return yn
    cosf, sneg, spos = rope
    return yn * cosf + pltpu.roll(yn, HEAD_DIM - ROPE_FREQS, 1) * sneg + pltpu.roll(yn, ROPE_FREQS, 1) * spos


def _inproj_kernel(*refs, d, latent, ctx_row):
    if latent:
        (x_ref, mods_ref, g1_ref, w_ref, rope_ref, gq_ref, gk_ref,
         q_ref, k_ref, v_ref, xr_ref, yr_ref) = refs
        row = pl.program_id(0)
    else:
        x_ref, mods_ref, g1_ref, w_ref, gk_ref, k_ref, v_ref, xr_ref = refs
        row = ctx_row
    sh = mods_ref[pl.ds(row, 1), 0:d]
    sc = mods_ref[pl.ds(row, 1), d:2 * d]
    h = _rms(x_ref[0], g1_ref[...]) * (1.0 + sc) + sh
    hb = h.astype(BF16)

    def proj(lo, hi):
        return jnp.dot(hb, w_ref[:, lo:hi], preferred_element_type=F32)

    o_k = ATT_WIDTH
    o_v = o_k + KV_WIDTH
    o_xr = o_v + KV_WIDTH
    rec_w = d - ATT_WIDTH
    o_yr = o_xr + rec_w
    rope = None
    if latent:
        rp = rope_ref[...]
        rope = (rp[:, 0:HEAD_DIM], rp[:, HEAD_DIM:2 * HEAD_DIM], rp[:, 2 * HEAD_DIM:3 * HEAD_DIM])
        q = proj(0, ATT_WIDTH)
        for hd in range(N_Q_HEADS):
            sl = slice(hd * HEAD_DIM, (hd + 1) * HEAD_DIM)
            q_ref[0, :, sl] = (_qk_norm_rope(q[:, sl], gq_ref[...], rope) * (HEAD_DIM ** -0.5)).astype(BF16)
    k = proj(o_k, o_v)
    for hd in range(N_KV_HEADS):
        sl = slice(hd * HEAD_DIM, (hd + 1) * HEAD_DIM)
        k_ref[0, :, sl] = _qk_norm_rope(k[:, sl], gk_ref[...], rope).astype(BF16)
    v_ref[0] = proj(o_v, o_xr).astype(BF16)
    xr_ref[0] = proj(o_xr, o_yr)
    if latent:
        yr_ref[0] = proj(o_yr, o_yr + rec_w)


def _inproj(x, mods, g1, w_in_bf, rope_tab, g_q, g_k, *, latent, ctx_row, tm):
    b, s, d = x.shape
    n = w_in_bf.shape[1]
    rec_w = d - ATT_WIDTH
    grid = (b, s // tm)
    row_spec = lambda w: pl.BlockSpec((1, tm, w), lambda bi, i: (bi, i, 0))
    full2 = lambda a: pl.BlockSpec(a.shape, lambda bi, i: (0, 0))
    in_specs = [row_spec(d), full2(mods), pl.BlockSpec((1, d), lambda bi, i: (0, 0)),
                pl.BlockSpec((d, n), lambda bi, i: (0, 0), pipeline_mode=pl.Buffered(1))]
    args = [x, mods, g1.reshape(1, d), w_in_bf]
    out_specs, out_shape = [], []
    if latent:
        in_specs += [pl.BlockSpec((tm, 3 * HEAD_DIM), lambda bi, i: (i, 0)),
                     pl.BlockSpec((1, HEAD_DIM), lambda bi, i: (0, 0))]
        args += [rope_tab, g_q.reshape(1, HEAD_DIM)]
        out_specs.append(row_spec(ATT_WIDTH))
        out_shape.append(jax.ShapeDtypeStruct((b, s, ATT_WIDTH), BF16))
    in_specs.append(pl.BlockSpec((1, HEAD_DIM), lambda bi, i: (0, 0)))
    args.append(g_k.reshape(1, HEAD_DIM))
    out_specs += [row_spec(KV_WIDTH), row_spec(KV_WIDTH), row_spec(rec_w)]
    out_shape += [jax.ShapeDtypeStruct((b, s, KV_WIDTH), BF16), jax.ShapeDtypeStruct((b, s, KV_WIDTH), BF16),
                  jax.ShapeDtypeStruct((b, s, rec_w), F32)]
    if latent:
        out_specs.append(row_spec(rec_w))
        out_shape.append(jax.ShapeDtypeStruct((b, s, rec_w), F32))
    vmem = d * n * 2 + 2 * tm * d * 4 + 2 * tm * n * 4 + 3 * tm * d * 4 + 2 * tm * n * 4
    return pl.pallas_call(
        functools.partial(_inproj_kernel, d=d, latent=latent, ctx_row=ctx_row),
        grid=grid, in_specs=in_specs, out_specs=out_specs, out_shape=out_shape,
        compiler_params=pltpu.CompilerParams(dimension_semantics=("arbitrary", "arbitrary"),
                                             vmem_limit_bytes=_vmem_limit(vmem)),
        name="inproj_latent" if latent else "inproj_ctx",
    )(*args)


def _attn_kernel(q_ref, k_ref, v_ref, o_ref):
    k = k_ref[0]
    v = v_ref[0]
    for g in range(GROUP):
        sl = slice(g * HEAD_DIM, (g + 1) * HEAD_DIM)
        s = lax.dot_general(q_ref[0, :, sl], k, (((1,), (1,)), ((), ())), preferred_element_type=F32)
        m = jnp.max(s, axis=-1, keepdims=True)
        p = jnp.exp(s - m)
        l = jnp.sum(p, axis=-1, keepdims=True)
        o = jnp.dot(p.astype(BF16), v, preferred_element_type=F32)
        o_ref[0, :, sl] = (o / l).astype(BF16)


def _attention(q, k_all, v_all, *, tq):
    b, s, _ = q.shape
    lk = k_all.shape[1]
    gw = GROUP * HEAD_DIM
    vmem = 4 * lk * HEAD_DIM * 2 * 2 + 4 * tq * gw * 2 + 4 * tq * lk * 4
    return pl.pallas_call(
        _attn_kernel,
        grid=(b, N_KV_HEADS, s // tq),
        in_specs=[pl.BlockSpec((1, tq, gw), lambda bi, h, i: (bi, i, h)),
                  pl.BlockSpec((1, lk, HEAD_DIM), lambda bi, h, i: (bi, 0, h)),
                  pl.BlockSpec((1, lk, HEAD_DIM), lambda bi, h, i: (bi, 0, h))],
        out_specs=pl.BlockSpec((1, tq, gw), lambda bi, h, i: (bi, i, h)),
        out_shape=jax.ShapeDtypeStruct((b, s, ATT_WIDTH), BF16),
        compiler_params=pltpu.CompilerParams(dimension_semantics=("arbitrary",) * 3,
                                             vmem_limit_bytes=_vmem_limit(vmem)),
        name="attention",
    )(q, k_all, v_all)


def _gelu_tanh(x):
    return 0.5 * x * (1.0 + jnp.tanh(math.sqrt(2.0 / math.pi) * (x + 0.044715 * x * x * x)))


def _rglru_kernel(xr_ref, xc_ref, yr_ref, cw_ref, cb_ref, wg_ref, bg_ref, lam_ref, o_ref,
                  xp, xpc, af, bf, ab, bb, caf, cbf, cab, cbb, *, s, c):
    nseg = SCAN_SEGMENTS
    seg = s // nseg
    cseg = c // nseg
    zeros_pad = jnp.zeros((CONV_PAD, LANES), F32)
    xp[0:CONV_PAD, :] = zeros_pad
    xp[CONV_PAD + s:2 * CONV_PAD + s, :] = zeros_pad
    xp[CONV_PAD:CONV_PAD + s, :] = xr_ref[0]
    xpc[0:CONV_PAD, :] = zeros_pad
    xpc[CONV_PAD + c:2 * CONV_PAD + c, :] = zeros_pad
    xpc[CONV_PAD:CONV_PAD + c, :] = xc_ref[0]

    cw = cw_ref[...]
    cb = cb_ref[...]
    wg = wg_ref[0]
    bg = bg_ref[...]
    sp = jax.nn.softplus(-lam_ref[...])

    def coeffs(src, lo, n):
        u = cb
        for j in range(CONV_W):
            u = u + src[CONV_PAD + lo + j - CONV_LEFT:CONV_PAD + lo + j - CONV_LEFT + n, :] * cw[j:j + 1, :]
        g = jnp.dot(u.astype(BF16), wg, preferred_element_type=F32) + bg
        out = []
        for r in range(2):
            ga = g[:, (2 * r) * LANES:(2 * r + 1) * LANES]
            gx = g[:, (2 * r + 1) * LANES:(2 * r + 2) * LANES]
            log_a = (-LRU_C) * jax.nn.sigmoid(ga) * sp[r:r + 1, :]
            a = jnp.exp(log_a)
            mult = jnp.sqrt(-jnp.tanh(log_a) * (1.0 + a * a))
            out.append((a, mult * jax.nn.sigmoid(gx) * u))
        return out

    (a0, b0), (a1, b1) = coeffs(xpc, 0, c)
    for q in range(nseg):
        rows = slice(q * cseg, (q + 1) * cseg)
        dst = pl.ds(q, cseg, stride=nseg)
        caf[dst, :] = a0[rows]
        cbf[dst, :] = b0[rows]
        cab[dst, :] = a1[rows]
        cbb[dst, :] = b1[rows]
    for q in range(nseg):
        (a0, b0), (a1, b1) = coeffs(xp, q * seg, seg)
        dst = pl.ds(q, seg, stride=nseg)
        af[dst, :] = a0
        bf[dst, :] = b0
        ab[dst, :] = a1
        bb[dst, :] = b1

    def scan(a_f, b_f, a_b, b_b, n, store):
        def body(j, carry):
            hf, pf, hb, pb = carry
            rf = pl.multiple_of(j * nseg, nseg)
            rb = pl.multiple_of((n - 1 - j) * nseg, nseg)
            av = a_f[pl.ds(rf, nseg), :]
            hf = av * hf + b_f[pl.ds(rf, nseg), :]
            pf = av * pf
            aw = a_b[pl.ds(rb, nseg), :]
            hb = aw * hb + b_b[pl.ds(rb, nseg), :]
            pb = aw * pb
            if store:
                a_f[pl.ds(rf, nseg), :] = pf
                b_f[pl.ds(rf, nseg), :] = hf
                a_b[pl.ds(rb, nseg), :] = pb
                b_b[pl.ds(rb, nseg), :] = hb
            return hf, pf, hb, pb
        z = jnp.zeros((nseg, LANES), F32)
        o = jnp.ones((nseg, LANES), F32)
        return lax.fori_loop(0, n, body, (z, o, z, o), unroll=8)

    def chain(h_end, p_end, h0, reverse):
        order = range(nseg - 1, -1, -1) if reverse else range(nseg)
        enter = [None] * nseg
        cur = h0
        for q in order:
            enter[q] = cur
            cur = h_end[q:q + 1, :] + p_end[q:q + 1, :] * cur
        return enter, cur

    zero_row = jnp.zeros((1, LANES), F32)
    hf, pf, hb, pb = scan(caf, cbf, cab, cbb, cseg, False)
    _, h0f = chain(hf, pf, zero_row, False)
    _, h0b = chain(hb, pb, zero_row, True)
    hf, pf, hb, pb = scan(af, bf, ab, bb, seg, True)
    enter_f, _ = chain(hf, pf, h0f, False)
    enter_b, _ = chain(hb, pb, h0b, True)
    for q in range(nseg):
        src = pl.ds(q, seg, stride=nseg)
        h = bf[src, :] + af[src, :] * enter_f[q] + bb[src, :] + ab[src, :] * enter_b[q]
        rows = slice(q * seg, (q + 1) * seg)
        o_ref[0, rows, :] = (h * _gelu_tanh(yr_ref[0, rows, :])).astype(BF16)


def _rglru(xr, xrc, yr, conv_w, conv_b, w_gates, b_gates, lam):
    b, s, w = xr.shape
    c = xrc.shape[1]
    nb = w // LANES
    slab = lambda n: pl.BlockSpec((1, n, LANES), lambda bi, j: (bi, 0, j))
    scr = lambda n: pltpu.VMEM((n, LANES), F32)
    vmem = (3 * 2 + 5) * s * LANES * 4 + 8 * s * LANES * 4
    return pl.pallas_call(
        functools.partial(_rglru_kernel, s=s, c=c),
        grid=(b, nb),
        in_specs=[slab(s), slab(c), slab(s),
                  pl.BlockSpec((CONV_W, LANES), lambda bi, j: (0, j)),
                  pl.BlockSpec((1, LANES), lambda bi, j: (0, j)),
                  pl.BlockSpec((1, LANES, 4 * LANES), lambda bi, j: (j, 0, 0)),
                  pl.BlockSpec((1, 4 * LANES), lambda bi, j: (0, j)),
                  pl.BlockSpec((2, LANES), lambda bi, j: (0, j))],
        out_specs=slab(s),
        out_shape=jax.ShapeDtypeStruct((b, s, w), BF16),
        scratch_shapes=[scr(s + 2 * CONV_PAD), scr(c + 2 * CONV_PAD),
                        scr(s), scr(s), scr(s), scr(s), scr(c), scr(c), scr(c), scr(c)],
        compiler_params=pltpu.CompilerParams(dimension_semantics=("arbitrary", "arbitrary"),
                                             vmem_limit_bytes=_vmem_limit(vmem)),
        name="rglru",
    )(xr, xrc, yr, conv_w, conv_b, w_gates, b_gates, lam)


def _pack_bf16_pair(lo, hi):
    lo_bits = lax.bitcast_convert_type(lo.astype(BF16).astype(F32), I32)
    hi_bits = lax.bitcast_convert_type(hi.astype(BF16).astype(F32), I32)
    return lax.shift_right_logical(lo_bits, 16) | (hi_bits & jnp.int32(-65536))


def _unpack_bf16_pair(w):
    lo = lax.bitcast_convert_type(lax.shift_left(w, 16), F32).astype(BF16)
    hi = lax.bitcast_convert_type(w & jnp.int32(-65536), F32).astype(BF16)
    return lo, hi


def _merge_kernel(att_ref, rec_ref, x_ref, mods_ref, ga_ref, gr_ref, wo_ref, g2_ref, wr_ref, br_ref,
                  x1_ref, h2_ref, idx_ref, gate_ref, mask_ref, *, d):
    row = pl.program_id(0)
    gt1 = mods_ref[pl.ds(row, 1), 2 * d:3 * d]
    sh2 = mods_ref[pl.ds(row, 1), 3 * d:4 * d]
    sc2 = mods_ref[pl.ds(row, 1), 4 * d:5 * d]
    an = _rms(att_ref[0].astype(F32), ga_ref[...]).astype(BF16)
    rn = _rms(rec_ref[0].astype(F32), gr_ref[...]).astype(BF16)
    mix = (jnp.dot(an, wo_ref[0:ATT_WIDTH, :], preferred_element_type=F32)
           + jnp.dot(rn, wo_ref[ATT_WIDTH:d, :], preferred_element_type=F32))
    x1 = x_ref[0] + gt1 * mix
    x1_ref[0] = x1
    h2 = _rms(x1, g2_ref[...]) * (1.0 + sc2) + sh2
    half = d // 2
    h2_ref[...] = _pack_bf16_pair(h2[:, 0:half], h2[:, half:d])
    h_hi = h2.astype(BF16)
    h_lo = (h2 - h_hi.astype(F32)).astype(BF16)
    w_split = wr_ref[...]
    part = (jnp.dot(h_hi, w_split, preferred_element_type=F32)
            + jnp.dot(h_lo, w_split, preferred_element_type=F32))
    logits = part[:, 0:N_EXPERTS] + part[:, N_EXPERTS:2 * N_EXPERTS] + br_ref[...]
    tm = logits.shape[0]
    lane = lax.broadcasted_iota(I32, (tm, N_EXPERTS), 1).astype(F32)
    col = lax.broadcasted_iota(I32, (tm, TOP_K), 1)
    idx = jnp.zeros((tm, TOP_K), F32)
    ex = jnp.zeros((tm, TOP_K), F32)
    mask = jnp.zeros((tm, N_EXPERTS), F32)
    rest = logits
    top = None
    for k in range(TOP_K):
        m = jnp.max(rest, axis=-1, keepdims=True)
        first = jnp.min(jnp.where(rest == m, lane, float(N_EXPERTS)), axis=-1, keepdims=True)
        sel = lane == first
        if k == 0:
            top = m
        idx = jnp.where(col == k, first, idx)
        ex = jnp.where(col == k, jnp.exp(m - top), ex)
        mask = jnp.where(sel, 1.0, mask)
        rest = jnp.where(sel, -jnp.inf, rest)
    idx_ref[...] = idx.astype(I32)
    gate_ref[...] = ex / jnp.sum(ex, axis=-1, keepdims=True)
    mask_ref[...] = mask


def _merge(att, rec, x, mods, g_att, g_rec, w_out_bf, g2, w_router, b_router, *, tm):
    b, s, d = x.shape
    t = b * s
    nt = s // tm
    rec_w = d - ATT_WIDTH
    row3 = lambda w: pl.BlockSpec((1, tm, w), lambda bi, i: (bi, i, 0))
    tok2 = lambda w: pl.BlockSpec((tm, w), lambda bi, i: (bi * nt + i, 0))
    const = lambda shape, **kw: pl.BlockSpec(shape, lambda bi, i: (0,) * len(shape), **kw)
    vmem = d * d * 2 + 2 * tm * (ATT_WIDTH + rec_w) * 2 + 4 * tm * d * 4 + tm * d * 4 + 8 * tm * d * 4
    return pl.pallas_call(
        functools.partial(_merge_kernel, d=d),
        grid=(b, nt),
        in_specs=[row3(ATT_WIDTH), row3(rec_w), row3(d), const(mods.shape),
                  const((1, ATT_WIDTH)), const((1, rec_w)),
                  const((d, d), pipeline_mode=pl.Buffered(1)), const((1, d)),
                  const((d, 2 * N_EXPERTS)), const((1, N_EXPERTS))],
        out_specs=[row3(d), tok2(d // 2), tok2(TOP_K), tok2(TOP_K), tok2(N_EXPERTS)],
        out_shape=[jax.ShapeDtypeStruct((b, s, d), F32), jax.ShapeDtypeStruct((t, d // 2), I32),
                   jax.ShapeDtypeStruct((t, TOP_K), I32), jax.ShapeDtypeStruct((t, TOP_K), F32),
                   jax.ShapeDtypeStruct((t, N_EXPERTS), F32)],
        compiler_params=pltpu.CompilerParams(dimension_semantics=("arbitrary", "arbitrary"),
                                             vmem_limit_bytes=_vmem_limit(vmem)),
        name="merge",
    )(att, rec, x, mods, g_att.reshape(1, -1), g_rec.reshape(1, -1), w_out_bf, g2.reshape(1, d),
      _split_bf16(w_router), b_router.reshape(1, N_EXPERTS))


def _split_bf16(w):
    hi = w.astype(BF16)
    lo = (w - hi.astype(F32)).astype(BF16)
    return jnp.concatenate([hi, lo], axis=1)


def _rank_kernel(mask_ref, idx_ref, rank_ref, cnt_ref, carry):
    @pl.when(pl.program_id(0) == 0)
    def _():
        carry[...] = jnp.zeros_like(carry)

    m = mask_ref[...]
    tb = m.shape[0]
    r = lax.broadcasted_iota(I32, (tb, tb), 0)
    cidx = lax.broadcasted_iota(I32, (tb, tb), 1)
    tri = jnp.where(cidx < r, 1.0, 0.0).astype(BF16)
    before = jnp.dot(tri, m.astype(BF16), preferred_element_type=F32) + carry[...]
    lane = lax.broadcasted_iota(I32, (tb, N_EXPERTS), 1)
    col = lax.broadcasted_iota(I32, (tb, TOP_K), 1)
    idx = idx_ref[...]
    rank = jnp.zeros((tb, TOP_K), F32)
    for k in range(TOP_K):
        pick = jnp.sum(jnp.where(lane == idx[:, k:k + 1], before, 0.0), axis=-1, keepdims=True)
        rank = jnp.where(col == k, pick, rank)
    rank_ref[...] = rank.astype(I32)
    carry[...] = carry[...] + jnp.sum(m, axis=0, keepdims=True)
    cnt_ref[...] = carry[...].astype(I32)


def _rank(mask, idx, *, tb):
    t = mask.shape[0]
    return pl.pallas_call(
        _rank_kernel,
        grid=(t // tb,),
        in_specs=[pl.BlockSpec((tb, N_EXPERTS), lambda i: (i, 0)), pl.BlockSpec((tb, TOP_K), lambda i: (i, 0))],
        out_specs=[pl.BlockSpec((tb, TOP_K), lambda i: (i, 0)), pl.BlockSpec((1, N_EXPERTS), lambda i: (0, 0))],
        out_shape=[jax.ShapeDtypeStruct((t, TOP_K), I32), jax.ShapeDtypeStruct((1, N_EXPERTS), I32)],
        scratch_shapes=[pltpu.VMEM((1, N_EXPERTS), F32)],
        compiler_params=pltpu.CompilerParams(dimension_semantics=("arbitrary",)),
        name="rank",
    )(mask, idx)


def _dispatch_kernel(dest_ref, fill_ref, h2_ref, xs_ref, zeros, sem, fill_sem, *, tb, n_blocks):
    base = pl.program_id(0) * tb

    @pl.when(pl.program_id(0) == 0)
    def _():
        zeros[...] = jnp.zeros_like(zeros)

        def zero_row(r):
            return pltpu.make_async_copy(zeros.at[pl.ds(0, 1)], xs_ref.at[pl.ds(r, 1)], fill_sem)

        def zero_block(bk):
            rows = pl.ds(pl.multiple_of(bk * MOE_BLOCK, MOE_BLOCK), MOE_BLOCK)
            return pltpu.make_async_copy(zeros, xs_ref.at[rows], fill_sem)

        def for_all_fills(act):
            for e in range(N_EXPERTS):
                lax.fori_loop(fill_ref[e], fill_ref[N_EXPERTS + e], lambda r, c: (act(zero_row(r)), c)[1], 0)
            lax.fori_loop(fill_ref[2 * N_EXPERTS], n_blocks, lambda bk, c: (act(zero_block(bk)), c)[1], 0)

        for_all_fills(lambda cp: cp.start())
        for_all_fills(lambda cp: cp.wait())

    def row_copy(i, k):
        return pltpu.make_async_copy(h2_ref.at[pl.ds(i, 1)],
                                     xs_ref.at[pl.ds(dest_ref[(base + i) * TOP_K + k], 1)], sem)

    def body(i, _):
        for k in range(TOP_K):
            row_copy(i, k).start()
        return 0

    lax.fori_loop(0, tb, body, 0, unroll=4)
    for k in range(TOP_K):
        pltpu.make_async_copy(h2_ref, xs_ref.at[pl.ds(0, tb)], sem).wait()


def _dispatch(dest_flat, fill, h2p, *, tb, n_blocks):
    t, wd = h2p.shape
    return pl.pallas_call(
        functools.partial(_dispatch_kernel, tb=tb, n_blocks=n_blocks),
        grid_spec=pltpu.PrefetchScalarGridSpec(
            num_scalar_prefetch=2, grid=(t // tb,),
            in_specs=[pl.BlockSpec((tb, wd), lambda i, dr, fl: (i, 0))],
            out_specs=pl.BlockSpec(memory_space=pl.ANY),
            scratch_shapes=[pltpu.VMEM((MOE_BLOCK, wd), I32), pltpu.SemaphoreType.DMA(()),
                            pltpu.SemaphoreType.DMA(())]),
        out_shape=jax.ShapeDtypeStruct((n_blocks * MOE_BLOCK, wd), I32),
        compiler_params=pltpu.CompilerParams(dimension_semantics=("arbitrary",), has_side_effects=True),
        name="dispatch",
    )(dest_flat, fill, h2p)


PAIR_CHUNK = 512


def _last_real_block(s, nb):
    return jnp.maximum(jnp.minimum(s, nb[0] - 1), 0)


def _expert_changed(be_ref, nb_ref):
    s = pl.program_id(1)
    return (s == 0) | ((s < nb_ref[0]) & (be_ref[s] != be_ref[jnp.maximum(s - 1, 0)]))


def _gu_kernel(be_ref, nb_ref, xs_ref, w_ref, b_ref, act_ref, wbf):
    @pl.when(_expert_changed(be_ref, nb_ref))
    def _():
        wbf[...] = w_ref[0].astype(BF16)

    @pl.when(pl.program_id(1) < nb_ref[0])
    def _():
        lo, hi = _unpack_bf16_pair(xs_ref[...])
        xb = jnp.concatenate([lo, hi], axis=1)
        gu = jnp.dot(xb, wbf[...], preferred_element_type=F32) + b_ref[0]
        tn = gu.shape[1]
        g = jnp.minimum(gu, SWIGLU_LIMIT)
        up1 = jnp.clip(gu, -SWIGLU_LIMIT, SWIGLU_LIMIT) + 1.0
        paired = (pltpu.roll(up1, tn - 1, 1) * (g * jax.nn.sigmoid(SWIGLU_ALPHA * g))).astype(BF16)
        r = lax.broadcasted_iota(I32, (PAIR_CHUNK, PAIR_CHUNK // 2), 0)
        c = lax.broadcasted_iota(I32, (PAIR_CHUNK, PAIR_CHUNK // 2), 1)
        sel = jnp.where(r == 2 * c, 1.0, 0.0).astype(BF16)
        for ch in range(tn // PAIR_CHUNK):
            act_ref[:, ch * (PAIR_CHUNK // 2):(ch + 1) * (PAIR_CHUNK // 2)] = jnp.dot(
                paired[:, ch * PAIR_CHUNK:(ch + 1) * PAIR_CHUNK], sel, preferred_element_type=F32).astype(BF16)

    @pl.when(pl.program_id(1) >= nb_ref[0])
    def _():
        act_ref[...] = jnp.zeros_like(act_ref)


def _moe_gu(block_e, n_valid, xs, w_gu, b_gu, *, tn):
    p, half = xs.shape
    d = 2 * half
    f2 = w_gu.shape[2]
    nj = f2 // tn
    nblk = p // MOE_BLOCK
    blk = _last_real_block
    vmem = 2 * d * tn * 4 + d * tn * 2 + 2 * MOE_BLOCK * half * 4 + 2 * MOE_BLOCK * tn + 8 * MOE_BLOCK * tn * 4
    return pl.pallas_call(
        _gu_kernel,
        grid_spec=pltpu.PrefetchScalarGridSpec(
            num_scalar_prefetch=2, grid=(nj, nblk),
            in_specs=[pl.BlockSpec((MOE_BLOCK, half), lambda j, s, be, nb: (blk(s, nb), 0)),
                      pl.BlockSpec((1, d, tn), lambda j, s, be, nb: (be[blk(s, nb)], 0, j)),
                      pl.BlockSpec((1, 1, tn), lambda j, s, be, nb: (be[blk(s, nb)], 0, j))],
            out_specs=pl.BlockSpec((MOE_BLOCK, tn // 2), lambda j, s, be, nb: (s, j)),
            scratch_shapes=[pltpu.VMEM((d, tn), BF16)]),
        out_shape=jax.ShapeDtypeStruct((p, f2 // 2), BF16),
        compiler_params=pltpu.CompilerParams(dimension_semantics=("arbitrary", "arbitrary"),
                                             vmem_limit_bytes=_vmem_limit(vmem)),
        name="moe_gu",
    )(block_e, n_valid, xs, w_gu, b_gu)


def _down_kernel(be_ref, nb_ref, act_ref, w_ref, b_ref, y_ref, wbf):
    @pl.when(_expert_changed(be_ref, nb_ref))
    def _():
        wbf[...] = w_ref[0].astype(BF16)

    @pl.when(pl.program_id(1) < nb_ref[0])
    def _():
        y = jnp.dot(act_ref[...], wbf[...], preferred_element_type=F32) + b_ref[0]
        half = y.shape[1] // 2
        y_ref[...] = _pack_bf16_pair(y[:, 0:half], y[:, half:2 * half])

    @pl.when(pl.program_id(1) >= nb_ref[0])
    def _():
        y_ref[...] = jnp.zeros_like(y_ref)


def _moe_down(block_e, n_valid, act, w_down, b_down):
    p, f = act.shape
    d = w_down.shape[2]
    nblk = p // MOE_BLOCK
    blk = _last_real_block
    vmem = 2 * f * d * 4 + f * d * 2 + 2 * MOE_BLOCK * f * 2 + 5 * MOE_BLOCK * d * 4
    return pl.pallas_call(
        _down_kernel,
        grid_spec=pltpu.PrefetchScalarGridSpec(
            num_scalar_prefetch=2, grid=(1, nblk),
            in_specs=[pl.BlockSpec((MOE_BLOCK, f), lambda j, s, be, nb: (blk(s, nb), 0)),
                      pl.BlockSpec((1, f, d), lambda j, s, be, nb: (be[blk(s, nb)], 0, 0)),
                      pl.BlockSpec((1, 1, d), lambda j, s, be, nb: (be[blk(s, nb)], 0, 0))],
            out_specs=pl.BlockSpec((MOE_BLOCK, d // 2), lambda j, s, be, nb: (s, 0)),
            scratch_shapes=[pltpu.VMEM((f, d), BF16)]),
        out_shape=jax.ShapeDtypeStruct((p, d // 2), I32),
        compiler_params=pltpu.CompilerParams(dimension_semantics=("arbitrary", "arbitrary"),
                                             vmem_limit_bytes=_vmem_limit(vmem)),
        name="moe_down",
    )(block_e, n_valid, act, w_down, b_down)


def _combine_kernel(dest_ref, ys_ref, x1_ref, gate_ref, mods_ref, o_ref, buf, sem, *, tb, d, nt, n_tiles):
    i = pl.program_id(0)
    slot = lax.rem(i, 2)

    def row_copy(tile, sl, r, k):
        return pltpu.make_async_copy(ys_ref.at[pl.ds(dest_ref[(tile * tb + r) * TOP_K + k], 1)],
                                     buf.at[sl, k, pl.ds(r, 1)], sem.at[sl])

    def gather(tile, sl):
        def body(r, _):
            for k in range(TOP_K):
                row_copy(tile, sl, r, k).start()
            return 0
        lax.fori_loop(0, tb, body, 0, unroll=2)

    @pl.when(i == 0)
    def _():
        gather(0, 0)

    @pl.when(i + 1 < n_tiles)
    def _():
        gather(i + 1, 1 - slot)

    for k in range(TOP_K):
        pltpu.make_async_copy(ys_ref.at[pl.ds(0, tb)], buf.at[slot, k], sem.at[slot]).wait()
    half = d // 2
    gate = gate_ref[...]
    y_lo = jnp.zeros((tb, half), F32)
    y_hi = jnp.zeros((tb, half), F32)
    for k in range(TOP_K):
        w = buf[slot, k]
        gk = gate[:, k:k + 1]
        y_lo = y_lo + lax.bitcast_convert_type(lax.shift_left(w, 16), F32) * gk
        y_hi = y_hi + lax.bitcast_convert_type(w & jnp.int32(-65536), F32) * gk
    gt2 = mods_ref[pl.ds(i // nt, 1), 5 * d:6 * d]
    o_ref[:, 0:half] = x1_ref[:, 0:half] + gt2[:, 0:half] * y_lo
    o_ref[:, half:d] = x1_ref[:, half:d] + gt2[:, half:d] * y_hi


def _combine(dest_flat, ys, x1, gate, mods, *, tb, seq):
    t, d = x1.shape
    n_tiles = t // tb
    nt = seq // tb
    vmem = 2 * TOP_K * tb * d * 2 + 4 * tb * d * 4 + 6 * tb * d * 4
    return pl.pallas_call(
        functools.partial(_combine_kernel, tb=tb, d=d, nt=nt, n_tiles=n_tiles),
        grid_spec=pltpu.PrefetchScalarGridSpec(
            num_scalar_prefetch=1, grid=(n_tiles,),
            in_specs=[pl.BlockSpec(memory_space=pl.ANY),
                      pl.BlockSpec((tb, d), lambda i, dr: (i, 0)),
                      pl.BlockSpec((tb, TOP_K), lambda i, dr: (i, 0)),
                      pl.BlockSpec(mods.shape, lambda i, dr: (0, 0))],
            out_specs=pl.BlockSpec((tb, d), lambda i, dr: (i, 0)),
            scratch_shapes=[pltpu.VMEM((2, TOP_K, tb, d // 2), I32), pltpu.SemaphoreType.DMA((2,))]),
        out_shape=jax.ShapeDtypeStruct((t, d), F32),
        compiler_params=pltpu.CompilerParams(dimension_semantics=("arbitrary",),
                                             vmem_limit_bytes=_vmem_limit(vmem)),
        name="combine",
    )(dest_flat, ys, x1, gate, mods)


def _rope_table(s):
    rows = s // GRID_W
    row = jnp.repeat(jnp.arange(rows), GRID_W).astype(F32)
    col = jnp.tile(jnp.arange(GRID_W), rows).astype(F32)
    inv = ROPE_THETA ** (-jnp.arange(ROPE_FREQS, dtype=F32) / ROPE_FREQS)
    ang_r = row[:, None] * inv
    ang_c = col[:, None] * inv
    z = jnp.zeros_like(ang_r)
    cosf = jnp.concatenate([jnp.cos(ang_r)] * 2 + [jnp.cos(ang_c)] * 2, axis=1)
    sneg = jnp.concatenate([-jnp.sin(ang_r), z, -jnp.sin(ang_c), z], axis=1)
    spos = jnp.concatenate([z, jnp.sin(ang_r), z, jnp.sin(ang_c)], axis=1)
    return jnp.concatenate([cosf, sneg, spos], axis=1)


def kernel(x, c, ctx, c_ctx, w_mod, b_mod, g_norm1, w_in, g_q, g_k, conv_w, conv_b, w_gate_a, b_gate_a,
           w_gate_x, b_gate_x, lru_lambda, g_att_out, g_rec_out, w_out, g_norm2, w_router, b_router,
           w_gate_up, b_gate_up, w_down, b_down):
    b, s, d = x.shape
    cl = ctx.shape[1]
    t = b * s
    assert w_mod.shape[0] == 1, "single-layer kernel"
    assert b + 1 <= SUBLANES and d - ATT_WIDTH == REC_BLOCKS * LANES
    assert s % (SCAN_SEGMENTS * SUBLANES) == 0 and cl % (SCAN_SEGMENTS * SUBLANES) == 0

    ctx_row = b
    c8 = jnp.zeros((SUBLANES, d), F32).at[:b].set(c).at[ctx_row].set(c_ctx)
    mods = _mod(c8, w_mod[0], b_mod[0])

    w_in_bf = w_in[0].astype(BF16)
    tm = min(512, s)
    q, k, v, xr, yr = _inproj(x, mods, g_norm1[0], w_in_bf, _rope_table(s), g_q[0], g_k[0],
                              latent=True, ctx_row=ctx_row, tm=tm)
    kc, vc, xrc = _inproj(ctx, mods, g_norm1[0], w_in_bf, None, g_q[0], g_k[0],
                          latent=False, ctx_row=ctx_row, tm=cl)

    att = _attention(q, jnp.concatenate([kc, k], axis=1), jnp.concatenate([vc, v], axis=1), tq=min(256, s))

    w_gates = jnp.concatenate([w_gate_a[0, 0], w_gate_x[0, 0], w_gate_a[0, 1], w_gate_x[0, 1]], axis=-1).astype(BF16)
    rw = d - ATT_WIDTH
    bias = lambda bb: bb.reshape(REC_BLOCKS, LANES)
    b_gates = jnp.concatenate([bias(b_gate_a[0, 0]), bias(b_gate_x[0, 0]), bias(b_gate_a[0, 1]), bias(b_gate_x[0, 1])],
                              axis=-1).reshape(1, 4 * rw)
    rec = _rglru(xr, xrc, yr, conv_w[0], conv_b[0].reshape(1, rw), w_gates, b_gates, lru_lambda[0])

    x1, h2p, top_idx, gate, mask = _merge(att, rec, x, mods, g_att_out[0], g_rec_out[0], w_out[0].astype(BF16),
                                          g_norm2[0], w_router[0], b_router[0], tm=tm)

    rank, counts = _rank(mask, top_idx, tb=MOE_BLOCK)
    counts = counts[0]
    padded = (counts + MOE_BLOCK - 1) // MOE_BLOCK * MOE_BLOCK
    pad_ends = jnp.cumsum(padded)
    pad_starts = pad_ends - padded
    dest = (pad_starts[top_idx] + rank).reshape(t * TOP_K)
    n_blocks = (t * TOP_K + N_EXPERTS * (MOE_BLOCK - 1) + MOE_BLOCK - 1) // MOE_BLOCK
    block_start = jnp.arange(n_blocks, dtype=I32) * MOE_BLOCK
    block_e = jnp.minimum(jnp.sum((pad_ends[None, :] <= block_start[:, None]).astype(I32), axis=1), N_EXPERTS - 1)
    n_valid = (pad_ends[N_EXPERTS - 1] // MOE_BLOCK).reshape(1).astype(I32)
    fill = jnp.concatenate([pad_starts + counts, pad_ends, n_valid]).astype(I32)

    xs = _dispatch(dest, fill, h2p, tb=MOE_BLOCK, n_blocks=n_blocks)

    f2 = w_gate_up.shape[3]
    act = _moe_gu(block_e, n_valid, xs, w_gate_up[0], b_gate_up[0].reshape(N_EXPERTS, 1, f2), tn=2048)
    ys = _moe_down(block_e, n_valid, act, w_down[0], b_down[0].reshape(N_EXPERTS, 1, d))

    return _combine(dest, ys, x1.reshape(t, d), gate, mods, tb=128, seq=s).reshape(b, s, d)
```

```python
import functools
import math

import jax
import jax.numpy as jnp
from jax import lax
from jax.experimental import pallas as pl
from jax.experimental.pallas import tpu as pltpu

F32 = jnp.float32
BF16 = jnp.bfloat16
I32 = jnp.int32

EPS = 1e-6
GRID_W = 64
HEAD_DIM = 128
N_Q_HEADS = 8
N_KV_HEADS = 2
GROUP = N_Q_HEADS // N_KV_HEADS
ATT_WIDTH = N_Q_HEADS * HEAD_DIM
KV_WIDTH = N_KV_HEADS * HEAD_DIM
ROPE_THETA = 10000.0
ROPE_FREQS = HEAD_DIM // 4
REC_BLOCKS = 8
CONV_W = 4
CONV_LEFT = 2
LRU_C = 8.0
N_EXPERTS = 32
TOP_K = 4
SWIGLU_LIMIT = 7.0
SWIGLU_ALPHA = 1.702
MOE_BLOCK = 256

V7X_VMEM_BYTES = 64 * 1024 * 1024
SUBLANES = 8
LANES = 128
SCAN_SEGMENTS = SUBLANES
CONV_PAD = SUBLANES

HIGHEST = lax.Precision.HIGHEST


def _vmem_limit(nbytes):
    return int(min(V7X_VMEM_BYTES - 4 * 1024 * 1024, max(nbytes, 16 * 1024 * 1024)))


def _rms(x, g):
    return x * lax.rsqrt(jnp.mean(x * x, axis=-1, keepdims=True) + EPS) * g


def _mod_kernel(c_ref, w_ref, b_ref, o_ref):
    c = c_ref[...]
    a = c * jax.nn.sigmoid(c)
    o_ref[...] = jnp.dot(a, w_ref[...], preferred_element_type=F32, precision=HIGHEST) + b_ref[...]


def _mod(c8, w_mod, b_mod):
    d, n = w_mod.shape
    tn = 1024
    return pl.pallas_call(
        _mod_kernel,
        grid=(n // tn,),
        in_specs=[pl.BlockSpec((SUBLANES, d), lambda j: (0, 0)),
                  pl.BlockSpec((d, tn), lambda j: (0, j)),
                  pl.BlockSpec((1, tn), lambda j: (0, j))],
        out_specs=pl.BlockSpec((SUBLANES, tn), lambda j: (0, j)),
        out_shape=jax.ShapeDtypeStruct((SUBLANES, n), F32),
        compiler_params=pltpu.CompilerParams(dimension_semantics=("arbitrary",),
                                             vmem_limit_bytes=_vmem_limit(3 * d * tn * 4)),
        name="mod",
    )(c8, w_mod, b_mod.reshape(1, n))


def _qk_norm_rope(y, g, rope):
    yn = _rms(y, g)
    if rope is None:
        return yn
    cosf, sneg, spos = rope
    return yn * cosf + pltpu.roll(yn, HEAD_DIM - ROPE_FREQS, 1) * sneg + pltpu.roll(yn, ROPE_FREQS, 1) * spos


def _inproj_kernel(*refs, d, latent, ctx_row):
    if latent:
        (x_ref, mods_ref, g1_ref, w_ref, rope_ref, gq_ref, gk_ref,
         q_ref, k_ref, v_ref, xr_ref, yr_ref) = refs
        row = pl.program_id(0)
    else:
        x_ref, mods_ref, g1_ref, w_ref, gk_ref, k_ref, v_ref, xr_ref = refs
        row = ctx_row
    sh = mods_ref[pl.ds(row, 1), 0:d]
    sc = mods_ref[pl.ds(row, 1), d:2 * d]
    h = _rms(x_ref[0], g1_ref[...]) * (1.0 + sc) + sh
    hb = h.astype(BF16)

    def proj(lo, hi):
        return jnp.dot(hb, w_ref[:, lo:hi], preferred_element_type=F32)

    o_k = ATT_WIDTH
    o_v = o_k + KV_WIDTH
    o_xr = o_v + KV_WIDTH
    rec_w = d - ATT_WIDTH
    o_yr = o_xr + rec_w
    rope = None
    if latent:
        rp = rope_ref[...]
        rope = (rp[:, 0:HEAD_DIM], rp[:, HEAD_DIM:2 * HEAD_DIM], rp[:, 2 * HEAD_DIM:3 * HEAD_DIM])
        q = proj(0, ATT_WIDTH)
        for hd in range(N_Q_HEADS):
            sl = slice(hd * HEAD_DIM, (hd + 1) * HEAD_DIM)
            q_ref[0, :, sl] = (_qk_norm_rope(q[:, sl], gq_ref[...], rope) * (HEAD_DIM ** -0.5)).astype(BF16)
    k = proj(o_k, o_v)
    for hd in range(N_KV_HEADS):
        sl = slice(hd * HEAD_DIM, (hd + 1) * HEAD_DIM)
        k_ref[0, :, sl] = _qk_norm_rope(k[:, sl], gk_ref[...], rope).astype(BF16)
    v_ref[0] = proj(o_v, o_xr).astype(BF16)
    xr_ref[0] = proj(o_xr, o_yr)
    if latent:
        yr_ref[0] = proj(o_yr, o_yr + rec_w)


def _inproj(x, mods, g1, w_in_bf, rope_tab, g_q, g_k, *, latent, ctx_row, tm):
    b, s, d = x.shape
    n = w_in_bf.shape[1]
    rec_w = d - ATT_WIDTH
    grid = (b, s // tm)
    row_spec = lambda w: pl.BlockSpec((1, tm, w), lambda bi, i: (bi, i, 0))
    full2 = lambda a: pl.BlockSpec(a.shape, lambda bi, i: (0, 0))
    in_specs = [row_spec(d), full2(mods), pl.BlockSpec((1, d), lambda bi, i: (0, 0)),
                pl.BlockSpec((d, n), lambda bi, i: (0, 0), pipeline_mode=pl.Buffered(1))]
    args = [x, mods, g1.reshape(1, d), w_in_bf]
    out_specs, out_shape = [], []
    if latent:
        in_specs += [pl.BlockSpec((tm, 3 * HEAD_DIM), lambda bi, i: (i, 0)),
                     pl.BlockSpec((1, HEAD_DIM), lambda bi, i: (0, 0))]
        args += [rope_tab, g_q.reshape(1, HEAD_DIM)]
        out_specs.append(row_spec(ATT_WIDTH))
        out_shape.append(jax.ShapeDtypeStruct((b, s, ATT_WIDTH), BF16))
    in_specs.append(pl.BlockSpec((1, HEAD_DIM), lambda bi, i: (0, 0)))
    args.append(g_k.reshape(1, HEAD_DIM))
    out_specs += [row_spec(KV_WIDTH), row_spec(KV_WIDTH), row_spec(rec_w)]
    out_shape += [jax.ShapeDtypeStruct((b, s, KV_WIDTH), BF16), jax.ShapeDtypeStruct((b, s, KV_WIDTH), BF16),
                  jax.ShapeDtypeStruct((b, s, rec_w), F32)]
    if latent:
        out_specs.append(row_spec(rec_w))
        out_shape.append(jax.ShapeDtypeStruct((b, s, rec_w), F32))
    vmem = d * n * 2 + 2 * tm * d * 4 + 2 * tm * n * 4 + 3 * tm * d * 4 + 2 * tm * n * 4
    return pl.pallas_call(
        functools.partial(_inproj_kernel, d=d, latent=latent, ctx_row=ctx_row),
        grid=grid, in_specs=in_specs, out_specs=out_specs, out_shape=out_shape,
        compiler_params=pltpu.CompilerParams(dimension_semantics=("arbitrary", "arbitrary"),
                                             vmem_limit_bytes=_vmem_limit(vmem)),
        name="inproj_latent" if latent else "inproj_ctx",
    )(*args)


def _attn_kernel(q_ref, k_ref, v_ref, o_ref):
    k = k_ref[0]
    v = v_ref[0]
    for g in range(GROUP):
        sl = slice(g * HEAD_DIM, (g + 1) * HEAD_DIM)
        s = lax.dot_general(q_ref[0, :, sl], k, (((1,), (1,)), ((), ())), preferred_element_type=F32)
        m = jnp.max(s, axis=-1, keepdims=True)
        p = jnp.exp(s - m).astype(BF16)
        o = jnp.dot(p, v, preferred_element_type=F32)
        o_ref[0, :, sl] = (o[:, 0:HEAD_DIM] / o[:, HEAD_DIM:2 * HEAD_DIM]).astype(BF16)


def _attention(q, k_all, v_all, *, tq):
    b, s, _ = q.shape
    lk = k_all.shape[1]
    gw = GROUP * HEAD_DIM
    vmem = 4 * lk * HEAD_DIM * 2 * 2 + 4 * tq * gw * 2 + 4 * tq * lk * 4
    return pl.pallas_call(
        _attn_kernel,
        grid=(b, N_KV_HEADS, s // tq),
        in_specs=[pl.BlockSpec((1, tq, gw), lambda bi, h, i: (bi, i, h)),
                  pl.BlockSpec((1, lk, HEAD_DIM), lambda bi, h, i: (bi, 0, h)),
                  pl.BlockSpec((1, lk, 2 * HEAD_DIM), lambda bi, h, i: (bi, 0, h))],
        out_specs=pl.BlockSpec((1, tq, gw), lambda bi, h, i: (bi, i, h)),
        out_shape=jax.ShapeDtypeStruct((b, s, ATT_WIDTH), BF16),
        compiler_params=pltpu.CompilerParams(dimension_semantics=("arbitrary",) * 3,
                                             vmem_limit_bytes=_vmem_limit(vmem)),
        name="attention",
    )(q, k_all, v_all)


def _gelu_tanh(x):
    return 0.5 * x * (1.0 + jnp.tanh(math.sqrt(2.0 / math.pi) * (x + 0.044715 * x * x * x)))


def _rglru_kernel(xr_ref, xc_ref, yr_ref, cw_ref, cb_ref, wg_ref, bg_ref, lam_ref, o_ref,
                  xp, xpc, af, bf, ab, bb, caf, cbf, cab, cbb, *, s, c):
    nseg = SCAN_SEGMENTS
    seg = s // nseg
    cseg = c // nseg
    zeros_pad = jnp.zeros((CONV_PAD, LANES), F32)
    xp[0:CONV_PAD, :] = zeros_pad
    xp[CONV_PAD + s:2 * CONV_PAD + s, :] = zeros_pad
    xp[CONV_PAD:CONV_PAD + s, :] = xr_ref[0]
    xpc[0:CONV_PAD, :] = zeros_pad
    xpc[CONV_PAD + c:2 * CONV_PAD + c, :] = zeros_pad
    xpc[CONV_PAD:CONV_PAD + c, :] = xc_ref[0]

    cw = cw_ref[...]
    cb = cb_ref[...]
    wg = wg_ref[0]
    bg = bg_ref[...]
    sp = jax.nn.softplus(-lam_ref[...])

    def coeffs(src, lo, n):
        u = cb
        for j in range(CONV_W):
            u = u + src[CONV_PAD + lo + j - CONV_LEFT:CONV_PAD + lo + j - CONV_LEFT + n, :] * cw[j:j + 1, :]
        g = jnp.dot(u.astype(BF16), wg, preferred_element_type=F32) + bg
        out = []
        for r in range(2):
            ga = g[:, (2 * r) * LANES:(2 * r + 1) * LANES]
            gx = g[:, (2 * r + 1) * LANES:(2 * r + 2) * LANES]
            log_a = (-LRU_C) * jax.nn.sigmoid(ga) * sp[r:r + 1, :]
            a = jnp.exp(log_a)
            mult = jnp.sqrt(-jnp.tanh(log_a) * (1.0 + a * a))
            out.append((a, mult * jax.nn.sigmoid(gx) * u))
        return out

    (a0, b0), (a1, b1) = coeffs(xpc, 0, c)
    for q in range(nseg):
        rows = slice(q * cseg, (q + 1) * cseg)
        dst = pl.ds(q, cseg, stride=nseg)
        caf[dst, :] = a0[rows]
        cbf[dst, :] = b0[rows]
        cab[dst, :] = a1[rows]
        cbb[dst, :] = b1[rows]
    for q in range(nseg):
        (a0, b0), (a1, b1) = coeffs(xp, q * seg, seg)
        dst = pl.ds(q, seg, stride=nseg)
        af[dst, :] = a0
        bf[dst, :] = b0
        ab[dst, :] = a1
        bb[dst, :] = b1

    def scan(a_f, b_f, a_b, b_b, n, store):
        def body(j, carry):
            hf, pf, hb, pb = carry
            rf = pl.multiple_of(j * nseg, nseg)
            rb = pl.multiple_of((n - 1 - j) * nseg, nseg)
            av = a_f[pl.ds(rf, nseg), :]
            hf = av * hf + b_f[pl.ds(rf, nseg), :]
            pf = av * pf
            aw = a_b[pl.ds(rb, nseg), :]
            hb = aw * hb + b_b[pl.ds(rb, nseg), :]
            pb = aw * pb
            if store:
                a_f[pl.ds(rf, nseg), :] = pf
                b_f[pl.ds(rf, nseg), :] = hf
                a_b[pl.ds(rb, nseg), :] = pb
                b_b[pl.ds(rb, nseg), :] = hb
            return hf, pf, hb, pb
        z = jnp.zeros((nseg, LANES), F32)
        o = jnp.ones((nseg, LANES), F32)
        return lax.fori_loop(0, n, body, (z, o, z, o), unroll=8)

    def chain(h_end, p_end, h0, reverse):
        order = range(nseg - 1, -1, -1) if reverse else range(nseg)
        enter = [None] * nseg
        cur = h0
        for q in order:
            enter[q] = cur
            cur = h_end[q:q + 1, :] + p_end[q:q + 1, :] * cur
        return enter, cur

    zero_row = jnp.zeros((1, LANES), F32)
    hf, pf, hb, pb = scan(caf, cbf, cab, cbb, cseg, False)
    _, h0f = chain(hf, pf, zero_row, False)
    _, h0b = chain(hb, pb, zero_row, True)
    hf, pf, hb, pb = scan(af, bf, ab, bb, seg, True)
    enter_f, _ = chain(hf, pf, h0f, False)
    enter_b, _ = chain(hb, pb, h0b, True)
    for q in range(nseg):
        src = pl.ds(q, seg, stride=nseg)
        h = bf[src, :] + af[src, :] * enter_f[q] + bb[src, :] + ab[src, :] * enter_b[q]
        rows = slice(q * seg, (q + 1) * seg)
        o_ref[0, rows, :] = (h * _gelu_tanh(yr_ref[0, rows, :])).astype(BF16)


def _rglru(xr, xrc, yr, conv_w, conv_b, w_gates, b_gates, lam):
    b, s, w = xr.shape
    c = xrc.shape[1]
    nb = w // LANES
    slab = lambda n: pl.BlockSpec((1, n, LANES), lambda bi, j: (bi, 0, j))
    scr = lambda n: pltpu.VMEM((n, LANES), F32)
    vmem = (3 * 2 + 5) * s * LANES * 4 + 8 * s * LANES * 4
    return pl.pallas_call(
        functools.partial(_rglru_kernel, s=s, c=c),
        grid=(b, nb),
        in_specs=[slab(s), slab(c), slab(s),
                  pl.BlockSpec((CONV_W, LANES), lambda bi, j: (0, j)),
                  pl.BlockSpec((1, LANES), lambda bi, j: (0, j)),
                  pl.BlockSpec((1, LANES, 4 * LANES), lambda bi, j: (j, 0, 0)),
                  pl.BlockSpec((1, 4 * LANES), lambda bi, j: (0, j)),
                  pl.BlockSpec((2, LANES), lambda bi, j: (0, j))],
        out_specs=slab(s),
        out_shape=jax.ShapeDtypeStruct((b, s, w), BF16),
        scratch_shapes=[scr(s + 2 * CONV_PAD), scr(c + 2 * CONV_PAD),
                        scr(s), scr(s), scr(s), scr(s), scr(c), scr(c), scr(c), scr(c)],
        compiler_params=pltpu.CompilerParams(dimension_semantics=("arbitrary", "arbitrary"),
                                             vmem_limit_bytes=_vmem_limit(vmem)),
        name="rglru",
    )(xr, xrc, yr, conv_w, conv_b, w_gates, b_gates, lam)


def _pack_bf16_pair(lo, hi):
    lo_bits = lax.bitcast_convert_type(lo.astype(BF16).astype(F32), I32)
    hi_bits = lax.bitcast_convert_type(hi.astype(BF16).astype(F32), I32)
    return lax.shift_right_logical(lo_bits, 16) | (hi_bits & jnp.int32(-65536))


def _unpack_bf16_pair(w):
    lo = lax.bitcast_convert_type(lax.shift_left(w, 16), F32).astype(BF16)
    hi = lax.bitcast_convert_type(w & jnp.int32(-65536), F32).astype(BF16)
    return lo, hi


def _merge_kernel(att_ref, rec_ref, x_ref, mods_ref, ga_ref, gr_ref, wo_ref, g2_ref, wr_ref, br_ref,
                  x1_ref, h2_ref, idx_ref, gate_ref, mask_ref, *, d):
    row = pl.program_id(0)
    gt1 = mods_ref[pl.ds(row, 1), 2 * d:3 * d]
    sh2 = mods_ref[pl.ds(row, 1), 3 * d:4 * d]
    sc2 = mods_ref[pl.ds(row, 1), 4 * d:5 * d]
    an = _rms(att_ref[0].astype(F32), ga_ref[...]).astype(BF16)
    rn = _rms(rec_ref[0].astype(F32), gr_ref[...]).astype(BF16)
    mix = (jnp.dot(an, wo_ref[0:ATT_WIDTH, :], preferred_element_type=F32)
           + jnp.dot(rn, wo_ref[ATT_WIDTH:d, :], preferred_element_type=F32))
    x1 = x_ref[0] + gt1 * mix
    x1_ref[0] = x1
    h2 = _rms(x1, g2_ref[...]) * (1.0 + sc2) + sh2
    half = d // 2
    h2_ref[...] = _pack_bf16_pair(h2[:, 0:half], h2[:, half:d])
    h_hi = h2.astype(BF16)
    h_lo = (h2 - h_hi.astype(F32)).astype(BF16)
    w_split = wr_ref[...]
    part = (jnp.dot(h_hi, w_split, preferred_element_type=F32)
            + jnp.dot(h_lo, w_split, preferred_element_type=F32))
    logits = part[:, 0:N_EXPERTS] + part[:, N_EXPERTS:2 * N_EXPERTS] + br_ref[...]
    tm = logits.shape[0]
    lane = lax.broadcasted_iota(I32, (tm, N_EXPERTS), 1).astype(F32)
    col = lax.broadcasted_iota(I32, (tm, TOP_K), 1)
    idx = jnp.zeros((tm, TOP_K), F32)
    ex = jnp.zeros((tm, TOP_K), F32)
    mask = jnp.zeros((tm, N_EXPERTS), F32)
    rest = logits
    top = None
    for k in range(TOP_K):
        m = jnp.max(rest, axis=-1, keepdims=True)
        first = jnp.min(jnp.where(rest == m, lane, float(N_EXPERTS)), axis=-1, keepdims=True)
        sel = lane == first
        if k == 0:
            top = m
        idx = jnp.where(col == k, first, idx)
        ex = jnp.where(col == k, jnp.exp(m - top), ex)
        mask = jnp.where(sel, 1.0, mask)
        rest = jnp.where(sel, -jnp.inf, rest)
    idx_ref[...] = idx.astype(I32)
    gate_ref[...] = ex / jnp.sum(ex, axis=-1, keepdims=True)
    mask_ref[...] = mask


def _merge(att, rec, x, mods, g_att, g_rec, w_out_bf, g2, w_router, b_router, *, tm):
    b, s, d = x.shape
    t = b * s
    nt = s // tm
    rec_w = d - ATT_WIDTH
    row3 = lambda w: pl.BlockSpec((1, tm, w), lambda bi, i: (bi, i, 0))
    tok2 = lambda w: pl.BlockSpec((tm, w), lambda bi, i: (bi * nt + i, 0))
    const = lambda shape, **kw: pl.BlockSpec(shape, lambda bi, i: (0,) * len(shape), **kw)
    vmem = d * d * 2 + 2 * tm * (ATT_WIDTH + rec_w) * 2 + 4 * tm * d * 4 + tm * d * 4 + 8 * tm * d * 4
    return pl.pallas_call(
        functools.partial(_merge_kernel, d=d),
        grid=(b, nt),
        in_specs=[row3(ATT_WIDTH), row3(rec_w), row3(d), const(mods.shape),
                  const((1, ATT_WIDTH)), const((1, rec_w)),
                  const((d, d), pipeline_mode=pl.Buffered(1)), const((1, d)),
                  const((d, 2 * N_EXPERTS)), const((1, N_EXPERTS))],
        out_specs=[row3(d), tok2(d // 2), tok2(TOP_K), tok2(TOP_K), tok2(N_EXPERTS)],
        out_shape=[jax.ShapeDtypeStruct((b, s, d), F32), jax.ShapeDtypeStruct((t, d // 2), I32),
                   jax.ShapeDtypeStruct((t, TOP_K), I32), jax.ShapeDtypeStruct((t, TOP_K), F32),
                   jax.ShapeDtypeStruct((t, N_EXPERTS), F32)],
        compiler_params=pltpu.CompilerParams(dimension_semantics=("arbitrary", "arbitrary"),
                                             vmem_limit_bytes=_vmem_limit(vmem)),
        name="merge",
    )(att, rec, x, mods, g_att.reshape(1, -1), g_rec.reshape(1, -1), w_out_bf, g2.reshape(1, d),
      _split_bf16(w_router), b_router.reshape(1, N_EXPERTS))


def _split_bf16(w):
    hi = w.astype(BF16)
    lo = (w - hi.astype(F32)).astype(BF16)
    return jnp.concatenate([hi, lo], axis=1)


def _rank_kernel(mask_ref, idx_ref, rank_ref, cnt_ref, carry):
    @pl.when(pl.program_id(0) == 0)
    def _():
        carry[...] = jnp.zeros_like(carry)

    m = mask_ref[...]
    tb = m.shape[0]
    r = lax.broadcasted_iota(I32, (tb, tb), 0)
    cidx = lax.broadcasted_iota(I32, (tb, tb), 1)
    tri = jnp.where(cidx < r, 1.0, 0.0).astype(BF16)
    before = jnp.dot(tri, m.astype(BF16), preferred_element_type=F32) + carry[...]
    lane = lax.broadcasted_iota(I32, (tb, N_EXPERTS), 1)
    col = lax.broadcasted_iota(I32, (tb, TOP_K), 1)
    idx = idx_ref[...]
    rank = jnp.zeros((tb, TOP_K), F32)
    for k in range(TOP_K):
        pick = jnp.sum(jnp.where(lane == idx[:, k:k + 1], before, 0.0), axis=-1, keepdims=True)
        rank = jnp.where(col == k, pick, rank)
    rank_ref[...] = rank.astype(I32)
    carry[...] = carry[...] + jnp.sum(m, axis=0, keepdims=True)
    cnt_ref[...] = carry[...].astype(I32)


def _rank(mask, idx, *, tb):
    t = mask.shape[0]
    return pl.pallas_call(
        _rank_kernel,
        grid=(t // tb,),
        in_specs=[pl.BlockSpec((tb, N_EXPERTS), lambda i: (i, 0)), pl.BlockSpec((tb, TOP_K), lambda i: (i, 0))],
        out_specs=[pl.BlockSpec((tb, TOP_K), lambda i: (i, 0)), pl.BlockSpec((1, N_EXPERTS), lambda i: (0, 0))],
        out_shape=[jax.ShapeDtypeStruct((t, TOP_K), I32), jax.ShapeDtypeStruct((1, N_EXPERTS), I32)],
        scratch_shapes=[pltpu.VMEM((1, N_EXPERTS), F32)],
        compiler_params=pltpu.CompilerParams(dimension_semantics=("arbitrary",)),
        name="rank",
    )(mask, idx)


def _dispatch_kernel(dest_ref, fill_ref, h2_ref, xs_ref, zeros, sem, fill_sem, *, tb, n_blocks):
    base = pl.program_id(0) * tb

    @pl.when(pl.program_id(0) == 0)
    def _():
        zeros[...] = jnp.zeros_like(zeros)

        def zero_row(r):
            return pltpu.make_async_copy(zeros.at[pl.ds(0, 1)], xs_ref.at[pl.ds(r, 1)], fill_sem)

        def zero_block(bk):
            rows = pl.ds(pl.multiple_of(bk * MOE_BLOCK, MOE_BLOCK), MOE_BLOCK)
            return pltpu.make_async_copy(zeros, xs_ref.at[rows], fill_sem)

        def for_all_fills(act):
            for e in range(N_EXPERTS):
                lax.fori_loop(fill_ref[e], fill_ref[N_EXPERTS + e], lambda r, c: (act(zero_row(r)), c)[1], 0)
            lax.fori_loop(fill_ref[2 * N_EXPERTS], n_blocks, lambda bk, c: (act(zero_block(bk)), c)[1], 0)

        for_all_fills(lambda cp: cp.start())
        for_all_fills(lambda cp: cp.wait())

    def row_copy(i, k):
        return pltpu.make_async_copy(h2_ref.at[pl.ds(i, 1)],
                                     xs_ref.at[pl.ds(dest_ref[(base + i) * TOP_K + k], 1)], sem)

    def body(i, _):
        for k in range(TOP_K):
            row_copy(i, k).start()
        return 0

    lax.fori_loop(0, tb, body, 0, unroll=4)
    for k in range(TOP_K):
        pltpu.make_async_copy(h2_ref, xs_ref.at[pl.ds(0, tb)], sem).wait()


def _dispatch(dest_flat, fill, h2p, *, tb, n_blocks):
    t, wd = h2p.shape
    return pl.pallas_call(
        functools.partial(_dispatch_kernel, tb=tb, n_blocks=n_blocks),
        grid_spec=pltpu.PrefetchScalarGridSpec(
            num_scalar_prefetch=2, grid=(t // tb,),
            in_specs=[pl.BlockSpec((tb, wd), lambda i, dr, fl: (i, 0))],
            out_specs=pl.BlockSpec(memory_space=pl.ANY),
            scratch_shapes=[pltpu.VMEM((MOE_BLOCK, wd), I32), pltpu.SemaphoreType.DMA(()),
                            pltpu.SemaphoreType.DMA(())]),
        out_shape=jax.ShapeDtypeStruct((n_blocks * MOE_BLOCK, wd), I32),
        compiler_params=pltpu.CompilerParams(dimension_semantics=("arbitrary",), has_side_effects=True),
        name="dispatch",
    )(dest_flat, fill, h2p)


PAIR_CHUNK = 512


def _last_real_block(s, meta):
    return jnp.maximum(jnp.minimum(s, meta[0] - 1), 0)


def _stream_expert_weights(ridx_ref, run_e_ref, meta_ref, w_hbm, wf32, wbf, sem, *, tn):
    j = pl.program_id(0)
    s = pl.program_id(1)
    n_runs = meta_ref[1]
    r = ridx_ref[s]
    first = (s < meta_ref[0]) & ((s == 0) | (r != ridx_ref[jnp.maximum(s - 1, 0)]))

    def tile_copy(run, sweep, slot):
        cols = pl.ds(pl.multiple_of(sweep * tn, tn), tn)
        return pltpu.make_async_copy(w_hbm.at[run_e_ref[run], :, cols], wf32.at[slot], sem.at[slot])

    @pl.when(first)
    def _():
        g = j * n_runs + r
        slot = lax.rem(g, 2)

        @pl.when(g == 0)
        def _():
            tile_copy(0, 0, 0).start()

        tile_copy(r, j, slot).wait()
        for static_slot in range(2):
            @pl.when(slot == static_slot)
            def _():
                wbf[...] = wf32[static_slot].astype(BF16)
        more_runs = r + 1 < n_runs

        @pl.when(more_runs)
        def _():
            tile_copy(r + 1, j, 1 - slot).start()

        @pl.when(jnp.logical_not(more_runs) & (j + 1 < pl.num_programs(0)))
        def _():
            tile_copy(0, j + 1, 1 - slot).start()


def _gu_kernel(be_ref, ridx_ref, run_e_ref, meta_ref, xs_ref, w_hbm, b_ref, act_ref, wf32, wbf, sem):
    del be_ref
    _stream_expert_weights(ridx_ref, run_e_ref, meta_ref, w_hbm, wf32, wbf, sem, tn=wbf.shape[1])
    nb_ref = meta_ref

    @pl.when(pl.program_id(1) < nb_ref[0])
    def _():
        lo, hi = _unpack_bf16_pair(xs_ref[...])
        xb = jnp.concatenate([lo, hi], axis=1)
        gu = jnp.dot(xb, wbf[...], preferred_element_type=F32) + b_ref[0]
        tn = gu.shape[1]
        g = jnp.minimum(gu, SWIGLU_LIMIT)
        up1 = jnp.clip(gu, -SWIGLU_LIMIT, SWIGLU_LIMIT) + 1.0
        paired = (pltpu.roll(up1, tn - 1, 1) * (g * jax.nn.sigmoid(SWIGLU_ALPHA * g))).astype(BF16)
        r = lax.broadcasted_iota(I32, (PAIR_CHUNK, PAIR_CHUNK // 2), 0)
        c = lax.broadcasted_iota(I32, (PAIR_CHUNK, PAIR_CHUNK // 2), 1)
        sel = jnp.where(r == 2 * c, 1.0, 0.0).astype(BF16)
        for ch in range(tn // PAIR_CHUNK):
            act_ref[:, ch * (PAIR_CHUNK // 2):(ch + 1) * (PAIR_CHUNK // 2)] = jnp.dot(
                paired[:, ch * PAIR_CHUNK:(ch + 1) * PAIR_CHUNK], sel, preferred_element_type=F32).astype(BF16)

    @pl.when(pl.program_id(1) >= nb_ref[0])
    def _():
        act_ref[...] = jnp.zeros_like(act_ref)


def _expert_stream_scratch(k, tn):
    return [pltpu.VMEM((2, k, tn), F32), pltpu.VMEM((k, tn), BF16), pltpu.SemaphoreType.DMA((2,))]


def _moe_gu(sched, xs, w_gu, b_gu, *, tn):
    p, half = xs.shape
    d = 2 * half
    f2 = w_gu.shape[2]
    nj = f2 // tn
    nblk = p // MOE_BLOCK
    blk = _last_real_block
    vmem = 2 * d * tn * 4 + d * tn * 2 + 2 * MOE_BLOCK * half * 4 + 2 * MOE_BLOCK * tn + 8 * MOE_BLOCK * tn * 4
    return pl.pallas_call(
        _gu_kernel,
        grid_spec=pltpu.PrefetchScalarGridSpec(
            num_scalar_prefetch=4, grid=(nj, nblk),
            in_specs=[pl.BlockSpec((MOE_BLOCK, half), lambda j, s, be, ri, re, mt: (blk(s, mt), 0)),
                      pl.BlockSpec(memory_space=pl.ANY),
                      pl.BlockSpec((1, 1, tn), lambda j, s, be, ri, re, mt: (be[blk(s, mt)], 0, j))],
            out_specs=pl.BlockSpec((MOE_BLOCK, tn // 2), lambda j, s, be, ri, re, mt: (s, j)),
            scratch_shapes=_expert_stream_scratch(d, tn)),
        out_shape=jax.ShapeDtypeStruct((p, f2 // 2), BF16),
        compiler_params=pltpu.CompilerParams(dimension_semantics=("arbitrary", "arbitrary"),
                                             vmem_limit_bytes=_vmem_limit(vmem)),
        name="moe_gu",
    )(*sched, xs, w_gu, b_gu)


def _down_kernel(be_ref, ridx_ref, run_e_ref, meta_ref, act_ref, w_hbm, b_ref, y_ref, wf32, wbf, sem):
    del be_ref
    _stream_expert_weights(ridx_ref, run_e_ref, meta_ref, w_hbm, wf32, wbf, sem, tn=wbf.shape[1])
    nb_ref = meta_ref

    @pl.when(pl.program_id(1) < nb_ref[0])
    def _():
        y = jnp.dot(act_ref[...], wbf[...], preferred_element_type=F32) + b_ref[0]
        half = y.shape[1] // 2
        y_ref[...] = _pack_bf16_pair(y[:, 0:half], y[:, half:2 * half])

    @pl.when(pl.program_id(1) >= nb_ref[0])
    def _():
        y_ref[...] = jnp.zeros_like(y_ref)


def _moe_down(sched, act, w_down, b_down):
    p, f = act.shape
    d = w_down.shape[2]
    nblk = p // MOE_BLOCK
    blk = _last_real_block
    vmem = 2 * f * d * 4 + f * d * 2 + 2 * MOE_BLOCK * f * 2 + 5 * MOE_BLOCK * d * 4
    return pl.pallas_call(
        _down_kernel,
        grid_spec=pltpu.PrefetchScalarGridSpec(
            num_scalar_prefetch=4, grid=(1, nblk),
            in_specs=[pl.BlockSpec((MOE_BLOCK, f), lambda j, s, be, ri, re, mt: (blk(s, mt), 0)),
                      pl.BlockSpec(memory_space=pl.ANY),
                      pl.BlockSpec((1, 1, d), lambda j, s, be, ri, re, mt: (be[blk(s, mt)], 0, 0))],
            out_specs=pl.BlockSpec((MOE_BLOCK, d // 2), lambda j, s, be, ri, re, mt: (s, 0)),
            scratch_shapes=_expert_stream_scratch(f, d)),
        out_shape=jax.ShapeDtypeStruct((p, d // 2), I32),
        compiler_params=pltpu.CompilerParams(dimension_semantics=("arbitrary", "arbitrary"),
                                             vmem_limit_bytes=_vmem_limit(vmem)),
        name="moe_down",
    )(*sched, act, w_down, b_down)


def _combine_kernel(dest_ref, ys_ref, x1_ref, gate_ref, mods_ref, o_ref, buf, sem, *, tb, d, nt, n_tiles):
    i = pl.program_id(0)
    slot = lax.rem(i, 2)

    def row_copy(tile, sl, r, k):
        return pltpu.make_async_copy(ys_ref.at[pl.ds(dest_ref[(tile * tb + r) * TOP_K + k], 1)],
                                     buf.at[sl, k, pl.ds(r, 1)], sem.at[sl])

    def gather(tile, sl):
        def body(r, _):
            for k in range(TOP_K):
                row_copy(tile, sl, r, k).start()
            return 0
        lax.fori_loop(0, tb, body, 0, unroll=2)

    @pl.when(i == 0)
    def _():
        gather(0, 0)

    @pl.when(i + 1 < n_tiles)
    def _():
        gather(i + 1, 1 - slot)

    for k in range(TOP_K):
        pltpu.make_async_copy(ys_ref.at[pl.ds(0, tb)], buf.at[slot, k], sem.at[slot]).wait()
    half = d // 2
    gate = gate_ref[...]
    y_lo = jnp.zeros((tb, half), F32)
    y_hi = jnp.zeros((tb, half), F32)
    for k in range(TOP_K):
        w = buf[slot, k]
        gk = gate[:, k:k + 1]
        y_lo = y_lo + lax.bitcast_convert_type(lax.shift_left(w, 16), F32) * gk
        y_hi = y_hi + lax.bitcast_convert_type(w & jnp.int32(-65536), F32) * gk
    gt2 = mods_ref[pl.ds(i // nt, 1), 5 * d:6 * d]
    o_ref[:, 0:half] = x1_ref[:, 0:half] + gt2[:, 0:half] * y_lo
    o_ref[:, half:d] = x1_ref[:, half:d] + gt2[:, half:d] * y_hi


def _combine(dest_flat, ys, x1, gate, mods, *, tb, seq):
    t, d = x1.shape
    n_tiles = t // tb
    nt = seq // tb
    vmem = 2 * TOP_K * tb * d * 2 + 4 * tb * d * 4 + 6 * tb * d * 4
    return pl.pallas_call(
        functools.partial(_combine_kernel, tb=tb, d=d, nt=nt, n_tiles=n_tiles),
        grid_spec=pltpu.PrefetchScalarGridSpec(
            num_scalar_prefetch=1, grid=(n_tiles,),
            in_specs=[pl.BlockSpec(memory_space=pl.ANY),
                      pl.BlockSpec((tb, d), lambda i, dr: (i, 0)),
                      pl.BlockSpec((tb, TOP_K), lambda i, dr: (i, 0)),
                      pl.BlockSpec(mods.shape, lambda i, dr: (0, 0))],
            out_specs=pl.BlockSpec((tb, d), lambda i, dr: (i, 0)),
            scratch_shapes=[pltpu.VMEM((2, TOP_K, tb, d // 2), I32), pltpu.SemaphoreType.DMA((2,))]),
        out_shape=jax.ShapeDtypeStruct((t, d), F32),
        compiler_params=pltpu.CompilerParams(dimension_semantics=("arbitrary",),
                                             vmem_limit_bytes=_vmem_limit(vmem)),
        name="combine",
    )(dest_flat, ys, x1, gate, mods)


def _rope_table(s):
    rows = s // GRID_W
    row = jnp.repeat(jnp.arange(rows), GRID_W).astype(F32)
    col = jnp.tile(jnp.arange(GRID_W), rows).astype(F32)
    inv = ROPE_THETA ** (-jnp.arange(ROPE_FREQS, dtype=F32) / ROPE_FREQS)
    ang_r = row[:, None] * inv
    ang_c = col[:, None] * inv
    z = jnp.zeros_like(ang_r)
    cosf = jnp.concatenate([jnp.cos(ang_r)] * 2 + [jnp.cos(ang_c)] * 2, axis=1)
    sneg = jnp.concatenate([-jnp.sin(ang_r), z, -jnp.sin(ang_c), z], axis=1)
    spos = jnp.concatenate([z, jnp.sin(ang_r), z, jnp.sin(ang_c)], axis=1)
    return jnp.concatenate([cosf, sneg, spos], axis=1)


def kernel(x, c, ctx, c_ctx, w_mod, b_mod, g_norm1, w_in, g_q, g_k, conv_w, conv_b, w_gate_a, b_gate_a,
           w_gate_x, b_gate_x, lru_lambda, g_att_out, g_rec_out, w_out, g_norm2, w_router, b_router,
           w_gate_up, b_gate_up, w_down, b_down):
    b, s, d = x.shape
    cl = ctx.shape[1]
    t = b * s
    assert w_mod.shape[0] == 1, "single-layer kernel"
    assert b + 1 <= SUBLANES and d - ATT_WIDTH == REC_BLOCKS * LANES
    assert s % (SCAN_SEGMENTS * SUBLANES) == 0 and cl % (SCAN_SEGMENTS * SUBLANES) == 0

    ctx_row = b
    c8 = jnp.zeros((SUBLANES, d), F32).at[:b].set(c).at[ctx_row].set(c_ctx)
    mods = _mod(c8, w_mod[0], b_mod[0])

    w_in_bf = w_in[0].astype(BF16)
    tm = min(512, s)
    q, k, v, xr, yr = _inproj(x, mods, g_norm1[0], w_in_bf, _rope_table(s), g_q[0], g_k[0],
                              latent=True, ctx_row=ctx_row, tm=tm)
    kc, vc, xrc = _inproj(ctx, mods, g_norm1[0], w_in_bf, None, g_q[0], g_k[0],
                          latent=False, ctx_row=ctx_row, tm=cl)

    v_all = jnp.concatenate([vc, v], axis=1).reshape(b, cl + s, N_KV_HEADS, HEAD_DIM)
    v_ones = jnp.concatenate([v_all, jnp.ones_like(v_all)], axis=-1).reshape(b, cl + s, 2 * KV_WIDTH)
    att = _attention(q, jnp.concatenate([kc, k], axis=1), v_ones, tq=min(256, s))

    w_gates = jnp.concatenate([w_gate_a[0, 0], w_gate_x[0, 0], w_gate_a[0, 1], w_gate_x[0, 1]], axis=-1).astype(BF16)
    rw = d - ATT_WIDTH
    bias = lambda bb: bb.reshape(REC_BLOCKS, LANES)
    b_gates = jnp.concatenate([bias(b_gate_a[0, 0]), bias(b_gate_x[0, 0]), bias(b_gate_a[0, 1]), bias(b_gate_x[0, 1])],
                              axis=-1).reshape(1, 4 * rw)
    rec = _rglru(xr, xrc, yr, conv_w[0], conv_b[0].reshape(1, rw), w_gates, b_gates, lru_lambda[0])

    x1, h2p, top_idx, gate, mask = _merge(att, rec, x, mods, g_att_out[0], g_rec_out[0], w_out[0].astype(BF16),
                                          g_norm2[0], w_router[0], b_router[0], tm=tm)

    rank, counts = _rank(mask, top_idx, tb=MOE_BLOCK)
    counts = counts[0]
    padded = (counts + MOE_BLOCK - 1) // MOE_BLOCK * MOE_BLOCK
    pad_ends = jnp.cumsum(padded)
    pad_starts = pad_ends - padded
    dest = (pad_starts[top_idx] + rank).reshape(t * TOP_K)
    n_blocks = (t * TOP_K + N_EXPERTS * (MOE_BLOCK - 1) + MOE_BLOCK - 1) // MOE_BLOCK
    block_start = jnp.arange(n_blocks, dtype=I32) * MOE_BLOCK
    block_e = jnp.minimum(jnp.sum((pad_ends[None, :] <= block_start[:, None]).astype(I32), axis=1), N_EXPERTS - 1)
    n_valid = (pad_ends[N_EXPERTS - 1] // MOE_BLOCK).reshape(1).astype(I32)
    fill = jnp.concatenate([pad_starts + counts, pad_ends, n_valid]).astype(I32)

    xs = _dispatch(dest, fill, h2p, tb=MOE_BLOCK, n_blocks=n_blocks)

    used = padded > 0
    run_of_expert = jnp.cumsum(used.astype(I32)) - 1
    experts = jnp.arange(N_EXPERTS, dtype=I32)
    run_e = jnp.sum(jnp.where(used[None, :] & (run_of_expert[None, :] == experts[:, None]), experts[None, :], 0), axis=1)
    meta = jnp.concatenate([n_valid, jnp.sum(used.astype(I32)).reshape(1)])
    sched = (block_e, run_of_expert[block_e], run_e.astype(I32), meta)

    f2 = w_gate_up.shape[3]
    act = _moe_gu(sched, xs, w_gate_up[0], b_gate_up[0].reshape(N_EXPERTS, 1, f2), tn=2048)
    ys = _moe_down(sched, act, w_down[0], b_down[0].reshape(N_EXPERTS, 1, d))

    return _combine(dest, ys, x1.reshape(t, d), gate, mods, tb=128, seq=s).reshape(b, s, d)
```

```python
import functools
import math

import jax
import jax.numpy as jnp
from jax import lax
from jax.experimental import pallas as pl
from jax.experimental.pallas import tpu as pltpu

F32 = jnp.float32
BF16 = jnp.bfloat16
I32 = jnp.int32

EPS = 1e-6
GRID_W = 64
HEAD_DIM = 128
N_Q_HEADS = 8
N_KV_HEADS = 2
GROUP = N_Q_HEADS // N_KV_HEADS
ATT_WIDTH = N_Q_HEADS * HEAD_DIM
KV_WIDTH = N_KV_HEADS * HEAD_DIM
ROPE_THETA = 10000.0
ROPE_FREQS = HEAD_DIM // 4
REC_BLOCKS = 8
CONV_W = 4
CONV_LEFT = 2
LRU_C = 8.0
N_EXPERTS = 32
TOP_K = 4
SWIGLU_LIMIT = 7.0
SWIGLU_ALPHA = 1.702
MOE_BLOCK = 256

V7X_VMEM_BYTES = 64 * 1024 * 1024
SUBLANES = 8
LANES = 128
SCAN_SEGMENTS = SUBLANES
CONV_PAD = SUBLANES

HIGHEST = lax.Precision.HIGHEST


def _vmem_limit(nbytes):
    return int(min(V7X_VMEM_BYTES - 4 * 1024 * 1024, max(nbytes, 16 * 1024 * 1024)))


def _rms(x, g):
    return x * lax.rsqrt(jnp.mean(x * x, axis=-1, keepdims=True) + EPS) * g


def _mod_kernel(c_ref, w_ref, b_ref, o_ref):
    c = c_ref[...]
    a = c * jax.nn.sigmoid(c)
    o_ref[...] = jnp.dot(a, w_ref[...], preferred_element_type=F32, precision=HIGHEST) + b_ref[...]


def _mod(c8, w_mod, b_mod):
    d, n = w_mod.shape
    tn = 1024
    return pl.pallas_call(
        _mod_kernel,
        grid=(n // tn,),
        in_specs=[pl.BlockSpec((SUBLANES, d), lambda j: (0, 0)),
                  pl.BlockSpec((d, tn), lambda j: (0, j)),
                  pl.BlockSpec((1, tn), lambda j: (0, j))],
        out_specs=pl.BlockSpec((SUBLANES, tn), lambda j: (0, j)),
        out_shape=jax.ShapeDtypeStruct((SUBLANES, n), F32),
        compiler_params=pltpu.CompilerParams(dimension_semantics=("arbitrary",),
                                             vmem_limit_bytes=_vmem_limit(3 * d * tn * 4)),
        name="mod",
    )(c8, w_mod, b_mod.reshape(1, n))


def _qk_norm_rope(y, g, rope):
    yn = _rms(y, g)
    if rope is None:
        return yn
    cosf, sneg, spos = rope
    return yn * cosf + pltpu.roll(yn, HEAD_DIM - ROPE_FREQS, 1) * sneg + pltpu.roll(yn, ROPE_FREQS, 1) * spos


def _inproj_kernel(*refs, d, latent, ctx_row):
    if latent:
        (x_ref, mods_ref, g1_ref, w_ref, rope_ref, gq_ref, gk_ref,
         q_ref, k_ref, v_ref, xr_ref, yr_ref) = refs
        row = pl.program_id(0)
    else:
        x_ref, mods_ref, g1_ref, w_ref, gk_ref, k_ref, v_ref, xr_ref = refs
        row = ctx_row
    sh = mods_ref[pl.ds(row, 1), 0:d]
    sc = mods_ref[pl.ds(row, 1), d:2 * d]
    h = _rms(x_ref[0], g1_ref[...]) * (1.0 + sc) + sh
    hb = h.astype(BF16)

    def proj(lo, hi):
        return jnp.dot(hb, w_ref[:, lo:hi], preferred_element_type=F32)

    o_k = ATT_WIDTH
    o_v = o_k + KV_WIDTH
    o_xr = o_v + KV_WIDTH
    rec_w = d - ATT_WIDTH
    o_yr = o_xr + rec_w
    rope = None
    if latent:
        rp = rope_ref[...]
        rope = (rp[:, 0:HEAD_DIM], rp[:, HEAD_DIM:2 * HEAD_DIM], rp[:, 2 * HEAD_DIM:3 * HEAD_DIM])
        q = proj(0, ATT_WIDTH)
        for hd in range(N_Q_HEADS):
            sl = slice(hd * HEAD_DIM, (hd + 1) * HEAD_DIM)
            q_ref[0, :, sl] = (_qk_norm_rope(q[:, sl], gq_ref[...], rope) * (HEAD_DIM ** -0.5)).astype(BF16)
    k = proj(o_k, o_v)
    for hd in range(N_KV_HEADS):
        sl = slice(hd * HEAD_DIM, (hd + 1) * HEAD_DIM)
        k_ref[0, :, sl] = _qk_norm_rope(k[:, sl], gk_ref[...], rope).astype(BF16)
    v_ref[0] = proj(o_v, o_xr).astype(BF16)
    xr_ref[0] = proj(o_xr, o_yr)
    if latent:
        yr_ref[0] = proj(o_yr, o_yr + rec_w)


def _inproj(x, mods, g1, w_in_bf, rope_tab, g_q, g_k, *, latent, ctx_row, tm):
    b, s, d = x.shape
    n = w_in_bf.shape[1]
    rec_w = d - ATT_WIDTH
    grid = (b, s // tm)
    row_spec = lambda w: pl.BlockSpec((1, tm, w), lambda bi, i: (bi, i, 0))
    full2 = lambda a: pl.BlockSpec(a.shape, lambda bi, i: (0, 0))
    in_specs = [row_spec(d), full2(mods), pl.BlockSpec((1, d), lambda bi, i: (0, 0)),
                pl.BlockSpec((d, n), lambda bi, i: (0, 0), pipeline_mode=pl.Buffered(1))]
    args = [x, mods, g1.reshape(1, d), w_in_bf]
    out_specs, out_shape = [], []
    if latent:
        in_specs += [pl.BlockSpec((tm, 3 * HEAD_DIM), lambda bi, i: (i, 0)),
                     pl.BlockSpec((1, HEAD_DIM), lambda bi, i: (0, 0))]
        args += [rope_tab, g_q.reshape(1, HEAD_DIM)]
        out_specs.append(row_spec(ATT_WIDTH))
        out_shape.append(jax.ShapeDtypeStruct((b, s, ATT_WIDTH), BF16))
    in_specs.append(pl.BlockSpec((1, HEAD_DIM), lambda bi, i: (0, 0)))
    args.append(g_k.reshape(1, HEAD_DIM))
    out_specs += [row_spec(KV_WIDTH), row_spec(KV_WIDTH), row_spec(rec_w)]
    out_shape += [jax.ShapeDtypeStruct((b, s, KV_WIDTH), BF16), jax.ShapeDtypeStruct((b, s, KV_WIDTH), BF16),
                  jax.ShapeDtypeStruct((b, s, rec_w), F32)]
    if latent:
        out_specs.append(row_spec(rec_w))
        out_shape.append(jax.ShapeDtypeStruct((b, s, rec_w), F32))
    vmem = d * n * 2 + 2 * tm * d * 4 + 2 * tm * n * 4 + 3 * tm * d * 4 + 2 * tm * n * 4
    return pl.pallas_call(
        functools.partial(_inproj_kernel, d=d, latent=latent, ctx_row=ctx_row),
        grid=grid, in_specs=in_specs, out_specs=out_specs, out_shape=out_shape,
        compiler_params=pltpu.CompilerParams(dimension_semantics=("arbitrary", "arbitrary"),
                                             vmem_limit_bytes=_vmem_limit(vmem)),
        name="inproj_latent" if latent else "inproj_ctx",
    )(*args)


def _attn_kernel(q_ref, k_ref, v_ref, o_ref):
    k = k_ref[0]
    v = v_ref[0]
    for g in range(GROUP):
        sl = slice(g * HEAD_DIM, (g + 1) * HEAD_DIM)
        s = lax.dot_general(q_ref[0, :, sl], k, (((1,), (1,)), ((), ())), preferred_element_type=F32)
        m = jnp.max(s, axis=-1, keepdims=True)
        p = jnp.exp(s - m).astype(BF16)
        o = jnp.dot(p, v, preferred_element_type=F32)
        o_ref[0, :, sl] = (o[:, 0:HEAD_DIM] / o[:, HEAD_DIM:2 * HEAD_DIM]).astype(BF16)


def _attention(q, k_all, v_all, *, tq):
    b, s, _ = q.shape
    lk = k_all.shape[1]
    gw = GROUP * HEAD_DIM
    vmem = 4 * lk * HEAD_DIM * 2 * 2 + 4 * tq * gw * 2 + 4 * tq * lk * 4
    return pl.pallas_call(
        _attn_kernel,
        grid=(b, N_KV_HEADS, s // tq),
        in_specs=[pl.BlockSpec((1, tq, gw), lambda bi, h, i: (bi, i, h)),
                  pl.BlockSpec((1, lk, HEAD_DIM), lambda bi, h, i: (bi, 0, h)),
                  pl.BlockSpec((1, lk, 2 * HEAD_DIM), lambda bi, h, i: (bi, 0, h))],
        out_specs=pl.BlockSpec((1, tq, gw), lambda bi, h, i: (bi, i, h)),
        out_shape=jax.ShapeDtypeStruct((b, s, ATT_WIDTH), BF16),
        compiler_params=pltpu.CompilerParams(dimension_semantics=("arbitrary",) * 3,
                                             vmem_limit_bytes=_vmem_limit(vmem)),
        name="attention",
    )(q, k_all, v_all)


def _gelu_tanh(x):
    return 0.5 * x * (1.0 + jnp.tanh(math.sqrt(2.0 / math.pi) * (x + 0.044715 * x * x * x)))


def _rglru_kernel(xr_ref, xc_ref, yr_ref, cw_ref, cb_ref, wg_ref, bg_ref, lam_ref, o_ref,
                  xp, xpc, af, bf, ab, bb, caf, cbf, cab, cbb, *, s, c):
    nseg = SCAN_SEGMENTS
    seg = s // nseg
    cseg = c // nseg
    zeros_pad = jnp.zeros((CONV_PAD, LANES), F32)
    xp[0:CONV_PAD, :] = zeros_pad
    xp[CONV_PAD + s:2 * CONV_PAD + s, :] = zeros_pad
    xp[CONV_PAD:CONV_PAD + s, :] = xr_ref[0]
    xpc[0:CONV_PAD, :] = zeros_pad
    xpc[CONV_PAD + c:2 * CONV_PAD + c, :] = zeros_pad
    xpc[CONV_PAD:CONV_PAD + c, :] = xc_ref[0]

    cw = cw_ref[...]
    cb = cb_ref[...]
    wg = wg_ref[0]
    bg = bg_ref[...]
    sp = jax.nn.softplus(-lam_ref[...])

    def coeffs(src, lo, n):
        u = cb
        for j in range(CONV_W):
            u = u + src[CONV_PAD + lo + j - CONV_LEFT:CONV_PAD + lo + j - CONV_LEFT + n, :] * cw[j:j + 1, :]
        g = jnp.dot(u.astype(BF16), wg, preferred_element_type=F32) + bg
        out = []
        for r in range(2):
            ga = g[:, (2 * r) * LANES:(2 * r + 1) * LANES]
            gx = g[:, (2 * r + 1) * LANES:(2 * r + 2) * LANES]
            log_a = (-LRU_C) * jax.nn.sigmoid(ga) * sp[r:r + 1, :]
            a = jnp.exp(log_a)
            mult = jnp.sqrt(-jnp.tanh(log_a) * (1.0 + a * a))
            out.append((a, mult * jax.nn.sigmoid(gx) * u))
        return out

    (a0, b0), (a1, b1) = coeffs(xpc, 0, c)
    for q in range(nseg):
        rows = slice(q * cseg, (q + 1) * cseg)
        dst = pl.ds(q, cseg, stride=nseg)
        caf[dst, :] = a0[rows]
        cbf[dst, :] = b0[rows]
        cab[dst, :] = a1[rows]
        cbb[dst, :] = b1[rows]
    for q in range(nseg):
        (a0, b0), (a1, b1) = coeffs(xp, q * seg, seg)
        dst = pl.ds(q, seg, stride=nseg)
        af[dst, :] = a0
        bf[dst, :] = b0
        ab[dst, :] = a1
        bb[dst, :] = b1

    def scan(a_f, b_f, a_b, b_b, n, store):
        def body(j, carry):
            hf, pf, hb, pb = carry
            rf = pl.multiple_of(j * nseg, nseg)
            rb = pl.multiple_of((n - 1 - j) * nseg, nseg)
            av = a_f[pl.ds(rf, nseg), :]
            hf = av * hf + b_f[pl.ds(rf, nseg), :]
            pf = av * pf
            aw = a_b[pl.ds(rb, nseg), :]
            hb = aw * hb + b_b[pl.ds(rb, nseg), :]
            pb = aw * pb
            if store:
                a_f[pl.ds(rf, nseg), :] = pf
                b_f[pl.ds(rf, nseg), :] = hf
                a_b[pl.ds(rb, nseg), :] = pb
                b_b[pl.ds(rb, nseg), :] = hb
            return hf, pf, hb, pb
        z = jnp.zeros((nseg, LANES), F32)
        o = jnp.ones((nseg, LANES), F32)
        return lax.fori_loop(0, n, body, (z, o, z, o), unroll=8)

    def chain(h_end, p_end, h0, reverse):
        order = range(nseg - 1, -1, -1) if reverse else range(nseg)
        enter = [None] * nseg
        cur = h0
        for q in order:
            enter[q] = cur
            cur = h_end[q:q + 1, :] + p_end[q:q + 1, :] * cur
        return enter, cur

    zero_row = jnp.zeros((1, LANES), F32)
    hf, pf, hb, pb = scan(caf, cbf, cab, cbb, cseg, False)
    _, h0f = chain(hf, pf, zero_row, False)
    _, h0b = chain(hb, pb, zero_row, True)
    hf, pf, hb, pb = scan(af, bf, ab, bb, seg, True)
    enter_f, _ = chain(hf, pf, h0f, False)
    enter_b, _ = chain(hb, pb, h0b, True)
    for q in range(nseg):
        src = pl.ds(q, seg, stride=nseg)
        h = bf[src, :] + af[src, :] * enter_f[q] + bb[src, :] + ab[src, :] * enter_b[q]
        rows = slice(q * seg, (q + 1) * seg)
        o_ref[0, rows, :] = (h * _gelu_tanh(yr_ref[0, rows, :])).astype(BF16)


def _rglru(xr, xrc, yr, conv_w, conv_b, w_gates, b_gates, lam):
    b, s, w = xr.shape
    c = xrc.shape[1]
    nb = w // LANES
    slab = lambda n: pl.BlockSpec((1, n, LANES), lambda bi, j: (bi, 0, j))
    scr = lambda n: pltpu.VMEM((n, LANES), F32)
    vmem = (3 * 2 + 5) * s * LANES * 4 + 8 * s * LANES * 4
    return pl.pallas_call(
        functools.partial(_rglru_kernel, s=s, c=c),
        grid=(b, nb),
        in_specs=[slab(s), slab(c), slab(s),
                  pl.BlockSpec((CONV_W, LANES), lambda bi, j: (0, j)),
                  pl.BlockSpec((1, LANES), lambda bi, j: (0, j)),
                  pl.BlockSpec((1, LANES, 4 * LANES), lambda bi, j: (j, 0, 0)),
                  pl.BlockSpec((1, 4 * LANES), lambda bi, j: (0, j)),
                  pl.BlockSpec((2, LANES), lambda bi, j: (0, j))],
        out_specs=slab(s),
        out_shape=jax.ShapeDtypeStruct((b, s, w), BF16),
        scratch_shapes=[scr(s + 2 * CONV_PAD), scr(c + 2 * CONV_PAD),
                        scr(s), scr(s), scr(s), scr(s), scr(c), scr(c), scr(c), scr(c)],
        compiler_params=pltpu.CompilerParams(dimension_semantics=("arbitrary", "arbitrary"),
                                             vmem_limit_bytes=_vmem_limit(vmem)),
        name="rglru",
    )(xr, xrc, yr, conv_w, conv_b, w_gates, b_gates, lam)


def _pack_bf16_pair(lo, hi):
    lo_bits = lax.bitcast_convert_type(lo.astype(BF16).astype(F32), I32)
    hi_bits = lax.bitcast_convert_type(hi.astype(BF16).astype(F32), I32)
    return lax.shift_right_logical(lo_bits, 16) | (hi_bits & jnp.int32(-65536))


def _unpack_bf16_pair(w):
    lo = lax.bitcast_convert_type(lax.shift_left(w, 16), F32).astype(BF16)
    hi = lax.bitcast_convert_type(w & jnp.int32(-65536), F32).astype(BF16)
    return lo, hi


def _merge_kernel(att_ref, rec_ref, x_ref, mods_ref, ga_ref, gr_ref, wo_ref, g2_ref, wr_ref, br_ref,
                  x1_ref, h2_ref, idx_ref, gate_ref, mask_ref, *, d):
    row = pl.program_id(0)
    gt1 = mods_ref[pl.ds(row, 1), 2 * d:3 * d]
    sh2 = mods_ref[pl.ds(row, 1), 3 * d:4 * d]
    sc2 = mods_ref[pl.ds(row, 1), 4 * d:5 * d]
    an = _rms(att_ref[0].astype(F32), ga_ref[...]).astype(BF16)
    rn = _rms(rec_ref[0].astype(F32), gr_ref[...]).astype(BF16)
    mix = (jnp.dot(an, wo_ref[0:ATT_WIDTH, :], preferred_element_type=F32)
           + jnp.dot(rn, wo_ref[ATT_WIDTH:d, :], preferred_element_type=F32))
    x1 = x_ref[0] + gt1 * mix
    x1_ref[0] = x1
    h2 = _rms(x1, g2_ref[...]) * (1.0 + sc2) + sh2
    half = d // 2
    h2_ref[...] = _pack_bf16_pair(h2[:, 0:half], h2[:, half:d])
    h_hi = h2.astype(BF16)
    h_lo = (h2 - h_hi.astype(F32)).astype(BF16)
    w_split = wr_ref[...]
    part = (jnp.dot(h_hi, w_split, preferred_element_type=F32)
            + jnp.dot(h_lo, w_split, preferred_element_type=F32))
    logits = part[:, 0:N_EXPERTS] + part[:, N_EXPERTS:2 * N_EXPERTS] + br_ref[...]
    tm = logits.shape[0]
    lane = lax.broadcasted_iota(I32, (tm, N_EXPERTS), 1).astype(F32)
    col = lax.broadcasted_iota(I32, (tm, TOP_K), 1)
    idx = jnp.zeros((tm, TOP_K), F32)
    ex = jnp.zeros((tm, TOP_K), F32)
    mask = jnp.zeros((tm, N_EXPERTS), F32)
    rest = logits
    top = None
    for k in range(TOP_K):
        m = jnp.max(rest, axis=-1, keepdims=True)
        first = jnp.min(jnp.where(rest == m, lane, float(N_EXPERTS)), axis=-1, keepdims=True)
        sel = lane == first
        if k == 0:
            top = m
        idx = jnp.where(col == k, first, idx)
        ex = jnp.where(col == k, jnp.exp(m - top), ex)
        mask = jnp.where(sel, 1.0, mask)
        rest = jnp.where(sel, -jnp.inf, rest)
    idx_ref[...] = idx.astype(I32)
    gate_ref[...] = ex / jnp.sum(ex, axis=-1, keepdims=True)
    mask_ref[...] = mask


def _merge(att, rec, x, mods, g_att, g_rec, w_out_bf, g2, w_router, b_router, *, tm):
    b, s, d = x.shape
    t = b * s
    nt = s // tm
    rec_w = d - ATT_WIDTH
    row3 = lambda w: pl.BlockSpec((1, tm, w), lambda bi, i: (bi, i, 0))
    tok2 = lambda w: pl.BlockSpec((tm, w), lambda bi, i: (bi * nt + i, 0))
    const = lambda shape, **kw: pl.BlockSpec(shape, lambda bi, i: (0,) * len(shape), **kw)
    vmem = d * d * 2 + 2 * tm * (ATT_WIDTH + rec_w) * 2 + 4 * tm * d * 4 + tm * d * 4 + 8 * tm * d * 4
    return pl.pallas_call(
        functools.partial(_merge_kernel, d=d),
        grid=(b, nt),
        in_specs=[row3(ATT_WIDTH), row3(rec_w), row3(d), const(mods.shape),
                  const((1, ATT_WIDTH)), const((1, rec_w)),
                  const((d, d), pipeline_mode=pl.Buffered(1)), const((1, d)),
                  const((d, 2 * N_EXPERTS)), const((1, N_EXPERTS))],
        out_specs=[row3(d), tok2(d // 2), tok2(TOP_K), tok2(TOP_K), tok2(N_EXPERTS)],
        out_shape=[jax.ShapeDtypeStruct((b, s, d), F32), jax.ShapeDtypeStruct((t, d // 2), I32),
                   jax.ShapeDtypeStruct((t, TOP_K), I32), jax.ShapeDtypeStruct((t, TOP_K), F32),
                   jax.ShapeDtypeStruct((t, N_EXPERTS), F32)],
        compiler_params=pltpu.CompilerParams(dimension_semantics=("arbitrary", "arbitrary"),
                                             vmem_limit_bytes=_vmem_limit(vmem)),
        name="merge",
    )(att, rec, x, mods, g_att.reshape(1, -1), g_rec.reshape(1, -1), w_out_bf, g2.reshape(1, d),
      _split_bf16(w_router), b_router.reshape(1, N_EXPERTS))


def _split_bf16(w):
    hi = w.astype(BF16)
    lo = (w - hi.astype(F32)).astype(BF16)
    return jnp.concatenate([hi, lo], axis=1)


def _rank_kernel(mask_ref, idx_ref, rank_ref, cnt_ref, carry):
    @pl.when(pl.program_id(0) == 0)
    def _():
        carry[...] = jnp.zeros_like(carry)

    m = mask_ref[...]
    tb = m.shape[0]
    r = lax.broadcasted_iota(I32, (tb, tb), 0)
    cidx = lax.broadcasted_iota(I32, (tb, tb), 1)
    tri = jnp.where(cidx < r, 1.0, 0.0).astype(BF16)
    before = jnp.dot(tri, m.astype(BF16), preferred_element_type=F32) + carry[...]
    lane = lax.broadcasted_iota(I32, (tb, N_EXPERTS), 1)
    col = lax.broadcasted_iota(I32, (tb, TOP_K), 1)
    idx = idx_ref[...]
    rank = jnp.zeros((tb, TOP_K), F32)
    for k in range(TOP_K):
        pick = jnp.sum(jnp.where(lane == idx[:, k:k + 1], before, 0.0), axis=-1, keepdims=True)
        rank = jnp.where(col == k, pick, rank)
    rank_ref[...] = rank.astype(I32)
    carry[...] = carry[...] + jnp.sum(m, axis=0, keepdims=True)
    cnt_ref[...] = carry[...].astype(I32)


def _rank(mask, idx, *, tb):
    t = mask.shape[0]
    return pl.pallas_call(
        _rank_kernel,
        grid=(t // tb,),
        in_specs=[pl.BlockSpec((tb, N_EXPERTS), lambda i: (i, 0)), pl.BlockSpec((tb, TOP_K), lambda i: (i, 0))],
        out_specs=[pl.BlockSpec((tb, TOP_K), lambda i: (i, 0)), pl.BlockSpec((1, N_EXPERTS), lambda i: (0, 0))],
        out_shape=[jax.ShapeDtypeStruct((t, TOP_K), I32), jax.ShapeDtypeStruct((1, N_EXPERTS), I32)],
        scratch_shapes=[pltpu.VMEM((1, N_EXPERTS), F32)],
        compiler_params=pltpu.CompilerParams(dimension_semantics=("arbitrary",)),
        name="rank",
    )(mask, idx)


def _dispatch_kernel(dest_ref, fill_ref, h2_ref, xs_ref, zeros, sem, fill_sem, *, tb, n_blocks):
    base = pl.program_id(0) * tb

    @pl.when(pl.program_id(0) == 0)
    def _():
        zeros[...] = jnp.zeros_like(zeros)

        def zero_row(r):
            return pltpu.make_async_copy(zeros.at[pl.ds(0, 1)], xs_ref.at[pl.ds(r, 1)], fill_sem)

        def zero_block(bk):
            rows = pl.ds(pl.multiple_of(bk * MOE_BLOCK, MOE_BLOCK), MOE_BLOCK)
            return pltpu.make_async_copy(zeros, xs_ref.at[rows], fill_sem)

        def for_all_fills(act):
            for e in range(N_EXPERTS):
                lax.fori_loop(fill_ref[e], fill_ref[N_EXPERTS + e], lambda r, c: (act(zero_row(r)), c)[1], 0)
            lax.fori_loop(fill_ref[2 * N_EXPERTS], n_blocks, lambda bk, c: (act(zero_block(bk)), c)[1], 0)

        for_all_fills(lambda cp: cp.start())
        for_all_fills(lambda cp: cp.wait())

    def row_copy(i, k):
        return pltpu.make_async_copy(h2_ref.at[pl.ds(i, 1)],
                                     xs_ref.at[pl.ds(dest_ref[(base + i) * TOP_K + k], 1)], sem)

    def body(i, _):
        for k in range(TOP_K):
            row_copy(i, k).start()
        return 0

    lax.fori_loop(0, tb, body, 0, unroll=4)
    for k in range(TOP_K):
        pltpu.make_async_copy(h2_ref, xs_ref.at[pl.ds(0, tb)], sem).wait()


def _dispatch(dest_flat, fill, h2p, *, tb, n_blocks):
    t, wd = h2p.shape
    return pl.pallas_call(
        functools.partial(_dispatch_kernel, tb=tb, n_blocks=n_blocks),
        grid_spec=pltpu.PrefetchScalarGridSpec(
            num_scalar_prefetch=2, grid=(t // tb,),
            in_specs=[pl.BlockSpec((tb, wd), lambda i, dr, fl: (i, 0))],
            out_specs=pl.BlockSpec(memory_space=pl.ANY),
            scratch_shapes=[pltpu.VMEM((MOE_BLOCK, wd), I32), pltpu.SemaphoreType.DMA(()),
                            pltpu.SemaphoreType.DMA(())]),
        out_shape=jax.ShapeDtypeStruct((n_blocks * MOE_BLOCK, wd), I32),
        compiler_params=pltpu.CompilerParams(dimension_semantics=("arbitrary",), has_side_effects=True),
        name="dispatch",
    )(dest_flat, fill, h2p)


PAIR_CHUNK = 512


def _expert_block_loop(ridx_ref, run_e_ref, meta_ref, x_hbm, w_hbm, o_hbm,
                       xbuf, obuf, zbuf, wf32, wbf, xsem, osem, zsem, wsem, *, n_blocks, compute):
    j = pl.program_id(0)
    nb = meta_ref[0]
    n_runs = meta_ref[1]
    tn = wbf.shape[1]
    ocols = obuf.shape[2]

    def rows(s):
        return pl.ds(pl.multiple_of(s * MOE_BLOCK, MOE_BLOCK), MOE_BLOCK)

    out_cols = pl.ds(pl.multiple_of(j * ocols, ocols), ocols)

    def x_copy(s, slot):
        return pltpu.make_async_copy(x_hbm.at[rows(s)], xbuf.at[slot], xsem.at[slot])

    def o_copy(s, slot):
        return pltpu.make_async_copy(obuf.at[slot], o_hbm.at[rows(s), out_cols], osem.at[slot])

    def z_copy(s):
        return pltpu.make_async_copy(zbuf, o_hbm.at[rows(s), out_cols], zsem)

    def w_copy(run, sweep, slot):
        cols = pl.ds(pl.multiple_of(sweep * tn, tn), tn)
        return pltpu.make_async_copy(w_hbm.at[run_e_ref[run], :, cols], wf32.at[slot], wsem.at[slot])

    def weights_for(s):
        r = ridx_ref[s]

        @pl.when((s == 0) | (r != ridx_ref[jnp.maximum(s - 1, 0)]))
        def _():
            g = j * n_runs + r
            slot = lax.rem(g, 2)

            @pl.when(g == 0)
            def _():
                w_copy(0, 0, 0).start()

            w_copy(r, j, slot).wait()
            for static_slot in range(2):
                @pl.when(slot == static_slot)
                def _():
                    wbf[...] = wf32[static_slot].astype(BF16)
            more_runs = r + 1 < n_runs

            @pl.when(more_runs)
            def _():
                w_copy(r + 1, j, 1 - slot).start()

            @pl.when(jnp.logical_not(more_runs) & (j + 1 < pl.num_programs(0)))
            def _():
                w_copy(0, j + 1, 1 - slot).start()

    zbuf[...] = jnp.zeros_like(zbuf)
    lax.fori_loop(nb, n_blocks, lambda s, c: (z_copy(s).start(), c)[1], 0)

    x_copy(0, 0).start()

    def pair(ip, carry):
        for slot in range(2):
            s = 2 * ip + slot

            @pl.when(s < nb)
            def _():
                weights_for(s)

                @pl.when(s + 1 < nb)
                def _():
                    x_copy(s + 1, 1 - slot).start()

                x_copy(s, slot).wait()

                @pl.when(s >= 2)
                def _():
                    o_copy(s - 2, slot).wait()

                obuf[slot] = compute(xbuf.at[slot], run_e_ref[ridx_ref[s]])
                o_copy(s, slot).start()
        return carry

    lax.fori_loop(0, (nb + 1) // 2, pair, 0)
    for slot in range(2):
        @pl.when(nb > slot)
        def _():
            o_copy(0, slot).wait()
    lax.fori_loop(nb, n_blocks, lambda s, c: (z_copy(s).wait(), c)[1], 0)


def _expert_loop_scratch(k, tn, x_cols, x_dtype, o_cols, o_dtype):
    dma = pltpu.SemaphoreType.DMA
    return [pltpu.VMEM((2, MOE_BLOCK, x_cols), x_dtype), pltpu.VMEM((2, MOE_BLOCK, o_cols), o_dtype),
            pltpu.VMEM((MOE_BLOCK, o_cols), o_dtype), pltpu.VMEM((2, k, tn), F32), pltpu.VMEM((k, tn), BF16),
            dma((2,)), dma((2,)), dma(()), dma((2,))]


def _gu_kernel(ridx_ref, run_e_ref, meta_ref, xs_hbm, w_hbm, b_ref, act_hbm, *scratch, n_blocks):
    wbf = scratch[4]
    tn = wbf.shape[1]
    bias_cols = pl.ds(pl.multiple_of(pl.program_id(0) * tn, tn), tn)

    def compute(x_ref, e):
        lo, hi = _unpack_bf16_pair(x_ref[...])
        xb = jnp.concatenate([lo, hi], axis=1)
        gu = jnp.dot(xb, wbf[...], preferred_element_type=F32) + b_ref[pl.ds(e, 1), bias_cols]
        g = jnp.minimum(gu, SWIGLU_LIMIT)
        up1 = jnp.clip(gu, -SWIGLU_LIMIT, SWIGLU_LIMIT) + 1.0
        paired = (pltpu.roll(up1, tn - 1, 1) * (g * jax.nn.sigmoid(SWIGLU_ALPHA * g))).astype(BF16)
        r = lax.broadcasted_iota(I32, (PAIR_CHUNK, PAIR_CHUNK // 2), 0)
        c = lax.broadcasted_iota(I32, (PAIR_CHUNK, PAIR_CHUNK // 2), 1)
        sel = jnp.where(r == 2 * c, 1.0, 0.0).astype(BF16)
        return jnp.concatenate(
            [jnp.dot(paired[:, ch * PAIR_CHUNK:(ch + 1) * PAIR_CHUNK], sel, preferred_element_type=F32).astype(BF16)
             for ch in range(tn // PAIR_CHUNK)], axis=1)

    _expert_block_loop(ridx_ref, run_e_ref, meta_ref, xs_hbm, w_hbm, act_hbm, *scratch,
                       n_blocks=n_blocks, compute=compute)


def _moe_gu(sched, xs, w_gu, b_gu, *, tn):
    p, half = xs.shape
    d = 2 * half
    f2 = w_gu.shape[2]
    nblk = p // MOE_BLOCK
    vmem = (2 * d * tn * 4 + d * tn * 2 + 2 * MOE_BLOCK * half * 4 + 3 * MOE_BLOCK * tn + 10 * MOE_BLOCK * tn * 4
            + 2 * b_gu.size * 4)
    return pl.pallas_call(
        functools.partial(_gu_kernel, n_blocks=nblk),
        grid_spec=pltpu.PrefetchScalarGridSpec(
            num_scalar_prefetch=3, grid=(f2 // tn,),
            in_specs=[pl.BlockSpec(memory_space=pl.ANY), pl.BlockSpec(memory_space=pl.ANY),
                      pl.BlockSpec(b_gu.shape, lambda j, ri, re, mt: (0, 0))],
            out_specs=pl.BlockSpec(memory_space=pl.ANY),
            scratch_shapes=_expert_loop_scratch(d, tn, half, I32, tn // 2, BF16)),
        out_shape=jax.ShapeDtypeStruct((p, f2 // 2), BF16),
        compiler_params=pltpu.CompilerParams(dimension_semantics=("arbitrary",), has_side_effects=True,
                                             vmem_limit_bytes=_vmem_limit(vmem)),
        name="moe_gu",
    )(*sched, xs, w_gu, b_gu)


def _down_kernel(ridx_ref, run_e_ref, meta_ref, act_hbm, w_hbm, b_ref, y_hbm, *scratch, n_blocks):
    wbf = scratch[4]

    def compute(a_ref, e):
        y = jnp.dot(a_ref[...], wbf[...], preferred_element_type=F32) + b_ref[pl.ds(e, 1), :]
        half = y.shape[1] // 2
        return _pack_bf16_pair(y[:, 0:half], y[:, half:2 * half])

    _expert_block_loop(ridx_ref, run_e_ref, meta_ref, act_hbm, w_hbm, y_hbm, *scratch,
                       n_blocks=n_blocks, compute=compute)


def _moe_down(sched, act, w_down, b_down):
    p, f = act.shape
    d = w_down.shape[2]
    nblk = p // MOE_BLOCK
    vmem = 2 * f * d * 4 + f * d * 2 + 2 * MOE_BLOCK * f * 2 + 3 * MOE_BLOCK * d * 2 + 5 * MOE_BLOCK * d * 4
    return pl.pallas_call(
        functools.partial(_down_kernel, n_blocks=nblk),
        grid_spec=pltpu.PrefetchScalarGridSpec(
            num_scalar_prefetch=3, grid=(1,),
            in_specs=[pl.BlockSpec(memory_space=pl.ANY), pl.BlockSpec(memory_space=pl.ANY),
                      pl.BlockSpec(b_down.shape, lambda j, ri, re, mt: (0, 0))],
            out_specs=pl.BlockSpec(memory_space=pl.ANY),
            scratch_shapes=_expert_loop_scratch(f, d, f, BF16, d // 2, I32)),
        out_shape=jax.ShapeDtypeStruct((p, d // 2), I32),
        compiler_params=pltpu.CompilerParams(dimension_semantics=("arbitrary",), has_side_effects=True,
                                             vmem_limit_bytes=_vmem_limit(vmem)),
        name="moe_down",
    )(*sched, act, w_down, b_down)


def _combine_kernel(dest_ref, ys_ref, x1_ref, gate_ref, mods_ref, o_ref, buf, sem, *, tb, d, nt, n_tiles):
    i = pl.program_id(0)
    slot = lax.rem(i, 2)

    def row_copy(tile, sl, r, k):
        return pltpu.make_async_copy(ys_ref.at[pl.ds(dest_ref[(tile * tb + r) * TOP_K + k], 1)],
                                     buf.at[sl, k, pl.ds(r, 1)], sem.at[sl])

    def gather(tile, sl):
        def body(r, _):
            for k in range(TOP_K):
                row_copy(tile, sl, r, k).start()
            return 0
        lax.fori_loop(0, tb, body, 0, unroll=2)

    @pl.when(i == 0)
    def _():
        gather(0, 0)

    @pl.when(i + 1 < n_tiles)
    def _():
        gather(i + 1, 1 - slot)

    for k in range(TOP_K):
        pltpu.make_async_copy(ys_ref.at[pl.ds(0, tb)], buf.at[slot, k], sem.at[slot]).wait()
    half = d // 2
    gate = gate_ref[...]
    y_lo = jnp.zeros((tb, half), F32)
    y_hi = jnp.zeros((tb, half), F32)
    for k in range(TOP_K):
        w = buf[slot, k]
        gk = gate[:, k:k + 1]
        y_lo = y_lo + lax.bitcast_convert_type(lax.shift_left(w, 16), F32) * gk
        y_hi = y_hi + lax.bitcast_convert_type(w & jnp.int32(-65536), F32) * gk
    gt2 = mods_ref[pl.ds(i // nt, 1), 5 * d:6 * d]
    o_ref[:, 0:half] = x1_ref[:, 0:half] + gt2[:, 0:half] * y_lo
    o_ref[:, half:d] = x1_ref[:, half:d] + gt2[:, half:d] * y_hi


def _combine(dest_flat, ys, x1, gate, mods, *, tb, seq):
    t, d = x1.shape
    n_tiles = t // tb
    nt = seq // tb
    vmem = 2 * TOP_K * tb * d * 2 + 4 * tb * d * 4 + 6 * tb * d * 4
    return pl.pallas_call(
        functools.partial(_combine_kernel, tb=tb, d=d, nt=nt, n_tiles=n_tiles),
        grid_spec=pltpu.PrefetchScalarGridSpec(
            num_scalar_prefetch=1, grid=(n_tiles,),
            in_specs=[pl.BlockSpec(memory_space=pl.ANY),
                      pl.BlockSpec((tb, d), lambda i, dr: (i, 0)),
                      pl.BlockSpec((tb, TOP_K), lambda i, dr: (i, 0)),
                      pl.BlockSpec(mods.shape, lambda i, dr: (0, 0))],
            out_specs=pl.BlockSpec((tb, d), lambda i, dr: (i, 0)),
            scratch_shapes=[pltpu.VMEM((2, TOP_K, tb, d // 2), I32), pltpu.SemaphoreType.DMA((2,))]),
        out_shape=jax.ShapeDtypeStruct((t, d), F32),
        compiler_params=pltpu.CompilerParams(dimension_semantics=("arbitrary",),
                                             vmem_limit_bytes=_vmem_limit(vmem)),
        name="combine",
    )(dest_flat, ys, x1, gate, mods)


def _rope_table(s):
    rows = s // GRID_W
    row = jnp.repeat(jnp.arange(rows), GRID_W).astype(F32)
    col = jnp.tile(jnp.arange(GRID_W), rows).astype(F32)
    inv = ROPE_THETA ** (-jnp.arange(ROPE_FREQS, dtype=F32) / ROPE_FREQS)
    ang_r = row[:, None] * inv
    ang_c = col[:, None] * inv
    z = jnp.zeros_like(ang_r)
    cosf = jnp.concatenate([jnp.cos(ang_r)] * 2 + [jnp.cos(ang_c)] * 2, axis=1)
    sneg = jnp.concatenate([-jnp.sin(ang_r), z, -jnp.sin(ang_c), z], axis=1)
    spos = jnp.concatenate([z, jnp.sin(ang_r), z, jnp.sin(ang_c)], axis=1)
    return jnp.concatenate([cosf, sneg, spos], axis=1)


def kernel(x, c, ctx, c_ctx, w_mod, b_mod, g_norm1, w_in, g_q, g_k, conv_w, conv_b, w_gate_a, b_gate_a,
           w_gate_x, b_gate_x, lru_lambda, g_att_out, g_rec_out, w_out, g_norm2, w_router, b_router,
           w_gate_up, b_gate_up, w_down, b_down):
    b, s, d = x.shape
    cl = ctx.shape[1]
    t = b * s
    assert w_mod.shape[0] == 1, "single-layer kernel"
    assert b + 1 <= SUBLANES and d - ATT_WIDTH == REC_BLOCKS * LANES
    assert s % (SCAN_SEGMENTS * SUBLANES) == 0 and cl % (SCAN_SEGMENTS * SUBLANES) == 0

    ctx_row = b
    c8 = jnp.zeros((SUBLANES, d), F32).at[:b].set(c).at[ctx_row].set(c_ctx)
    mods = _mod(c8, w_mod[0], b_mod[0])

    w_in_bf = w_in[0].astype(BF16)
    tm = min(512, s)
    q, k, v, xr, yr = _inproj(x, mods, g_norm1[0], w_in_bf, _rope_table(s), g_q[0], g_k[0],
                              latent=True, ctx_row=ctx_row, tm=tm)
    kc, vc, xrc = _inproj(ctx, mods, g_norm1[0], w_in_bf, None, g_q[0], g_k[0],
                          latent=False, ctx_row=ctx_row, tm=cl)

    v_all = jnp.concatenate([vc, v], axis=1).reshape(b, cl + s, N_KV_HEADS, HEAD_DIM)
    v_ones = jnp.concatenate([v_all, jnp.ones_like(v_all)], axis=-1).reshape(b, cl + s, 2 * KV_WIDTH)
    att = _attention(q, jnp.concatenate([kc, k], axis=1), v_ones, tq=min(256, s))

    w_gates = jnp.concatenate([w_gate_a[0, 0], w_gate_x[0, 0], w_gate_a[0, 1], w_gate_x[0, 1]], axis=-1).astype(BF16)
    rw = d - ATT_WIDTH
    bias = lambda bb: bb.reshape(REC_BLOCKS, LANES)
    b_gates = jnp.concatenate([bias(b_gate_a[0, 0]), bias(b_gate_x[0, 0]), bias(b_gate_a[0, 1]), bias(b_gate_x[0, 1])],
                              axis=-1).reshape(1, 4 * rw)
    rec = _rglru(xr, xrc, yr, conv_w[0], conv_b[0].reshape(1, rw), w_gates, b_gates, lru_lambda[0])

    x1, h2p, top_idx, gate, mask = _merge(att, rec, x, mods, g_att_out[0], g_rec_out[0], w_out[0].astype(BF16),
                                          g_norm2[0], w_router[0], b_router[0], tm=tm)

    rank, counts = _rank(mask, top_idx, tb=MOE_BLOCK)
    counts = counts[0]
    padded = (counts + MOE_BLOCK - 1) // MOE_BLOCK * MOE_BLOCK
    pad_ends = jnp.cumsum(padded)
    pad_starts = pad_ends - padded
    dest = (pad_starts[top_idx] + rank).reshape(t * TOP_K)
    n_blocks = (t * TOP_K + N_EXPERTS * (MOE_BLOCK - 1) + MOE_BLOCK - 1) // MOE_BLOCK
    block_start = jnp.arange(n_blocks, dtype=I32) * MOE_BLOCK
    block_e = jnp.minimum(jnp.sum((pad_ends[None, :] <= block_start[:, None]).astype(I32), axis=1), N_EXPERTS - 1)
    n_valid = (pad_ends[N_EXPERTS - 1] // MOE_BLOCK).reshape(1).astype(I32)
    fill = jnp.concatenate([pad_starts + counts, pad_ends, n_valid]).astype(I32)

    xs = _dispatch(dest, fill, h2p, tb=MOE_BLOCK, n_blocks=n_blocks)

    used = padded > 0
    run_of_expert = jnp.cumsum(used.astype(I32)) - 1
    experts = jnp.arange(N_EXPERTS, dtype=I32)
    run_e = jnp.sum(jnp.where(used[None, :] & (run_of_expert[None, :] == experts[:, None]), experts[None, :], 0), axis=1)
    meta = jnp.concatenate([n_valid, jnp.sum(used.astype(I32)).reshape(1)])
    sched = (run_of_expert[block_e], run_e.astype(I32), meta)

    act = _moe_gu(sched, xs, w_gate_up[0], b_gate_up[0], tn=2048)
    ys = _moe_down(sched, act, w_down[0], b_down[0])

    return _combine(dest, ys, x1.reshape(t, d), gate, mods, tb=128, seq=s).reshape(b, s, d)
```

```python
import functools
import math

import jax
import jax.numpy as jnp
from jax import lax
from jax.experimental import pallas as pl
from jax.experimental.pallas import tpu as pltpu

F32 = jnp.float32
BF16 = jnp.bfloat16
I32 = jnp.int32

EPS = 1e-6
GRID_W = 64
HEAD_DIM = 128
N_Q_HEADS = 8
N_KV_HEADS = 2
GROUP = N_Q_HEADS // N_KV_HEADS
ATT_WIDTH = N_Q_HEADS * HEAD_DIM
KV_WIDTH = N_KV_HEADS * HEAD_DIM
ROPE_THETA = 10000.0
ROPE_FREQS = HEAD_DIM // 4
REC_BLOCKS = 8
CONV_W = 4
CONV_LEFT = 2
LRU_C = 8.0
N_EXPERTS = 32
TOP_K = 4
SWIGLU_LIMIT = 7.0
SWIGLU_ALPHA = 1.702
MOE_BLOCK = 512

V7X_VMEM_BYTES = 64 * 1024 * 1024
SUBLANES = 8
LANES = 128
SCAN_SEGMENTS = SUBLANES
CONV_PAD = SUBLANES

HIGHEST = lax.Precision.HIGHEST


def _vmem_limit(nbytes):
    return int(min(V7X_VMEM_BYTES - 4 * 1024 * 1024, max(nbytes, 16 * 1024 * 1024)))


def _rms(x, g):
    return x * lax.rsqrt(jnp.mean(x * x, axis=-1, keepdims=True) + EPS) * g


def _mod_kernel(c_ref, w_ref, b_ref, o_ref):
    c = c_ref[...]
    a = c * jax.nn.sigmoid(c)
    o_ref[...] = jnp.dot(a, w_ref[...], preferred_element_type=F32, precision=HIGHEST) + b_ref[...]


def _mod(c8, w_mod, b_mod):
    d, n = w_mod.shape
    tn = 1024
    return pl.pallas_call(
        _mod_kernel,
        grid=(n // tn,),
        in_specs=[pl.BlockSpec((SUBLANES, d), lambda j: (0, 0)),
                  pl.BlockSpec((d, tn), lambda j: (0, j)),
                  pl.BlockSpec((1, tn), lambda j: (0, j))],
        out_specs=pl.BlockSpec((SUBLANES, tn), lambda j: (0, j)),
        out_shape=jax.ShapeDtypeStruct((SUBLANES, n), F32),
        compiler_params=pltpu.CompilerParams(dimension_semantics=("arbitrary",),
                                             vmem_limit_bytes=_vmem_limit(3 * d * tn * 4)),
        name="mod",
    )(c8, w_mod, b_mod.reshape(1, n))


def _qk_norm_rope(y, g, rope):
    yn = _rms(y, g)
    if rope is None:
        return yn
    cosf, sneg, spos = rope
    return yn * cosf + pltpu.roll(yn, HEAD_DIM - ROPE_FREQS, 1) * sneg + pltpu.roll(yn, ROPE_FREQS, 1) * spos


def _inproj_kernel(*refs, d, latent, ctx_row):
    if latent:
        (x_ref, mods_ref, g1_ref, w_ref, rope_ref, gq_ref, gk_ref,
         q_ref, k_ref, v_ref, xr_ref, yr_ref) = refs
        row = pl.program_id(0)
    else:
        x_ref, mods_ref, g1_ref, w_ref, gk_ref, k_ref, v_ref, xr_ref = refs
        row = ctx_row
    sh = mods_ref[pl.ds(row, 1), 0:d]
    sc = mods_ref[pl.ds(row, 1), d:2 * d]
    h = _rms(x_ref[0], g1_ref[...]) * (1.0 + sc) + sh
    hb = h.astype(BF16)

    def proj(lo, hi):
        return jnp.dot(hb, w_ref[:, lo:hi], preferred_element_type=F32)

    o_k = ATT_WIDTH
    o_v = o_k + KV_WIDTH
    o_xr = o_v + KV_WIDTH
    rec_w = d - ATT_WIDTH
    o_yr = o_xr + rec_w
    rope = None
    if latent:
        rp = rope_ref[...]
        rope = (rp[:, 0:HEAD_DIM], rp[:, HEAD_DIM:2 * HEAD_DIM], rp[:, 2 * HEAD_DIM:3 * HEAD_DIM])
        q = proj(0, ATT_WIDTH)
        for hd in range(N_Q_HEADS):
            sl = slice(hd * HEAD_DIM, (hd + 1) * HEAD_DIM)
            q_ref[0, :, sl] = (_qk_norm_rope(q[:, sl], gq_ref[...], rope) * (HEAD_DIM ** -0.5)).astype(BF16)
    k = proj(o_k, o_v)
    for hd in range(N_KV_HEADS):
        sl = slice(hd * HEAD_DIM, (hd + 1) * HEAD_DIM)
        k_ref[0, :, sl] = _qk_norm_rope(k[:, sl], gk_ref[...], rope).astype(BF16)
    v_ref[0] = proj(o_v, o_xr).astype(BF16)
    xr_ref[0] = proj(o_xr, o_yr)
    if latent:
        yr_ref[0] = proj(o_yr, o_yr + rec_w)


def _inproj(x, mods, g1, w_in_bf, rope_tab, g_q, g_k, *, latent, ctx_row, tm):
    b, s, d = x.shape
    n = w_in_bf.shape[1]
    rec_w = d - ATT_WIDTH
    grid = (b, s // tm)
    row_spec = lambda w: pl.BlockSpec((1, tm, w), lambda bi, i: (bi, i, 0))
    full2 = lambda a: pl.BlockSpec(a.shape, lambda bi, i: (0, 0))
    in_specs = [row_spec(d), full2(mods), pl.BlockSpec((1, d), lambda bi, i: (0, 0)),
                pl.BlockSpec((d, n), lambda bi, i: (0, 0), pipeline_mode=pl.Buffered(1))]
    args = [x, mods, g1.reshape(1, d), w_in_bf]
    out_specs, out_shape = [], []
    if latent:
        in_specs += [pl.BlockSpec((tm, 3 * HEAD_DIM), lambda bi, i: (i, 0)),
                     pl.BlockSpec((1, HEAD_DIM), lambda bi, i: (0, 0))]
        args += [rope_tab, g_q.reshape(1, HEAD_DIM)]
        out_specs.append(row_spec(ATT_WIDTH))
        out_shape.append(jax.ShapeDtypeStruct((b, s, ATT_WIDTH), BF16))
    in_specs.append(pl.BlockSpec((1, HEAD_DIM), lambda bi, i: (0, 0)))
    args.append(g_k.reshape(1, HEAD_DIM))
    out_specs += [row_spec(KV_WIDTH), row_spec(KV_WIDTH), row_spec(rec_w)]
    out_shape += [jax.ShapeDtypeStruct((b, s, KV_WIDTH), BF16), jax.ShapeDtypeStruct((b, s, KV_WIDTH), BF16),
                  jax.ShapeDtypeStruct((b, s, rec_w), F32)]
    if latent:
        out_specs.append(row_spec(rec_w))
        out_shape.append(jax.ShapeDtypeStruct((b, s, rec_w), F32))
    vmem = d * n * 2 + 2 * tm * d * 4 + 2 * tm * n * 4 + 3 * tm * d * 4 + 2 * tm * n * 4
    return pl.pallas_call(
        functools.partial(_inproj_kernel, d=d, latent=latent, ctx_row=ctx_row),
        grid=grid, in_specs=in_specs, out_specs=out_specs, out_shape=out_shape,
        compiler_params=pltpu.CompilerParams(dimension_semantics=("arbitrary", "arbitrary"),
                                             vmem_limit_bytes=_vmem_limit(vmem)),
        name="inproj_latent" if latent else "inproj_ctx",
    )(*args)


def _attn_kernel(q_ref, k_ref, v_ref, o_ref):
    k = k_ref[0]
    v = v_ref[0]
    for g in range(GROUP):
        sl = slice(g * HEAD_DIM, (g + 1) * HEAD_DIM)
        s = lax.dot_general(q_ref[0, :, sl], k, (((1,), (1,)), ((), ())), preferred_element_type=F32)
        m = jnp.max(s, axis=-1, keepdims=True)
        p = jnp.exp(s - m).astype(BF16)
        o = jnp.dot(p, v, preferred_element_type=F32)
        o_ref[0, :, sl] = (o[:, 0:HEAD_DIM] / o[:, HEAD_DIM:2 * HEAD_DIM]).astype(BF16)


def _attention(q, k_all, v_all, *, tq):
    b, s, _ = q.shape
    lk = k_all.shape[1]
    gw = GROUP * HEAD_DIM
    vmem = 4 * lk * HEAD_DIM * 2 * 2 + 4 * tq * gw * 2 + 4 * tq * lk * 4
    return pl.pallas_call(
        _attn_kernel,
        grid=(b, N_KV_HEADS, s // tq),
        in_specs=[pl.BlockSpec((1, tq, gw), lambda bi, h, i: (bi, i, h)),
                  pl.BlockSpec((1, lk, HEAD_DIM), lambda bi, h, i: (bi, 0, h)),
                  pl.BlockSpec((1, lk, 2 * HEAD_DIM), lambda bi, h, i: (bi, 0, h))],
        out_specs=pl.BlockSpec((1, tq, gw), lambda bi, h, i: (bi, i, h)),
        out_shape=jax.ShapeDtypeStruct((b, s, ATT_WIDTH), BF16),
        compiler_params=pltpu.CompilerParams(dimension_semantics=("arbitrary",) * 3,
                                             vmem_limit_bytes=_vmem_limit(vmem)),
        name="attention",
    )(q, k_all, v_all)


def _gelu_tanh(x):
    return 0.5 * x * (1.0 + jnp.tanh(math.sqrt(2.0 / math.pi) * (x + 0.044715 * x * x * x)))


def _rglru_kernel(xr_ref, xc_ref, yr_ref, cw_ref, cb_ref, wg_ref, bg_ref, lam_ref, o_ref,
                  xp, xpc, af, bf, ab, bb, caf, cbf, cab, cbb, *, s, c):
    nseg = SCAN_SEGMENTS
    seg = s // nseg
    cseg = c // nseg
    zeros_pad = jnp.zeros((CONV_PAD, LANES), F32)
    xp[0:CONV_PAD, :] = zeros_pad
    xp[CONV_PAD + s:2 * CONV_PAD + s, :] = zeros_pad
    xp[CONV_PAD:CONV_PAD + s, :] = xr_ref[0]
    xpc[0:CONV_PAD, :] = zeros_pad
    xpc[CONV_PAD + c:2 * CONV_PAD + c, :] = zeros_pad
    xpc[CONV_PAD:CONV_PAD + c, :] = xc_ref[0]

    cw = cw_ref[...]
    cb = cb_ref[...]
    wg = wg_ref[0]
    bg = bg_ref[...]
    sp = jax.nn.softplus(-lam_ref[...])

    def coeffs(src, lo, n):
        u = cb
        for j in range(CONV_W):
            u = u + src[CONV_PAD + lo + j - CONV_LEFT:CONV_PAD + lo + j - CONV_LEFT + n, :] * cw[j:j + 1, :]
        g = jnp.dot(u.astype(BF16), wg, preferred_element_type=F32) + bg
        out = []
        for r in range(2):
            ga = g[:, (2 * r) * LANES:(2 * r + 1) * LANES]
            gx = g[:, (2 * r + 1) * LANES:(2 * r + 2) * LANES]
            log_a = (-LRU_C) * jax.nn.sigmoid(ga) * sp[r:r + 1, :]
            a = jnp.exp(log_a)
            mult = jnp.sqrt(-jnp.tanh(log_a) * (1.0 + a * a))
            out.append((a, mult * jax.nn.sigmoid(gx) * u))
        return out

    (a0, b0), (a1, b1) = coeffs(xpc, 0, c)
    for q in range(nseg):
        rows = slice(q * cseg, (q + 1) * cseg)
        dst = pl.ds(q, cseg, stride=nseg)
        caf[dst, :] = a0[rows]
        cbf[dst, :] = b0[rows]
        cab[dst, :] = a1[rows]
        cbb[dst, :] = b1[rows]
    for q in range(nseg):
        (a0, b0), (a1, b1) = coeffs(xp, q * seg, seg)
        dst = pl.ds(q, seg, stride=nseg)
        af[dst, :] = a0
        bf[dst, :] = b0
        ab[dst, :] = a1
        bb[dst, :] = b1

    def scan(a_f, b_f, a_b, b_b, n, store):
        def body(j, carry):
            hf, pf, hb, pb = carry
            rf = pl.multiple_of(j * nseg, nseg)
            rb = pl.multiple_of((n - 1 - j) * nseg, nseg)
            av = a_f[pl.ds(rf, nseg), :]
            hf = av * hf + b_f[pl.ds(rf, nseg), :]
            pf = av * pf
            aw = a_b[pl.ds(rb, nseg), :]
            hb = aw * hb + b_b[pl.ds(rb, nseg), :]
            pb = aw * pb
            if store:
                a_f[pl.ds(rf, nseg), :] = pf
                b_f[pl.ds(rf, nseg), :] = hf
                a_b[pl.ds(rb, nseg), :] = pb
                b_b[pl.ds(rb, nseg), :] = hb
            return hf, pf, hb, pb
        z = jnp.zeros((nseg, LANES), F32)
        o = jnp.ones((nseg, LANES), F32)
        return lax.fori_loop(0, n, body, (z, o, z, o), unroll=8)

    def chain(h_end, p_end, h0, reverse):
        order = range(nseg - 1, -1, -1) if reverse else range(nseg)
        enter = [None] * nseg
        cur = h0
        for q in order:
            enter[q] = cur
            cur = h_end[q:q + 1, :] + p_end[q:q + 1, :] * cur
        return enter, cur

    zero_row = jnp.zeros((1, LANES), F32)
    hf, pf, hb, pb = scan(caf, cbf, cab, cbb, cseg, False)
    _, h0f = chain(hf, pf, zero_row, False)
    _, h0b = chain(hb, pb, zero_row, True)
    hf, pf, hb, pb = scan(af, bf, ab, bb, seg, True)
    enter_f, _ = chain(hf, pf, h0f, False)
    enter_b, _ = chain(hb, pb, h0b, True)
    for q in range(nseg):
        src = pl.ds(q, seg, stride=nseg)
        h = bf[src, :] + af[src, :] * enter_f[q] + bb[src, :] + ab[src, :] * enter_b[q]
        rows = slice(q * seg, (q + 1) * seg)
        o_ref[0, rows, :] = (h * _gelu_tanh(yr_ref[0, rows, :])).astype(BF16)


def _rglru(xr, xrc, yr, conv_w, conv_b, w_gates, b_gates, lam):
    b, s, w = xr.shape
    c = xrc.shape[1]
    nb = w // LANES
    slab = lambda n: pl.BlockSpec((1, n, LANES), lambda bi, j: (bi, 0, j))
    scr = lambda n: pltpu.VMEM((n, LANES), F32)
    vmem = (3 * 2 + 5) * s * LANES * 4 + 8 * s * LANES * 4
    return pl.pallas_call(
        functools.partial(_rglru_kernel, s=s, c=c),
        grid=(b, nb),
        in_specs=[slab(s), slab(c), slab(s),
                  pl.BlockSpec((CONV_W, LANES), lambda bi, j: (0, j)),
                  pl.BlockSpec((1, LANES), lambda bi, j: (0, j)),
                  pl.BlockSpec((1, LANES, 4 * LANES), lambda bi, j: (j, 0, 0)),
                  pl.BlockSpec((1, 4 * LANES), lambda bi, j: (0, j)),
                  pl.BlockSpec((2, LANES), lambda bi, j: (0, j))],
        out_specs=slab(s),
        out_shape=jax.ShapeDtypeStruct((b, s, w), BF16),
        scratch_shapes=[scr(s + 2 * CONV_PAD), scr(c + 2 * CONV_PAD),
                        scr(s), scr(s), scr(s), scr(s), scr(c), scr(c), scr(c), scr(c)],
        compiler_params=pltpu.CompilerParams(dimension_semantics=("arbitrary", "arbitrary"),
                                             vmem_limit_bytes=_vmem_limit(vmem)),
        name="rglru",
    )(xr, xrc, yr, conv_w, conv_b, w_gates, b_gates, lam)


def _pack_bf16_pair(lo, hi):
    lo_bits = lax.bitcast_convert_type(lo.astype(BF16).astype(F32), I32)
    hi_bits = lax.bitcast_convert_type(hi.astype(BF16).astype(F32), I32)
    return lax.shift_right_logical(lo_bits, 16) | (hi_bits & jnp.int32(-65536))


def _unpack_bf16_pair(w):
    lo = lax.bitcast_convert_type(lax.shift_left(w, 16), F32).astype(BF16)
    hi = lax.bitcast_convert_type(w & jnp.int32(-65536), F32).astype(BF16)
    return lo, hi


def _merge_kernel(att_ref, rec_ref, x_ref, mods_ref, ga_ref, gr_ref, wo_ref, g2_ref, wr_ref, br_ref,
                  x1_ref, h2_ref, idx_ref, gate_ref, mask_ref, *, d):
    row = pl.program_id(0)
    gt1 = mods_ref[pl.ds(row, 1), 2 * d:3 * d]
    sh2 = mods_ref[pl.ds(row, 1), 3 * d:4 * d]
    sc2 = mods_ref[pl.ds(row, 1), 4 * d:5 * d]
    an = _rms(att_ref[0].astype(F32), ga_ref[...]).astype(BF16)
    rn = _rms(rec_ref[0].astype(F32), gr_ref[...]).astype(BF16)
    mix = (jnp.dot(an, wo_ref[0:ATT_WIDTH, :], preferred_element_type=F32)
           + jnp.dot(rn, wo_ref[ATT_WIDTH:d, :], preferred_element_type=F32))
    x1 = x_ref[0] + gt1 * mix
    x1_ref[0] = x1
    h2 = _rms(x1, g2_ref[...]) * (1.0 + sc2) + sh2
    half = d // 2
    h2_ref[...] = _pack_bf16_pair(h2[:, 0:half], h2[:, half:d])
    h_hi = h2.astype(BF16)
    h_lo = (h2 - h_hi.astype(F32)).astype(BF16)
    w_split = wr_ref[...]
    part = (jnp.dot(h_hi, w_split, preferred_element_type=F32)
            + jnp.dot(h_lo, w_split, preferred_element_type=F32))
    logits = part[:, 0:N_EXPERTS] + part[:, N_EXPERTS:2 * N_EXPERTS] + br_ref[...]
    tm = logits.shape[0]
    lane = lax.broadcasted_iota(I32, (tm, N_EXPERTS), 1).astype(F32)
    col = lax.broadcasted_iota(I32, (tm, TOP_K), 1)
    idx = jnp.zeros((tm, TOP_K), F32)
    ex = jnp.zeros((tm, TOP_K), F32)
    mask = jnp.zeros((tm, N_EXPERTS), F32)
    rest = logits
    top = None
    for k in range(TOP_K):
        m = jnp.max(rest, axis=-1, keepdims=True)
        first = jnp.min(jnp.where(rest == m, lane, float(N_EXPERTS)), axis=-1, keepdims=True)
        sel = lane == first
        if k == 0:
            top = m
        idx = jnp.where(col == k, first, idx)
        ex = jnp.where(col == k, jnp.exp(m - top), ex)
        mask = jnp.where(sel, 1.0, mask)
        rest = jnp.where(sel, -jnp.inf, rest)
    idx_ref[...] = idx.astype(I32)
    gate_ref[...] = ex / jnp.sum(ex, axis=-1, keepdims=True)
    mask_ref[...] = mask


def _merge(att, rec, x, mods, g_att, g_rec, w_out_bf, g2, w_router, b_router, *, tm):
    b, s, d = x.shape
    t = b * s
    nt = s // tm
    rec_w = d - ATT_WIDTH
    row3 = lambda w: pl.BlockSpec((1, tm, w), lambda bi, i: (bi, i, 0))
    tok2 = lambda w: pl.BlockSpec((tm, w), lambda bi, i: (bi * nt + i, 0))
    const = lambda shape, **kw: pl.BlockSpec(shape, lambda bi, i: (0,) * len(shape), **kw)
    vmem = d * d * 2 + 2 * tm * (ATT_WIDTH + rec_w) * 2 + 4 * tm * d * 4 + tm * d * 4 + 8 * tm * d * 4
    return pl.pallas_call(
        functools.partial(_merge_kernel, d=d),
        grid=(b, nt),
        in_specs=[row3(ATT_WIDTH), row3(rec_w), row3(d), const(mods.shape),
                  const((1, ATT_WIDTH)), const((1, rec_w)),
                  const((d, d), pipeline_mode=pl.Buffered(1)), const((1, d)),
                  const((d, 2 * N_EXPERTS)), const((1, N_EXPERTS))],
        out_specs=[row3(d), tok2(d // 2), tok2(TOP_K), tok2(TOP_K), tok2(N_EXPERTS)],
        out_shape=[jax.ShapeDtypeStruct((b, s, d), F32), jax.ShapeDtypeStruct((t, d // 2), I32),
                   jax.ShapeDtypeStruct((t, TOP_K), I32), jax.ShapeDtypeStruct((t, TOP_K), F32),
                   jax.ShapeDtypeStruct((t, N_EXPERTS), F32)],
        compiler_params=pltpu.CompilerParams(dimension_semantics=("arbitrary", "arbitrary"),
                                             vmem_limit_bytes=_vmem_limit(vmem)),
        name="merge",
    )(att, rec, x, mods, g_att.reshape(1, -1), g_rec.reshape(1, -1), w_out_bf, g2.reshape(1, d),
      _split_bf16(w_router), b_router.reshape(1, N_EXPERTS))


def _split_bf16(w):
    hi = w.astype(BF16)
    lo = (w - hi.astype(F32)).astype(BF16)
    return jnp.concatenate([hi, lo], axis=1)


def _rank_kernel(mask_ref, idx_ref, rank_ref, cnt_ref, carry):
    @pl.when(pl.program_id(0) == 0)
    def _():
        carry[...] = jnp.zeros_like(carry)

    m = mask_ref[...]
    tb = m.shape[0]
    r = lax.broadcasted_iota(I32, (tb, tb), 0)
    cidx = lax.broadcasted_iota(I32, (tb, tb), 1)
    tri = jnp.where(cidx < r, 1.0, 0.0).astype(BF16)
    before = jnp.dot(tri, m.astype(BF16), preferred_element_type=F32) + carry[...]
    lane = lax.broadcasted_iota(I32, (tb, N_EXPERTS), 1)
    col = lax.broadcasted_iota(I32, (tb, TOP_K), 1)
    idx = idx_ref[...]
    rank = jnp.zeros((tb, TOP_K), F32)
    for k in range(TOP_K):
        pick = jnp.sum(jnp.where(lane == idx[:, k:k + 1], before, 0.0), axis=-1, keepdims=True)
        rank = jnp.where(col == k, pick, rank)
    rank_ref[...] = rank.astype(I32)
    carry[...] = carry[...] + jnp.sum(m, axis=0, keepdims=True)
    cnt_ref[...] = carry[...].astype(I32)


def _rank(mask, idx, *, tb):
    t = mask.shape[0]
    return pl.pallas_call(
        _rank_kernel,
        grid=(t // tb,),
        in_specs=[pl.BlockSpec((tb, N_EXPERTS), lambda i: (i, 0)), pl.BlockSpec((tb, TOP_K), lambda i: (i, 0))],
        out_specs=[pl.BlockSpec((tb, TOP_K), lambda i: (i, 0)), pl.BlockSpec((1, N_EXPERTS), lambda i: (0, 0))],
        out_shape=[jax.ShapeDtypeStruct((t, TOP_K), I32), jax.ShapeDtypeStruct((1, N_EXPERTS), I32)],
        scratch_shapes=[pltpu.VMEM((1, N_EXPERTS), F32)],
        compiler_params=pltpu.CompilerParams(dimension_semantics=("arbitrary",)),
        name="rank",
    )(mask, idx)


def _dispatch_kernel(dest_ref, fill_ref, h2_ref, xs_ref, zeros, sem, fill_sem, *, tb, n_blocks):
    base = pl.program_id(0) * tb

    @pl.when(pl.program_id(0) == 0)
    def _():
        zeros[...] = jnp.zeros_like(zeros)

        def zero_row(r):
            return pltpu.make_async_copy(zeros.at[pl.ds(0, 1)], xs_ref.at[pl.ds(r, 1)], fill_sem)

        def zero_block(bk):
            rows = pl.ds(pl.multiple_of(bk * MOE_BLOCK, MOE_BLOCK), MOE_BLOCK)
            return pltpu.make_async_copy(zeros, xs_ref.at[rows], fill_sem)

        def for_all_fills(act):
            for e in range(N_EXPERTS):
                lax.fori_loop(fill_ref[e], fill_ref[N_EXPERTS + e], lambda r, c: (act(zero_row(r)), c)[1], 0)
            lax.fori_loop(fill_ref[2 * N_EXPERTS], n_blocks, lambda bk, c: (act(zero_block(bk)), c)[1], 0)

        for_all_fills(lambda cp: cp.start())
        for_all_fills(lambda cp: cp.wait())

    def row_copy(i, k):
        return pltpu.make_async_copy(h2_ref.at[pl.ds(i, 1)],
                                     xs_ref.at[pl.ds(dest_ref[(base + i) * TOP_K + k], 1)], sem)

    def body(i, _):
        for k in range(TOP_K):
            row_copy(i, k).start()
        return 0

    lax.fori_loop(0, tb, body, 0, unroll=4)
    for k in range(TOP_K):
        pltpu.make_async_copy(h2_ref, xs_ref.at[pl.ds(0, tb)], sem).wait()


def _dispatch(dest_flat, fill, h2p, *, tb, n_blocks):
    t, wd = h2p.shape
    return pl.pallas_call(
        functools.partial(_dispatch_kernel, tb=tb, n_blocks=n_blocks),
        grid_spec=pltpu.PrefetchScalarGridSpec(
            num_scalar_prefetch=2, grid=(t // tb,),
            in_specs=[pl.BlockSpec((tb, wd), lambda i, dr, fl: (i, 0))],
            out_specs=pl.BlockSpec(memory_space=pl.ANY),
            scratch_shapes=[pltpu.VMEM((MOE_BLOCK, wd), I32), pltpu.SemaphoreType.DMA(()),
                            pltpu.SemaphoreType.DMA(())]),
        out_shape=jax.ShapeDtypeStruct((n_blocks * MOE_BLOCK, wd), I32),
        compiler_params=pltpu.CompilerParams(dimension_semantics=("arbitrary",), has_side_effects=True),
        name="dispatch",
    )(dest_flat, fill, h2p)


PAIR_CHUNK = 512


def _last_real_block(s, meta):
    return jnp.maximum(jnp.minimum(s, meta[0] - 1), 0)


def _stream_expert_weights(ridx_ref, run_e_ref, meta_ref, w_hbm, wf32, wbf, sem):
    j = pl.program_id(0)
    s = pl.program_id(1)
    tn = wbf.shape[1]
    n_runs = meta_ref[1]
    r = ridx_ref[s]
    first = (s < meta_ref[0]) & ((s == 0) | (r != ridx_ref[jnp.maximum(s - 1, 0)]))

    def tile_copy(run, sweep):
        cols = pl.ds(pl.multiple_of(sweep * tn, tn), tn)
        return pltpu.make_async_copy(w_hbm.at[run_e_ref[run], :, cols], wf32, sem)

    @pl.when(first)
    def _():
        @pl.when((j == 0) & (r == 0))
        def _():
            tile_copy(0, 0).start()

        tile_copy(r, j).wait()
        wbf[...] = wf32[...].astype(BF16)
        more_runs = r + 1 < n_runs

        @pl.when(more_runs)
        def _():
            tile_copy(r + 1, j).start()

        @pl.when(jnp.logical_not(more_runs) & (j + 1 < pl.num_programs(0)))
        def _():
            tile_copy(0, j + 1).start()


def _expert_stream_scratch(k, tn):
    return [pltpu.VMEM((k, tn), F32), pltpu.VMEM((k, tn), BF16), pltpu.SemaphoreType.DMA(())]


def _gu_kernel(be_ref, ridx_ref, run_e_ref, meta_ref, xs_ref, w_hbm, b_ref, act_ref, wf32, wbf, sem):
    del be_ref
    _stream_expert_weights(ridx_ref, run_e_ref, meta_ref, w_hbm, wf32, wbf, sem)
    tn = wbf.shape[1]

    @pl.when(pl.program_id(1) < meta_ref[0])
    def _():
        lo, hi = _unpack_bf16_pair(xs_ref[...])
        xb = jnp.concatenate([lo, hi], axis=1)
        gu = jnp.dot(xb, wbf[...], preferred_element_type=F32) + b_ref[0]
        g = jnp.minimum(gu, SWIGLU_LIMIT)
        up1 = jnp.clip(gu, -SWIGLU_LIMIT, SWIGLU_LIMIT) + 1.0
        paired = (pltpu.roll(up1, tn - 1, 1) * (g * jax.nn.sigmoid(SWIGLU_ALPHA * g))).astype(BF16)
        r = lax.broadcasted_iota(I32, (PAIR_CHUNK, PAIR_CHUNK // 2), 0)
        c = lax.broadcasted_iota(I32, (PAIR_CHUNK, PAIR_CHUNK // 2), 1)
        sel = jnp.where(r == 2 * c, 1.0, 0.0).astype(BF16)
        for ch in range(tn // PAIR_CHUNK):
            act_ref[:, ch * (PAIR_CHUNK // 2):(ch + 1) * (PAIR_CHUNK // 2)] = jnp.dot(
                paired[:, ch * PAIR_CHUNK:(ch + 1) * PAIR_CHUNK], sel, preferred_element_type=F32).astype(BF16)

    @pl.when(pl.program_id(1) >= meta_ref[0])
    def _():
        act_ref[...] = jnp.zeros_like(act_ref)


def _moe_gu(sched, xs, w_gu, b_gu, *, tn):
    p, half = xs.shape
    d = 2 * half
    f2 = w_gu.shape[2]
    nj = f2 // tn
    nblk = p // MOE_BLOCK
    blk = _last_real_block
    vmem = d * tn * 4 + d * tn * 2 + 2 * MOE_BLOCK * half * 4 + 2 * MOE_BLOCK * tn + 8 * MOE_BLOCK * tn * 4
    return pl.pallas_call(
        _gu_kernel,
        grid_spec=pltpu.PrefetchScalarGridSpec(
            num_scalar_prefetch=4, grid=(nj, nblk),
            in_specs=[pl.BlockSpec((MOE_BLOCK, half), lambda j, s, be, ri, re, mt: (blk(s, mt), 0)),
                      pl.BlockSpec(memory_space=pl.ANY),
                      pl.BlockSpec((1, 1, tn), lambda j, s, be, ri, re, mt: (be[blk(s, mt)], 0, j))],
            out_specs=pl.BlockSpec((MOE_BLOCK, tn // 2), lambda j, s, be, ri, re, mt: (s, j)),
            scratch_shapes=_expert_stream_scratch(d, tn)),
        out_shape=jax.ShapeDtypeStruct((p, f2 // 2), BF16),
        compiler_params=pltpu.CompilerParams(dimension_semantics=("arbitrary", "arbitrary"),
                                             vmem_limit_bytes=_vmem_limit(vmem)),
        name="moe_gu",
    )(*sched, xs, w_gu, b_gu)


def _down_kernel(be_ref, ridx_ref, run_e_ref, meta_ref, act_ref, w_hbm, b_ref, y_ref, wf32, wbf, sem):
    del be_ref
    _stream_expert_weights(ridx_ref, run_e_ref, meta_ref, w_hbm, wf32, wbf, sem)

    @pl.when(pl.program_id(1) < meta_ref[0])
    def _():
        y = jnp.dot(act_ref[...], wbf[...], preferred_element_type=F32) + b_ref[0]
        half = y.shape[1] // 2
        y_ref[...] = _pack_bf16_pair(y[:, 0:half], y[:, half:2 * half])

    @pl.when(pl.program_id(1) >= meta_ref[0])
    def _():
        y_ref[...] = jnp.zeros_like(y_ref)


def _moe_down(sched, act, w_down, b_down):
    p, f = act.shape
    d = w_down.shape[2]
    nblk = p // MOE_BLOCK
    blk = _last_real_block
    vmem = f * d * 4 + f * d * 2 + 2 * MOE_BLOCK * f * 2 + 5 * MOE_BLOCK * d * 4
    return pl.pallas_call(
        _down_kernel,
        grid_spec=pltpu.PrefetchScalarGridSpec(
            num_scalar_prefetch=4, grid=(1, nblk),
            in_specs=[pl.BlockSpec((MOE_BLOCK, f), lambda j, s, be, ri, re, mt: (blk(s, mt), 0)),
                      pl.BlockSpec(memory_space=pl.ANY),
                      pl.BlockSpec((1, 1, d), lambda j, s, be, ri, re, mt: (be[blk(s, mt)], 0, 0))],
            out_specs=pl.BlockSpec((MOE_BLOCK, d // 2), lambda j, s, be, ri, re, mt: (s, 0)),
            scratch_shapes=_expert_stream_scratch(f, d)),
        out_shape=jax.ShapeDtypeStruct((p, d // 2), I32),
        compiler_params=pltpu.CompilerParams(dimension_semantics=("arbitrary", "arbitrary"),
                                             vmem_limit_bytes=_vmem_limit(vmem)),
        name="moe_down",
    )(*sched, act, w_down, b_down)


def _combine_kernel(dest_ref, ys_ref, x1_ref, gate_ref, mods_ref, o_ref, buf, sem, *, tb, d, nt, n_tiles):
    i = pl.program_id(0)
    slot = lax.rem(i, 2)

    def row_copy(tile, sl, r, k):
        return pltpu.make_async_copy(ys_ref.at[pl.ds(dest_ref[(tile * tb + r) * TOP_K + k], 1)],
                                     buf.at[sl, k, pl.ds(r, 1)], sem.at[sl])

    def gather(tile, sl):
        def body(r, _):
            for k in range(TOP_K):
                row_copy(tile, sl, r, k).start()
            return 0
        lax.fori_loop(0, tb, body, 0, unroll=2)

    @pl.when(i == 0)
    def _():
        gather(0, 0)

    @pl.when(i + 1 < n_tiles)
    def _():
        gather(i + 1, 1 - slot)

    for k in range(TOP_K):
        pltpu.make_async_copy(ys_ref.at[pl.ds(0, tb)], buf.at[slot, k], sem.at[slot]).wait()
    half = d // 2
    gate = gate_ref[...]
    y_lo = jnp.zeros((tb, half), F32)
    y_hi = jnp.zeros((tb, half), F32)
    for k in range(TOP_K):
        w = buf[slot, k]
        gk = gate[:, k:k + 1]
        y_lo = y_lo + lax.bitcast_convert_type(lax.shift_left(w, 16), F32) * gk
        y_hi = y_hi + lax.bitcast_convert_type(w & jnp.int32(-65536), F32) * gk
    gt2 = mods_ref[pl.ds(i // nt, 1), 5 * d:6 * d]
    o_ref[:, 0:half] = x1_ref[:, 0:half] + gt2[:, 0:half] * y_lo
    o_ref[:, half:d] = x1_ref[:, half:d] + gt2[:, half:d] * y_hi


def _combine(dest_flat, ys, x1, gate, mods, *, tb, seq):
    t, d = x1.shape
    n_tiles = t // tb
    nt = seq // tb
    vmem = 2 * TOP_K * tb * d * 2 + 4 * tb * d * 4 + 6 * tb * d * 4
    return pl.pallas_call(
        functools.partial(_combine_kernel, tb=tb, d=d, nt=nt, n_tiles=n_tiles),
        grid_spec=pltpu.PrefetchScalarGridSpec(
            num_scalar_prefetch=1, grid=(n_tiles,),
            in_specs=[pl.BlockSpec(memory_space=pl.ANY),
                      pl.BlockSpec((tb, d), lambda i, dr: (i, 0)),
                      pl.BlockSpec((tb, TOP_K), lambda i, dr: (i, 0)),
                      pl.BlockSpec(mods.shape, lambda i, dr: (0, 0))],
            out_specs=pl.BlockSpec((tb, d), lambda i, dr: (i, 0)),
            scratch_shapes=[pltpu.VMEM((2, TOP_K, tb, d // 2), I32), pltpu.SemaphoreType.DMA((2,))]),
        out_shape=jax.ShapeDtypeStruct((t, d), F32),
        compiler_params=pltpu.CompilerParams(dimension_semantics=("arbitrary",),
                                             vmem_limit_bytes=_vmem_limit(vmem)),
        name="combine",
    )(dest_flat, ys, x1, gate, mods)


def _rope_table(s):
    rows = s // GRID_W
    row = jnp.repeat(jnp.arange(rows), GRID_W).astype(F32)
    col = jnp.tile(jnp.arange(GRID_W), rows).astype(F32)
    inv = ROPE_THETA ** (-jnp.arange(ROPE_FREQS, dtype=F32) / ROPE_FREQS)
    ang_r = row[:, None] * inv
    ang_c = col[:, None] * inv
    z = jnp.zeros_like(ang_r)
    cosf = jnp.concatenate([jnp.cos(ang_r)] * 2 + [jnp.cos(ang_c)] * 2, axis=1)
    sneg = jnp.concatenate([-jnp.sin(ang_r), z, -jnp.sin(ang_c), z], axis=1)
    spos = jnp.concatenate([z, jnp.sin(ang_r), z, jnp.sin(ang_c)], axis=1)
    return jnp.concatenate([cosf, sneg, spos], axis=1)


def kernel(x, c, ctx, c_ctx, w_mod, b_mod, g_norm1, w_in, g_q, g_k, conv_w, conv_b, w_gate_a, b_gate_a,
           w_gate_x, b_gate_x, lru_lambda, g_att_out, g_rec_out, w_out, g_norm2, w_router, b_router,
           w_gate_up, b_gate_up, w_down, b_down):
    b, s, d = x.shape
    cl = ctx.shape[1]
    t = b * s
    assert w_mod.shape[0] == 1, "single-layer kernel"
    assert b + 1 <= SUBLANES and d - ATT_WIDTH == REC_BLOCKS * LANES
    assert s % (SCAN_SEGMENTS * SUBLANES) == 0 and cl % (SCAN_SEGMENTS * SUBLANES) == 0

    ctx_row = b
    c8 = jnp.zeros((SUBLANES, d), F32).at[:b].set(c).at[ctx_row].set(c_ctx)
    mods = _mod(c8, w_mod[0], b_mod[0])

    w_in_bf = w_in[0].astype(BF16)
    tm = min(512, s)
    q, k, v, xr, yr = _inproj(x, mods, g_norm1[0], w_in_bf, _rope_table(s), g_q[0], g_k[0],
                              latent=True, ctx_row=ctx_row, tm=tm)
    kc, vc, xrc = _inproj(ctx, mods, g_norm1[0], w_in_bf, None, g_q[0], g_k[0],
                          latent=False, ctx_row=ctx_row, tm=cl)

    v_all = jnp.concatenate([vc, v], axis=1).reshape(b, cl + s, N_KV_HEADS, HEAD_DIM)
    v_ones = jnp.concatenate([v_all, jnp.ones_like(v_all)], axis=-1).reshape(b, cl + s, 2 * KV_WIDTH)
    att = _attention(q, jnp.concatenate([kc, k], axis=1), v_ones, tq=min(256, s))

    w_gates = jnp.concatenate([w_gate_a[0, 0], w_gate_x[0, 0], w_gate_a[0, 1], w_gate_x[0, 1]], axis=-1).astype(BF16)
    rw = d - ATT_WIDTH
    bias = lambda bb: bb.reshape(REC_BLOCKS, LANES)
    b_gates = jnp.concatenate([bias(b_gate_a[0, 0]), bias(b_gate_x[0, 0]), bias(b_gate_a[0, 1]), bias(b_gate_x[0, 1])],
                              axis=-1).reshape(1, 4 * rw)
    rec = _rglru(xr, xrc, yr, conv_w[0], conv_b[0].reshape(1, rw), w_gates, b_gates, lru_lambda[0])

    x1, h2p, top_idx, gate, mask = _merge(att, rec, x, mods, g_att_out[0], g_rec_out[0], w_out[0].astype(BF16),
                                          g_norm2[0], w_router[0], b_router[0], tm=tm)

    rank, counts = _rank(mask, top_idx, tb=MOE_BLOCK)
    counts = counts[0]
    padded = (counts + MOE_BLOCK - 1) // MOE_BLOCK * MOE_BLOCK
    pad_ends = jnp.cumsum(padded)
    pad_starts = pad_ends - padded
    dest = (pad_starts[top_idx] + rank).reshape(t * TOP_K)
    n_blocks = (t * TOP_K + N_EXPERTS * (MOE_BLOCK - 1) + MOE_BLOCK - 1) // MOE_BLOCK
    block_start = jnp.arange(n_blocks, dtype=I32) * MOE_BLOCK
    block_e = jnp.minimum(jnp.sum((pad_ends[None, :] <= block_start[:, None]).astype(I32), axis=1), N_EXPERTS - 1)
    n_valid = (pad_ends[N_EXPERTS - 1] // MOE_BLOCK).reshape(1).astype(I32)
    fill = jnp.concatenate([pad_starts + counts, pad_ends, n_valid]).astype(I32)

    xs = _dispatch(dest, fill, h2p, tb=MOE_BLOCK, n_blocks=n_blocks)

    used = padded > 0
    run_of_expert = jnp.cumsum(used.astype(I32)) - 1
    experts = jnp.arange(N_EXPERTS, dtype=I32)
    run_e = jnp.sum(jnp.where(used[None, :] & (run_of_expert[None, :] == experts[:, None]), experts[None, :], 0), axis=1)
    meta = jnp.concatenate([n_valid, jnp.sum(used.astype(I32)).reshape(1)])
    sched = (block_e, run_of_expert[block_e], run_e.astype(I32), meta)

    f2 = w_gate_up.shape[3]
    act = _moe_gu(sched, xs, w_gate_up[0], b_gate_up[0].reshape(N_EXPERTS, 1, f2), tn=2048)
    ys = _moe_down(sched, act, w_down[0], b_down[0].reshape(N_EXPERTS, 1, d))

    return _combine(dest, ys, x1.reshape(t, d), gate, mods, tb=128, seq=s).reshape(b, s, d)
```

```python
import functools
import math

import jax
import jax.numpy as jnp
import numpy as np
from jax import lax
from jax.experimental import pallas as pl
from jax.experimental.pallas import tpu as pltpu

F32 = jnp.float32
BF16 = jnp.bfloat16
I32 = jnp.int32

EPS = 1e-6
GRID_W = 64
HEAD_DIM = 128
N_Q_HEADS = 8
N_KV_HEADS = 2
GROUP = N_Q_HEADS // N_KV_HEADS
ATT_WIDTH = N_Q_HEADS * HEAD_DIM
KV_WIDTH = N_KV_HEADS * HEAD_DIM
ROPE_THETA = 10000.0
ROPE_FREQS = HEAD_DIM // 4
REC_BLOCKS = 8
CONV_W = 4
CONV_LEFT = 2
LRU_C = 8.0
N_EXPERTS = 32
TOP_K = 4
SWIGLU_LIMIT = 7.0
SWIGLU_ALPHA = 1.702
MOE_BLOCK = 512

V7X_VMEM_BYTES = 64 * 1024 * 1024
SUBLANES = 8
LANES = 128
SCAN_SEGMENTS = SUBLANES
CONV_PAD = SUBLANES

HIGHEST = lax.Precision.HIGHEST


def _vmem_limit(nbytes):
    return int(min(V7X_VMEM_BYTES - 4 * 1024 * 1024, max(nbytes, 16 * 1024 * 1024)))


def _rms(x, g):
    return x * lax.rsqrt(jnp.mean(x * x, axis=-1, keepdims=True) + EPS) * g


def _mod_kernel(c_ref, w_ref, b_ref, o_ref):
    c = c_ref[...]
    a = c * jax.nn.sigmoid(c)
    o_ref[...] = jnp.dot(a, w_ref[...], preferred_element_type=F32, precision=HIGHEST) + b_ref[...]


def _mod(c8, w_mod, b_mod):
    d, n = w_mod.shape
    tn = 1024
    return pl.pallas_call(
        _mod_kernel,
        grid=(n // tn,),
        in_specs=[pl.BlockSpec((SUBLANES, d), lambda j: (0, 0)),
                  pl.BlockSpec((d, tn), lambda j: (0, j)),
                  pl.BlockSpec((1, tn), lambda j: (0, j))],
        out_specs=pl.BlockSpec((SUBLANES, tn), lambda j: (0, j)),
        out_shape=jax.ShapeDtypeStruct((SUBLANES, n), F32),
        compiler_params=pltpu.CompilerParams(dimension_semantics=("arbitrary",),
                                             vmem_limit_bytes=_vmem_limit(3 * d * tn * 4)),
        name="mod",
    )(c8, w_mod, b_mod.reshape(1, n))


def _qk_norm_rope(y, g, rope):
    yn = _rms(y, g)
    if rope is None:
        return yn
    cosf, sneg, spos = rope
    return yn * cosf + pltpu.roll(yn, HEAD_DIM - ROPE_FREQS, 1) * sneg + pltpu.roll(yn, ROPE_FREQS, 1) * spos


def _inproj_kernel(*refs, d, latent, ctx_row):
    if latent:
        (x_ref, mods_ref, g1_ref, w_ref, rope_ref, gq_ref, gk_ref,
         q_ref, k_ref, v_ref, xr_ref, yr_ref) = refs
        row = pl.program_id(0)
    else:
        x_ref, mods_ref, g1_ref, w_ref, gk_ref, k_ref, v_ref, xr_ref = refs
        row = ctx_row
    sh = mods_ref[pl.ds(row, 1), 0:d]
    sc = mods_ref[pl.ds(row, 1), d:2 * d]
    h = _rms(x_ref[0], g1_ref[...]) * (1.0 + sc) + sh
    hb = h.astype(BF16)

    def proj(lo, hi):
        return jnp.dot(hb, w_ref[:, lo:hi], preferred_element_type=F32)

    o_k = ATT_WIDTH
    o_v = o_k + KV_WIDTH
    o_xr = o_v + KV_WIDTH
    rec_w = d - ATT_WIDTH
    o_yr = o_xr + rec_w
    rope = None
    if latent:
        rp = rope_ref[...]
        rope = (rp[:, 0:HEAD_DIM], rp[:, HEAD_DIM:2 * HEAD_DIM], rp[:, 2 * HEAD_DIM:3 * HEAD_DIM])
        q = proj(0, ATT_WIDTH)
        for hd in range(N_Q_HEADS):
            sl = slice(hd * HEAD_DIM, (hd + 1) * HEAD_DIM)
            q_ref[0, :, sl] = (_qk_norm_rope(q[:, sl], gq_ref[...], rope) * (HEAD_DIM ** -0.5)).astype(BF16)
    k = proj(o_k, o_v)
    for hd in range(N_KV_HEADS):
        sl = slice(hd * HEAD_DIM, (hd + 1) * HEAD_DIM)
        k_ref[0, :, sl] = _qk_norm_rope(k[:, sl], gk_ref[...], rope).astype(BF16)
    v = proj(o_v, o_xr).astype(BF16)
    for hd in range(N_KV_HEADS):
        v_ref[0, :, 2 * hd * HEAD_DIM:(2 * hd + 1) * HEAD_DIM] = v[:, hd * HEAD_DIM:(hd + 1) * HEAD_DIM]
        v_ref[0, :, (2 * hd + 1) * HEAD_DIM:(2 * hd + 2) * HEAD_DIM] = jnp.ones((v.shape[0], HEAD_DIM), BF16)
    xr_ref[0] = proj(o_xr, o_yr)
    if latent:
        yr_ref[0] = proj(o_yr, o_yr + rec_w)


def _inproj(x, mods, g1, w_in_bf, rope_tab, g_q, g_k, *, latent, ctx_row, tm):
    b, s, d = x.shape
    n = w_in_bf.shape[1]
    rec_w = d - ATT_WIDTH
    grid = (b, s // tm)
    row_spec = lambda w: pl.BlockSpec((1, tm, w), lambda bi, i: (bi, i, 0))
    full2 = lambda a: pl.BlockSpec(a.shape, lambda bi, i: (0, 0))
    in_specs = [row_spec(d), full2(mods), pl.BlockSpec((1, d), lambda bi, i: (0, 0)),
                pl.BlockSpec((d, n), lambda bi, i: (0, 0), pipeline_mode=pl.Buffered(1))]
    args = [x, mods, g1.reshape(1, d), w_in_bf]
    out_specs, out_shape = [], []
    if latent:
        in_specs += [pl.BlockSpec((tm, 3 * HEAD_DIM), lambda bi, i: (i, 0)),
                     pl.BlockSpec((1, HEAD_DIM), lambda bi, i: (0, 0))]
        args += [rope_tab, g_q.reshape(1, HEAD_DIM)]
        out_specs.append(row_spec(ATT_WIDTH))
        out_shape.append(jax.ShapeDtypeStruct((b, s, ATT_WIDTH), BF16))
    in_specs.append(pl.BlockSpec((1, HEAD_DIM), lambda bi, i: (0, 0)))
    args.append(g_k.reshape(1, HEAD_DIM))
    out_specs += [row_spec(KV_WIDTH), row_spec(2 * KV_WIDTH), row_spec(rec_w)]
    out_shape += [jax.ShapeDtypeStruct((b, s, KV_WIDTH), BF16), jax.ShapeDtypeStruct((b, s, 2 * KV_WIDTH), BF16),
                  jax.ShapeDtypeStruct((b, s, rec_w), F32)]
    if latent:
        out_specs.append(row_spec(rec_w))
        out_shape.append(jax.ShapeDtypeStruct((b, s, rec_w), F32))
    vmem = d * n * 2 + 2 * tm * d * 4 + 2 * tm * n * 4 + 3 * tm * d * 4 + 2 * tm * n * 4
    return pl.pallas_call(
        functools.partial(_inproj_kernel, d=d, latent=latent, ctx_row=ctx_row),
        grid=grid, in_specs=in_specs, out_specs=out_specs, out_shape=out_shape,
        compiler_params=pltpu.CompilerParams(dimension_semantics=("arbitrary", "arbitrary"),
                                             vmem_limit_bytes=_vmem_limit(vmem)),
        name="inproj_latent" if latent else "inproj_ctx",
    )(*args)


def _attn_kernel(q_ref, kc_ref, k_ref, vc_ref, v_ref, o_ref, kbuf, vbuf):
    s_lat = k_ref.shape[1]

    @pl.when(pl.program_id(2) == 0)
    def _():
        kbuf[0:s_lat, :] = k_ref[0]
        kbuf[s_lat:, :] = kc_ref[0]
        vbuf[0:s_lat, :] = v_ref[0]
        vbuf[s_lat:, :] = vc_ref[0]

    k = kbuf[...]
    v = vbuf[...]
    for g in range(GROUP):
        sl = slice(g * HEAD_DIM, (g + 1) * HEAD_DIM)
        s = lax.dot_general(q_ref[0, :, sl], k, (((1,), (1,)), ((), ())), preferred_element_type=F32)
        m = jnp.max(s, axis=-1, keepdims=True)
        o = jnp.dot(jnp.exp(s - m).astype(BF16), v, preferred_element_type=F32)
        o_ref[0, :, sl] = (o[:, 0:HEAD_DIM] / o[:, HEAD_DIM:2 * HEAD_DIM]).astype(BF16)


def _attention(q, kc, k, vc, v, *, tq):
    b, s, _ = q.shape
    lk = s + kc.shape[1]
    gw = GROUP * HEAD_DIM
    vmem = 3 * lk * HEAD_DIM * 2 * 3 + 4 * tq * gw * 2 + 4 * tq * lk * 4
    kv = lambda a, w: pl.BlockSpec((1, a.shape[1], w), lambda bi, h, i: (bi, 0, h))
    return pl.pallas_call(
        _attn_kernel,
        grid=(b, N_KV_HEADS, s // tq),
        in_specs=[pl.BlockSpec((1, tq, gw), lambda bi, h, i: (bi, i, h)),
                  kv(kc, HEAD_DIM), kv(k, HEAD_DIM), kv(vc, 2 * HEAD_DIM), kv(v, 2 * HEAD_DIM)],
        out_specs=pl.BlockSpec((1, tq, gw), lambda bi, h, i: (bi, i, h)),
        out_shape=jax.ShapeDtypeStruct((b, s, ATT_WIDTH), BF16),
        scratch_shapes=[pltpu.VMEM((lk, HEAD_DIM), BF16), pltpu.VMEM((lk, 2 * HEAD_DIM), BF16)],
        compiler_params=pltpu.CompilerParams(dimension_semantics=("arbitrary",) * 3,
                                             vmem_limit_bytes=_vmem_limit(vmem)),
        name="attention",
    )(q, kc, k, vc, v)


def _gelu_tanh(x):
    return 0.5 * x * (1.0 + jnp.tanh(math.sqrt(2.0 / math.pi) * (x + 0.044715 * x * x * x)))


def _rglru_kernel(xr_ref, xc_ref, yr_ref, cw_ref, cb_ref, wg_ref, bg_ref, lam_ref, o_ref,
                  xp, xpc, af, bf, ab, bb, caf, cbf, cab, cbb, *, s, c):
    nseg = SCAN_SEGMENTS
    seg = s // nseg
    cseg = c // nseg
    zeros_pad = jnp.zeros((CONV_PAD, LANES), F32)
    xp[0:CONV_PAD, :] = zeros_pad
    xp[CONV_PAD + s:2 * CONV_PAD + s, :] = zeros_pad
    xp[CONV_PAD:CONV_PAD + s, :] = xr_ref[0]
    xpc[0:CONV_PAD, :] = zeros_pad
    xpc[CONV_PAD + c:2 * CONV_PAD + c, :] = zeros_pad
    xpc[CONV_PAD:CONV_PAD + c, :] = xc_ref[0]

    cw = cw_ref[...]
    cb = cb_ref[...]
    wg = wg_ref[0]
    bg = bg_ref[...]
    sp = jax.nn.softplus(-lam_ref[...])

    def coeffs(src, lo, n):
        u = cb
        for j in range(CONV_W):
            u = u + src[CONV_PAD + lo + j - CONV_LEFT:CONV_PAD + lo + j - CONV_LEFT + n, :] * cw[j:j + 1, :]
        g = jnp.dot(u.astype(BF16), wg, preferred_element_type=F32) + bg
        out = []
        for r in range(2):
            ga = g[:, (2 * r) * LANES:(2 * r + 1) * LANES]
            gx = g[:, (2 * r + 1) * LANES:(2 * r + 2) * LANES]
            log_a = (-LRU_C) * jax.nn.sigmoid(ga) * sp[r:r + 1, :]
            a = jnp.exp(log_a)
            mult = jnp.sqrt(-jnp.tanh(log_a) * (1.0 + a * a))
            out.append((a, mult * jax.nn.sigmoid(gx) * u))
        return out

    (a0, b0), (a1, b1) = coeffs(xpc, 0, c)
    for q in range(nseg):
        rows = slice(q * cseg, (q + 1) * cseg)
        dst = pl.ds(q, cseg, stride=nseg)
        caf[dst, :] = a0[rows]
        cbf[dst, :] = b0[rows]
        cab[dst, :] = a1[rows]
        cbb[dst, :] = b1[rows]
    for q in range(nseg):
        (a0, b0), (a1, b1) = coeffs(xp, q * seg, seg)
        dst = pl.ds(q, seg, stride=nseg)
        af[dst, :] = a0
        bf[dst, :] = b0
        ab[dst, :] = a1
        bb[dst, :] = b1

    def scan(a_f, b_f, a_b, b_b, n, store):
        def body(j, carry):
            hf, pf, hb, pb = carry
            rf = pl.multiple_of(j * nseg, nseg)
            rb = pl.multiple_of((n - 1 - j) * nseg, nseg)
            av = a_f[pl.ds(rf, nseg), :]
            hf = av * hf + b_f[pl.ds(rf, nseg), :]
            pf = av * pf
            aw = a_b[pl.ds(rb, nseg), :]
            hb = aw * hb + b_b[pl.ds(rb, nseg), :]
            pb = aw * pb
            if store:
                a_f[pl.ds(rf, nseg), :] = pf
                b_f[pl.ds(rf, nseg), :] = hf
                a_b[pl.ds(rb, nseg), :] = pb
                b_b[pl.ds(rb, nseg), :] = hb
            return hf, pf, hb, pb
        z = jnp.zeros((nseg, LANES), F32)
        o = jnp.ones((nseg, LANES), F32)
        return lax.fori_loop(0, n, body, (z, o, z, o), unroll=8)

    def chain(h_end, p_end, h0, reverse):
        order = range(nseg - 1, -1, -1) if reverse else range(nseg)
        enter = [None] * nseg
        cur = h0
        for q in order:
            enter[q] = cur
            cur = h_end[q:q + 1, :] + p_end[q:q + 1, :] * cur
        return enter, cur

    zero_row = jnp.zeros((1, LANES), F32)
    hf, pf, hb, pb = scan(caf, cbf, cab, cbb, cseg, False)
    _, h0f = chain(hf, pf, zero_row, False)
    _, h0b = chain(hb, pb, zero_row, True)
    hf, pf, hb, pb = scan(af, bf, ab, bb, seg, True)
    enter_f, _ = chain(hf, pf, h0f, False)
    enter_b, _ = chain(hb, pb, h0b, True)
    for q in range(nseg):
        src = pl.ds(q, seg, stride=nseg)
        h = bf[src, :] + af[src, :] * enter_f[q] + bb[src, :] + ab[src, :] * enter_b[q]
        rows = slice(q * seg, (q + 1) * seg)
        o_ref[0, rows, :] = (h * _gelu_tanh(yr_ref[0, rows, :])).astype(BF16)


def _rglru(xr, xrc, yr, conv_w, conv_b, w_gates, b_gates, lam):
    b, s, w = xr.shape
    c = xrc.shape[1]
    nb = w // LANES
    slab = lambda n: pl.BlockSpec((1, n, LANES), lambda bi, j: (bi, 0, j))
    scr = lambda n: pltpu.VMEM((n, LANES), F32)
    vmem = (3 * 2 + 5) * s * LANES * 4 + 8 * s * LANES * 4
    return pl.pallas_call(
        functools.partial(_rglru_kernel, s=s, c=c),
        grid=(b, nb),
        in_specs=[slab(s), slab(c), slab(s),
                  pl.BlockSpec((CONV_W, LANES), lambda bi, j: (0, j)),
                  pl.BlockSpec((1, LANES), lambda bi, j: (0, j)),
                  pl.BlockSpec((1, LANES, 4 * LANES), lambda bi, j: (j, 0, 0)),
                  pl.BlockSpec((1, 4 * LANES), lambda bi, j: (0, j)),
                  pl.BlockSpec((2, LANES), lambda bi, j: (0, j))],
        out_specs=slab(s),
        out_shape=jax.ShapeDtypeStruct((b, s, w), BF16),
        scratch_shapes=[scr(s + 2 * CONV_PAD), scr(c + 2 * CONV_PAD),
                        scr(s), scr(s), scr(s), scr(s), scr(c), scr(c), scr(c), scr(c)],
        compiler_params=pltpu.CompilerParams(dimension_semantics=("arbitrary", "arbitrary"),
                                             vmem_limit_bytes=_vmem_limit(vmem)),
        name="rglru",
    )(xr, xrc, yr, conv_w, conv_b, w_gates, b_gates, lam)


def _pack_bf16_pair(lo, hi):
    lo_bits = lax.bitcast_convert_type(lo.astype(BF16).astype(F32), I32)
    hi_bits = lax.bitcast_convert_type(hi.astype(BF16).astype(F32), I32)
    return lax.shift_right_logical(lo_bits, 16) | (hi_bits & jnp.int32(-65536))


def _unpack_bf16_pair(w):
    lo = lax.bitcast_convert_type(lax.shift_left(w, 16), F32).astype(BF16)
    hi = lax.bitcast_convert_type(w & jnp.int32(-65536), F32).astype(BF16)
    return lo, hi


def _merge_kernel(att_ref, rec_ref, x_ref, mods_ref, ga_ref, gr_ref, wo_ref, g2_ref, wr_ref, br_ref,
                  x1_ref, h2_ref, idx_ref, gate_ref, mask_ref, *, d):
    row = pl.program_id(0)
    gt1 = mods_ref[pl.ds(row, 1), 2 * d:3 * d]
    sh2 = mods_ref[pl.ds(row, 1), 3 * d:4 * d]
    sc2 = mods_ref[pl.ds(row, 1), 4 * d:5 * d]
    an = _rms(att_ref[0].astype(F32), ga_ref[...]).astype(BF16)
    rn = _rms(rec_ref[0].astype(F32), gr_ref[...]).astype(BF16)
    mix = (jnp.dot(an, wo_ref[0:ATT_WIDTH, :], preferred_element_type=F32)
           + jnp.dot(rn, wo_ref[ATT_WIDTH:d, :], preferred_element_type=F32))
    x1 = x_ref[0] + gt1 * mix
    x1_ref[0] = x1
    h2 = _rms(x1, g2_ref[...]) * (1.0 + sc2) + sh2
    half = d // 2
    h2_ref[...] = _pack_bf16_pair(h2[:, 0:half], h2[:, half:d])
    h_hi = h2.astype(BF16)
    h_lo = (h2 - h_hi.astype(F32)).astype(BF16)
    w_split = wr_ref[...]
    part = (jnp.dot(h_hi, w_split, preferred_element_type=F32)
            + jnp.dot(h_lo, w_split, preferred_element_type=F32))
    logits = part[:, 0:N_EXPERTS] + part[:, N_EXPERTS:2 * N_EXPERTS] + br_ref[...]
    tm = logits.shape[0]
    lane = lax.broadcasted_iota(I32, (tm, N_EXPERTS), 1).astype(F32)
    col = lax.broadcasted_iota(I32, (tm, TOP_K), 1)
    idx = jnp.zeros((tm, TOP_K), F32)
    ex = jnp.zeros((tm, TOP_K), F32)
    mask = jnp.zeros((tm, N_EXPERTS), F32)
    rest = logits
    top = None
    for k in range(TOP_K):
        m = jnp.max(rest, axis=-1, keepdims=True)
        first = jnp.min(jnp.where(rest == m, lane, float(N_EXPERTS)), axis=-1, keepdims=True)
        sel = lane == first
        if k == 0:
            top = m
        idx = jnp.where(col == k, first, idx)
        ex = jnp.where(col == k, jnp.exp(m - top), ex)
        mask = jnp.where(sel, 1.0, mask)
        rest = jnp.where(sel, -jnp.inf, rest)
    idx_ref[...] = idx.astype(I32)
    gate_ref[...] = ex / jnp.sum(ex, axis=-1, keepdims=True)
    mask_ref[...] = mask


def _merge(att, rec, x, mods, g_att, g_rec, w_out_bf, g2, w_router, b_router, *, tm):
    b, s, d = x.shape
    t = b * s
    nt = s // tm
    rec_w = d - ATT_WIDTH
    row3 = lambda w: pl.BlockSpec((1, tm, w), lambda bi, i: (bi, i, 0))
    tok2 = lambda w: pl.BlockSpec((tm, w), lambda bi, i: (bi * nt + i, 0))
    const = lambda shape, **kw: pl.BlockSpec(shape, lambda bi, i: (0,) * len(shape), **kw)
    vmem = d * d * 2 + 2 * tm * (ATT_WIDTH + rec_w) * 2 + 4 * tm * d * 4 + tm * d * 4 + 8 * tm * d * 4
    return pl.pallas_call(
        functools.partial(_merge_kernel, d=d),
        grid=(b, nt),
        in_specs=[row3(ATT_WIDTH), row3(rec_w), row3(d), const(mods.shape),
                  const((1, ATT_WIDTH)), const((1, rec_w)),
                  const((d, d), pipeline_mode=pl.Buffered(1)), const((1, d)),
                  const((d, 2 * N_EXPERTS)), const((1, N_EXPERTS))],
        out_specs=[row3(d), tok2(d // 2), tok2(TOP_K), tok2(TOP_K), tok2(N_EXPERTS)],
        out_shape=[jax.ShapeDtypeStruct((b, s, d), F32), jax.ShapeDtypeStruct((t, d // 2), I32),
                   jax.ShapeDtypeStruct((t, TOP_K), I32), jax.ShapeDtypeStruct((t, TOP_K), F32),
                   jax.ShapeDtypeStruct((t, N_EXPERTS), F32)],
        compiler_params=pltpu.CompilerParams(dimension_semantics=("arbitrary", "arbitrary"),
                                             vmem_limit_bytes=_vmem_limit(vmem)),
        name="merge",
    )(att, rec, x, mods, g_att.reshape(1, -1), g_rec.reshape(1, -1), w_out_bf, g2.reshape(1, d),
      _split_bf16(w_router), b_router.reshape(1, N_EXPERTS))


def _split_bf16(w):
    hi = w.astype(BF16)
    lo = (w - hi.astype(F32)).astype(BF16)
    return jnp.concatenate([hi, lo], axis=1)


def _rank_kernel(mask_ref, idx_ref, rank_ref, cnt_ref, carry):
    @pl.when(pl.program_id(0) == 0)
    def _():
        carry[...] = jnp.zeros_like(carry)

    m = mask_ref[...]
    tb = m.shape[0]
    r = lax.broadcasted_iota(I32, (tb, tb), 0)
    cidx = lax.broadcasted_iota(I32, (tb, tb), 1)
    tri = jnp.where(cidx < r, 1.0, 0.0).astype(BF16)
    before = jnp.dot(tri, m.astype(BF16), preferred_element_type=F32) + carry[...]
    lane = lax.broadcasted_iota(I32, (tb, N_EXPERTS), 1)
    col = lax.broadcasted_iota(I32, (tb, TOP_K), 1)
    idx = idx_ref[...]
    rank = jnp.zeros((tb, TOP_K), F32)
    for k in range(TOP_K):
        pick = jnp.sum(jnp.where(lane == idx[:, k:k + 1], before, 0.0), axis=-1, keepdims=True)
        rank = jnp.where(col == k, pick, rank)
    rank_ref[...] = rank.astype(I32)
    carry[...] = carry[...] + jnp.sum(m, axis=0, keepdims=True)
    cnt_ref[...] = carry[...].astype(I32)


def _rank(mask, idx, *, tb):
    t = mask.shape[0]
    return pl.pallas_call(
        _rank_kernel,
        grid=(t // tb,),
        in_specs=[pl.BlockSpec((tb, N_EXPERTS), lambda i: (i, 0)), pl.BlockSpec((tb, TOP_K), lambda i: (i, 0))],
        out_specs=[pl.BlockSpec((tb, TOP_K), lambda i: (i, 0)), pl.BlockSpec((1, N_EXPERTS), lambda i: (0, 0))],
        out_shape=[jax.ShapeDtypeStruct((t, TOP_K), I32), jax.ShapeDtypeStruct((1, N_EXPERTS), I32)],
        scratch_shapes=[pltpu.VMEM((1, N_EXPERTS), F32)],
        compiler_params=pltpu.CompilerParams(dimension_semantics=("arbitrary",)),
        name="rank",
    )(mask, idx)


def _dispatch_kernel(dest_ref, fill_ref, h2_ref, xs_ref, zeros, sem, fill_sem, *, tb, n_blocks):
    base = pl.program_id(0) * tb

    @pl.when(pl.program_id(0) == 0)
    def _():
        zeros[...] = jnp.zeros_like(zeros)

        def zero_row(r):
            return pltpu.make_async_copy(zeros.at[pl.ds(0, 1)], xs_ref.at[pl.ds(r, 1)], fill_sem)

        def zero_block(bk):
            rows = pl.ds(pl.multiple_of(bk * MOE_BLOCK, MOE_BLOCK), MOE_BLOCK)
            return pltpu.make_async_copy(zeros, xs_ref.at[rows], fill_sem)

        def for_all_fills(act):
            for e in range(N_EXPERTS):
                lax.fori_loop(fill_ref[e], fill_ref[N_EXPERTS + e], lambda r, c: (act(zero_row(r)), c)[1], 0)
            lax.fori_loop(fill_ref[2 * N_EXPERTS], n_blocks, lambda bk, c: (act(zero_block(bk)), c)[1], 0)

        for_all_fills(lambda cp: cp.start())
        for_all_fills(lambda cp: cp.wait())

    def row_copy(i, k):
        return pltpu.make_async_copy(h2_ref.at[pl.ds(i, 1)],
                                     xs_ref.at[pl.ds(dest_ref[(base + i) * TOP_K + k], 1)], sem)

    def body(i, _):
        for k in range(TOP_K):
            row_copy(i, k).start()
        return 0

    lax.fori_loop(0, tb, body, 0, unroll=4)
    for k in range(TOP_K):
        pltpu.make_async_copy(h2_ref, xs_ref.at[pl.ds(0, tb)], sem).wait()


def _dispatch(dest_flat, fill, h2p, *, tb, n_blocks):
    t, wd = h2p.shape
    return pl.pallas_call(
        functools.partial(_dispatch_kernel, tb=tb, n_blocks=n_blocks),
        grid_spec=pltpu.PrefetchScalarGridSpec(
            num_scalar_prefetch=2, grid=(t // tb,),
            in_specs=[pl.BlockSpec((tb, wd), lambda i, dr, fl: (i, 0))],
            out_specs=pl.BlockSpec(memory_space=pl.ANY),
            scratch_shapes=[pltpu.VMEM((MOE_BLOCK, wd), I32), pltpu.SemaphoreType.DMA(()),
                            pltpu.SemaphoreType.DMA(())]),
        out_shape=jax.ShapeDtypeStruct((n_blocks * MOE_BLOCK, wd), I32),
        compiler_params=pltpu.CompilerParams(dimension_semantics=("arbitrary",), has_side_effects=True),
        name="dispatch",
    )(dest_flat, fill, h2p)


PAIR_CHUNK = 512


def _last_real_block(s, meta):
    return jnp.maximum(jnp.minimum(s, meta[0] - 1), 0)


def _stream_expert_weights(ridx_ref, run_e_ref, meta_ref, w_hbm, wf32, wbf, sem):
    j = pl.program_id(0)
    s = pl.program_id(1)
    tn = wbf.shape[1]
    n_runs = meta_ref[1]
    r = ridx_ref[s]
    first = (s < meta_ref[0]) & ((s == 0) | (r != ridx_ref[jnp.maximum(s - 1, 0)]))

    def tile_copy(run, sweep):
        cols = pl.ds(pl.multiple_of(sweep * tn, tn), tn)
        return pltpu.make_async_copy(w_hbm.at[run_e_ref[run], :, cols], wf32, sem)

    @pl.when(first)
    def _():
        @pl.when((j == 0) & (r == 0))
        def _():
            tile_copy(0, 0).start()

        tile_copy(r, j).wait()
        wbf[...] = wf32[...].astype(BF16)
        more_runs = r + 1 < n_runs

        @pl.when(more_runs)
        def _():
            tile_copy(r + 1, j).start()

        @pl.when(jnp.logical_not(more_runs) & (j + 1 < pl.num_programs(0)))
        def _():
            tile_copy(0, j + 1).start()


def _expert_stream_scratch(k, tn):
    return [pltpu.VMEM((k, tn), F32), pltpu.VMEM((k, tn), BF16), pltpu.SemaphoreType.DMA(())]


def _gu_kernel(be_ref, ridx_ref, run_e_ref, meta_ref, xs_ref, w_hbm, b_ref, act_ref, wf32, wbf, sem):
    del be_ref
    _stream_expert_weights(ridx_ref, run_e_ref, meta_ref, w_hbm, wf32, wbf, sem)
    tn = wbf.shape[1]

    @pl.when(pl.program_id(1) < meta_ref[0])
    def _():
        lo, hi = _unpack_bf16_pair(xs_ref[...])
        xb = jnp.concatenate([lo, hi], axis=1)
        gu = jnp.dot(xb, wbf[...], preferred_element_type=F32) + b_ref[0]
        g = jnp.minimum(gu, SWIGLU_LIMIT)
        up1 = jnp.clip(gu, -SWIGLU_LIMIT, SWIGLU_LIMIT) + 1.0
        paired = (pltpu.roll(up1, tn - 1, 1) * (g * jax.nn.sigmoid(SWIGLU_ALPHA * g))).astype(BF16)
        r = lax.broadcasted_iota(I32, (PAIR_CHUNK, PAIR_CHUNK // 2), 0)
        c = lax.broadcasted_iota(I32, (PAIR_CHUNK, PAIR_CHUNK // 2), 1)
        sel = jnp.where(r == 2 * c, 1.0, 0.0).astype(BF16)
        for ch in range(tn // PAIR_CHUNK):
            act_ref[:, ch * (PAIR_CHUNK // 2):(ch + 1) * (PAIR_CHUNK // 2)] = jnp.dot(
                paired[:, ch * PAIR_CHUNK:(ch + 1) * PAIR_CHUNK], sel, preferred_element_type=F32).astype(BF16)

    @pl.when(pl.program_id(1) >= meta_ref[0])
    def _():
        act_ref[...] = jnp.zeros_like(act_ref)


def _moe_gu(sched, xs, w_gu, b_gu, *, tn):
    p, half = xs.shape
    d = 2 * half
    f2 = w_gu.shape[2]
    nj = f2 // tn
    nblk = p // MOE_BLOCK
    blk = _last_real_block
    vmem = d * tn * 4 + d * tn * 2 + 2 * MOE_BLOCK * half * 4 + 2 * MOE_BLOCK * tn + 8 * MOE_BLOCK * tn * 4
    return pl.pallas_call(
        _gu_kernel,
        grid_spec=pltpu.PrefetchScalarGridSpec(
            num_scalar_prefetch=4, grid=(nj, nblk),
            in_specs=[pl.BlockSpec((MOE_BLOCK, half), lambda j, s, be, ri, re, mt: (blk(s, mt), 0)),
                      pl.BlockSpec(memory_space=pl.ANY),
                      pl.BlockSpec((1, 1, tn), lambda j, s, be, ri, re, mt: (be[blk(s, mt)], 0, j))],
            out_specs=pl.BlockSpec((MOE_BLOCK, tn // 2), lambda j, s, be, ri, re, mt: (s, j)),
            scratch_shapes=_expert_stream_scratch(d, tn)),
        out_shape=jax.ShapeDtypeStruct((p, f2 // 2), BF16),
        compiler_params=pltpu.CompilerParams(dimension_semantics=("arbitrary", "arbitrary"),
                                             vmem_limit_bytes=_vmem_limit(vmem)),
        name="moe_gu",
    )(*sched, xs, w_gu, b_gu)


def _down_kernel(be_ref, ridx_ref, run_e_ref, meta_ref, act_ref, w_hbm, b_ref, y_ref, wf32, wbf, sem):
    del be_ref
    _stream_expert_weights(ridx_ref, run_e_ref, meta_ref, w_hbm, wf32, wbf, sem)

    @pl.when(pl.program_id(1) < meta_ref[0])
    def _():
        y = jnp.dot(act_ref[...], wbf[...], preferred_element_type=F32) + b_ref[0]
        half = y.shape[1] // 2
        y_ref[...] = _pack_bf16_pair(y[:, 0:half], y[:, half:2 * half])

    @pl.when(pl.program_id(1) >= meta_ref[0])
    def _():
        y_ref[...] = jnp.zeros_like(y_ref)


def _moe_down(sched, act, w_down, b_down):
    p, f = act.shape
    d = w_down.shape[2]
    nblk = p // MOE_BLOCK
    blk = _last_real_block
    vmem = f * d * 4 + f * d * 2 + 2 * MOE_BLOCK * f * 2 + 5 * MOE_BLOCK * d * 4
    return pl.pallas_call(
        _down_kernel,
        grid_spec=pltpu.PrefetchScalarGridSpec(
            num_scalar_prefetch=4, grid=(1, nblk),
            in_specs=[pl.BlockSpec((MOE_BLOCK, f), lambda j, s, be, ri, re, mt: (blk(s, mt), 0)),
                      pl.BlockSpec(memory_space=pl.ANY),
                      pl.BlockSpec((1, 1, d), lambda j, s, be, ri, re, mt: (be[blk(s, mt)], 0, 0))],
            out_specs=pl.BlockSpec((MOE_BLOCK, d // 2), lambda j, s, be, ri, re, mt: (s, 0)),
            scratch_shapes=_expert_stream_scratch(f, d)),
        out_shape=jax.ShapeDtypeStruct((p, d // 2), I32),
        compiler_params=pltpu.CompilerParams(dimension_semantics=("arbitrary", "arbitrary"),
                                             vmem_limit_bytes=_vmem_limit(vmem)),
        name="moe_down",
    )(*sched, act, w_down, b_down)


def _combine_kernel(dest_ref, ys_ref, x1_ref, gate_ref, mods_ref, o_ref, buf, sem, *, tb, d, nt, n_tiles):
    i = pl.program_id(0)
    slot = lax.rem(i, 2)

    def row_copy(tile, sl, r, k):
        return pltpu.make_async_copy(ys_ref.at[pl.ds(dest_ref[(tile * tb + r) * TOP_K + k], 1)],
                                     buf.at[sl, k, pl.ds(r, 1)], sem.at[sl])

    def gather(tile, sl):
        def body(r, _):
            for k in range(TOP_K):
                row_copy(tile, sl, r, k).start()
            return 0
        lax.fori_loop(0, tb, body, 0, unroll=4)

    @pl.when(i == 0)
    def _():
        gather(0, 0)

    @pl.when(i + 1 < n_tiles)
    def _():
        gather(i + 1, 1 - slot)

    for k in range(TOP_K):
        pltpu.make_async_copy(ys_ref.at[pl.ds(0, tb)], buf.at[slot, k], sem.at[slot]).wait()
    half = d // 2
    gate = gate_ref[...]
    y_lo = jnp.zeros((tb, half), F32)
    y_hi = jnp.zeros((tb, half), F32)
    for k in range(TOP_K):
        w = buf[slot, k]
        gk = gate[:, k:k + 1]
        y_lo = y_lo + lax.bitcast_convert_type(lax.shift_left(w, 16), F32) * gk
        y_hi = y_hi + lax.bitcast_convert_type(w & jnp.int32(-65536), F32) * gk
    gt2 = mods_ref[pl.ds(i // nt, 1), 5 * d:6 * d]
    o_ref[:, 0:half] = x1_ref[:, 0:half] + gt2[:, 0:half] * y_lo
    o_ref[:, half:d] = x1_ref[:, half:d] + gt2[:, half:d] * y_hi


def _combine(dest_flat, ys, x1, gate, mods, *, tb, seq):
    t, d = x1.shape
    n_tiles = t // tb
    nt = seq // tb
    vmem = 2 * TOP_K * tb * d * 2 + 4 * tb * d * 4 + 6 * tb * d * 4
    return pl.pallas_call(
        functools.partial(_combine_kernel, tb=tb, d=d, nt=nt, n_tiles=n_tiles),
        grid_spec=pltpu.PrefetchScalarGridSpec(
            num_scalar_prefetch=1, grid=(n_tiles,),
            in_specs=[pl.BlockSpec(memory_space=pl.ANY),
                      pl.BlockSpec((tb, d), lambda i, dr: (i, 0)),
                      pl.BlockSpec((tb, TOP_K), lambda i, dr: (i, 0)),
                      pl.BlockSpec(mods.shape, lambda i, dr: (0, 0))],
            out_specs=pl.BlockSpec((tb, d), lambda i, dr: (i, 0)),
            scratch_shapes=[pltpu.VMEM((2, TOP_K, tb, d // 2), I32), pltpu.SemaphoreType.DMA((2,))]),
        out_shape=jax.ShapeDtypeStruct((t, d), F32),
        compiler_params=pltpu.CompilerParams(dimension_semantics=("arbitrary",),
                                             vmem_limit_bytes=_vmem_limit(vmem)),
        name="combine",
    )(dest_flat, ys, x1, gate, mods)


def _rope_table(s):
    rows = s // GRID_W
    row = np.repeat(np.arange(rows), GRID_W).astype(np.float32)
    col = np.tile(np.arange(GRID_W), rows).astype(np.float32)
    inv = (np.float32(ROPE_THETA) ** (-np.arange(ROPE_FREQS, dtype=np.float32) / np.float32(ROPE_FREQS))).astype(np.float32)
    ang_r = row[:, None] * inv
    ang_c = col[:, None] * inv
    z = np.zeros_like(ang_r)
    cosf = np.concatenate([np.cos(ang_r)] * 2 + [np.cos(ang_c)] * 2, axis=1)
    sneg = np.concatenate([-np.sin(ang_r), z, -np.sin(ang_c), z], axis=1)
    spos = np.concatenate([z, np.sin(ang_r), z, np.sin(ang_c)], axis=1)
    return jnp.asarray(np.concatenate([cosf, sneg, spos], axis=1), dtype=F32)


def kernel(x, c, ctx, c_ctx, w_mod, b_mod, g_norm1, w_in, g_q, g_k, conv_w, conv_b, w_gate_a, b_gate_a,
           w_gate_x, b_gate_x, lru_lambda, g_att_out, g_rec_out, w_out, g_norm2, w_router, b_router,
           w_gate_up, b_gate_up, w_down, b_down):
    b, s, d = x.shape
    cl = ctx.shape[1]
    t = b * s
    assert w_mod.shape[0] == 1, "single-layer kernel"
    assert b + 1 <= SUBLANES and d - ATT_WIDTH == REC_BLOCKS * LANES
    assert s % (SCAN_SEGMENTS * SUBLANES) == 0 and cl % (SCAN_SEGMENTS * SUBLANES) == 0

    ctx_row = b
    c8 = jnp.zeros((SUBLANES, d), F32).at[:b].set(c).at[ctx_row].set(c_ctx)
    mods = _mod(c8, w_mod[0], b_mod[0])

    w_in_bf = w_in[0].astype(BF16)
    tm = min(512, s)
    q, k, v, xr, yr = _inproj(x, mods, g_norm1[0], w_in_bf, _rope_table(s), g_q[0], g_k[0],
                              latent=True, ctx_row=ctx_row, tm=tm)
    kc, vc, xrc = _inproj(ctx, mods, g_norm1[0], w_in_bf, None, g_q[0], g_k[0],
                          latent=False, ctx_row=ctx_row, tm=cl)

    att = _attention(q, kc, k, vc, v, tq=min(256, s))

    w_gates = jnp.concatenate([w_gate_a[0, 0], w_gate_x[0, 0], w_gate_a[0, 1], w_gate_x[0, 1]], axis=-1).astype(BF16)
    rw = d - ATT_WIDTH
    bias = lambda bb: bb.reshape(REC_BLOCKS, LANES)
    b_gates = jnp.concatenate([bias(b_gate_a[0, 0]), bias(b_gate_x[0, 0]), bias(b_gate_a[0, 1]), bias(b_gate_x[0, 1])],
                              axis=-1).reshape(1, 4 * rw)
    rec = _rglru(xr, xrc, yr, conv_w[0], conv_b[0].reshape(1, rw), w_gates, b_gates, lru_lambda[0])

    x1, h2p, top_idx, gate, mask = _merge(att, rec, x, mods, g_att_out[0], g_rec_out[0], w_out[0].astype(BF16),
                                          g_norm2[0], w_router[0], b_router[0], tm=tm)

    rank, counts = _rank(mask, top_idx, tb=MOE_BLOCK)
    counts = counts[0]
    padded = (counts + MOE_BLOCK - 1) // MOE_BLOCK * MOE_BLOCK
    pad_ends = jnp.cumsum(padded)
    pad_starts = pad_ends - padded
    dest = (pad_starts[top_idx] + rank).reshape(t * TOP_K)
    n_blocks = (t * TOP_K + N_EXPERTS * (MOE_BLOCK - 1) + MOE_BLOCK - 1) // MOE_BLOCK
    block_start = jnp.arange(n_blocks, dtype=I32) * MOE_BLOCK
    block_e = jnp.minimum(jnp.sum((pad_ends[None, :] <= block_start[:, None]).astype(I32), axis=1), N_EXPERTS - 1)
    n_valid = (pad_ends[N_EXPERTS - 1] // MOE_BLOCK).reshape(1).astype(I32)
    fill = jnp.concatenate([pad_starts + counts, pad_ends, n_valid]).astype(I32)

    xs = _dispatch(dest, fill, h2p, tb=MOE_BLOCK, n_blocks=n_blocks)

    used = padded > 0
    run_of_expert = jnp.cumsum(used.astype(I32)) - 1
    experts = jnp.arange(N_EXPERTS, dtype=I32)
    run_e = jnp.sum(jnp.where(used[None, :] & (run_of_expert[None, :] == experts[:, None]), experts[None, :], 0), axis=1)
    meta = jnp.concatenate([n_valid, jnp.sum(used.astype(I32)).reshape(1)])
    sched = (block_e, run_of_expert[block_e], run_e.astype(I32), meta)

    f2 = w_gate_up.shape[3]
    act = _moe_gu(sched, xs, w_gate_up[0], b_gate_up[0].reshape(N_EXPERTS, 1, f2), tn=2048)
    ys = _moe_down(sched, act, w_down[0], b_down[0].reshape(N_EXPERTS, 1, d))

    return _combine(dest, ys, x1.reshape(t, d), gate, mods, tb=256, seq=s).reshape(b, s, d)
```

```python
import functools
import math

import jax
import jax.numpy as jnp
import numpy as np
from jax import lax
from jax.experimental import pallas as pl
from jax.experimental.pallas import tpu as pltpu

F32 = jnp.float32
BF16 = jnp.bfloat16
I32 = jnp.int32

EPS = 1e-6
GRID_W = 64
HEAD_DIM = 128
N_Q_HEADS = 8
N_KV_HEADS = 2
GROUP = N_Q_HEADS // N_KV_HEADS
ATT_WIDTH = N_Q_HEADS * HEAD_DIM
KV_WIDTH = N_KV_HEADS * HEAD_DIM
ROPE_THETA = 10000.0
ROPE_FREQS = HEAD_DIM // 4
REC_BLOCKS = 8
CONV_W = 4
CONV_LEFT = 2
LRU_C = 8.0
N_EXPERTS = 32
TOP_K = 4
SWIGLU_LIMIT = 7.0
SWIGLU_ALPHA = 1.702
MOE_BLOCK = 512

V7X_VMEM_BYTES = 64 * 1024 * 1024
SUBLANES = 8
LANES = 128
SCAN_SEGMENTS = SUBLANES
CONV_PAD = SUBLANES

HIGHEST = lax.Precision.HIGHEST


def _vmem_limit(nbytes):
    return int(min(V7X_VMEM_BYTES - 4 * 1024 * 1024, max(nbytes, 16 * 1024 * 1024)))


def _rms(x, g):
    return x * lax.rsqrt(jnp.mean(x * x, axis=-1, keepdims=True) + EPS) * g


def _mod_kernel(c_ref, w_ref, b_ref, o_ref):
    c = c_ref[...]
    a = c * jax.nn.sigmoid(c)
    o_ref[...] = jnp.dot(a, w_ref[...], preferred_element_type=F32, precision=HIGHEST) + b_ref[...]


def _mod(c8, w_mod, b_mod):
    d, n = w_mod.shape
    tn = 1024
    return pl.pallas_call(
        _mod_kernel,
        grid=(n // tn,),
        in_specs=[pl.BlockSpec((SUBLANES, d), lambda j: (0, 0)),
                  pl.BlockSpec((d, tn), lambda j: (0, j)),
                  pl.BlockSpec((1, tn), lambda j: (0, j))],
        out_specs=pl.BlockSpec((SUBLANES, tn), lambda j: (0, j)),
        out_shape=jax.ShapeDtypeStruct((SUBLANES, n), F32),
        compiler_params=pltpu.CompilerParams(dimension_semantics=("arbitrary",),
                                             vmem_limit_bytes=_vmem_limit(3 * d * tn * 4)),
        name="mod",
    )(c8, w_mod, b_mod.reshape(1, n))


def _qk_norm_rope(y, g, rope):
    yn = _rms(y, g)
    if rope is None:
        return yn
    cosf, sneg, spos = rope
    return yn * cosf + pltpu.roll(yn, HEAD_DIM - ROPE_FREQS, 1) * sneg + pltpu.roll(yn, ROPE_FREQS, 1) * spos


def _inproj_kernel(*refs, d, latent, ctx_row):
    if latent:
        (x_ref, mods_ref, g1_ref, w_ref, rope_ref, gq_ref, gk_ref,
         q_ref, k_ref, v_ref, xr_ref, yr_ref) = refs
        row = pl.program_id(0)
    else:
        x_ref, mods_ref, g1_ref, w_ref, gk_ref, k_ref, v_ref, xr_ref = refs
        row = ctx_row
    sh = mods_ref[pl.ds(row, 1), 0:d]
    sc = mods_ref[pl.ds(row, 1), d:2 * d]
    h = _rms(x_ref[0], g1_ref[...]) * (1.0 + sc) + sh
    hb = h.astype(BF16)

    def proj(lo, hi):
        return jnp.dot(hb, w_ref[:, lo:hi], preferred_element_type=F32)

    o_k = ATT_WIDTH
    o_v = o_k + KV_WIDTH
    o_xr = o_v + KV_WIDTH
    rec_w = d - ATT_WIDTH
    o_yr = o_xr + rec_w
    rope = None
    if latent:
        rp = rope_ref[...]
        rope = (rp[:, 0:HEAD_DIM], rp[:, HEAD_DIM:2 * HEAD_DIM], rp[:, 2 * HEAD_DIM:3 * HEAD_DIM])
        q = proj(0, ATT_WIDTH)
        for hd in range(N_Q_HEADS):
            sl = slice(hd * HEAD_DIM, (hd + 1) * HEAD_DIM)
            q_ref[0, :, sl] = (_qk_norm_rope(q[:, sl], gq_ref[...], rope) * (HEAD_DIM ** -0.5)).astype(BF16)
    k = proj(o_k, o_v)
    for hd in range(N_KV_HEADS):
        sl = slice(hd * HEAD_DIM, (hd + 1) * HEAD_DIM)
        k_ref[0, :, sl] = _qk_norm_rope(k[:, sl], gk_ref[...], rope).astype(BF16)
    v = proj(o_v, o_xr).astype(BF16)
    for hd in range(N_KV_HEADS):
        v_ref[0, :, 2 * hd * HEAD_DIM:(2 * hd + 1) * HEAD_DIM] = v[:, hd * HEAD_DIM:(hd + 1) * HEAD_DIM]
        v_ref[0, :, (2 * hd + 1) * HEAD_DIM:(2 * hd + 2) * HEAD_DIM] = jnp.ones((v.shape[0], HEAD_DIM), BF16)
    xr_ref[0] = proj(o_xr, o_yr)
    if latent:
        yr_ref[0] = proj(o_yr, o_yr + rec_w)


def _inproj(x, mods, g1, w_in_bf, rope_tab, g_q, g_k, *, latent, ctx_row, tm):
    b, s, d = x.shape
    n = w_in_bf.shape[1]
    rec_w = d - ATT_WIDTH
    grid = (b, s // tm)
    row_spec = lambda w: pl.BlockSpec((1, tm, w), lambda bi, i: (bi, i, 0))
    full2 = lambda a: pl.BlockSpec(a.shape, lambda bi, i: (0, 0))
    in_specs = [row_spec(d), full2(mods), pl.BlockSpec((1, d), lambda bi, i: (0, 0)),
                pl.BlockSpec((d, n), lambda bi, i: (0, 0), pipeline_mode=pl.Buffered(1))]
    args = [x, mods, g1.reshape(1, d), w_in_bf]
    out_specs, out_shape = [], []
    if latent:
        in_specs += [pl.BlockSpec((tm, 3 * HEAD_DIM), lambda bi, i: (i, 0)),
                     pl.BlockSpec((1, HEAD_DIM), lambda bi, i: (0, 0))]
        args += [rope_tab, g_q.reshape(1, HEAD_DIM)]
        out_specs.append(row_spec(ATT_WIDTH))
        out_shape.append(jax.ShapeDtypeStruct((b, s, ATT_WIDTH), BF16))
    in_specs.append(pl.BlockSpec((1, HEAD_DIM), lambda bi, i: (0, 0)))
    args.append(g_k.reshape(1, HEAD_DIM))
    out_specs += [row_spec(KV_WIDTH), row_spec(2 * KV_WIDTH), row_spec(rec_w)]
    out_shape += [jax.ShapeDtypeStruct((b, s, KV_WIDTH), BF16), jax.ShapeDtypeStruct((b, s, 2 * KV_WIDTH), BF16),
                  jax.ShapeDtypeStruct((b, s, rec_w), F32)]
    if latent:
        out_specs.append(row_spec(rec_w))
        out_shape.append(jax.ShapeDtypeStruct((b, s, rec_w), F32))
    vmem = d * n * 2 + 2 * tm * d * 4 + 2 * tm * n * 4 + 3 * tm * d * 4 + 2 * tm * n * 4
    return pl.pallas_call(
        functools.partial(_inproj_kernel, d=d, latent=latent, ctx_row=ctx_row),
        grid=grid, in_specs=in_specs, out_specs=out_specs, out_shape=out_shape,
        compiler_params=pltpu.CompilerParams(dimension_semantics=("arbitrary", "arbitrary"),
                                             vmem_limit_bytes=_vmem_limit(vmem)),
        name="inproj_latent" if latent else "inproj_ctx",
    )(*args)


def _attn_kernel(q_ref, kc_ref, k_ref, vc_ref, v_ref, o_ref, kbuf, vbuf):
    s_lat = k_ref.shape[1]

    @pl.when(pl.program_id(2) == 0)
    def _():
        kbuf[0:s_lat, :] = k_ref[0]
        kbuf[s_lat:, :] = kc_ref[0]
        vbuf[0:s_lat, :] = v_ref[0]
        vbuf[s_lat:, :] = vc_ref[0]

    k = kbuf[...]
    v = vbuf[...]
    for g in range(GROUP):
        sl = slice(g * HEAD_DIM, (g + 1) * HEAD_DIM)
        s = lax.dot_general(q_ref[0, :, sl], k, (((1,), (1,)), ((), ())), preferred_element_type=F32)
        m = jnp.max(s, axis=-1, keepdims=True)
        o = jnp.dot(jnp.exp(s - m).astype(BF16), v, preferred_element_type=F32)
        o_ref[0, :, sl] = (o[:, 0:HEAD_DIM] / o[:, HEAD_DIM:2 * HEAD_DIM]).astype(BF16)


def _attention(q, kc, k, vc, v, *, tq):
    b, s, _ = q.shape
    lk = s + kc.shape[1]
    gw = GROUP * HEAD_DIM
    vmem = 3 * lk * HEAD_DIM * 2 * 3 + 4 * tq * gw * 2 + 4 * tq * lk * 4
    kv = lambda a, w: pl.BlockSpec((1, a.shape[1], w), lambda bi, h, i: (bi, 0, h))
    return pl.pallas_call(
        _attn_kernel,
        grid=(b, N_KV_HEADS, s // tq),
        in_specs=[pl.BlockSpec((1, tq, gw), lambda bi, h, i: (bi, i, h)),
                  kv(kc, HEAD_DIM), kv(k, HEAD_DIM), kv(vc, 2 * HEAD_DIM), kv(v, 2 * HEAD_DIM)],
        out_specs=pl.BlockSpec((1, tq, gw), lambda bi, h, i: (bi, i, h)),
        out_shape=jax.ShapeDtypeStruct((b, s, ATT_WIDTH), BF16),
        scratch_shapes=[pltpu.VMEM((lk, HEAD_DIM), BF16), pltpu.VMEM((lk, 2 * HEAD_DIM), BF16)],
        compiler_params=pltpu.CompilerParams(dimension_semantics=("arbitrary",) * 3,
                                             vmem_limit_bytes=_vmem_limit(vmem)),
        name="attention",
    )(q, kc, k, vc, v)


def _gelu_tanh(x):
    return 0.5 * x * (1.0 + jnp.tanh(math.sqrt(2.0 / math.pi) * (x + 0.044715 * x * x * x)))


def _rglru_kernel(xr_ref, xc_ref, yr_ref, cw_ref, cb_ref, wg_ref, bg_ref, lam_ref, o_ref,
                  xp, xpc, af, bf, ab, bb, caf, cbf, cab, cbb, *, s, c):
    nseg = SCAN_SEGMENTS
    seg = s // nseg
    cseg = c // nseg
    zeros_pad = jnp.zeros((CONV_PAD, LANES), F32)
    xp[0:CONV_PAD, :] = zeros_pad
    xp[CONV_PAD + s:2 * CONV_PAD + s, :] = zeros_pad
    xp[CONV_PAD:CONV_PAD + s, :] = xr_ref[0]
    xpc[0:CONV_PAD, :] = zeros_pad
    xpc[CONV_PAD + c:2 * CONV_PAD + c, :] = zeros_pad
    xpc[CONV_PAD:CONV_PAD + c, :] = xc_ref[0]

    cw = cw_ref[...]
    cb = cb_ref[...]
    wg = wg_ref[0]
    bg = bg_ref[...]
    sp = jax.nn.softplus(-lam_ref[...])

    def coeffs(src, lo, n):
        u = cb
        for j in range(CONV_W):
            u = u + src[CONV_PAD + lo + j - CONV_LEFT:CONV_PAD + lo + j - CONV_LEFT + n, :] * cw[j:j + 1, :]
        g = jnp.dot(u.astype(BF16), wg, preferred_element_type=F32) + bg
        out = []
        for r in range(2):
            ga = g[:, (2 * r) * LANES:(2 * r + 1) * LANES]
            gx = g[:, (2 * r + 1) * LANES:(2 * r + 2) * LANES]
            log_a = (-LRU_C) * jax.nn.sigmoid(ga) * sp[r:r + 1, :]
            a = jnp.exp(log_a)
            mult = jnp.sqrt(-jnp.tanh(log_a) * (1.0 + a * a))
            out.append((a, mult * jax.nn.sigmoid(gx) * u))
        return out

    (a0, b0), (a1, b1) = coeffs(xpc, 0, c)
    for q in range(nseg):
        rows = slice(q * cseg, (q + 1) * cseg)
        dst = pl.ds(q, cseg, stride=nseg)
        caf[dst, :] = a0[rows]
        cbf[dst, :] = b0[rows]
        cab[dst, :] = a1[rows]
        cbb[dst, :] = b1[rows]
    for q in range(nseg):
        (a0, b0), (a1, b1) = coeffs(xp, q * seg, seg)
        dst = pl.ds(q, seg, stride=nseg)
        af[dst, :] = a0
        bf[dst, :] = b0
        ab[dst, :] = a1
        bb[dst, :] = b1

    def scan(a_f, b_f, a_b, b_b, n, store):
        def body(j, carry):
            hf, pf, hb, pb = carry
            rf = pl.multiple_of(j * nseg, nseg)
            rb = pl.multiple_of((n - 1 - j) * nseg, nseg)
            av = a_f[pl.ds(rf, nseg), :]
            hf = av * hf + b_f[pl.ds(rf, nseg), :]
            pf = av * pf
            aw = a_b[pl.ds(rb, nseg), :]
            hb = aw * hb + b_b[pl.ds(rb, nseg), :]
            pb = aw * pb
            if store:
                a_f[pl.ds(rf, nseg), :] = pf
                b_f[pl.ds(rf, nseg), :] = hf
                a_b[pl.ds(rb, nseg), :] = pb
                b_b[pl.ds(rb, nseg), :] = hb
            return hf, pf, hb, pb
        z = jnp.zeros((nseg, LANES), F32)
        o = jnp.ones((nseg, LANES), F32)
        return lax.fori_loop(0, n, body, (z, o, z, o), unroll=8)

    def chain(h_end, p_end, h0, reverse):
        order = range(nseg - 1, -1, -1) if reverse else range(nseg)
        enter = [None] * nseg
        cur = h0
        for q in order:
            enter[q] = cur
            cur = h_end[q:q + 1, :] + p_end[q:q + 1, :] * cur
        return enter, cur

    zero_row = jnp.zeros((1, LANES), F32)
    hf, pf, hb, pb = scan(caf, cbf, cab, cbb, cseg, False)
    _, h0f = chain(hf, pf, zero_row, False)
    _, h0b = chain(hb, pb, zero_row, True)
    hf, pf, hb, pb = scan(af, bf, ab, bb, seg, True)
    enter_f, _ = chain(hf, pf, h0f, False)
    enter_b, _ = chain(hb, pb, h0b, True)
    for q in range(nseg):
        src = pl.ds(q, seg, stride=nseg)
        h = bf[src, :] + af[src, :] * enter_f[q] + bb[src, :] + ab[src, :] * enter_b[q]
        rows = slice(q * seg, (q + 1) * seg)
        o_ref[0, rows, :] = (h * _gelu_tanh(yr_ref[0, rows, :])).astype(BF16)


def _rglru(xr, xrc, yr, conv_w, conv_b, w_gates, b_gates, lam):
    b, s, w = xr.shape
    c = xrc.shape[1]
    nb = w // LANES
    slab = lambda n: pl.BlockSpec((1, n, LANES), lambda bi, j: (bi, 0, j))
    scr = lambda n: pltpu.VMEM((n, LANES), F32)
    vmem = (3 * 2 + 5) * s * LANES * 4 + 8 * s * LANES * 4
    return pl.pallas_call(
        functools.partial(_rglru_kernel, s=s, c=c),
        grid=(b, nb),
        in_specs=[slab(s), slab(c), slab(s),
                  pl.BlockSpec((CONV_W, LANES), lambda bi, j: (0, j)),
                  pl.BlockSpec((1, LANES), lambda bi, j: (0, j)),
                  pl.BlockSpec((1, LANES, 4 * LANES), lambda bi, j: (j, 0, 0)),
                  pl.BlockSpec((1, 4 * LANES), lambda bi, j: (0, j)),
                  pl.BlockSpec((2, LANES), lambda bi, j: (0, j))],
        out_specs=slab(s),
        out_shape=jax.ShapeDtypeStruct((b, s, w), BF16),
        scratch_shapes=[scr(s + 2 * CONV_PAD), scr(c + 2 * CONV_PAD),
                        scr(s), scr(s), scr(s), scr(s), scr(c), scr(c), scr(c), scr(c)],
        compiler_params=pltpu.CompilerParams(dimension_semantics=("arbitrary", "arbitrary"),
                                             vmem_limit_bytes=_vmem_limit(vmem)),
        name="rglru",
    )(xr, xrc, yr, conv_w, conv_b, w_gates, b_gates, lam)


def _pack_bf16_pair(lo, hi):
    lo_bits = lax.bitcast_convert_type(lo.astype(BF16).astype(F32), I32)
    hi_bits = lax.bitcast_convert_type(hi.astype(BF16).astype(F32), I32)
    return lax.shift_right_logical(lo_bits, 16) | (hi_bits & jnp.int32(-65536))


def _unpack_bf16_pair(w):
    lo = lax.bitcast_convert_type(lax.shift_left(w, 16), F32).astype(BF16)
    hi = lax.bitcast_convert_type(w & jnp.int32(-65536), F32).astype(BF16)
    return lo, hi


def _merge_kernel(att_ref, rec_ref, x_ref, mods_ref, ga_ref, gr_ref, wo_ref, g2_ref, wr_ref, br_ref,
                  x1_ref, h2_ref, idx_ref, gate_ref, mask_ref, *, d):
    row = pl.program_id(0)
    gt1 = mods_ref[pl.ds(row, 1), 2 * d:3 * d]
    sh2 = mods_ref[pl.ds(row, 1), 3 * d:4 * d]
    sc2 = mods_ref[pl.ds(row, 1), 4 * d:5 * d]
    an = _rms(att_ref[0].astype(F32), ga_ref[...]).astype(BF16)
    rn = _rms(rec_ref[0].astype(F32), gr_ref[...]).astype(BF16)
    mix = (jnp.dot(an, wo_ref[0:ATT_WIDTH, :], preferred_element_type=F32)
           + jnp.dot(rn, wo_ref[ATT_WIDTH:d, :], preferred_element_type=F32))
    x1 = x_ref[0] + gt1 * mix
    x1_ref[0] = x1
    h2 = _rms(x1, g2_ref[...]) * (1.0 + sc2) + sh2
    half = d // 2
    h2_ref[...] = _pack_bf16_pair(h2[:, 0:half], h2[:, half:d])
    h_hi = h2.astype(BF16)
    h_lo = (h2 - h_hi.astype(F32)).astype(BF16)
    w_split = wr_ref[...]
    part = (jnp.dot(h_hi, w_split, preferred_element_type=F32)
            + jnp.dot(h_lo, w_split, preferred_element_type=F32))
    logits = part[:, 0:N_EXPERTS] + part[:, N_EXPERTS:2 * N_EXPERTS] + br_ref[...]
    tm = logits.shape[0]
    lane = lax.broadcasted_iota(I32, (tm, N_EXPERTS), 1).astype(F32)
    col = lax.broadcasted_iota(I32, (tm, TOP_K), 1)
    idx = jnp.zeros((tm, TOP_K), F32)
    ex = jnp.zeros((tm, TOP_K), F32)
    mask = jnp.zeros((tm, N_EXPERTS), F32)
    rest = logits
    top = None
    for k in range(TOP_K):
        m = jnp.max(rest, axis=-1, keepdims=True)
        first = jnp.min(jnp.where(rest == m, lane, float(N_EXPERTS)), axis=-1, keepdims=True)
        sel = lane == first
        if k == 0:
            top = m
        idx = jnp.where(col == k, first, idx)
        ex = jnp.where(col == k, jnp.exp(m - top), ex)
        mask = jnp.where(sel, 1.0, mask)
        rest = jnp.where(sel, -jnp.inf, rest)
    idx_ref[...] = idx.astype(I32)
    gate_ref[...] = ex / jnp.sum(ex, axis=-1, keepdims=True)
    mask_ref[...] = mask


def _merge(att, rec, x, mods, g_att, g_rec, w_out_bf, g2, w_router, b_router, *, tm):
    b, s, d = x.shape
    t = b * s
    nt = s // tm
    rec_w = d - ATT_WIDTH
    row3 = lambda w: pl.BlockSpec((1, tm, w), lambda bi, i: (bi, i, 0))
    tok2 = lambda w: pl.BlockSpec((tm, w), lambda bi, i: (bi * nt + i, 0))
    const = lambda shape, **kw: pl.BlockSpec(shape, lambda bi, i: (0,) * len(shape), **kw)
    vmem = d * d * 2 + 2 * tm * (ATT_WIDTH + rec_w) * 2 + 4 * tm * d * 4 + tm * d * 4 + 8 * tm * d * 4
    return pl.pallas_call(
        functools.partial(_merge_kernel, d=d),
        grid=(b, nt),
        in_specs=[row3(ATT_WIDTH), row3(rec_w), row3(d), const(mods.shape),
                  const((1, ATT_WIDTH)), const((1, rec_w)),
                  const((d, d), pipeline_mode=pl.Buffered(1)), const((1, d)),
                  const((d, 2 * N_EXPERTS)), const((1, N_EXPERTS))],
        out_specs=[row3(d), tok2(d // 2), tok2(TOP_K), tok2(TOP_K), tok2(N_EXPERTS)],
        out_shape=[jax.ShapeDtypeStruct((b, s, d), F32), jax.ShapeDtypeStruct((t, d // 2), I32),
                   jax.ShapeDtypeStruct((t, TOP_K), I32), jax.ShapeDtypeStruct((t, TOP_K), F32),
                   jax.ShapeDtypeStruct((t, N_EXPERTS), F32)],
        compiler_params=pltpu.CompilerParams(dimension_semantics=("arbitrary", "arbitrary"),
                                             vmem_limit_bytes=_vmem_limit(vmem)),
        name="merge",
    )(att, rec, x, mods, g_att.reshape(1, -1), g_rec.reshape(1, -1), w_out_bf, g2.reshape(1, d),
      _split_bf16(w_router), b_router.reshape(1, N_EXPERTS))


def _split_bf16(w):
    hi = w.astype(BF16)
    lo = (w - hi.astype(F32)).astype(BF16)
    return jnp.concatenate([hi, lo], axis=1)


def _rank_kernel(mask_ref, idx_ref, rank_ref, cnt_ref, carry):
    @pl.when(pl.program_id(0) == 0)
    def _():
        carry[...] = jnp.zeros_like(carry)

    m = mask_ref[...]
    tb = m.shape[0]
    r = lax.broadcasted_iota(I32, (tb, tb), 0)
    cidx = lax.broadcasted_iota(I32, (tb, tb), 1)
    tri = jnp.where(cidx < r, 1.0, 0.0).astype(BF16)
    before = jnp.dot(tri, m.astype(BF16), preferred_element_type=F32) + carry[...]
    lane = lax.broadcasted_iota(I32, (tb, N_EXPERTS), 1)
    col = lax.broadcasted_iota(I32, (tb, TOP_K), 1)
    idx = idx_ref[...]
    rank = jnp.zeros((tb, TOP_K), F32)
    for k in range(TOP_K):
        pick = jnp.sum(jnp.where(lane == idx[:, k:k + 1], before, 0.0), axis=-1, keepdims=True)
        rank = jnp.where(col == k, pick, rank)
    rank_ref[...] = rank.astype(I32)
    carry[...] = carry[...] + jnp.sum(m, axis=0, keepdims=True)
    cnt_ref[...] = carry[...].astype(I32)


def _rank(mask, idx, *, tb):
    t = mask.shape[0]
    return pl.pallas_call(
        _rank_kernel,
        grid=(t // tb,),
        in_specs=[pl.BlockSpec((tb, N_EXPERTS), lambda i: (i, 0)), pl.BlockSpec((tb, TOP_K), lambda i: (i, 0))],
        out_specs=[pl.BlockSpec((tb, TOP_K), lambda i: (i, 0)), pl.BlockSpec((1, N_EXPERTS), lambda i: (0, 0))],
        out_shape=[jax.ShapeDtypeStruct((t, TOP_K), I32), jax.ShapeDtypeStruct((1, N_EXPERTS), I32)],
        scratch_shapes=[pltpu.VMEM((1, N_EXPERTS), F32)],
        compiler_params=pltpu.CompilerParams(dimension_semantics=("arbitrary",)),
        name="rank",
    )(mask, idx)


def _dispatch_kernel(dest_ref, fill_ref, h2_ref, xs_ref, zeros, sem, fill_sem, *, tb, n_blocks):
    base = pl.program_id(0) * tb

    @pl.when(pl.program_id(0) == 0)
    def _():
        zeros[...] = jnp.zeros_like(zeros)

        def zero_row(r):
            return pltpu.make_async_copy(zeros.at[pl.ds(0, 1)], xs_ref.at[pl.ds(r, 1)], fill_sem)

        def zero_block(bk):
            rows = pl.ds(pl.multiple_of(bk * MOE_BLOCK, MOE_BLOCK), MOE_BLOCK)
            return pltpu.make_async_copy(zeros, xs_ref.at[rows], fill_sem)

        def for_all_fills(act):
            for e in range(N_EXPERTS):
                lax.fori_loop(fill_ref[e], fill_ref[N_EXPERTS + e], lambda r, c: (act(zero_row(r)), c)[1], 0)
            lax.fori_loop(fill_ref[2 * N_EXPERTS], n_blocks, lambda bk, c: (act(zero_block(bk)), c)[1], 0)

        for_all_fills(lambda cp: cp.start())
        for_all_fills(lambda cp: cp.wait())

    def row_copy(i, k):
        return pltpu.make_async_copy(h2_ref.at[pl.ds(i, 1)],
                                     xs_ref.at[pl.ds(dest_ref[(base + i) * TOP_K + k], 1)], sem)

    def body(i, _):
        for k in range(TOP_K):
            row_copy(i, k).start(priority=k % 2)
        return 0

    lax.fori_loop(0, tb, body, 0, unroll=4)
    for k in range(TOP_K):
        pltpu.make_async_copy(h2_ref, xs_ref.at[pl.ds(0, tb)], sem).wait()


def _dispatch(dest_flat, fill, h2p, *, tb, n_blocks):
    t, wd = h2p.shape
    return pl.pallas_call(
        functools.partial(_dispatch_kernel, tb=tb, n_blocks=n_blocks),
        grid_spec=pltpu.PrefetchScalarGridSpec(
            num_scalar_prefetch=2, grid=(t // tb,),
            in_specs=[pl.BlockSpec((tb, wd), lambda i, dr, fl: (i, 0))],
            out_specs=pl.BlockSpec(memory_space=pl.ANY),
            scratch_shapes=[pltpu.VMEM((MOE_BLOCK, wd), I32), pltpu.SemaphoreType.DMA(()),
                            pltpu.SemaphoreType.DMA(())]),
        out_shape=jax.ShapeDtypeStruct((n_blocks * MOE_BLOCK, wd), I32),
        compiler_params=pltpu.CompilerParams(dimension_semantics=("arbitrary",), has_side_effects=True),
        name="dispatch",
    )(dest_flat, fill, h2p)


PAIR_CHUNK = 512


def _last_real_block(s, meta):
    return jnp.maximum(jnp.minimum(s, meta[0] - 1), 0)


def _stream_expert_weights(ridx_ref, run_e_ref, meta_ref, w_hbm, wf32, wbf, sem):
    j = pl.program_id(0)
    s = pl.program_id(1)
    tn = wbf.shape[1]
    n_runs = meta_ref[1]
    r = ridx_ref[s]
    first = (s < meta_ref[0]) & ((s == 0) | (r != ridx_ref[jnp.maximum(s - 1, 0)]))

    def tile_copy(run, sweep):
        cols = pl.ds(pl.multiple_of(sweep * tn, tn), tn)
        return pltpu.make_async_copy(w_hbm.at[run_e_ref[run], :, cols], wf32, sem)

    @pl.when(first)
    def _():
        @pl.when((j == 0) & (r == 0))
        def _():
            tile_copy(0, 0).start()

        tile_copy(r, j).wait()
        wbf[...] = wf32[...].astype(BF16)
        more_runs = r + 1 < n_runs

        @pl.when(more_runs)
        def _():
            tile_copy(r + 1, j).start()

        @pl.when(jnp.logical_not(more_runs) & (j + 1 < pl.num_programs(0)))
        def _():
            tile_copy(0, j + 1).start()


def _expert_stream_scratch(k, tn):
    return [pltpu.VMEM((k, tn), F32), pltpu.VMEM((k, tn), BF16), pltpu.SemaphoreType.DMA(())]


def _for_real_rows(rows_ref, meta_ref, out_ref, compute_rows):
    s = pl.program_id(1)
    real = s < meta_ref[0]
    half = MOE_BLOCK // 2
    upper_half_used = rows_ref[s] > half

    @pl.when(real & upper_half_used)
    def _():
        compute_rows(MOE_BLOCK)

    @pl.when(real & jnp.logical_not(upper_half_used))
    def _():
        compute_rows(half)
        out_ref[half:, :] = jnp.zeros((half, out_ref.shape[1]), out_ref.dtype)

    @pl.when(jnp.logical_not(real))
    def _():
        out_ref[...] = jnp.zeros_like(out_ref)


def _gu_kernel(be_ref, ridx_ref, run_e_ref, meta_ref, rows_ref, xs_ref, w_hbm, b_ref, act_ref, wf32, wbf, sem):
    del be_ref
    _stream_expert_weights(ridx_ref, run_e_ref, meta_ref, w_hbm, wf32, wbf, sem)
    tn = wbf.shape[1]

    def compute_rows(n):
        lo, hi = _unpack_bf16_pair(xs_ref[0:n, :])
        xb = jnp.concatenate([lo, hi], axis=1)
        gu = jnp.dot(xb, wbf[...], preferred_element_type=F32) + b_ref[0]
        g = jnp.minimum(gu, SWIGLU_LIMIT)
        up1 = jnp.clip(gu, -SWIGLU_LIMIT, SWIGLU_LIMIT) + 1.0
        paired = (pltpu.roll(up1, tn - 1, 1) * (g * jax.nn.sigmoid(SWIGLU_ALPHA * g))).astype(BF16)
        r = lax.broadcasted_iota(I32, (PAIR_CHUNK, PAIR_CHUNK // 2), 0)
        c = lax.broadcasted_iota(I32, (PAIR_CHUNK, PAIR_CHUNK // 2), 1)
        sel = jnp.where(r == 2 * c, 1.0, 0.0).astype(BF16)
        for ch in range(tn // PAIR_CHUNK):
            act_ref[0:n, ch * (PAIR_CHUNK // 2):(ch + 1) * (PAIR_CHUNK // 2)] = jnp.dot(
                paired[:, ch * PAIR_CHUNK:(ch + 1) * PAIR_CHUNK], sel, preferred_element_type=F32).astype(BF16)

    _for_real_rows(rows_ref, meta_ref, act_ref, compute_rows)


def _moe_gu(sched, xs, w_gu, b_gu, *, tn):
    p, half = xs.shape
    d = 2 * half
    f2 = w_gu.shape[2]
    nj = f2 // tn
    nblk = p // MOE_BLOCK
    blk = _last_real_block
    vmem = d * tn * 4 + d * tn * 2 + 2 * MOE_BLOCK * half * 4 + 2 * MOE_BLOCK * tn + 8 * MOE_BLOCK * tn * 4
    return pl.pallas_call(
        _gu_kernel,
        grid_spec=pltpu.PrefetchScalarGridSpec(
            num_scalar_prefetch=5, grid=(nj, nblk),
            in_specs=[pl.BlockSpec((MOE_BLOCK, half), lambda j, s, be, ri, re, mt, rw: (blk(s, mt), 0)),
                      pl.BlockSpec(memory_space=pl.ANY),
                      pl.BlockSpec((1, 1, tn), lambda j, s, be, ri, re, mt, rw: (be[blk(s, mt)], 0, j))],
            out_specs=pl.BlockSpec((MOE_BLOCK, tn // 2), lambda j, s, be, ri, re, mt, rw: (s, j)),
            scratch_shapes=_expert_stream_scratch(d, tn)),
        out_shape=jax.ShapeDtypeStruct((p, f2 // 2), BF16),
        compiler_params=pltpu.CompilerParams(dimension_semantics=("arbitrary", "arbitrary"),
                                             vmem_limit_bytes=_vmem_limit(vmem)),
        name="moe_gu",
    )(*sched, xs, w_gu, b_gu)


def _down_kernel(be_ref, ridx_ref, run_e_ref, meta_ref, rows_ref, act_ref, w_hbm, b_ref, y_ref, wf32, wbf, sem):
    del be_ref
    _stream_expert_weights(ridx_ref, run_e_ref, meta_ref, w_hbm, wf32, wbf, sem)

    def compute_rows(n):
        y = jnp.dot(act_ref[0:n, :], wbf[...], preferred_element_type=F32) + b_ref[0]
        half = y.shape[1] // 2
        y_ref[0:n, :] = _pack_bf16_pair(y[:, 0:half], y[:, half:2 * half])

    _for_real_rows(rows_ref, meta_ref, y_ref, compute_rows)


def _moe_down(sched, act, w_down, b_down):
    p, f = act.shape
    d = w_down.shape[2]
    nblk = p // MOE_BLOCK
    blk = _last_real_block
    vmem = f * d * 4 + f * d * 2 + 2 * MOE_BLOCK * f * 2 + 5 * MOE_BLOCK * d * 4
    return pl.pallas_call(
        _down_kernel,
        grid_spec=pltpu.PrefetchScalarGridSpec(
            num_scalar_prefetch=5, grid=(1, nblk),
            in_specs=[pl.BlockSpec((MOE_BLOCK, f), lambda j, s, be, ri, re, mt, rw: (blk(s, mt), 0)),
                      pl.BlockSpec(memory_space=pl.ANY),
                      pl.BlockSpec((1, 1, d), lambda j, s, be, ri, re, mt, rw: (be[blk(s, mt)], 0, 0))],
            out_specs=pl.BlockSpec((MOE_BLOCK, d // 2), lambda j, s, be, ri, re, mt, rw: (s, 0)),
            scratch_shapes=_expert_stream_scratch(f, d)),
        out_shape=jax.ShapeDtypeStruct((p, d // 2), I32),
        compiler_params=pltpu.CompilerParams(dimension_semantics=("arbitrary", "arbitrary"),
                                             vmem_limit_bytes=_vmem_limit(vmem)),
        name="moe_down",
    )(*sched, act, w_down, b_down)


def _combine_kernel(dest_ref, ys_ref, x1_ref, gate_ref, mods_ref, o_ref, buf, sem, *, tb, d, nt, n_tiles):
    i = pl.program_id(0)
    slot = lax.rem(i, 2)

    def row_copy(tile, sl, r, k):
        return pltpu.make_async_copy(ys_ref.at[pl.ds(dest_ref[(tile * tb + r) * TOP_K + k], 1)],
                                     buf.at[sl, k, pl.ds(r, 1)], sem.at[sl])

    def gather(tile, sl):
        def body(r, _):
            for k in range(TOP_K):
                row_copy(tile, sl, r, k).start(priority=k % 2)
            return 0
        lax.fori_loop(0, tb, body, 0, unroll=4)

    @pl.when(i == 0)
    def _():
        gather(0, 0)

    @pl.when(i + 1 < n_tiles)
    def _():
        gather(i + 1, 1 - slot)

    for k in range(TOP_K):
        pltpu.make_async_copy(ys_ref.at[pl.ds(0, tb)], buf.at[slot, k], sem.at[slot]).wait()
    half = d // 2
    gate = gate_ref[...]
    y_lo = jnp.zeros((tb, half), F32)
    y_hi = jnp.zeros((tb, half), F32)
    for k in range(TOP_K):
        w = buf[slot, k]
        gk = gate[:, k:k + 1]
        y_lo = y_lo + lax.bitcast_convert_type(lax.shift_left(w, 16), F32) * gk
        y_hi = y_hi + lax.bitcast_convert_type(w & jnp.int32(-65536), F32) * gk
    gt2 = mods_ref[pl.ds(i // nt, 1), 5 * d:6 * d]
    o_ref[:, 0:half] = x1_ref[:, 0:half] + gt2[:, 0:half] * y_lo
    o_ref[:, half:d] = x1_ref[:, half:d] + gt2[:, half:d] * y_hi


def _combine(dest_flat, ys, x1, gate, mods, *, tb, seq):
    t, d = x1.shape
    n_tiles = t // tb
    nt = seq // tb
    vmem = 2 * TOP_K * tb * d * 2 + 4 * tb * d * 4 + 6 * tb * d * 4
    return pl.pallas_call(
        functools.partial(_combine_kernel, tb=tb, d=d, nt=nt, n_tiles=n_tiles),
        grid_spec=pltpu.PrefetchScalarGridSpec(
            num_scalar_prefetch=1, grid=(n_tiles,),
            in_specs=[pl.BlockSpec(memory_space=pl.ANY),
                      pl.BlockSpec((tb, d), lambda i, dr: (i, 0)),
                      pl.BlockSpec((tb, TOP_K), lambda i, dr: (i, 0)),
                      pl.BlockSpec(mods.shape, lambda i, dr: (0, 0))],
            out_specs=pl.BlockSpec((tb, d), lambda i, dr: (i, 0)),
            scratch_shapes=[pltpu.VMEM((2, TOP_K, tb, d // 2), I32), pltpu.SemaphoreType.DMA((2,))]),
        out_shape=jax.ShapeDtypeStruct((t, d), F32),
        compiler_params=pltpu.CompilerParams(dimension_semantics=("arbitrary",),
                                             vmem_limit_bytes=_vmem_limit(vmem)),
        name="combine",
    )(dest_flat, ys, x1, gate, mods)


def _rope_table(s):
    rows = s // GRID_W
    row = np.repeat(np.arange(rows), GRID_W).astype(np.float32)
    col = np.tile(np.arange(GRID_W), rows).astype(np.float32)
    inv = (np.float32(ROPE_THETA) ** (-np.arange(ROPE_FREQS, dtype=np.float32) / np.float32(ROPE_FREQS))).astype(np.float32)
    ang_r = row[:, None] * inv
    ang_c = col[:, None] * inv
    z = np.zeros_like(ang_r)
    cosf = np.concatenate([np.cos(ang_r)] * 2 + [np.cos(ang_c)] * 2, axis=1)
    sneg = np.concatenate([-np.sin(ang_r), z, -np.sin(ang_c), z], axis=1)
    spos = np.concatenate([z, np.sin(ang_r), z, np.sin(ang_c)], axis=1)
    return jnp.asarray(np.concatenate([cosf, sneg, spos], axis=1), dtype=F32)


def kernel(x, c, ctx, c_ctx, w_mod, b_mod, g_norm1, w_in, g_q, g_k, conv_w, conv_b, w_gate_a, b_gate_a,
           w_gate_x, b_gate_x, lru_lambda, g_att_out, g_rec_out, w_out, g_norm2, w_router, b_router,
           w_gate_up, b_gate_up, w_down, b_down):
    b, s, d = x.shape
    cl = ctx.shape[1]
    t = b * s
    assert w_mod.shape[0] == 1, "single-layer kernel"
    assert b + 1 <= SUBLANES and d - ATT_WIDTH == REC_BLOCKS * LANES
    assert s % (SCAN_SEGMENTS * SUBLANES) == 0 and cl % (SCAN_SEGMENTS * SUBLANES) == 0

    ctx_row = b
    c8 = jnp.zeros((SUBLANES, d), F32).at[:b].set(c).at[ctx_row].set(c_ctx)
    mods = _mod(c8, w_mod[0], b_mod[0])

    w_in_bf = w_in[0].astype(BF16)
    tm = min(512, s)
    q, k, v, xr, yr = _inproj(x, mods, g_norm1[0], w_in_bf, _rope_table(s), g_q[0], g_k[0],
                              latent=True, ctx_row=ctx_row, tm=tm)
    kc, vc, xrc = _inproj(ctx, mods, g_norm1[0], w_in_bf, None, g_q[0], g_k[0],
                          latent=False, ctx_row=ctx_row, tm=cl)

    att = _attention(q, kc, k, vc, v, tq=min(256, s))

    w_gates = jnp.concatenate([w_gate_a[0, 0], w_gate_x[0, 0], w_gate_a[0, 1], w_gate_x[0, 1]], axis=-1).astype(BF16)
    rw = d - ATT_WIDTH
    bias = lambda bb: bb.reshape(REC_BLOCKS, LANES)
    b_gates = jnp.concatenate([bias(b_gate_a[0, 0]), bias(b_gate_x[0, 0]), bias(b_gate_a[0, 1]), bias(b_gate_x[0, 1])],
                              axis=-1).reshape(1, 4 * rw)
    rec = _rglru(xr, xrc, yr, conv_w[0], conv_b[0].reshape(1, rw), w_gates, b_gates, lru_lambda[0])

    x1, h2p, top_idx, gate, mask = _merge(att, rec, x, mods, g_att_out[0], g_rec_out[0], w_out[0].astype(BF16),
                                          g_norm2[0], w_router[0], b_router[0], tm=tm)

    rank, counts = _rank(mask, top_idx, tb=MOE_BLOCK)
    counts = counts[0]
    padded = (counts + MOE_BLOCK - 1) // MOE_BLOCK * MOE_BLOCK
    pad_ends = jnp.cumsum(padded)
    pad_starts = pad_ends - padded
    dest = (pad_starts[top_idx] + rank).reshape(t * TOP_K)
    n_blocks = (t * TOP_K + N_EXPERTS * (MOE_BLOCK - 1) + MOE_BLOCK - 1) // MOE_BLOCK
    block_start = jnp.arange(n_blocks, dtype=I32) * MOE_BLOCK
    block_e = jnp.minimum(jnp.sum((pad_ends[None, :] <= block_start[:, None]).astype(I32), axis=1), N_EXPERTS - 1)
    n_valid = (pad_ends[N_EXPERTS - 1] // MOE_BLOCK).reshape(1).astype(I32)
    fill = jnp.concatenate([pad_starts + counts, pad_ends, n_valid]).astype(I32)

    xs = _dispatch(dest, fill, h2p, tb=MOE_BLOCK, n_blocks=n_blocks)

    used = padded > 0
    run_of_expert = jnp.cumsum(used.astype(I32)) - 1
    experts = jnp.arange(N_EXPERTS, dtype=I32)
    run_e = jnp.sum(jnp.where(used[None, :] & (run_of_expert[None, :] == experts[:, None]), experts[None, :], 0), axis=1)
    meta = jnp.concatenate([n_valid, jnp.sum(used.astype(I32)).reshape(1)])
    block_rows = jnp.clip((pad_starts + counts)[block_e] - block_start, 0, MOE_BLOCK).astype(I32)
    sched = (block_e, run_of_expert[block_e], run_e.astype(I32), meta, block_rows)

    f2 = w_gate_up.shape[3]
    act = _moe_gu(sched, xs, w_gate_up[0], b_gate_up[0].reshape(N_EXPERTS, 1, f2), tn=2048)
    ys = _moe_down(sched, act, w_down[0], b_down[0].reshape(N_EXPERTS, 1, d))

    return _combine(dest, ys, x1.reshape(t, d), gate, mods, tb=256, seq=s).reshape(b, s, d)
```

```python
import functools
import math

import jax
import jax.numpy as jnp
import numpy as np
from jax import lax
from jax.experimental import pallas as pl
from jax.experimental.pallas import tpu as pltpu

F32 = jnp.float32
BF16 = jnp.bfloat16
I32 = jnp.int32

EPS = 1e-6
GRID_W = 64
HEAD_DIM = 128
N_Q_HEADS = 8
N_KV_HEADS = 2
GROUP = N_Q_HEADS // N_KV_HEADS
ATT_WIDTH = N_Q_HEADS * HEAD_DIM
KV_WIDTH = N_KV_HEADS * HEAD_DIM
ROPE_THETA = 10000.0
ROPE_FREQS = HEAD_DIM // 4
REC_BLOCKS = 8
CONV_W = 4
CONV_LEFT = 2
LRU_C = 8.0
N_EXPERTS = 32
TOP_K = 4
SWIGLU_LIMIT = 7.0
SWIGLU_ALPHA = 1.702
MOE_BLOCK = 512

V7X_VMEM_BYTES = 64 * 1024 * 1024
SUBLANES = 8
LANES = 128
SCAN_SEGMENTS = SUBLANES
CONV_PAD = SUBLANES

HIGHEST = lax.Precision.HIGHEST


def _vmem_limit(nbytes):
    return int(min(V7X_VMEM_BYTES - 4 * 1024 * 1024, max(nbytes, 16 * 1024 * 1024)))


def _rms(x, g):
    return x * lax.rsqrt(jnp.mean(x * x, axis=-1, keepdims=True) + EPS) * g


def _mod_kernel(c_ref, w_ref, b_ref, o_ref):
    c = c_ref[...]
    a = c * jax.nn.sigmoid(c)
    o_ref[...] = jnp.dot(a, w_ref[...], preferred_element_type=F32, precision=HIGHEST) + b_ref[...]


def _mod(c8, w_mod, b_mod):
    d, n = w_mod.shape
    tn = 2048
    return pl.pallas_call(
        _mod_kernel,
        grid=(n // tn,),
        in_specs=[pl.BlockSpec((SUBLANES, d), lambda j: (0, 0)),
                  pl.BlockSpec((d, tn), lambda j: (0, j)),
                  pl.BlockSpec((1, tn), lambda j: (0, j))],
        out_specs=pl.BlockSpec((SUBLANES, tn), lambda j: (0, j)),
        out_shape=jax.ShapeDtypeStruct((SUBLANES, n), F32),
        compiler_params=pltpu.CompilerParams(dimension_semantics=("arbitrary",),
                                             vmem_limit_bytes=_vmem_limit(3 * d * tn * 4)),
        name="mod",
    )(c8, w_mod, b_mod.reshape(1, n))


def _qk_norm_rope(y, g, rope):
    yn = _rms(y, g)
    if rope is None:
        return yn
    cosf, sneg, spos = rope
    return yn * cosf + pltpu.roll(yn, HEAD_DIM - ROPE_FREQS, 1) * sneg + pltpu.roll(yn, ROPE_FREQS, 1) * spos


def _inproj_kernel(*refs, d, latent, ctx_row):
    if latent:
        (x_ref, mods_ref, g1_ref, w_ref, rope_ref, gq_ref, gk_ref,
         q_ref, k_ref, v_ref, xr_ref, yr_ref) = refs
        row = pl.program_id(0)
    else:
        x_ref, mods_ref, g1_ref, w_ref, gk_ref, k_ref, v_ref, xr_ref = refs
        row = ctx_row
    sh = mods_ref[pl.ds(row, 1), 0:d]
    sc = mods_ref[pl.ds(row, 1), d:2 * d]
    h = _rms(x_ref[0], g1_ref[...]) * (1.0 + sc) + sh
    hb = h.astype(BF16)

    def proj(lo, hi):
        return jnp.dot(hb, w_ref[:, lo:hi], preferred_element_type=F32)

    o_k = ATT_WIDTH
    o_v = o_k + KV_WIDTH
    o_xr = o_v + KV_WIDTH
    rec_w = d - ATT_WIDTH
    o_yr = o_xr + rec_w
    rope = None
    if latent:
        rp = rope_ref[...]
        rope = (rp[:, 0:HEAD_DIM], rp[:, HEAD_DIM:2 * HEAD_DIM], rp[:, 2 * HEAD_DIM:3 * HEAD_DIM])
        q = proj(0, ATT_WIDTH)
        for hd in range(N_Q_HEADS):
            sl = slice(hd * HEAD_DIM, (hd + 1) * HEAD_DIM)
            q_ref[0, :, sl] = (_qk_norm_rope(q[:, sl], gq_ref[...], rope) * (HEAD_DIM ** -0.5)).astype(BF16)
    k = proj(o_k, o_v)
    for hd in range(N_KV_HEADS):
        sl = slice(hd * HEAD_DIM, (hd + 1) * HEAD_DIM)
        k_ref[0, :, sl] = _qk_norm_rope(k[:, sl], gk_ref[...], rope).astype(BF16)
    v = proj(o_v, o_xr).astype(BF16)
    for hd in range(N_KV_HEADS):
        v_ref[0, :, 2 * hd * HEAD_DIM:(2 * hd + 1) * HEAD_DIM] = v[:, hd * HEAD_DIM:(hd + 1) * HEAD_DIM]
        v_ref[0, :, (2 * hd + 1) * HEAD_DIM:(2 * hd + 2) * HEAD_DIM] = jnp.ones((v.shape[0], HEAD_DIM), BF16)
    xr_ref[0] = proj(o_xr, o_yr)
    if latent:
        yr_ref[0] = proj(o_yr, o_yr + rec_w)


def _inproj(x, mods, g1, w_in_bf, rope_tab, g_q, g_k, *, latent, ctx_row, tm):
    b, s, d = x.shape
    n = w_in_bf.shape[1]
    rec_w = d - ATT_WIDTH
    grid = (b, s // tm)
    row_spec = lambda w: pl.BlockSpec((1, tm, w), lambda bi, i: (bi, i, 0))
    full2 = lambda a: pl.BlockSpec(a.shape, lambda bi, i: (0, 0))
    in_specs = [row_spec(d), full2(mods), pl.BlockSpec((1, d), lambda bi, i: (0, 0)),
                pl.BlockSpec((d, n), lambda bi, i: (0, 0), pipeline_mode=pl.Buffered(1))]
    args = [x, mods, g1.reshape(1, d), w_in_bf]
    out_specs, out_shape = [], []
    if latent:
        in_specs += [pl.BlockSpec((tm, 3 * HEAD_DIM), lambda bi, i: (i, 0)),
                     pl.BlockSpec((1, HEAD_DIM), lambda bi, i: (0, 0))]
        args += [rope_tab, g_q.reshape(1, HEAD_DIM)]
        out_specs.append(row_spec(ATT_WIDTH))
        out_shape.append(jax.ShapeDtypeStruct((b, s, ATT_WIDTH), BF16))
    in_specs.append(pl.BlockSpec((1, HEAD_DIM), lambda bi, i: (0, 0)))
    args.append(g_k.reshape(1, HEAD_DIM))
    out_specs += [row_spec(KV_WIDTH), row_spec(2 * KV_WIDTH), row_spec(rec_w)]
    out_shape += [jax.ShapeDtypeStruct((b, s, KV_WIDTH), BF16), jax.ShapeDtypeStruct((b, s, 2 * KV_WIDTH), BF16),
                  jax.ShapeDtypeStruct((b, s, rec_w), F32)]
    if latent:
        out_specs.append(row_spec(rec_w))
        out_shape.append(jax.ShapeDtypeStruct((b, s, rec_w), F32))
    vmem = d * n * 2 + 2 * tm * d * 4 + 2 * tm * n * 4 + 3 * tm * d * 4 + 2 * tm * n * 4
    return pl.pallas_call(
        functools.partial(_inproj_kernel, d=d, latent=latent, ctx_row=ctx_row),
        grid=grid, in_specs=in_specs, out_specs=out_specs, out_shape=out_shape,
        compiler_params=pltpu.CompilerParams(dimension_semantics=("arbitrary", "arbitrary"),
                                             vmem_limit_bytes=_vmem_limit(vmem)),
        name="inproj_latent" if latent else "inproj_ctx",
    )(*args)


def _attn_kernel(q_ref, kc_ref, k_ref, vc_ref, v_ref, o_ref, kbuf, vbuf):
    s_lat = k_ref.shape[1]

    @pl.when(pl.program_id(2) == 0)
    def _():
        kbuf[0:s_lat, :] = k_ref[0]
        kbuf[s_lat:, :] = kc_ref[0]
        vbuf[0:s_lat, :] = v_ref[0]
        vbuf[s_lat:, :] = vc_ref[0]

    k = kbuf[...]
    v = vbuf[...]
    for g in range(GROUP):
        sl = slice(g * HEAD_DIM, (g + 1) * HEAD_DIM)
        s = lax.dot_general(q_ref[0, :, sl], k, (((1,), (1,)), ((), ())), preferred_element_type=F32)
        m = jnp.max(s, axis=-1, keepdims=True)
        o = jnp.dot(jnp.exp(s - m).astype(BF16), v, preferred_element_type=F32)
        o_ref[0, :, sl] = (o[:, 0:HEAD_DIM] / o[:, HEAD_DIM:2 * HEAD_DIM]).astype(BF16)


def _attention(q, kc, k, vc, v, *, tq):
    b, s, _ = q.shape
    lk = s + kc.shape[1]
    gw = GROUP * HEAD_DIM
    vmem = 3 * lk * HEAD_DIM * 2 * 3 + 4 * tq * gw * 2 + 4 * tq * lk * 4
    kv = lambda a, w: pl.BlockSpec((1, a.shape[1], w), lambda bi, h, i: (bi, 0, h))
    return pl.pallas_call(
        _attn_kernel,
        grid=(b, N_KV_HEADS, s // tq),
        in_specs=[pl.BlockSpec((1, tq, gw), lambda bi, h, i: (bi, i, h)),
                  kv(kc, HEAD_DIM), kv(k, HEAD_DIM), kv(vc, 2 * HEAD_DIM), kv(v, 2 * HEAD_DIM)],
        out_specs=pl.BlockSpec((1, tq, gw), lambda bi, h, i: (bi, i, h)),
        out_shape=jax.ShapeDtypeStruct((b, s, ATT_WIDTH), BF16),
        scratch_shapes=[pltpu.VMEM((lk, HEAD_DIM), BF16), pltpu.VMEM((lk, 2 * HEAD_DIM), BF16)],
        compiler_params=pltpu.CompilerParams(dimension_semantics=("arbitrary",) * 3,
                                             vmem_limit_bytes=_vmem_limit(vmem)),
        name="attention",
    )(q, kc, k, vc, v)


def _gelu_tanh(x):
    return 0.5 * x * (1.0 + jnp.tanh(math.sqrt(2.0 / math.pi) * (x + 0.044715 * x * x * x)))


def _rglru_kernel(xr_ref, xc_ref, yr_ref, cw_ref, cb_ref, wg_ref, bg_ref, lam_ref, o_ref,
                  xp, xpc, af, bf, ab, bb, caf, cbf, cab, cbb, *, s, c):
    nseg = SCAN_SEGMENTS
    seg = s // nseg
    cseg = c // nseg
    zeros_pad = jnp.zeros((CONV_PAD, LANES), F32)
    xp[0:CONV_PAD, :] = zeros_pad
    xp[CONV_PAD + s:2 * CONV_PAD + s, :] = zeros_pad
    xp[CONV_PAD:CONV_PAD + s, :] = xr_ref[0]
    xpc[0:CONV_PAD, :] = zeros_pad
    xpc[CONV_PAD + c:2 * CONV_PAD + c, :] = zeros_pad
    xpc[CONV_PAD:CONV_PAD + c, :] = xc_ref[0]

    cw = cw_ref[...]
    cb = cb_ref[...]
    wg = wg_ref[0]
    bg = bg_ref[...]
    sp = jax.nn.softplus(-lam_ref[...])

    def coeffs(src, lo, n):
        u = cb
        for j in range(CONV_W):
            u = u + src[CONV_PAD + lo + j - CONV_LEFT:CONV_PAD + lo + j - CONV_LEFT + n, :] * cw[j:j + 1, :]
        g = jnp.dot(u.astype(BF16), wg, preferred_element_type=F32) + bg
        out = []
        for r in range(2):
            ga = g[:, (2 * r) * LANES:(2 * r + 1) * LANES]
            gx = g[:, (2 * r + 1) * LANES:(2 * r + 2) * LANES]
            log_a = (-LRU_C) * jax.nn.sigmoid(ga) * sp[r:r + 1, :]
            a = jnp.exp(log_a)
            mult = jnp.sqrt(-jnp.tanh(log_a) * (1.0 + a * a))
            out.append((a, mult * jax.nn.sigmoid(gx) * u))
        return out

    (a0, b0), (a1, b1) = coeffs(xpc, 0, c)
    for q in range(nseg):
        rows = slice(q * cseg, (q + 1) * cseg)
        dst = pl.ds(q, cseg, stride=nseg)
        caf[dst, :] = a0[rows]
        cbf[dst, :] = b0[rows]
        cab[dst, :] = a1[rows]
        cbb[dst, :] = b1[rows]
    for q in range(nseg):
        (a0, b0), (a1, b1) = coeffs(xp, q * seg, seg)
        dst = pl.ds(q, seg, stride=nseg)
        af[dst, :] = a0
        bf[dst, :] = b0
        ab[dst, :] = a1
        bb[dst, :] = b1

    def scan(a_f, b_f, a_b, b_b, n, store):
        def body(j, carry):
            hf, pf, hb, pb = carry
            rf = pl.multiple_of(j * nseg, nseg)
            rb = pl.multiple_of((n - 1 - j) * nseg, nseg)
            av = a_f[pl.ds(rf, nseg), :]
            hf = av * hf + b_f[pl.ds(rf, nseg), :]
            pf = av * pf
            aw = a_b[pl.ds(rb, nseg), :]
            hb = aw * hb + b_b[pl.ds(rb, nseg), :]
            pb = aw * pb
            if store:
                a_f[pl.ds(rf, nseg), :] = pf
                b_f[pl.ds(rf, nseg), :] = hf
                a_b[pl.ds(rb, nseg), :] = pb
                b_b[pl.ds(rb, nseg), :] = hb
            return hf, pf, hb, pb
        z = jnp.zeros((nseg, LANES), F32)
        o = jnp.ones((nseg, LANES), F32)
        return lax.fori_loop(0, n, body, (z, o, z, o), unroll=8)

    def chain(h_end, p_end, h0, reverse):
        order = range(nseg - 1, -1, -1) if reverse else range(nseg)
        enter = [None] * nseg
        cur = h0
        for q in order:
            enter[q] = cur
            cur = h_end[q:q + 1, :] + p_end[q:q + 1, :] * cur
        return enter, cur

    zero_row = jnp.zeros((1, LANES), F32)
    hf, pf, hb, pb = scan(caf, cbf, cab, cbb, cseg, False)
    _, h0f = chain(hf, pf, zero_row, False)
    _, h0b = chain(hb, pb, zero_row, True)
    hf, pf, hb, pb = scan(af, bf, ab, bb, seg, True)
    enter_f, _ = chain(hf, pf, h0f, False)
    enter_b, _ = chain(hb, pb, h0b, True)
    for q in range(nseg):
        src = pl.ds(q, seg, stride=nseg)
        h = bf[src, :] + af[src, :] * enter_f[q] + bb[src, :] + ab[src, :] * enter_b[q]
        rows = slice(q * seg, (q + 1) * seg)
        o_ref[0, rows, :] = (h * _gelu_tanh(yr_ref[0, rows, :])).astype(BF16)


def _rglru(xr, xrc, yr, conv_w, conv_b, w_gates, b_gates, lam):
    b, s, w = xr.shape
    c = xrc.shape[1]
    nb = w // LANES
    slab = lambda n: pl.BlockSpec((1, n, LANES), lambda bi, j: (bi, 0, j))
    scr = lambda n: pltpu.VMEM((n, LANES), F32)
    vmem = (3 * 2 + 5) * s * LANES * 4 + 8 * s * LANES * 4
    return pl.pallas_call(
        functools.partial(_rglru_kernel, s=s, c=c),
        grid=(b, nb),
        in_specs=[slab(s), slab(c), slab(s),
                  pl.BlockSpec((CONV_W, LANES), lambda bi, j: (0, j)),
                  pl.BlockSpec((1, LANES), lambda bi, j: (0, j)),
                  pl.BlockSpec((1, LANES, 4 * LANES), lambda bi, j: (j, 0, 0)),
                  pl.BlockSpec((1, 4 * LANES), lambda bi, j: (0, j)),
                  pl.BlockSpec((2, LANES), lambda bi, j: (0, j))],
        out_specs=slab(s),
        out_shape=jax.ShapeDtypeStruct((b, s, w), BF16),
        scratch_shapes=[scr(s + 2 * CONV_PAD), scr(c + 2 * CONV_PAD),
                        scr(s), scr(s), scr(s), scr(s), scr(c), scr(c), scr(c), scr(c)],
        compiler_params=pltpu.CompilerParams(dimension_semantics=("arbitrary", "arbitrary"),
                                             vmem_limit_bytes=_vmem_limit(vmem)),
        name="rglru",
    )(xr, xrc, yr, conv_w, conv_b, w_gates, b_gates, lam)


def _pack_bf16_pair(lo, hi):
    lo_bits = lax.bitcast_convert_type(lo.astype(BF16).astype(F32), I32)
    hi_bits = lax.bitcast_convert_type(hi.astype(BF16).astype(F32), I32)
    return lax.shift_right_logical(lo_bits, 16) | (hi_bits & jnp.int32(-65536))


def _unpack_bf16_pair(w):
    lo = lax.bitcast_convert_type(lax.shift_left(w, 16), F32).astype(BF16)
    hi = lax.bitcast_convert_type(w & jnp.int32(-65536), F32).astype(BF16)
    return lo, hi


def _merge_kernel(att_ref, rec_ref, x_ref, mods_ref, ga_ref, gr_ref, wo_ref, g2_ref, wr_ref, br_ref,
                  x1_ref, h2_ref, idx_ref, gate_ref, mask_ref, *, d):
    row = pl.program_id(0)
    gt1 = mods_ref[pl.ds(row, 1), 2 * d:3 * d]
    sh2 = mods_ref[pl.ds(row, 1), 3 * d:4 * d]
    sc2 = mods_ref[pl.ds(row, 1), 4 * d:5 * d]
    an = _rms(att_ref[0].astype(F32), ga_ref[...]).astype(BF16)
    rn = _rms(rec_ref[0].astype(F32), gr_ref[...]).astype(BF16)
    mix = (jnp.dot(an, wo_ref[0:ATT_WIDTH, :], preferred_element_type=F32)
           + jnp.dot(rn, wo_ref[ATT_WIDTH:d, :], preferred_element_type=F32))
    x1 = x_ref[0] + gt1 * mix
    x1_ref[0] = x1
    h2 = _rms(x1, g2_ref[...]) * (1.0 + sc2) + sh2
    half = d // 2
    h2_ref[...] = _pack_bf16_pair(h2[:, 0:half], h2[:, half:d])
    h_hi = h2.astype(BF16)
    h_lo = (h2 - h_hi.astype(F32)).astype(BF16)
    w_split = wr_ref[...]
    part = (jnp.dot(h_hi, w_split, preferred_element_type=F32)
            + jnp.dot(h_lo, w_split, preferred_element_type=F32))
    logits = part[:, 0:N_EXPERTS] + part[:, N_EXPERTS:2 * N_EXPERTS] + br_ref[...]
    tm = logits.shape[0]
    lane = lax.broadcasted_iota(I32, (tm, N_EXPERTS), 1).astype(F32)
    col = lax.broadcasted_iota(I32, (tm, TOP_K), 1)
    idx = jnp.zeros((tm, TOP_K), F32)
    ex = jnp.zeros((tm, TOP_K), F32)
    mask = jnp.zeros((tm, N_EXPERTS), F32)
    rest = logits
    top = None
    for k in range(TOP_K):
        m = jnp.max(rest, axis=-1, keepdims=True)
        first = jnp.min(jnp.where(rest == m, lane, float(N_EXPERTS)), axis=-1, keepdims=True)
        sel = lane == first
        if k == 0:
            top = m
        idx = jnp.where(col == k, first, idx)
        ex = jnp.where(col == k, jnp.exp(m - top), ex)
        mask = jnp.where(sel, 1.0, mask)
        rest = jnp.where(sel, -jnp.inf, rest)
    idx_ref[...] = idx.astype(I32)
    gate_ref[...] = ex / jnp.sum(ex, axis=-1, keepdims=True)
    mask_ref[...] = mask


def _merge(att, rec, x, mods, g_att, g_rec, w_out_bf, g2, w_router, b_router, *, tm):
    b, s, d = x.shape
    t = b * s
    nt = s // tm
    rec_w = d - ATT_WIDTH
    row3 = lambda w: pl.BlockSpec((1, tm, w), lambda bi, i: (bi, i, 0))
    tok2 = lambda w: pl.BlockSpec((tm, w), lambda bi, i: (bi * nt + i, 0))
    const = lambda shape, **kw: pl.BlockSpec(shape, lambda bi, i: (0,) * len(shape), **kw)
    vmem = d * d * 2 + 2 * tm * (ATT_WIDTH + rec_w) * 2 + 4 * tm * d * 4 + tm * d * 4 + 8 * tm * d * 4
    return pl.pallas_call(
        functools.partial(_merge_kernel, d=d),
        grid=(b, nt),
        in_specs=[row3(ATT_WIDTH), row3(rec_w), row3(d), const(mods.shape),
                  const((1, ATT_WIDTH)), const((1, rec_w)),
                  const((d, d), pipeline_mode=pl.Buffered(1)), const((1, d)),
                  const((d, 2 * N_EXPERTS)), const((1, N_EXPERTS))],
        out_specs=[row3(d), tok2(d // 2), tok2(TOP_K), tok2(TOP_K), tok2(N_EXPERTS)],
        out_shape=[jax.ShapeDtypeStruct((b, s, d), F32), jax.ShapeDtypeStruct((t, d // 2), I32),
                   jax.ShapeDtypeStruct((t, TOP_K), I32), jax.ShapeDtypeStruct((t, TOP_K), F32),
                   jax.ShapeDtypeStruct((t, N_EXPERTS), F32)],
        compiler_params=pltpu.CompilerParams(dimension_semantics=("arbitrary", "arbitrary"),
                                             vmem_limit_bytes=_vmem_limit(vmem)),
        name="merge",
    )(att, rec, x, mods, g_att.reshape(1, -1), g_rec.reshape(1, -1), w_out_bf, g2.reshape(1, d),
      _split_bf16(w_router), b_router.reshape(1, N_EXPERTS))


def _split_bf16(w):
    hi = w.astype(BF16)
    lo = (w - hi.astype(F32)).astype(BF16)
    return jnp.concatenate([hi, lo], axis=1)


def _rank_kernel(mask_ref, idx_ref, rank_ref, cnt_ref, carry):
    @pl.when(pl.program_id(0) == 0)
    def _():
        carry[...] = jnp.zeros_like(carry)

    m = mask_ref[...]
    tb = m.shape[0]
    r = lax.broadcasted_iota(I32, (tb, tb), 0)
    cidx = lax.broadcasted_iota(I32, (tb, tb), 1)
    tri = jnp.where(cidx < r, 1.0, 0.0).astype(BF16)
    before = jnp.dot(tri, m.astype(BF16), preferred_element_type=F32) + carry[...]
    lane = lax.broadcasted_iota(I32, (tb, N_EXPERTS), 1)
    col = lax.broadcasted_iota(I32, (tb, TOP_K), 1)
    idx = idx_ref[...]
    rank = jnp.zeros((tb, TOP_K), F32)
    for k in range(TOP_K):
        pick = jnp.sum(jnp.where(lane == idx[:, k:k + 1], before, 0.0), axis=-1, keepdims=True)
        rank = jnp.where(col == k, pick, rank)
    rank_ref[...] = rank.astype(I32)
    carry[...] = carry[...] + jnp.sum(m, axis=0, keepdims=True)
    cnt_ref[...] = carry[...].astype(I32)


def _rank(mask, idx, *, tb):
    t = mask.shape[0]
    return pl.pallas_call(
        _rank_kernel,
        grid=(t // tb,),
        in_specs=[pl.BlockSpec((tb, N_EXPERTS), lambda i: (i, 0)), pl.BlockSpec((tb, TOP_K), lambda i: (i, 0))],
        out_specs=[pl.BlockSpec((tb, TOP_K), lambda i: (i, 0)), pl.BlockSpec((1, N_EXPERTS), lambda i: (0, 0))],
        out_shape=[jax.ShapeDtypeStruct((t, TOP_K), I32), jax.ShapeDtypeStruct((1, N_EXPERTS), I32)],
        scratch_shapes=[pltpu.VMEM((1, N_EXPERTS), F32)],
        compiler_params=pltpu.CompilerParams(dimension_semantics=("arbitrary",)),
        name="rank",
    )(mask, idx)


def _dispatch_kernel(dest_ref, fill_ref, h2_ref, xs_ref, zeros, sem, fill_sem, *, tb, n_blocks):
    base = pl.program_id(0) * tb

    @pl.when(pl.program_id(0) == 0)
    def _():
        zeros[...] = jnp.zeros_like(zeros)

        def zero_row(r):
            return pltpu.make_async_copy(zeros.at[pl.ds(0, 1)], xs_ref.at[pl.ds(r, 1)], fill_sem)

        def zero_block(bk):
            rows = pl.ds(pl.multiple_of(bk * MOE_BLOCK, MOE_BLOCK), MOE_BLOCK)
            return pltpu.make_async_copy(zeros, xs_ref.at[rows], fill_sem)

        def zero_rows(start, n):
            return pltpu.make_async_copy(zeros.at[pl.ds(0, n)],
                                         xs_ref.at[pl.ds(pl.multiple_of(start, SUBLANES), n)], fill_sem)

        def for_all_fills(act):
            for e in range(N_EXPERTS):
                lo = fill_ref[e]
                hi = fill_ref[N_EXPERTS + e]
                lo8 = jnp.minimum((lo + SUBLANES - 1) // SUBLANES * SUBLANES, hi)
                lax.fori_loop(lo, lo8, lambda r, c: (act(zero_row(r)), c)[1], 0)
                rest = hi - lo8
                pos = lo8
                size = MOE_BLOCK // 2
                while size >= SUBLANES:
                    take = rest & size

                    @pl.when(take != 0)
                    def _(pos=pos, size=size):
                        act(zero_rows(pos, size))

                    pos = pos + take
                    size //= 2
            lax.fori_loop(fill_ref[2 * N_EXPERTS], n_blocks, lambda bk, c: (act(zero_block(bk)), c)[1], 0)

        for_all_fills(lambda cp: cp.start())
        for_all_fills(lambda cp: cp.wait())

    def row_copy(i, k):
        return pltpu.make_async_copy(h2_ref.at[pl.ds(i, 1)],
                                     xs_ref.at[pl.ds(dest_ref[(base + i) * TOP_K + k], 1)], sem)

    def body(i, _):
        for k in range(TOP_K):
            row_copy(i, k).start(priority=k % 2)
        return 0

    lax.fori_loop(0, tb, body, 0, unroll=4)
    for k in range(TOP_K):
        pltpu.make_async_copy(h2_ref, xs_ref.at[pl.ds(0, tb)], sem).wait()


def _dispatch(dest_flat, fill, h2p, *, tb, n_blocks):
    t, wd = h2p.shape
    return pl.pallas_call(
        functools.partial(_dispatch_kernel, tb=tb, n_blocks=n_blocks),
        grid_spec=pltpu.PrefetchScalarGridSpec(
            num_scalar_prefetch=2, grid=(t // tb,),
            in_specs=[pl.BlockSpec((tb, wd), lambda i, dr, fl: (i, 0))],
            out_specs=pl.BlockSpec(memory_space=pl.ANY),
            scratch_shapes=[pltpu.VMEM((MOE_BLOCK, wd), I32), pltpu.SemaphoreType.DMA(()),
                            pltpu.SemaphoreType.DMA(())]),
        out_shape=jax.ShapeDtypeStruct((n_blocks * MOE_BLOCK, wd), I32),
        compiler_params=pltpu.CompilerParams(dimension_semantics=("arbitrary",), has_side_effects=True),
        name="dispatch",
    )(dest_flat, fill, h2p)


PAIR_CHUNK = 512


def _last_real_block(s, meta):
    return jnp.maximum(jnp.minimum(s, meta[0] - 1), 0)


def _stream_expert_weights(ridx_ref, run_e_ref, meta_ref, w_hbm, wf32, wbf, sem):
    j = pl.program_id(0)
    s = pl.program_id(1)
    tn = wbf.shape[1]
    n_runs = meta_ref[1]
    r = ridx_ref[s]
    first = (s < meta_ref[0]) & ((s == 0) | (r != ridx_ref[jnp.maximum(s - 1, 0)]))

    def tile_copy(run, sweep):
        cols = pl.ds(pl.multiple_of(sweep * tn, tn), tn)
        return pltpu.make_async_copy(w_hbm.at[run_e_ref[run], :, cols], wf32, sem)

    @pl.when(first)
    def _():
        @pl.when((j == 0) & (r == 0))
        def _():
            tile_copy(0, 0).start()

        tile_copy(r, j).wait()
        wbf[...] = wf32[...].astype(BF16)
        more_runs = r + 1 < n_runs

        @pl.when(more_runs)
        def _():
            tile_copy(r + 1, j).start()

        @pl.when(jnp.logical_not(more_runs) & (j + 1 < pl.num_programs(0)))
        def _():
            tile_copy(0, j + 1).start()


def _expert_stream_scratch(k, tn):
    return [pltpu.VMEM((k, tn), F32), pltpu.VMEM((k, tn), BF16), pltpu.SemaphoreType.DMA(())]


def _for_real_rows(rows_ref, meta_ref, out_ref, compute_rows):
    s = pl.program_id(1)
    real = s < meta_ref[0]
    half = MOE_BLOCK // 2
    upper_half_used = rows_ref[s] > half

    @pl.when(real & upper_half_used)
    def _():
        compute_rows(MOE_BLOCK)

    @pl.when(real & jnp.logical_not(upper_half_used))
    def _():
        compute_rows(half)
        out_ref[half:, :] = jnp.zeros((half, out_ref.shape[1]), out_ref.dtype)

    @pl.when(jnp.logical_not(real))
    def _():
        out_ref[...] = jnp.zeros_like(out_ref)


def _gu_kernel(be_ref, ridx_ref, run_e_ref, meta_ref, rows_ref, xs_ref, w_hbm, b_ref, act_ref, wf32, wbf, sem):
    del be_ref
    _stream_expert_weights(ridx_ref, run_e_ref, meta_ref, w_hbm, wf32, wbf, sem)
    tn = wbf.shape[1]

    def compute_rows(n):
        lo, hi = _unpack_bf16_pair(xs_ref[0:n, :])
        xb = jnp.concatenate([lo, hi], axis=1)
        gu = jnp.dot(xb, wbf[...], preferred_element_type=F32) + b_ref[0]
        g = jnp.minimum(gu, SWIGLU_LIMIT)
        up1 = jnp.clip(gu, -SWIGLU_LIMIT, SWIGLU_LIMIT) + 1.0
        paired = (pltpu.roll(up1, tn - 1, 1) * (g * jax.nn.sigmoid(SWIGLU_ALPHA * g))).astype(BF16)
        r = lax.broadcasted_iota(I32, (PAIR_CHUNK, PAIR_CHUNK // 2), 0)
        c = lax.broadcasted_iota(I32, (PAIR_CHUNK, PAIR_CHUNK // 2), 1)
        sel = jnp.where(r == 2 * c, 1.0, 0.0).astype(BF16)
        for ch in range(tn // PAIR_CHUNK):
            act_ref[0:n, ch * (PAIR_CHUNK // 2):(ch + 1) * (PAIR_CHUNK // 2)] = jnp.dot(
                paired[:, ch * PAIR_CHUNK:(ch + 1) * PAIR_CHUNK], sel, preferred_element_type=F32).astype(BF16)

    _for_real_rows(rows_ref, meta_ref, act_ref, compute_rows)


def _moe_gu(sched, xs, w_gu, b_gu, *, tn):
    p, half = xs.shape
    d = 2 * half
    f2 = w_gu.shape[2]
    nj = f2 // tn
    nblk = p // MOE_BLOCK
    blk = _last_real_block
    vmem = d * tn * 4 + d * tn * 2 + 2 * MOE_BLOCK * half * 4 + 2 * MOE_BLOCK * tn + 8 * MOE_BLOCK * tn * 4
    return pl.pallas_call(
        _gu_kernel,
        grid_spec=pltpu.PrefetchScalarGridSpec(
            num_scalar_prefetch=5, grid=(nj, nblk),
            in_specs=[pl.BlockSpec((MOE_BLOCK, half), lambda j, s, be, ri, re, mt, rw: (blk(s, mt), 0)),
                      pl.BlockSpec(memory_space=pl.ANY),
                      pl.BlockSpec((1, 1, tn), lambda j, s, be, ri, re, mt, rw: (be[blk(s, mt)], 0, j))],
            out_specs=pl.BlockSpec((MOE_BLOCK, tn // 2), lambda j, s, be, ri, re, mt, rw: (s, j)),
            scratch_shapes=_expert_stream_scratch(d, tn)),
        out_shape=jax.ShapeDtypeStruct((p, f2 // 2), BF16),
        compiler_params=pltpu.CompilerParams(dimension_semantics=("arbitrary", "arbitrary"),
                                             vmem_limit_bytes=_vmem_limit(vmem)),
        name="moe_gu",
    )(*sched, xs, w_gu, b_gu)


def _down_kernel(be_ref, ridx_ref, run_e_ref, meta_ref, rows_ref, act_ref, w_hbm, b_ref, y_ref, wf32, wbf, sem):
    del be_ref
    _stream_expert_weights(ridx_ref, run_e_ref, meta_ref, w_hbm, wf32, wbf, sem)

    def compute_rows(n):
        y = jnp.dot(act_ref[0:n, :], wbf[...], preferred_element_type=F32) + b_ref[0]
        half = y.shape[1] // 2
        y_ref[0:n, :] = _pack_bf16_pair(y[:, 0:half], y[:, half:2 * half])

    _for_real_rows(rows_ref, meta_ref, y_ref, compute_rows)


def _moe_down(sched, act, w_down, b_down):
    p, f = act.shape
    d = w_down.shape[2]
    nblk = p // MOE_BLOCK
    blk = _last_real_block
    vmem = f * d * 4 + f * d * 2 + 2 * MOE_BLOCK * f * 2 + 5 * MOE_BLOCK * d * 4
    return pl.pallas_call(
        _down_kernel,
        grid_spec=pltpu.PrefetchScalarGridSpec(
            num_scalar_prefetch=5, grid=(1, nblk),
            in_specs=[pl.BlockSpec((MOE_BLOCK, f), lambda j, s, be, ri, re, mt, rw: (blk(s, mt), 0)),
                      pl.BlockSpec(memory_space=pl.ANY),
                      pl.BlockSpec((1, 1, d), lambda j, s, be, ri, re, mt, rw: (be[blk(s, mt)], 0, 0))],
            out_specs=pl.BlockSpec((MOE_BLOCK, d // 2), lambda j, s, be, ri, re, mt, rw: (s, 0)),
            scratch_shapes=_expert_stream_scratch(f, d)),
        out_shape=jax.ShapeDtypeStruct((p, d // 2), I32),
        compiler_params=pltpu.CompilerParams(dimension_semantics=("arbitrary", "arbitrary"),
                                             vmem_limit_bytes=_vmem_limit(vmem)),
        name="moe_down",
    )(*sched, act, w_down, b_down)


def _combine_kernel(dest_ref, ys_ref, x1_ref, gate_ref, mods_ref, o_ref, buf, sem, *, tb, d, nt, n_tiles):
    i = pl.program_id(0)
    slot = lax.rem(i, 2)

    def row_copy(tile, sl, r, k):
        return pltpu.make_async_copy(ys_ref.at[pl.ds(dest_ref[(tile * tb + r) * TOP_K + k], 1)],
                                     buf.at[sl, k, pl.ds(r, 1)], sem.at[sl])

    def gather(tile, sl):
        def body(r, _):
            for k in range(TOP_K):
                row_copy(tile, sl, r, k).start(priority=k % 2)
            return 0
        lax.fori_loop(0, tb, body, 0, unroll=4)

    @pl.when(i == 0)
    def _():
        gather(0, 0)

    @pl.when(i + 1 < n_tiles)
    def _():
        gather(i + 1, 1 - slot)

    for k in range(TOP_K):
        pltpu.make_async_copy(ys_ref.at[pl.ds(0, tb)], buf.at[slot, k], sem.at[slot]).wait()
    half = d // 2
    gate = gate_ref[...]
    y_lo = jnp.zeros((tb, half), F32)
    y_hi = jnp.zeros((tb, half), F32)
    for k in range(TOP_K):
        w = buf[slot, k]
        gk = gate[:, k:k + 1]
        y_lo = y_lo + lax.bitcast_convert_type(lax.shift_left(w, 16), F32) * gk
        y_hi = y_hi + lax.bitcast_convert_type(w & jnp.int32(-65536), F32) * gk
    gt2 = mods_ref[pl.ds(i // nt, 1), 5 * d:6 * d]
    o_ref[:, 0:half] = x1_ref[:, 0:half] + gt2[:, 0:half] * y_lo
    o_ref[:, half:d] = x1_ref[:, half:d] + gt2[:, half:d] * y_hi


def _combine(dest_flat, ys, x1, gate, mods, *, tb, seq):
    t, d = x1.shape
    n_tiles = t // tb
    nt = seq // tb
    vmem = 2 * TOP_K * tb * d * 2 + 4 * tb * d * 4 + 6 * tb * d * 4
    return pl.pallas_call(
        functools.partial(_combine_kernel, tb=tb, d=d, nt=nt, n_tiles=n_tiles),
        grid_spec=pltpu.PrefetchScalarGridSpec(
            num_scalar_prefetch=1, grid=(n_tiles,),
            in_specs=[pl.BlockSpec(memory_space=pl.ANY),
                      pl.BlockSpec((tb, d), lambda i, dr: (i, 0)),
                      pl.BlockSpec((tb, TOP_K), lambda i, dr: (i, 0)),
                      pl.BlockSpec(mods.shape, lambda i, dr: (0, 0))],
            out_specs=pl.BlockSpec((tb, d), lambda i, dr: (i, 0)),
            scratch_shapes=[pltpu.VMEM((2, TOP_K, tb, d // 2), I32), pltpu.SemaphoreType.DMA((2,))]),
        out_shape=jax.ShapeDtypeStruct((t, d), F32),
        compiler_params=pltpu.CompilerParams(dimension_semantics=("arbitrary",),
                                             vmem_limit_bytes=_vmem_limit(vmem)),
        name="combine",
    )(dest_flat, ys, x1, gate, mods)


def _rope_table(s):
    rows = s // GRID_W
    row = np.repeat(np.arange(rows), GRID_W).astype(np.float32)
    col = np.tile(np.arange(GRID_W), rows).astype(np.float32)
    inv = (np.float32(ROPE_THETA) ** (-np.arange(ROPE_FREQS, dtype=np.float32) / np.float32(ROPE_FREQS))).astype(np.float32)
    ang_r = row[:, None] * inv
    ang_c = col[:, None] * inv
    z = np.zeros_like(ang_r)
    cosf = np.concatenate([np.cos(ang_r)] * 2 + [np.cos(ang_c)] * 2, axis=1)
    sneg = np.concatenate([-np.sin(ang_r), z, -np.sin(ang_c), z], axis=1)
    spos = np.concatenate([z, np.sin(ang_r), z, np.sin(ang_c)], axis=1)
    return jnp.asarray(np.concatenate([cosf, sneg, spos], axis=1), dtype=F32)


def kernel(x, c, ctx, c_ctx, w_mod, b_mod, g_norm1, w_in, g_q, g_k, conv_w, conv_b, w_gate_a, b_gate_a,
           w_gate_x, b_gate_x, lru_lambda, g_att_out, g_rec_out, w_out, g_norm2, w_router, b_router,
           w_gate_up, b_gate_up, w_down, b_down):
    b, s, d = x.shape
    cl = ctx.shape[1]
    t = b * s
    assert w_mod.shape[0] == 1, "single-layer kernel"
    assert b + 1 <= SUBLANES and d - ATT_WIDTH == REC_BLOCKS * LANES
    assert s % (SCAN_SEGMENTS * SUBLANES) == 0 and cl % (SCAN_SEGMENTS * SUBLANES) == 0

    ctx_row = b
    c8 = jnp.zeros((SUBLANES, d), F32).at[:b].set(c).at[ctx_row].set(c_ctx)
    mods = _mod(c8, w_mod[0], b_mod[0])

    w_in_bf = w_in[0].astype(BF16)
    tm = min(512, s)
    q, k, v, xr, yr = _inproj(x, mods, g_norm1[0], w_in_bf, _rope_table(s), g_q[0], g_k[0],
                              latent=True, ctx_row=ctx_row, tm=tm)
    kc, vc, xrc = _inproj(ctx, mods, g_norm1[0], w_in_bf, None, g_q[0], g_k[0],
                          latent=False, ctx_row=ctx_row, tm=cl)

    att = _attention(q, kc, k, vc, v, tq=min(256, s))

    w_gates = jnp.concatenate([w_gate_a[0, 0], w_gate_x[0, 0], w_gate_a[0, 1], w_gate_x[0, 1]], axis=-1).astype(BF16)
    rw = d - ATT_WIDTH
    bias = lambda bb: bb.reshape(REC_BLOCKS, LANES)
    b_gates = jnp.concatenate([bias(b_gate_a[0, 0]), bias(b_gate_x[0, 0]), bias(b_gate_a[0, 1]), bias(b_gate_x[0, 1])],
                              axis=-1).reshape(1, 4 * rw)
    rec = _rglru(xr, xrc, yr, conv_w[0], conv_b[0].reshape(1, rw), w_gates, b_gates, lru_lambda[0])

    x1, h2p, top_idx, gate, mask = _merge(att, rec, x, mods, g_att_out[0], g_rec_out[0], w_out[0].astype(BF16),
                                          g_norm2[0], w_router[0], b_router[0], tm=tm)

    rank, counts = _rank(mask, top_idx, tb=MOE_BLOCK)
    counts = counts[0]
    padded = (counts + MOE_BLOCK - 1) // MOE_BLOCK * MOE_BLOCK
    pad_ends = jnp.cumsum(padded)
    pad_starts = pad_ends - padded
    dest = (pad_starts[top_idx] + rank).reshape(t * TOP_K)
    n_blocks = (t * TOP_K + N_EXPERTS * (MOE_BLOCK - 1) + MOE_BLOCK - 1) // MOE_BLOCK
    block_start = jnp.arange(n_blocks, dtype=I32) * MOE_BLOCK
    block_e = jnp.minimum(jnp.sum((pad_ends[None, :] <= block_start[:, None]).astype(I32), axis=1), N_EXPERTS - 1)
    n_valid = (pad_ends[N_EXPERTS - 1] // MOE_BLOCK).reshape(1).astype(I32)
    fill = jnp.concatenate([pad_starts + counts, pad_ends, n_valid]).astype(I32)

    xs = _dispatch(dest, fill, h2p, tb=MOE_BLOCK, n_blocks=n_blocks)

    used = padded > 0
    run_of_expert = jnp.cumsum(used.astype(I32)) - 1
    experts = jnp.arange(N_EXPERTS, dtype=I32)
    run_e = jnp.sum(jnp.where(used[None, :] & (run_of_expert[None, :] == experts[:, None]), experts[None, :], 0), axis=1)
    meta = jnp.concatenate([n_valid, jnp.sum(used.astype(I32)).reshape(1)])
    block_rows = jnp.clip((pad_starts + counts)[block_e] - block_start, 0, MOE_BLOCK).astype(I32)
    sched = (block_e, run_of_expert[block_e], run_e.astype(I32), meta, block_rows)

    f2 = w_gate_up.shape[3]
    act = _moe_gu(sched, xs, w_gate_up[0], b_gate_up[0].reshape(N_EXPERTS, 1, f2), tn=2048)
    ys = _moe_down(sched, act, w_down[0], b_down[0].reshape(N_EXPERTS, 1, d))

    return _combine(dest, ys, x1.reshape(t, d), gate, mods, tb=256, seq=s).reshape(b, s, d)
```

```python
import functools
import math

import jax
import jax.numpy as jnp
import numpy as np
from jax import lax
from jax.experimental import pallas as pl
from jax.experimental.pallas import tpu as pltpu

F32 = jnp.float32
BF16 = jnp.bfloat16
I32 = jnp.int32

EPS = 1e-6
GRID_W = 64
HEAD_DIM = 128
N_Q_HEADS = 8
N_KV_HEADS = 2
GROUP = N_Q_HEADS // N_KV_HEADS
ATT_WIDTH = N_Q_HEADS * HEAD_DIM
KV_WIDTH = N_KV_HEADS * HEAD_DIM
ROPE_THETA = 10000.0
ROPE_FREQS = HEAD_DIM // 4
REC_BLOCKS = 8
CONV_W = 4
CONV_LEFT = 2
LRU_C = 8.0
N_EXPERTS = 32
TOP_K = 4
SWIGLU_LIMIT = 7.0
SWIGLU_ALPHA = 1.702
MOE_BLOCK = 512

V7X_VMEM_BYTES = 64 * 1024 * 1024
SUBLANES = 8
LANES = 128
SCAN_SEGMENTS = SUBLANES
CONV_PAD = SUBLANES

HIGHEST = lax.Precision.HIGHEST


def _vmem_limit(nbytes):
    return int(min(V7X_VMEM_BYTES - 4 * 1024 * 1024, max(nbytes, 16 * 1024 * 1024)))


def _rms(x, g):
    return x * lax.rsqrt(jnp.mean(x * x, axis=-1, keepdims=True) + EPS) * g


def _mod_kernel(c_ref, w_ref, b_ref, o_ref):
    c = c_ref[...]
    a = c * jax.nn.sigmoid(c)
    o_ref[...] = jnp.dot(a, w_ref[...], preferred_element_type=F32, precision=HIGHEST) + b_ref[...]


def _mod(c8, w_mod, b_mod):
    d, n = w_mod.shape
    tn = 2048
    return pl.pallas_call(
        _mod_kernel,
        grid=(n // tn,),
        in_specs=[pl.BlockSpec((SUBLANES, d), lambda j: (0, 0)),
                  pl.BlockSpec((d, tn), lambda j: (0, j)),
                  pl.BlockSpec((1, tn), lambda j: (0, j))],
        out_specs=pl.BlockSpec((SUBLANES, tn), lambda j: (0, j)),
        out_shape=jax.ShapeDtypeStruct((SUBLANES, n), F32),
        compiler_params=pltpu.CompilerParams(dimension_semantics=("arbitrary",),
                                             vmem_limit_bytes=_vmem_limit(3 * d * tn * 4)),
        name="mod",
    )(c8, w_mod, b_mod.reshape(1, n))


def _qk_norm_rope(y, g, rope):
    yn = _rms(y, g)
    if rope is None:
        return yn
    cosf, sneg, spos = rope
    return yn * cosf + pltpu.roll(yn, HEAD_DIM - ROPE_FREQS, 1) * sneg + pltpu.roll(yn, ROPE_FREQS, 1) * spos


def _inproj_kernel(*refs, d, latent, ctx_row):
    if latent:
        (x_ref, mods_ref, g1_ref, w_ref, rope_ref, gq_ref, gk_ref,
         q_ref, k_ref, v_ref, xr_ref, yr_ref) = refs
        row = pl.program_id(0)
    else:
        x_ref, mods_ref, g1_ref, w_ref, gk_ref, k_ref, v_ref, xr_ref = refs
        row = ctx_row
    sh = mods_ref[pl.ds(row, 1), 0:d]
    sc = mods_ref[pl.ds(row, 1), d:2 * d]
    h = _rms(x_ref[0], g1_ref[...]) * (1.0 + sc) + sh
    hb = h.astype(BF16)

    def proj(lo, hi):
        return jnp.dot(hb, w_ref[:, lo:hi], preferred_element_type=F32)

    o_k = ATT_WIDTH
    o_v = o_k + KV_WIDTH
    o_xr = o_v + KV_WIDTH
    rec_w = d - ATT_WIDTH
    o_yr = o_xr + rec_w
    rope = None
    if latent:
        rp = rope_ref[...]
        rope = (rp[:, 0:HEAD_DIM], rp[:, HEAD_DIM:2 * HEAD_DIM], rp[:, 2 * HEAD_DIM:3 * HEAD_DIM])
        q = proj(0, ATT_WIDTH)
        for hd in range(N_Q_HEADS):
            sl = slice(hd * HEAD_DIM, (hd + 1) * HEAD_DIM)
            q_ref[0, :, sl] = (_qk_norm_rope(q[:, sl], gq_ref[...], rope) * (HEAD_DIM ** -0.5)).astype(BF16)
    k = proj(o_k, o_v)
    for hd in range(N_KV_HEADS):
        sl = slice(hd * HEAD_DIM, (hd + 1) * HEAD_DIM)
        k_ref[0, :, sl] = _qk_norm_rope(k[:, sl], gk_ref[...], rope).astype(BF16)
    v = proj(o_v, o_xr).astype(BF16)
    for hd in range(N_KV_HEADS):
        v_ref[0, :, 2 * hd * HEAD_DIM:(2 * hd + 1) * HEAD_DIM] = v[:, hd * HEAD_DIM:(hd + 1) * HEAD_DIM]
        v_ref[0, :, (2 * hd + 1) * HEAD_DIM:(2 * hd + 2) * HEAD_DIM] = jnp.ones((v.shape[0], HEAD_DIM), BF16)
    xr_ref[0] = proj(o_xr, o_yr)
    if latent:
        yr_ref[0] = proj(o_yr, o_yr + rec_w)


def _inproj(x, mods, g1, w_in_bf, rope_tab, g_q, g_k, *, latent, ctx_row, tm):
    b, s, d = x.shape
    n = w_in_bf.shape[1]
    rec_w = d - ATT_WIDTH
    grid = (b, s // tm)
    row_spec = lambda w: pl.BlockSpec((1, tm, w), lambda bi, i: (bi, i, 0))
    full2 = lambda a: pl.BlockSpec(a.shape, lambda bi, i: (0, 0))
    in_specs = [row_spec(d), full2(mods), pl.BlockSpec((1, d), lambda bi, i: (0, 0)),
                pl.BlockSpec((d, n), lambda bi, i: (0, 0), pipeline_mode=pl.Buffered(1))]
    args = [x, mods, g1.reshape(1, d), w_in_bf]
    out_specs, out_shape = [], []
    if latent:
        in_specs += [pl.BlockSpec((tm, 3 * HEAD_DIM), lambda bi, i: (i, 0)),
                     pl.BlockSpec((1, HEAD_DIM), lambda bi, i: (0, 0))]
        args += [rope_tab, g_q.reshape(1, HEAD_DIM)]
        out_specs.append(row_spec(ATT_WIDTH))
        out_shape.append(jax.ShapeDtypeStruct((b, s, ATT_WIDTH), BF16))
    in_specs.append(pl.BlockSpec((1, HEAD_DIM), lambda bi, i: (0, 0)))
    args.append(g_k.reshape(1, HEAD_DIM))
    out_specs += [row_spec(KV_WIDTH), row_spec(2 * KV_WIDTH), row_spec(rec_w)]
    out_shape += [jax.ShapeDtypeStruct((b, s, KV_WIDTH), BF16), jax.ShapeDtypeStruct((b, s, 2 * KV_WIDTH), BF16),
                  jax.ShapeDtypeStruct((b, s, rec_w), F32)]
    if latent:
        out_specs.append(row_spec(rec_w))
        out_shape.append(jax.ShapeDtypeStruct((b, s, rec_w), F32))
    vmem = d * n * 2 + 2 * tm * d * 4 + 2 * tm * n * 4 + 3 * tm * d * 4 + 2 * tm * n * 4
    return pl.pallas_call(
        functools.partial(_inproj_kernel, d=d, latent=latent, ctx_row=ctx_row),
        grid=grid, in_specs=in_specs, out_specs=out_specs, out_shape=out_shape,
        compiler_params=pltpu.CompilerParams(dimension_semantics=("arbitrary", "arbitrary"),
                                             vmem_limit_bytes=_vmem_limit(vmem)),
        name="inproj_latent" if latent else "inproj_ctx",
    )(*args)


def _attn_kernel(q_ref, kc_ref, k_ref, vc_ref, v_ref, o_ref, kbuf, vbuf):
    s_lat = k_ref.shape[1]

    @pl.when(pl.program_id(2) == 0)
    def _():
        kbuf[0:s_lat, :] = k_ref[0]
        kbuf[s_lat:, :] = kc_ref[0]
        vbuf[0:s_lat, :] = v_ref[0]
        vbuf[s_lat:, :] = vc_ref[0]

    k = kbuf[...]
    v = vbuf[...]
    for g in range(GROUP):
        sl = slice(g * HEAD_DIM, (g + 1) * HEAD_DIM)
        s = lax.dot_general(q_ref[0, :, sl], k, (((1,), (1,)), ((), ())), preferred_element_type=F32)
        m = jnp.max(s, axis=-1, keepdims=True)
        o = jnp.dot(jnp.exp(s - m).astype(BF16), v, preferred_element_type=F32)
        o_ref[0, :, sl] = (o[:, 0:HEAD_DIM] / o[:, HEAD_DIM:2 * HEAD_DIM]).astype(BF16)


def _attention(q, kc, k, vc, v, *, tq):
    b, s, _ = q.shape
    lk = s + kc.shape[1]
    gw = GROUP * HEAD_DIM
    vmem = 3 * lk * HEAD_DIM * 2 * 3 + 4 * tq * gw * 2 + 4 * tq * lk * 4
    kv = lambda a, w: pl.BlockSpec((1, a.shape[1], w), lambda bi, h, i: (bi, 0, h))
    return pl.pallas_call(
        _attn_kernel,
        grid=(b, N_KV_HEADS, s // tq),
        in_specs=[pl.BlockSpec((1, tq, gw), lambda bi, h, i: (bi, i, h)),
                  kv(kc, HEAD_DIM), kv(k, HEAD_DIM), kv(vc, 2 * HEAD_DIM), kv(v, 2 * HEAD_DIM)],
        out_specs=pl.BlockSpec((1, tq, gw), lambda bi, h, i: (bi, i, h)),
        out_shape=jax.ShapeDtypeStruct((b, s, ATT_WIDTH), BF16),
        scratch_shapes=[pltpu.VMEM((lk, HEAD_DIM), BF16), pltpu.VMEM((lk, 2 * HEAD_DIM), BF16)],
        compiler_params=pltpu.CompilerParams(dimension_semantics=("arbitrary",) * 3,
                                             vmem_limit_bytes=_vmem_limit(vmem)),
        name="attention",
    )(q, kc, k, vc, v)


def _gelu_tanh(x):
    return 0.5 * x * (1.0 + jnp.tanh(math.sqrt(2.0 / math.pi) * (x + 0.044715 * x * x * x)))


def _rglru_kernel(xr_ref, xc_ref, yr_ref, cw_ref, cb_ref, wg_ref, bg_ref, lam_ref, o_ref,
                  xp, xpc, af, bf, ab, bb, caf, cbf, cab, cbb, *, s, c):
    nseg = SCAN_SEGMENTS
    seg = s // nseg
    cseg = c // nseg
    zeros_pad = jnp.zeros((CONV_PAD, LANES), F32)
    xp[0:CONV_PAD, :] = zeros_pad
    xp[CONV_PAD + s:2 * CONV_PAD + s, :] = zeros_pad
    xp[CONV_PAD:CONV_PAD + s, :] = xr_ref[0]
    xpc[0:CONV_PAD, :] = zeros_pad
    xpc[CONV_PAD + c:2 * CONV_PAD + c, :] = zeros_pad
    xpc[CONV_PAD:CONV_PAD + c, :] = xc_ref[0]

    cw = cw_ref[...]
    cb = cb_ref[...]
    wg = wg_ref[0]
    bg = bg_ref[...]
    sp = jax.nn.softplus(-lam_ref[...])

    def coeffs(src, lo, n):
        u = cb
        for j in range(CONV_W):
            u = u + src[CONV_PAD + lo + j - CONV_LEFT:CONV_PAD + lo + j - CONV_LEFT + n, :] * cw[j:j + 1, :]
        g = jnp.dot(u.astype(BF16), wg, preferred_element_type=F32) + bg
        out = []
        for r in range(2):
            ga = g[:, (2 * r) * LANES:(2 * r + 1) * LANES]
            gx = g[:, (2 * r + 1) * LANES:(2 * r + 2) * LANES]
            log_a = (-LRU_C) * jax.nn.sigmoid(ga) * sp[r:r + 1, :]
            a = jnp.exp(log_a)
            mult = jnp.sqrt(-jnp.tanh(log_a) * (1.0 + a * a))
            out.append((a, mult * jax.nn.sigmoid(gx) * u))
        return out

    (a0, b0), (a1, b1) = coeffs(xpc, 0, c)
    for q in range(nseg):
        rows = slice(q * cseg, (q + 1) * cseg)
        dst = pl.ds(q, cseg, stride=nseg)
        caf[dst, :] = a0[rows]
        cbf[dst, :] = b0[rows]
        cab[dst, :] = a1[rows]
        cbb[dst, :] = b1[rows]
    for q in range(nseg):
        (a0, b0), (a1, b1) = coeffs(xp, q * seg, seg)
        dst = pl.ds(q, seg, stride=nseg)
        af[dst, :] = a0
        bf[dst, :] = b0
        ab[dst, :] = a1
        bb[dst, :] = b1

    def scan(a_f, b_f, a_b, b_b, n, store):
        def body(j, carry):
            hf, pf, hb, pb = carry
            rf = pl.multiple_of(j * nseg, nseg)
            rb = pl.multiple_of((n - 1 - j) * nseg, nseg)
            av = a_f[pl.ds(rf, nseg), :]
            hf = av * hf + b_f[pl.ds(rf, nseg), :]
            pf = av * pf
            aw = a_b[pl.ds(rb, nseg), :]
            hb = aw * hb + b_b[pl.ds(rb, nseg), :]
            pb = aw * pb
            if store:
                a_f[pl.ds(rf, nseg), :] = pf
                b_f[pl.ds(rf, nseg), :] = hf
                a_b[pl.ds(rb, nseg), :] = pb
                b_b[pl.ds(rb, nseg), :] = hb
            return hf, pf, hb, pb
        z = jnp.zeros((nseg, LANES), F32)
        o = jnp.ones((nseg, LANES), F32)
        return lax.fori_loop(0, n, body, (z, o, z, o), unroll=8)

    def chain(h_end, p_end, h0, reverse):
        order = range(nseg - 1, -1, -1) if reverse else range(nseg)
        enter = [None] * nseg
        cur = h0
        for q in order:
            enter[q] = cur
            cur = h_end[q:q + 1, :] + p_end[q:q + 1, :] * cur
        return enter, cur

    zero_row = jnp.zeros((1, LANES), F32)
    hf, pf, hb, pb = scan(caf, cbf, cab, cbb, cseg, False)
    _, h0f = chain(hf, pf, zero_row, False)
    _, h0b = chain(hb, pb, zero_row, True)
    hf, pf, hb, pb = scan(af, bf, ab, bb, seg, True)
    enter_f, _ = chain(hf, pf, h0f, False)
    enter_b, _ = chain(hb, pb, h0b, True)
    for q in range(nseg):
        src = pl.ds(q, seg, stride=nseg)
        h = bf[src, :] + af[src, :] * enter_f[q] + bb[src, :] + ab[src, :] * enter_b[q]
        rows = slice(q * seg, (q + 1) * seg)
        o_ref[0, rows, :] = (h * _gelu_tanh(yr_ref[0, rows, :])).astype(BF16)


def _rglru(xr, xrc, yr, conv_w, conv_b, w_gates, b_gates, lam):
    b, s, w = xr.shape
    c = xrc.shape[1]
    nb = w // LANES
    slab = lambda n: pl.BlockSpec((1, n, LANES), lambda bi, j: (bi, 0, j))
    scr = lambda n: pltpu.VMEM((n, LANES), F32)
    vmem = (3 * 2 + 5) * s * LANES * 4 + 8 * s * LANES * 4
    return pl.pallas_call(
        functools.partial(_rglru_kernel, s=s, c=c),
        grid=(b, nb),
        in_specs=[slab(s), slab(c), slab(s),
                  pl.BlockSpec((CONV_W, LANES), lambda bi, j: (0, j)),
                  pl.BlockSpec((1, LANES), lambda bi, j: (0, j)),
                  pl.BlockSpec((1, LANES, 4 * LANES), lambda bi, j: (j, 0, 0)),
                  pl.BlockSpec((1, 4 * LANES), lambda bi, j: (0, j)),
                  pl.BlockSpec((2, LANES), lambda bi, j: (0, j))],
        out_specs=slab(s),
        out_shape=jax.ShapeDtypeStruct((b, s, w), BF16),
        scratch_shapes=[scr(s + 2 * CONV_PAD), scr(c + 2 * CONV_PAD),
                        scr(s), scr(s), scr(s), scr(s), scr(c), scr(c), scr(c), scr(c)],
        compiler_params=pltpu.CompilerParams(dimension_semantics=("arbitrary", "arbitrary"),
                                             vmem_limit_bytes=_vmem_limit(vmem)),
        name="rglru",
    )(xr, xrc, yr, conv_w, conv_b, w_gates, b_gates, lam)


def _pack_bf16_pair(lo, hi):
    lo_bits = lax.bitcast_convert_type(lo.astype(BF16).astype(F32), I32)
    hi_bits = lax.bitcast_convert_type(hi.astype(BF16).astype(F32), I32)
    return lax.shift_right_logical(lo_bits, 16) | (hi_bits & jnp.int32(-65536))


def _unpack_bf16_pair(w):
    lo = lax.bitcast_convert_type(lax.shift_left(w, 16), F32).astype(BF16)
    hi = lax.bitcast_convert_type(w & jnp.int32(-65536), F32).astype(BF16)
    return lo, hi


def _merge_kernel(att_ref, rec_ref, x_ref, mods_ref, ga_ref, gr_ref, wo_ref, g2_ref, wr_ref, br_ref,
                  x1_ref, h2_ref, idx_ref, gate_ref, mask_ref, *, d):
    row = pl.program_id(0)
    gt1 = mods_ref[pl.ds(row, 1), 2 * d:3 * d]
    sh2 = mods_ref[pl.ds(row, 1), 3 * d:4 * d]
    sc2 = mods_ref[pl.ds(row, 1), 4 * d:5 * d]
    def rows_chain(rows):
        an = _rms(att_ref[0, rows, :].astype(F32), ga_ref[...]).astype(BF16)
        rn = _rms(rec_ref[0, rows, :].astype(F32), gr_ref[...]).astype(BF16)
        mix = (jnp.dot(an, wo_ref[0:ATT_WIDTH, :], preferred_element_type=F32)
               + jnp.dot(rn, wo_ref[ATT_WIDTH:d, :], preferred_element_type=F32))
        x1 = x_ref[0, rows, :] + gt1 * mix
        x1_ref[0, rows, :] = x1
        h2 = _rms(x1, g2_ref[...]) * (1.0 + sc2) + sh2
        half = d // 2
        h2_ref[rows, :] = _pack_bf16_pair(h2[:, 0:half], h2[:, half:d])
        h_hi = h2.astype(BF16)
        h_lo = (h2 - h_hi.astype(F32)).astype(BF16)
        w_split = wr_ref[...]
        part = (jnp.dot(h_hi, w_split, preferred_element_type=F32)
                + jnp.dot(h_lo, w_split, preferred_element_type=F32))
        logits = part[:, 0:N_EXPERTS] + part[:, N_EXPERTS:2 * N_EXPERTS] + br_ref[...]
        n = logits.shape[0]
        lane = lax.broadcasted_iota(I32, (n, N_EXPERTS), 1).astype(F32)
        col = lax.broadcasted_iota(I32, (n, TOP_K), 1)
        idx = jnp.zeros((n, TOP_K), F32)
        ex = jnp.zeros((n, TOP_K), F32)
        mask = jnp.zeros((n, N_EXPERTS), F32)
        rest = logits
        top = None
        for k in range(TOP_K):
            m = jnp.max(rest, axis=-1, keepdims=True)
            first = jnp.min(jnp.where(rest == m, lane, float(N_EXPERTS)), axis=-1, keepdims=True)
            sel = lane == first
            if k == 0:
                top = m
            idx = jnp.where(col == k, first, idx)
            ex = jnp.where(col == k, jnp.exp(m - top), ex)
            mask = jnp.where(sel, 1.0, mask)
            rest = jnp.where(sel, -jnp.inf, rest)
        idx_ref[rows, :] = idx.astype(I32)
        gate_ref[rows, :] = ex / jnp.sum(ex, axis=-1, keepdims=True)
        mask_ref[rows, :] = mask

    tm = x_ref.shape[1]
    n_chains = 2 if tm % (2 * SUBLANES) == 0 else 1
    for ci in range(n_chains):
        rows_chain(slice(ci * (tm // n_chains), (ci + 1) * (tm // n_chains)))


def _merge(att, rec, x, mods, g_att, g_rec, w_out_bf, g2, w_router, b_router, *, tm):
    b, s, d = x.shape
    t = b * s
    nt = s // tm
    rec_w = d - ATT_WIDTH
    row3 = lambda w: pl.BlockSpec((1, tm, w), lambda bi, i: (bi, i, 0))
    tok2 = lambda w: pl.BlockSpec((tm, w), lambda bi, i: (bi * nt + i, 0))
    const = lambda shape, **kw: pl.BlockSpec(shape, lambda bi, i: (0,) * len(shape), **kw)
    vmem = d * d * 2 + 2 * tm * (ATT_WIDTH + rec_w) * 2 + 4 * tm * d * 4 + tm * d * 4 + 8 * tm * d * 4
    return pl.pallas_call(
        functools.partial(_merge_kernel, d=d),
        grid=(b, nt),
        in_specs=[row3(ATT_WIDTH), row3(rec_w), row3(d), const(mods.shape),
                  const((1, ATT_WIDTH)), const((1, rec_w)),
                  const((d, d), pipeline_mode=pl.Buffered(1)), const((1, d)),
                  const((d, 2 * N_EXPERTS)), const((1, N_EXPERTS))],
        out_specs=[row3(d), tok2(d // 2), tok2(TOP_K), tok2(TOP_K), tok2(N_EXPERTS)],
        out_shape=[jax.ShapeDtypeStruct((b, s, d), F32), jax.ShapeDtypeStruct((t, d // 2), I32),
                   jax.ShapeDtypeStruct((t, TOP_K), I32), jax.ShapeDtypeStruct((t, TOP_K), F32),
                   jax.ShapeDtypeStruct((t, N_EXPERTS), F32)],
        compiler_params=pltpu.CompilerParams(dimension_semantics=("arbitrary", "arbitrary"),
                                             vmem_limit_bytes=_vmem_limit(vmem)),
        name="merge",
    )(att, rec, x, mods, g_att.reshape(1, -1), g_rec.reshape(1, -1), w_out_bf, g2.reshape(1, d),
      _split_bf16(w_router), b_router.reshape(1, N_EXPERTS))


def _split_bf16(w):
    hi = w.astype(BF16)
    lo = (w - hi.astype(F32)).astype(BF16)
    return jnp.concatenate([hi, lo], axis=1)


def _rank_kernel(mask_ref, idx_ref, rank_ref, cnt_ref, carry):
    @pl.when(pl.program_id(0) == 0)
    def _():
        carry[...] = jnp.zeros_like(carry)

    m = mask_ref[...]
    tb = m.shape[0]
    r = lax.broadcasted_iota(I32, (tb, tb), 0)
    cidx = lax.broadcasted_iota(I32, (tb, tb), 1)
    tri = jnp.where(cidx < r, 1.0, 0.0).astype(BF16)
    before = jnp.dot(tri, m.astype(BF16), preferred_element_type=F32) + carry[...]
    lane = lax.broadcasted_iota(I32, (tb, N_EXPERTS), 1)
    col = lax.broadcasted_iota(I32, (tb, TOP_K), 1)
    idx = idx_ref[...]
    rank = jnp.zeros((tb, TOP_K), F32)
    for k in range(TOP_K):
        pick = jnp.sum(jnp.where(lane == idx[:, k:k + 1], before, 0.0), axis=-1, keepdims=True)
        rank = jnp.where(col == k, pick, rank)
    rank_ref[...] = rank.astype(I32)
    carry[...] = carry[...] + jnp.sum(m, axis=0, keepdims=True)
    cnt_ref[...] = carry[...].astype(I32)


def _rank(mask, idx, *, tb):
    t = mask.shape[0]
    return pl.pallas_call(
        _rank_kernel,
        grid=(t // tb,),
        in_specs=[pl.BlockSpec((tb, N_EXPERTS), lambda i: (i, 0)), pl.BlockSpec((tb, TOP_K), lambda i: (i, 0))],
        out_specs=[pl.BlockSpec((tb, TOP_K), lambda i: (i, 0)), pl.BlockSpec((1, N_EXPERTS), lambda i: (0, 0))],
        out_shape=[jax.ShapeDtypeStruct((t, TOP_K), I32), jax.ShapeDtypeStruct((1, N_EXPERTS), I32)],
        scratch_shapes=[pltpu.VMEM((1, N_EXPERTS), F32)],
        compiler_params=pltpu.CompilerParams(dimension_semantics=("arbitrary",)),
        name="rank",
    )(mask, idx)


def _dispatch_kernel(dest_ref, fill_ref, h2_ref, xs_ref, zeros, sem, fill_sem, *, tb, n_blocks):
    base = pl.program_id(0) * tb

    @pl.when(pl.program_id(0) == 0)
    def _():
        zeros[...] = jnp.zeros_like(zeros)

        def zero_row(r):
            return pltpu.make_async_copy(zeros.at[pl.ds(0, 1)], xs_ref.at[pl.ds(r, 1)], fill_sem)

        def zero_block(bk):
            rows = pl.ds(pl.multiple_of(bk * MOE_BLOCK, MOE_BLOCK), MOE_BLOCK)
            return pltpu.make_async_copy(zeros, xs_ref.at[rows], fill_sem)

        def zero_rows(start, n):
            return pltpu.make_async_copy(zeros.at[pl.ds(0, n)],
                                         xs_ref.at[pl.ds(pl.multiple_of(start, SUBLANES), n)], fill_sem)

        def for_all_fills(act):
            for e in range(N_EXPERTS):
                lo = fill_ref[e]
                hi = fill_ref[N_EXPERTS + e]
                lo8 = jnp.minimum((lo + SUBLANES - 1) // SUBLANES * SUBLANES, hi)
                lax.fori_loop(lo, lo8, lambda r, c: (act(zero_row(r)), c)[1], 0)
                rest = hi - lo8
                pos = lo8
                size = MOE_BLOCK // 2
                while size >= SUBLANES:
                    take = rest & size

                    @pl.when(take != 0)
                    def _(pos=pos, size=size):
                        act(zero_rows(pos, size))

                    pos = pos + take
                    size //= 2
            lax.fori_loop(fill_ref[2 * N_EXPERTS], n_blocks, lambda bk, c: (act(zero_block(bk)), c)[1], 0)

        for_all_fills(lambda cp: cp.start())
        for_all_fills(lambda cp: cp.wait())

    def row_copy(i, k):
        return pltpu.make_async_copy(h2_ref.at[pl.ds(i, 1)],
                                     xs_ref.at[pl.ds(dest_ref[(base + i) * TOP_K + k], 1)], sem)

    def body(i, _):
        for k in range(TOP_K):
            row_copy(i, k).start(priority=k % 2)
        return 0

    lax.fori_loop(0, tb, body, 0, unroll=4)
    for k in range(TOP_K):
        pltpu.make_async_copy(h2_ref, xs_ref.at[pl.ds(0, tb)], sem).wait()


def _dispatch(dest_flat, fill, h2p, *, tb, n_blocks):
    t, wd = h2p.shape
    return pl.pallas_call(
        functools.partial(_dispatch_kernel, tb=tb, n_blocks=n_blocks),
        grid_spec=pltpu.PrefetchScalarGridSpec(
            num_scalar_prefetch=2, grid=(t // tb,),
            in_specs=[pl.BlockSpec((tb, wd), lambda i, dr, fl: (i, 0))],
            out_specs=pl.BlockSpec(memory_space=pl.ANY),
            scratch_shapes=[pltpu.VMEM((MOE_BLOCK, wd), I32), pltpu.SemaphoreType.DMA(()),
                            pltpu.SemaphoreType.DMA(())]),
        out_shape=jax.ShapeDtypeStruct((n_blocks * MOE_BLOCK, wd), I32),
        compiler_params=pltpu.CompilerParams(dimension_semantics=("arbitrary",), has_side_effects=True),
        name="dispatch",
    )(dest_flat, fill, h2p)


PAIR_CHUNK = 512


def _last_real_block(s, meta):
    return jnp.maximum(jnp.minimum(s, meta[0] - 1), 0)


def _stream_expert_weights(ridx_ref, run_e_ref, meta_ref, w_hbm, wf32, wbf, sem):
    j = pl.program_id(0)
    s = pl.program_id(1)
    tn = wbf.shape[1]
    n_runs = meta_ref[1]
    r = ridx_ref[s]
    first = (s < meta_ref[0]) & ((s == 0) | (r != ridx_ref[jnp.maximum(s - 1, 0)]))

    def tile_copy(run, sweep):
        cols = pl.ds(pl.multiple_of(sweep * tn, tn), tn)
        return pltpu.make_async_copy(w_hbm.at[run_e_ref[run], :, cols], wf32, sem)

    @pl.when(first)
    def _():
        @pl.when((j == 0) & (r == 0))
        def _():
            tile_copy(0, 0).start()

        tile_copy(r, j).wait()
        wbf[...] = wf32[...].astype(BF16)
        more_runs = r + 1 < n_runs

        @pl.when(more_runs)
        def _():
            tile_copy(r + 1, j).start()

        @pl.when(jnp.logical_not(more_runs) & (j + 1 < pl.num_programs(0)))
        def _():
            tile_copy(0, j + 1).start()


def _expert_stream_scratch(k, tn):
    return [pltpu.VMEM((k, tn), F32), pltpu.VMEM((k, tn), BF16), pltpu.SemaphoreType.DMA(())]


def _for_real_rows(rows_ref, meta_ref, out_ref, compute_rows):
    s = pl.program_id(1)
    real = s < meta_ref[0]
    half = MOE_BLOCK // 2
    upper_half_used = rows_ref[s] > half

    @pl.when(real & upper_half_used)
    def _():
        compute_rows(MOE_BLOCK)

    @pl.when(real & jnp.logical_not(upper_half_used))
    def _():
        compute_rows(half)
        out_ref[half:, :] = jnp.zeros((half, out_ref.shape[1]), out_ref.dtype)

    @pl.when(jnp.logical_not(real))
    def _():
        out_ref[...] = jnp.zeros_like(out_ref)


def _gu_kernel(be_ref, ridx_ref, run_e_ref, meta_ref, rows_ref, xs_ref, w_hbm, b_ref, act_ref, wf32, wbf, sem):
    del be_ref
    _stream_expert_weights(ridx_ref, run_e_ref, meta_ref, w_hbm, wf32, wbf, sem)
    tn = wbf.shape[1]

    def compute_rows(n):
        lo, hi = _unpack_bf16_pair(xs_ref[0:n, :])
        xb = jnp.concatenate([lo, hi], axis=1)
        gu = jnp.dot(xb, wbf[...], preferred_element_type=F32) + b_ref[0]
        g = jnp.minimum(gu, SWIGLU_LIMIT)
        up1 = jnp.clip(gu, -SWIGLU_LIMIT, SWIGLU_LIMIT) + 1.0
        paired = (pltpu.roll(up1, tn - 1, 1) * (g * jax.nn.sigmoid(SWIGLU_ALPHA * g))).astype(BF16)
        r = lax.broadcasted_iota(I32, (PAIR_CHUNK, PAIR_CHUNK // 2), 0)
        c = lax.broadcasted_iota(I32, (PAIR_CHUNK, PAIR_CHUNK // 2), 1)
        sel = jnp.where(r == 2 * c, 1.0, 0.0).astype(BF16)
        for ch in range(tn // PAIR_CHUNK):
            act_ref[0:n, ch * (PAIR_CHUNK // 2):(ch + 1) * (PAIR_CHUNK // 2)] = jnp.dot(
                paired[:, ch * PAIR_CHUNK:(ch + 1) * PAIR_CHUNK], sel, preferred_element_type=F32).astype(BF16)

    _for_real_rows(rows_ref, meta_ref, act_ref, compute_rows)


def _moe_gu(sched, xs, w_gu, b_gu, *, tn):
    p, half = xs.shape
    d = 2 * half
    f2 = w_gu.shape[2]
    nj = f2 // tn
    nblk = p // MOE_BLOCK
    blk = _last_real_block
    vmem = d * tn * 4 + d * tn * 2 + 2 * MOE_BLOCK * half * 4 + 2 * MOE_BLOCK * tn + 8 * MOE_BLOCK * tn * 4
    return pl.pallas_call(
        _gu_kernel,
        grid_spec=pltpu.PrefetchScalarGridSpec(
            num_scalar_prefetch=5, grid=(nj, nblk),
            in_specs=[pl.BlockSpec((MOE_BLOCK, half), lambda j, s, be, ri, re, mt, rw: (blk(s, mt), 0)),
                      pl.BlockSpec(memory_space=pl.ANY),
                      pl.BlockSpec((1, 1, tn), lambda j, s, be, ri, re, mt, rw: (be[blk(s, mt)], 0, j))],
            out_specs=pl.BlockSpec((MOE_BLOCK, tn // 2), lambda j, s, be, ri, re, mt, rw: (s, j)),
            scratch_shapes=_expert_stream_scratch(d, tn)),
        out_shape=jax.ShapeDtypeStruct((p, f2 // 2), BF16),
        compiler_params=pltpu.CompilerParams(dimension_semantics=("arbitrary", "arbitrary"),
                                             vmem_limit_bytes=_vmem_limit(vmem)),
        name="moe_gu",
    )(*sched, xs, w_gu, b_gu)


def _down_kernel(be_ref, ridx_ref, run_e_ref, meta_ref, rows_ref, act_ref, w_hbm, b_ref, y_ref, wf32, wbf, sem):
    del be_ref
    _stream_expert_weights(ridx_ref, run_e_ref, meta_ref, w_hbm, wf32, wbf, sem)

    def compute_rows(n):
        y = jnp.dot(act_ref[0:n, :], wbf[...], preferred_element_type=F32) + b_ref[0]
        half = y.shape[1] // 2
        y_ref[0:n, :] = _pack_bf16_pair(y[:, 0:half], y[:, half:2 * half])

    _for_real_rows(rows_ref, meta_ref, y_ref, compute_rows)


def _moe_down(sched, act, w_down, b_down):
    p, f = act.shape
    d = w_down.shape[2]
    nblk = p // MOE_BLOCK
    blk = _last_real_block
    vmem = f * d * 4 + f * d * 2 + 2 * MOE_BLOCK * f * 2 + 5 * MOE_BLOCK * d * 4
    return pl.pallas_call(
        _down_kernel,
        grid_spec=pltpu.PrefetchScalarGridSpec(
            num_scalar_prefetch=5, grid=(1, nblk),
            in_specs=[pl.BlockSpec((MOE_BLOCK, f), lambda j, s, be, ri, re, mt, rw: (blk(s, mt), 0)),
                      pl.BlockSpec(memory_space=pl.ANY),
                      pl.BlockSpec((1, 1, d), lambda j, s, be, ri, re, mt, rw: (be[blk(s, mt)], 0, 0))],
            out_specs=pl.BlockSpec((MOE_BLOCK, d // 2), lambda j, s, be, ri, re, mt, rw: (s, 0)),
            scratch_shapes=_expert_stream_scratch(f, d)),
        out_shape=jax.ShapeDtypeStruct((p, d // 2), I32),
        compiler_params=pltpu.CompilerParams(dimension_semantics=("arbitrary", "arbitrary"),
                                             vmem_limit_bytes=_vmem_limit(vmem)),
        name="moe_down",
    )(*sched, act, w_down, b_down)


def _combine_kernel(dest_ref, ys_ref, x1_ref, gate_ref, mods_ref, o_ref, buf, sem, *, tb, d, nt, n_tiles):
    i = pl.program_id(0)
    slot = lax.rem(i, 2)

    def row_copy(tile, sl, r, k):
        return pltpu.make_async_copy(ys_ref.at[pl.ds(dest_ref[(tile * tb + r) * TOP_K + k], 1)],
                                     buf.at[sl, k, pl.ds(r, 1)], sem.at[sl])

    def gather(tile, sl):
        def body(r, _):
            for k in range(TOP_K):
                row_copy(tile, sl, r, k).start(priority=k % 2)
            return 0
        lax.fori_loop(0, tb, body, 0, unroll=4)

    @pl.when(i == 0)
    def _():
        gather(0, 0)

    @pl.when(i + 1 < n_tiles)
    def _():
        gather(i + 1, 1 - slot)

    for k in range(TOP_K):
        pltpu.make_async_copy(ys_ref.at[pl.ds(0, tb)], buf.at[slot, k], sem.at[slot]).wait()
    half = d // 2
    gate = gate_ref[...]
    y_lo = jnp.zeros((tb, half), F32)
    y_hi = jnp.zeros((tb, half), F32)
    for k in range(TOP_K):
        w = buf[slot, k]
        gk = gate[:, k:k + 1]
        y_lo = y_lo + lax.bitcast_convert_type(lax.shift_left(w, 16), F32) * gk
        y_hi = y_hi + lax.bitcast_convert_type(w & jnp.int32(-65536), F32) * gk
    gt2 = mods_ref[pl.ds(i // nt, 1), 5 * d:6 * d]
    o_ref[:, 0:half] = x1_ref[:, 0:half] + gt2[:, 0:half] * y_lo
    o_ref[:, half:d] = x1_ref[:, half:d] + gt2[:, half:d] * y_hi


def _combine(dest_flat, ys, x1, gate, mods, *, tb, seq):
    t, d = x1.shape
    n_tiles = t // tb
    nt = seq // tb
    vmem = 2 * TOP_K * tb * d * 2 + 4 * tb * d * 4 + 6 * tb * d * 4
    return pl.pallas_call(
        functools.partial(_combine_kernel, tb=tb, d=d, nt=nt, n_tiles=n_tiles),
        grid_spec=pltpu.PrefetchScalarGridSpec(
            num_scalar_prefetch=1, grid=(n_tiles,),
            in_specs=[pl.BlockSpec(memory_space=pl.ANY),
                      pl.BlockSpec((tb, d), lambda i, dr: (i, 0)),
                      pl.BlockSpec((tb, TOP_K), lambda i, dr: (i, 0)),
                      pl.BlockSpec(mods.shape, lambda i, dr: (0, 0))],
            out_specs=pl.BlockSpec((tb, d), lambda i, dr: (i, 0)),
            scratch_shapes=[pltpu.VMEM((2, TOP_K, tb, d // 2), I32), pltpu.SemaphoreType.DMA((2,))]),
        out_shape=jax.ShapeDtypeStruct((t, d), F32),
        compiler_params=pltpu.CompilerParams(dimension_semantics=("arbitrary",),
                                             vmem_limit_bytes=_vmem_limit(vmem)),
        name="combine",
    )(dest_flat, ys, x1, gate, mods)


def _rope_table(s):
    rows = s // GRID_W
    row = np.repeat(np.arange(rows), GRID_W).astype(np.float32)
    col = np.tile(np.arange(GRID_W), rows).astype(np.float32)
    inv = (np.float32(ROPE_THETA) ** (-np.arange(ROPE_FREQS, dtype=np.float32) / np.float32(ROPE_FREQS))).astype(np.float32)
    ang_r = row[:, None] * inv
    ang_c = col[:, None] * inv
    z = np.zeros_like(ang_r)
    cosf = np.concatenate([np.cos(ang_r)] * 2 + [np.cos(ang_c)] * 2, axis=1)
    sneg = np.concatenate([-np.sin(ang_r), z, -np.sin(ang_c), z], axis=1)
    spos = np.concatenate([z, np.sin(ang_r), z, np.sin(ang_c)], axis=1)
    return jnp.asarray(np.concatenate([cosf, sneg, spos], axis=1), dtype=F32)


def kernel(x, c, ctx, c_ctx, w_mod, b_mod, g_norm1, w_in, g_q, g_k, conv_w, conv_b, w_gate_a, b_gate_a,
           w_gate_x, b_gate_x, lru_lambda, g_att_out, g_rec_out, w_out, g_norm2, w_router, b_router,
           w_gate_up, b_gate_up, w_down, b_down):
    b, s, d = x.shape
    cl = ctx.shape[1]
    t = b * s
    assert w_mod.shape[0] == 1, "single-layer kernel"
    assert b + 1 <= SUBLANES and d - ATT_WIDTH == REC_BLOCKS * LANES
    assert s % (SCAN_SEGMENTS * SUBLANES) == 0 and cl % (SCAN_SEGMENTS * SUBLANES) == 0

    ctx_row = b
    c8 = jnp.zeros((SUBLANES, d), F32).at[:b].set(c).at[ctx_row].set(c_ctx)
    mods = _mod(c8, w_mod[0], b_mod[0])

    w_in_bf = w_in[0].astype(BF16)
    tm = min(512, s)
    q, k, v, xr, yr = _inproj(x, mods, g_norm1[0], w_in_bf, _rope_table(s), g_q[0], g_k[0],
                              latent=True, ctx_row=ctx_row, tm=tm)
    kc, vc, xrc = _inproj(ctx, mods, g_norm1[0], w_in_bf, None, g_q[0], g_k[0],
                          latent=False, ctx_row=ctx_row, tm=cl)

    att = _attention(q, kc, k, vc, v, tq=min(256, s))

    w_gates = jnp.concatenate([w_gate_a[0, 0], w_gate_x[0, 0], w_gate_a[0, 1], w_gate_x[0, 1]], axis=-1).astype(BF16)
    rw = d - ATT_WIDTH
    bias = lambda bb: bb.reshape(REC_BLOCKS, LANES)
    b_gates = jnp.concatenate([bias(b_gate_a[0, 0]), bias(b_gate_x[0, 0]), bias(b_gate_a[0, 1]), bias(b_gate_x[0, 1])],
                              axis=-1).reshape(1, 4 * rw)
    rec = _rglru(xr, xrc, yr, conv_w[0], conv_b[0].reshape(1, rw), w_gates, b_gates, lru_lambda[0])

    x1, h2p, top_idx, gate, mask = _merge(att, rec, x, mods, g_att_out[0], g_rec_out[0], w_out[0].astype(BF16),
                                          g_norm2[0], w_router[0], b_router[0], tm=tm)

    rank, counts = _rank(mask, top_idx, tb=MOE_BLOCK)
    counts = counts[0]
    padded = (counts + MOE_BLOCK - 1) // MOE_BLOCK * MOE_BLOCK
    pad_ends = jnp.cumsum(padded)
    pad_starts = pad_ends - padded
    dest = (pad_starts[top_idx] + rank).reshape(t * TOP_K)
    n_blocks = (t * TOP_K + N_EXPERTS * (MOE_BLOCK - 1) + MOE_BLOCK - 1) // MOE_BLOCK
    block_start = jnp.arange(n_blocks, dtype=I32) * MOE_BLOCK
    block_e = jnp.minimum(jnp.sum((pad_ends[None, :] <= block_start[:, None]).astype(I32), axis=1), N_EXPERTS - 1)
    n_valid = (pad_ends[N_EXPERTS - 1] // MOE_BLOCK).reshape(1).astype(I32)
    fill = jnp.concatenate([pad_starts + counts, pad_ends, n_valid]).astype(I32)

    xs = _dispatch(dest, fill, h2p, tb=MOE_BLOCK, n_blocks=n_blocks)

    used = padded > 0
    run_of_expert = jnp.cumsum(used.astype(I32)) - 1
    experts = jnp.arange(N_EXPERTS, dtype=I32)
    run_e = jnp.sum(jnp.where(used[None, :] & (run_of_expert[None, :] == experts[:, None]), experts[None, :], 0), axis=1)
    meta = jnp.concatenate([n_valid, jnp.sum(used.astype(I32)).reshape(1)])
    block_rows = jnp.clip((pad_starts + counts)[block_e] - block_start, 0, MOE_BLOCK).astype(I32)
    sched = (block_e, run_of_expert[block_e], run_e.astype(I32), meta, block_rows)

    f2 = w_gate_up.shape[3]
    act = _moe_gu(sched, xs, w_gate_up[0], b_gate_up[0].reshape(N_EXPERTS, 1, f2), tn=2048)
    ys = _moe_down(sched, act, w_down[0], b_down[0].reshape(N_EXPERTS, 1, d))

    return _combine(dest, ys, x1.reshape(t, d), gate, mods, tb=256, seq=s).reshape(b, s, d)
```

```python
import functools
import math

import jax
import jax.numpy as jnp
import numpy as np
from jax import lax
from jax.experimental import pallas as pl
from jax.experimental.pallas import tpu as pltpu

F32 = jnp.float32
BF16 = jnp.bfloat16
I32 = jnp.int32

EPS = 1e-6
GRID_W = 64
HEAD_DIM = 128
N_Q_HEADS = 8
N_KV_HEADS = 2
GROUP = N_Q_HEADS // N_KV_HEADS
ATT_WIDTH = N_Q_HEADS * HEAD_DIM
KV_WIDTH = N_KV_HEADS * HEAD_DIM
ROPE_THETA = 10000.0
ROPE_FREQS = HEAD_DIM // 4
REC_BLOCKS = 8
CONV_W = 4
CONV_LEFT = 2
LRU_C = 8.0
N_EXPERTS = 32
TOP_K = 4
SWIGLU_LIMIT = 7.0
SWIGLU_ALPHA = 1.702
MOE_BLOCK = 512

V7X_VMEM_BYTES = 64 * 1024 * 1024
SUBLANES = 8
LANES = 128
SCAN_SEGMENTS = SUBLANES
CONV_PAD = SUBLANES

HIGHEST = lax.Precision.HIGHEST


def _vmem_limit(nbytes):
    return int(min(V7X_VMEM_BYTES - 4 * 1024 * 1024, max(nbytes, 16 * 1024 * 1024)))


def _rms(x, g):
    return x * lax.rsqrt(jnp.mean(x * x, axis=-1, keepdims=True) + EPS) * g


def _mod_kernel(c_ref, w_ref, b_ref, o_ref):
    c = c_ref[...]
    a = c * jax.nn.sigmoid(c)
    o_ref[...] = jnp.dot(a, w_ref[...], preferred_element_type=F32, precision=HIGHEST) + b_ref[...]


def _mod(c8, w_mod, b_mod):
    d, n = w_mod.shape
    tn = 2048
    return pl.pallas_call(
        _mod_kernel,
        grid=(n // tn,),
        in_specs=[pl.BlockSpec((SUBLANES, d), lambda j: (0, 0)),
                  pl.BlockSpec((d, tn), lambda j: (0, j)),
                  pl.BlockSpec((1, tn), lambda j: (0, j))],
        out_specs=pl.BlockSpec((SUBLANES, tn), lambda j: (0, j)),
        out_shape=jax.ShapeDtypeStruct((SUBLANES, n), F32),
        compiler_params=pltpu.CompilerParams(dimension_semantics=("arbitrary",),
                                             vmem_limit_bytes=_vmem_limit(3 * d * tn * 4)),
        name="mod",
    )(c8, w_mod, b_mod.reshape(1, n))


def _qk_norm_rope(y, g, rope):
    yn = _rms(y, g)
    if rope is None:
        return yn
    cosf, sneg, spos = rope
    return yn * cosf + pltpu.roll(yn, HEAD_DIM - ROPE_FREQS, 1) * sneg + pltpu.roll(yn, ROPE_FREQS, 1) * spos


def _inproj_kernel(*refs, d, latent, ctx_row):
    if latent:
        (x_ref, mods_ref, g1_ref, w_ref, rope_ref, gq_ref, gk_ref,
         q_ref, k_ref, v_ref, xr_ref, yr_ref) = refs
        row = pl.program_id(0)
    else:
        x_ref, mods_ref, g1_ref, w_ref, gk_ref, k_ref, v_ref, xr_ref = refs
        row = ctx_row
    sh = mods_ref[pl.ds(row, 1), 0:d]
    sc = mods_ref[pl.ds(row, 1), d:2 * d]
    h = _rms(x_ref[0], g1_ref[...]) * (1.0 + sc) + sh
    hb = h.astype(BF16)

    def proj(lo, hi):
        return jnp.dot(hb, w_ref[:, lo:hi], preferred_element_type=F32)

    o_k = ATT_WIDTH
    o_v = o_k + KV_WIDTH
    o_xr = o_v + KV_WIDTH
    rec_w = d - ATT_WIDTH
    o_yr = o_xr + rec_w
    rope = None
    if latent:
        rp = rope_ref[...]
        rope = (rp[:, 0:HEAD_DIM], rp[:, HEAD_DIM:2 * HEAD_DIM], rp[:, 2 * HEAD_DIM:3 * HEAD_DIM])
        q = proj(0, ATT_WIDTH)
        for hd in range(N_Q_HEADS):
            sl = slice(hd * HEAD_DIM, (hd + 1) * HEAD_DIM)
            q_ref[0, :, sl] = (_qk_norm_rope(q[:, sl], gq_ref[...], rope) * (HEAD_DIM ** -0.5)).astype(BF16)
    k = proj(o_k, o_v)
    for hd in range(N_KV_HEADS):
        sl = slice(hd * HEAD_DIM, (hd + 1) * HEAD_DIM)
        k_ref[0, :, sl] = _qk_norm_rope(k[:, sl], gk_ref[...], rope).astype(BF16)
    v = proj(o_v, o_xr).astype(BF16)
    for hd in range(N_KV_HEADS):
        v_ref[0, :, 2 * hd * HEAD_DIM:(2 * hd + 1) * HEAD_DIM] = v[:, hd * HEAD_DIM:(hd + 1) * HEAD_DIM]
        v_ref[0, :, (2 * hd + 1) * HEAD_DIM:(2 * hd + 2) * HEAD_DIM] = jnp.ones((v.shape[0], HEAD_DIM), BF16)
    xr_ref[0] = proj(o_xr, o_yr)
    if latent:
        yr_ref[0] = proj(o_yr, o_yr + rec_w)


def _inproj(x, mods, g1, w_in_bf, rope_tab, g_q, g_k, *, latent, ctx_row, tm):
    b, s, d = x.shape
    n = w_in_bf.shape[1]
    rec_w = d - ATT_WIDTH
    grid = (b, s // tm)
    row_spec = lambda w: pl.BlockSpec((1, tm, w), lambda bi, i: (bi, i, 0))
    full2 = lambda a: pl.BlockSpec(a.shape, lambda bi, i: (0, 0))
    in_specs = [row_spec(d), full2(mods), pl.BlockSpec((1, d), lambda bi, i: (0, 0)),
                pl.BlockSpec((d, n), lambda bi, i: (0, 0), pipeline_mode=pl.Buffered(1))]
    args = [x, mods, g1.reshape(1, d), w_in_bf]
    out_specs, out_shape = [], []
    if latent:
        in_specs += [pl.BlockSpec((tm, 3 * HEAD_DIM), lambda bi, i: (i, 0)),
                     pl.BlockSpec((1, HEAD_DIM), lambda bi, i: (0, 0))]
        args += [rope_tab, g_q.reshape(1, HEAD_DIM)]
        out_specs.append(row_spec(ATT_WIDTH))
        out_shape.append(jax.ShapeDtypeStruct((b, s, ATT_WIDTH), BF16))
    in_specs.append(pl.BlockSpec((1, HEAD_DIM), lambda bi, i: (0, 0)))
    args.append(g_k.reshape(1, HEAD_DIM))
    out_specs += [row_spec(KV_WIDTH), row_spec(2 * KV_WIDTH), row_spec(rec_w)]
    out_shape += [jax.ShapeDtypeStruct((b, s, KV_WIDTH), BF16), jax.ShapeDtypeStruct((b, s, 2 * KV_WIDTH), BF16),
                  jax.ShapeDtypeStruct((b, s, rec_w), F32)]
    if latent:
        out_specs.append(row_spec(rec_w))
        out_shape.append(jax.ShapeDtypeStruct((b, s, rec_w), F32))
    vmem = d * n * 2 + 2 * tm * d * 4 + 2 * tm * n * 4 + 3 * tm * d * 4 + 2 * tm * n * 4
    return pl.pallas_call(
        functools.partial(_inproj_kernel, d=d, latent=latent, ctx_row=ctx_row),
        grid=grid, in_specs=in_specs, out_specs=out_specs, out_shape=out_shape,
        compiler_params=pltpu.CompilerParams(dimension_semantics=("arbitrary", "arbitrary"),
                                             vmem_limit_bytes=_vmem_limit(vmem)),
        name="inproj_latent" if latent else "inproj_ctx",
    )(*args)


def _attn_kernel(q_ref, kc_ref, k_ref, vc_ref, v_ref, o_ref, kbuf, vbuf):
    s_lat = k_ref.shape[1]

    @pl.when(pl.program_id(2) == 0)
    def _():
        kbuf[0:s_lat, :] = k_ref[0]
        kbuf[s_lat:, :] = kc_ref[0]
        vbuf[0:s_lat, :] = v_ref[0]
        vbuf[s_lat:, :] = vc_ref[0]

    k = kbuf[...]
    v = vbuf[...]
    for g in range(GROUP):
        sl = slice(g * HEAD_DIM, (g + 1) * HEAD_DIM)
        s = lax.dot_general(q_ref[0, :, sl], k, (((1,), (1,)), ((), ())), preferred_element_type=F32)
        m = jnp.max(s, axis=-1, keepdims=True)
        o = jnp.dot(jnp.exp(s - m).astype(BF16), v, preferred_element_type=F32)
        o_ref[0, :, sl] = (o[:, 0:HEAD_DIM] / o[:, HEAD_DIM:2 * HEAD_DIM]).astype(BF16)


def _attention(q, kc, k, vc, v, *, tq):
    b, s, _ = q.shape
    lk = s + kc.shape[1]
    gw = GROUP * HEAD_DIM
    vmem = 3 * lk * HEAD_DIM * 2 * 3 + 4 * tq * gw * 2 + 4 * tq * lk * 4
    kv = lambda a, w: pl.BlockSpec((1, a.shape[1], w), lambda bi, h, i: (bi, 0, h))
    return pl.pallas_call(
        _attn_kernel,
        grid=(b, N_KV_HEADS, s // tq),
        in_specs=[pl.BlockSpec((1, tq, gw), lambda bi, h, i: (bi, i, h)),
                  kv(kc, HEAD_DIM), kv(k, HEAD_DIM), kv(vc, 2 * HEAD_DIM), kv(v, 2 * HEAD_DIM)],
        out_specs=pl.BlockSpec((1, tq, gw), lambda bi, h, i: (bi, i, h)),
        out_shape=jax.ShapeDtypeStruct((b, s, ATT_WIDTH), BF16),
        scratch_shapes=[pltpu.VMEM((lk, HEAD_DIM), BF16), pltpu.VMEM((lk, 2 * HEAD_DIM), BF16)],
        compiler_params=pltpu.CompilerParams(dimension_semantics=("arbitrary",) * 3,
                                             vmem_limit_bytes=_vmem_limit(vmem)),
        name="attention",
    )(q, kc, k, vc, v)


def _gelu_tanh(x):
    return 0.5 * x * (1.0 + jnp.tanh(math.sqrt(2.0 / math.pi) * (x + 0.044715 * x * x * x)))


def _rglru_kernel(xr_ref, xc_ref, yr_ref, cw_ref, cb_ref, wg_ref, bg_ref, lam_ref, o_ref,
                  xp, xpc, af, bf, ab, bb, caf, cbf, cab, cbb, *, s, c):
    nseg = SCAN_SEGMENTS
    seg = s // nseg
    cseg = c // nseg
    zeros_pad = jnp.zeros((CONV_PAD, LANES), F32)
    xp[0:CONV_PAD, :] = zeros_pad
    xp[CONV_PAD + s:2 * CONV_PAD + s, :] = zeros_pad
    xp[CONV_PAD:CONV_PAD + s, :] = xr_ref[0]
    xpc[0:CONV_PAD, :] = zeros_pad
    xpc[CONV_PAD + c:2 * CONV_PAD + c, :] = zeros_pad
    xpc[CONV_PAD:CONV_PAD + c, :] = xc_ref[0]

    cw = cw_ref[...]
    cb = cb_ref[...]
    wg = wg_ref[0]
    bg = bg_ref[...]
    sp = jax.nn.softplus(-lam_ref[...])

    def coeffs(src, lo, n):
        u = cb
        for j in range(CONV_W):
            u = u + src[CONV_PAD + lo + j - CONV_LEFT:CONV_PAD + lo + j - CONV_LEFT + n, :] * cw[j:j + 1, :]
        g = jnp.dot(u.astype(BF16), wg, preferred_element_type=F32) + bg
        out = []
        for r in range(2):
            ga = g[:, (2 * r) * LANES:(2 * r + 1) * LANES]
            gx = g[:, (2 * r + 1) * LANES:(2 * r + 2) * LANES]
            log_a = (-LRU_C) * jax.nn.sigmoid(ga) * sp[r:r + 1, :]
            a = jnp.exp(log_a)
            mult = jnp.sqrt(-jnp.tanh(log_a) * (1.0 + a * a))
            out.append((a, mult * jax.nn.sigmoid(gx) * u))
        return out

    (a0, b0), (a1, b1) = coeffs(xpc, 0, c)
    for q in range(nseg):
        rows = slice(q * cseg, (q + 1) * cseg)
        dst = pl.ds(q, cseg, stride=nseg)
        caf[dst, :] = a0[rows]
        cbf[dst, :] = b0[rows]
        cab[dst, :] = a1[rows]
        cbb[dst, :] = b1[rows]
    for q in range(nseg):
        (a0, b0), (a1, b1) = coeffs(xp, q * seg, seg)
        dst = pl.ds(q, seg, stride=nseg)
        af[dst, :] = a0
        bf[dst, :] = b0
        ab[dst, :] = a1
        bb[dst, :] = b1

    def scan(a_f, b_f, a_b, b_b, n, store):
        def body(j, carry):
            hf, pf, hb, pb = carry
            rf = pl.multiple_of(j * nseg, nseg)
            rb = pl.multiple_of((n - 1 - j) * nseg, nseg)
            av = a_f[pl.ds(rf, nseg), :]
            hf = av * hf + b_f[pl.ds(rf, nseg), :]
            pf = av * pf
            aw = a_b[pl.ds(rb, nseg), :]
            hb = aw * hb + b_b[pl.ds(rb, nseg), :]
            pb = aw * pb
            if store:
                a_f[pl.ds(rf, nseg), :] = pf
                b_f[pl.ds(rf, nseg), :] = hf
                a_b[pl.ds(rb, nseg), :] = pb
                b_b[pl.ds(rb, nseg), :] = hb
            return hf, pf, hb, pb
        z = jnp.zeros((nseg, LANES), F32)
        o = jnp.ones((nseg, LANES), F32)
        return lax.fori_loop(0, n, body, (z, o, z, o), unroll=8)

    def chain(h_end, p_end, h0, reverse):
        order = range(nseg - 1, -1, -1) if reverse else range(nseg)
        enter = [None] * nseg
        cur = h0
        for q in order:
            enter[q] = cur
            cur = h_end[q:q + 1, :] + p_end[q:q + 1, :] * cur
        return enter, cur

    zero_row = jnp.zeros((1, LANES), F32)
    hf, pf, hb, pb = scan(caf, cbf, cab, cbb, cseg, False)
    _, h0f = chain(hf, pf, zero_row, False)
    _, h0b = chain(hb, pb, zero_row, True)
    hf, pf, hb, pb = scan(af, bf, ab, bb, seg, True)
    enter_f, _ = chain(hf, pf, h0f, False)
    enter_b, _ = chain(hb, pb, h0b, True)
    for q in range(nseg):
        src = pl.ds(q, seg, stride=nseg)
        h = bf[src, :] + af[src, :] * enter_f[q] + bb[src, :] + ab[src, :] * enter_b[q]
        rows = slice(q * seg, (q + 1) * seg)
        o_ref[0, rows, :] = (h * _gelu_tanh(yr_ref[0, rows, :])).astype(BF16)


def _rglru(xr, xrc, yr, conv_w, conv_b, w_gates, b_gates, lam):
    b, s, w = xr.shape
    c = xrc.shape[1]
    nb = w // LANES
    slab = lambda n: pl.BlockSpec((1, n, LANES), lambda bi, j: (bi, 0, j))
    scr = lambda n: pltpu.VMEM((n, LANES), F32)
    vmem = (3 * 2 + 5) * s * LANES * 4 + 8 * s * LANES * 4
    return pl.pallas_call(
        functools.partial(_rglru_kernel, s=s, c=c),
        grid=(b, nb),
        in_specs=[slab(s), slab(c), slab(s),
                  pl.BlockSpec((CONV_W, LANES), lambda bi, j: (0, j)),
                  pl.BlockSpec((1, LANES), lambda bi, j: (0, j)),
                  pl.BlockSpec((1, LANES, 4 * LANES), lambda bi, j: (j, 0, 0)),
                  pl.BlockSpec((1, 4 * LANES), lambda bi, j: (0, j)),
                  pl.BlockSpec((2, LANES), lambda bi, j: (0, j))],
        out_specs=slab(s),
        out_shape=jax.ShapeDtypeStruct((b, s, w), BF16),
        scratch_shapes=[scr(s + 2 * CONV_PAD), scr(c + 2 * CONV_PAD),
                        scr(s), scr(s), scr(s), scr(s), scr(c), scr(c), scr(c), scr(c)],
        compiler_params=pltpu.CompilerParams(dimension_semantics=("arbitrary", "arbitrary"),
                                             vmem_limit_bytes=_vmem_limit(vmem)),
        name="rglru",
    )(xr, xrc, yr, conv_w, conv_b, w_gates, b_gates, lam)


def _pack_bf16_pair(lo, hi):
    lo_bits = lax.bitcast_convert_type(lo.astype(BF16).astype(F32), I32)
    hi_bits = lax.bitcast_convert_type(hi.astype(BF16).astype(F32), I32)
    return lax.shift_right_logical(lo_bits, 16) | (hi_bits & jnp.int32(-65536))


def _unpack_bf16_pair(w):
    lo = lax.bitcast_convert_type(lax.shift_left(w, 16), F32).astype(BF16)
    hi = lax.bitcast_convert_type(w & jnp.int32(-65536), F32).astype(BF16)
    return lo, hi


def _merge_kernel(att_ref, rec_ref, x_ref, mods_ref, ga_ref, gr_ref, wo_ref, g2_ref, wr_ref, br_ref,
                  x1_ref, h2_ref, idx_ref, gate_ref, mask_ref, *, d):
    row = pl.program_id(0)
    gt1 = mods_ref[pl.ds(row, 1), 2 * d:3 * d]
    sh2 = mods_ref[pl.ds(row, 1), 3 * d:4 * d]
    sc2 = mods_ref[pl.ds(row, 1), 4 * d:5 * d]
    def rows_chain(rows):
        an = _rms(att_ref[0, rows, :].astype(F32), ga_ref[...]).astype(BF16)
        rn = _rms(rec_ref[0, rows, :].astype(F32), gr_ref[...]).astype(BF16)
        mix = (jnp.dot(an, wo_ref[0:ATT_WIDTH, :], preferred_element_type=F32)
               + jnp.dot(rn, wo_ref[ATT_WIDTH:d, :], preferred_element_type=F32))
        x1 = x_ref[0, rows, :] + gt1 * mix
        x1_ref[0, rows, :] = x1
        h2 = _rms(x1, g2_ref[...]) * (1.0 + sc2) + sh2
        half = d // 2
        h2_ref[rows, :] = _pack_bf16_pair(h2[:, 0:half], h2[:, half:d])
        h_hi = h2.astype(BF16)
        h_lo = (h2 - h_hi.astype(F32)).astype(BF16)
        w_split = wr_ref[...]
        part = (jnp.dot(h_hi, w_split, preferred_element_type=F32)
                + jnp.dot(h_lo, w_split, preferred_element_type=F32))
        logits = part[:, 0:N_EXPERTS] + part[:, N_EXPERTS:2 * N_EXPERTS] + br_ref[...]
        n = logits.shape[0]
        lane = lax.broadcasted_iota(I32, (n, N_EXPERTS), 1).astype(F32)
        col = lax.broadcasted_iota(I32, (n, TOP_K), 1)
        idx = jnp.zeros((n, TOP_K), F32)
        ex = jnp.zeros((n, TOP_K), F32)
        mask = jnp.zeros((n, N_EXPERTS), F32)
        rest = logits
        top = None
        for k in range(TOP_K):
            m = jnp.max(rest, axis=-1, keepdims=True)
            first = jnp.min(jnp.where(rest == m, lane, float(N_EXPERTS)), axis=-1, keepdims=True)
            sel = lane == first
            if k == 0:
                top = m
            idx = jnp.where(col == k, first, idx)
            ex = jnp.where(col == k, jnp.exp(m - top), ex)
            mask = jnp.where(sel, 1.0, mask)
            rest = jnp.where(sel, -jnp.inf, rest)
        idx_ref[rows, :] = idx.astype(I32)
        gate_ref[rows, :] = ex / jnp.sum(ex, axis=-1, keepdims=True)
        mask_ref[rows, :] = mask

    tm = x_ref.shape[1]
    n_chains = 2 if tm % (2 * SUBLANES) == 0 else 1
    for ci in range(n_chains):
        rows_chain(slice(ci * (tm // n_chains), (ci + 1) * (tm // n_chains)))


def _merge(att, rec, x, mods, g_att, g_rec, w_out_bf, g2, w_router, b_router, *, tm):
    b, s, d = x.shape
    t = b * s
    nt = s // tm
    rec_w = d - ATT_WIDTH
    row3 = lambda w: pl.BlockSpec((1, tm, w), lambda bi, i: (bi, i, 0))
    tok2 = lambda w: pl.BlockSpec((tm, w), lambda bi, i: (bi * nt + i, 0))
    const = lambda shape, **kw: pl.BlockSpec(shape, lambda bi, i: (0,) * len(shape), **kw)
    vmem = d * d * 2 + 2 * tm * (ATT_WIDTH + rec_w) * 2 + 4 * tm * d * 4 + tm * d * 4 + 8 * tm * d * 4
    return pl.pallas_call(
        functools.partial(_merge_kernel, d=d),
        grid=(b, nt),
        in_specs=[row3(ATT_WIDTH), row3(rec_w), row3(d), const(mods.shape),
                  const((1, ATT_WIDTH)), const((1, rec_w)),
                  const((d, d), pipeline_mode=pl.Buffered(1)), const((1, d)),
                  const((d, 2 * N_EXPERTS)), const((1, N_EXPERTS))],
        out_specs=[row3(d), tok2(d // 2), tok2(TOP_K), tok2(TOP_K), tok2(N_EXPERTS)],
        out_shape=[jax.ShapeDtypeStruct((b, s, d), F32), jax.ShapeDtypeStruct((t, d // 2), I32),
                   jax.ShapeDtypeStruct((t, TOP_K), I32), jax.ShapeDtypeStruct((t, TOP_K), F32),
                   jax.ShapeDtypeStruct((t, N_EXPERTS), F32)],
        compiler_params=pltpu.CompilerParams(dimension_semantics=("arbitrary", "arbitrary"),
                                             vmem_limit_bytes=_vmem_limit(vmem)),
        name="merge",
    )(att, rec, x, mods, g_att.reshape(1, -1), g_rec.reshape(1, -1), w_out_bf, g2.reshape(1, d),
      _split_bf16(w_router), b_router.reshape(1, N_EXPERTS))


def _split_bf16(w):
    hi = w.astype(BF16)
    lo = (w - hi.astype(F32)).astype(BF16)
    return jnp.concatenate([hi, lo], axis=1)


def _rank_kernel(mask_ref, idx_ref, rank_ref, cnt_ref, carry):
    @pl.when(pl.program_id(0) == 0)
    def _():
        carry[...] = jnp.zeros_like(carry)

    m = mask_ref[...]
    tb = m.shape[0]
    r = lax.broadcasted_iota(I32, (tb, tb), 0)
    cidx = lax.broadcasted_iota(I32, (tb, tb), 1)
    tri = jnp.where(cidx < r, 1.0, 0.0).astype(BF16)
    before = jnp.dot(tri, m.astype(BF16), preferred_element_type=F32) + carry[...]
    lane = lax.broadcasted_iota(I32, (tb, N_EXPERTS), 1)
    col = lax.broadcasted_iota(I32, (tb, TOP_K), 1)
    idx = idx_ref[...]
    rank = jnp.zeros((tb, TOP_K), F32)
    for k in range(TOP_K):
        pick = jnp.sum(jnp.where(lane == idx[:, k:k + 1], before, 0.0), axis=-1, keepdims=True)
        rank = jnp.where(col == k, pick, rank)
    rank_ref[...] = rank.astype(I32)
    carry[...] = carry[...] + jnp.sum(m, axis=0, keepdims=True)
    cnt_ref[...] = carry[...].astype(I32)


def _rank(mask, idx, *, tb):
    t = mask.shape[0]
    return pl.pallas_call(
        _rank_kernel,
        grid=(t // tb,),
        in_specs=[pl.BlockSpec((tb, N_EXPERTS), lambda i: (i, 0)), pl.BlockSpec((tb, TOP_K), lambda i: (i, 0))],
        out_specs=[pl.BlockSpec((tb, TOP_K), lambda i: (i, 0)), pl.BlockSpec((1, N_EXPERTS), lambda i: (0, 0))],
        out_shape=[jax.ShapeDtypeStruct((t, TOP_K), I32), jax.ShapeDtypeStruct((1, N_EXPERTS), I32)],
        scratch_shapes=[pltpu.VMEM((1, N_EXPERTS), F32)],
        compiler_params=pltpu.CompilerParams(dimension_semantics=("arbitrary",)),
        name="rank",
    )(mask, idx)


def _dispatch_kernel(dest_ref, fill_ref, h2_ref, xs_ref, zeros, sem, fill_sem, *, tb, n_blocks):
    base = pl.program_id(0) * tb

    @pl.when(pl.program_id(0) == 0)
    def _():
        zeros[...] = jnp.zeros_like(zeros)

        def zero_row(r):
            return pltpu.make_async_copy(zeros.at[pl.ds(0, 1)], xs_ref.at[pl.ds(r, 1)], fill_sem)

        def zero_block(bk):
            rows = pl.ds(pl.multiple_of(bk * MOE_BLOCK, MOE_BLOCK), MOE_BLOCK)
            return pltpu.make_async_copy(zeros, xs_ref.at[rows], fill_sem)

        def zero_rows(start, n):
            return pltpu.make_async_copy(zeros.at[pl.ds(0, n)],
                                         xs_ref.at[pl.ds(pl.multiple_of(start, SUBLANES), n)], fill_sem)

        def for_all_fills(act):
            for e in range(N_EXPERTS):
                lo = fill_ref[e]
                hi = fill_ref[N_EXPERTS + e]
                lo8 = jnp.minimum((lo + SUBLANES - 1) // SUBLANES * SUBLANES, hi)
                lax.fori_loop(lo, lo8, lambda r, c: (act(zero_row(r)), c)[1], 0)
                rest = hi - lo8
                pos = lo8
                size = MOE_BLOCK // 2
                while size >= SUBLANES:
                    take = rest & size

                    @pl.when(take != 0)
                    def _(pos=pos, size=size):
                        act(zero_rows(pos, size))

                    pos = pos + take
                    size //= 2
            lax.fori_loop(fill_ref[2 * N_EXPERTS], n_blocks, lambda bk, c: (act(zero_block(bk)), c)[1], 0)

        for_all_fills(lambda cp: cp.start())
        for_all_fills(lambda cp: cp.wait())

    def row_copy(i, k):
        return pltpu.make_async_copy(h2_ref.at[pl.ds(i, 1)],
                                     xs_ref.at[pl.ds(dest_ref[(base + i) * TOP_K + k], 1)], sem)

    def body(i, _):
        for k in range(TOP_K):
            row_copy(i, k).start(priority=k % 2)
        return 0

    lax.fori_loop(0, tb, body, 0, unroll=4)
    for k in range(TOP_K):
        pltpu.make_async_copy(h2_ref, xs_ref.at[pl.ds(0, tb)], sem).wait()


def _dispatch(dest_flat, fill, h2p, *, tb, n_blocks):
    t, wd = h2p.shape
    return pl.pallas_call(
        functools.partial(_dispatch_kernel, tb=tb, n_blocks=n_blocks),
        grid_spec=pltpu.PrefetchScalarGridSpec(
            num_scalar_prefetch=2, grid=(t // tb,),
            in_specs=[pl.BlockSpec((tb, wd), lambda i, dr, fl: (i, 0))],
            out_specs=pl.BlockSpec(memory_space=pl.ANY),
            scratch_shapes=[pltpu.VMEM((MOE_BLOCK, wd), I32), pltpu.SemaphoreType.DMA(()),
                            pltpu.SemaphoreType.DMA(())]),
        out_shape=jax.ShapeDtypeStruct((n_blocks * MOE_BLOCK, wd), I32),
        compiler_params=pltpu.CompilerParams(dimension_semantics=("arbitrary",), has_side_effects=True),
        name="dispatch",
    )(dest_flat, fill, h2p)


PAIR_CHUNK = 512
MOE_ROW_STEPS = 4


def _last_real_block(s, meta):
    return jnp.maximum(jnp.minimum(s, meta[0] - 1), 0)


def _stream_expert_weights(ridx_ref, run_e_ref, meta_ref, w_hbm, wf32, wbf, sem):
    j = pl.program_id(0)
    s = pl.program_id(1)
    tn = wbf.shape[1]
    n_runs = meta_ref[1]
    r = ridx_ref[s]
    first = (s < meta_ref[0]) & ((s == 0) | (r != ridx_ref[jnp.maximum(s - 1, 0)]))

    def tile_copy(run, sweep):
        cols = pl.ds(pl.multiple_of(sweep * tn, tn), tn)
        return pltpu.make_async_copy(w_hbm.at[run_e_ref[run], :, cols], wf32, sem)

    @pl.when(first)
    def _():
        @pl.when((j == 0) & (r == 0))
        def _():
            tile_copy(0, 0).start()

        tile_copy(r, j).wait()
        wbf[...] = wf32[...].astype(BF16)
        more_runs = r + 1 < n_runs

        @pl.when(more_runs)
        def _():
            tile_copy(r + 1, j).start()

        @pl.when(jnp.logical_not(more_runs) & (j + 1 < pl.num_programs(0)))
        def _():
            tile_copy(0, j + 1).start()


def _expert_stream_scratch(k, tn):
    return [pltpu.VMEM((k, tn), F32), pltpu.VMEM((k, tn), BF16), pltpu.SemaphoreType.DMA(())]


def _for_real_rows(rows_ref, meta_ref, out_ref, compute_rows):
    s = pl.program_id(1)
    real = s < meta_ref[0]
    quarter = MOE_BLOCK // MOE_ROW_STEPS
    used_quarters = (rows_ref[s] + quarter - 1) // quarter

    for q in range(1, MOE_ROW_STEPS + 1):
        @pl.when(real & (used_quarters == q))
        def _(n=q * quarter):
            compute_rows(n)
            if n < MOE_BLOCK:
                out_ref[n:, :] = jnp.zeros((MOE_BLOCK - n, out_ref.shape[1]), out_ref.dtype)

    @pl.when(jnp.logical_not(real))
    def _():
        out_ref[...] = jnp.zeros_like(out_ref)


def _gu_kernel(be_ref, ridx_ref, run_e_ref, meta_ref, rows_ref, xs_ref, w_hbm, b_ref, act_ref, wf32, wbf, sem):
    del be_ref
    _stream_expert_weights(ridx_ref, run_e_ref, meta_ref, w_hbm, wf32, wbf, sem)
    tn = wbf.shape[1]

    def compute_rows(n):
        lo, hi = _unpack_bf16_pair(xs_ref[0:n, :])
        xb = jnp.concatenate([lo, hi], axis=1)
        gu = jnp.dot(xb, wbf[...], preferred_element_type=F32) + b_ref[0]
        g = jnp.minimum(gu, SWIGLU_LIMIT)
        up1 = jnp.clip(gu, -SWIGLU_LIMIT, SWIGLU_LIMIT) + 1.0
        paired = (pltpu.roll(up1, tn - 1, 1) * (g * jax.nn.sigmoid(SWIGLU_ALPHA * g))).astype(BF16)
        r = lax.broadcasted_iota(I32, (PAIR_CHUNK, PAIR_CHUNK // 2), 0)
        c = lax.broadcasted_iota(I32, (PAIR_CHUNK, PAIR_CHUNK // 2), 1)
        sel = jnp.where(r == 2 * c, 1.0, 0.0).astype(BF16)
        for ch in range(tn // PAIR_CHUNK):
            act_ref[0:n, ch * (PAIR_CHUNK // 2):(ch + 1) * (PAIR_CHUNK // 2)] = jnp.dot(
                paired[:, ch * PAIR_CHUNK:(ch + 1) * PAIR_CHUNK], sel, preferred_element_type=F32).astype(BF16)

    _for_real_rows(rows_ref, meta_ref, act_ref, compute_rows)


def _moe_gu(sched, xs, w_gu, b_gu, *, tn):
    p, half = xs.shape
    d = 2 * half
    f2 = w_gu.shape[2]
    nj = f2 // tn
    nblk = p // MOE_BLOCK
    blk = _last_real_block
    vmem = d * tn * 4 + d * tn * 2 + 2 * MOE_BLOCK * half * 4 + 2 * MOE_BLOCK * tn + 8 * MOE_BLOCK * tn * 4
    return pl.pallas_call(
        _gu_kernel,
        grid_spec=pltpu.PrefetchScalarGridSpec(
            num_scalar_prefetch=5, grid=(nj, nblk),
            in_specs=[pl.BlockSpec((MOE_BLOCK, half), lambda j, s, be, ri, re, mt, rw: (blk(s, mt), 0)),
                      pl.BlockSpec(memory_space=pl.ANY),
                      pl.BlockSpec((1, 1, tn), lambda j, s, be, ri, re, mt, rw: (be[blk(s, mt)], 0, j))],
            out_specs=pl.BlockSpec((MOE_BLOCK, tn // 2), lambda j, s, be, ri, re, mt, rw: (s, j)),
            scratch_shapes=_expert_stream_scratch(d, tn)),
        out_shape=jax.ShapeDtypeStruct((p, f2 // 2), BF16),
        compiler_params=pltpu.CompilerParams(dimension_semantics=("arbitrary", "arbitrary"),
                                             vmem_limit_bytes=_vmem_limit(vmem)),
        name="moe_gu",
    )(*sched, xs, w_gu, b_gu)


def _down_kernel(be_ref, ridx_ref, run_e_ref, meta_ref, rows_ref, act_ref, w_hbm, b_ref, y_ref, wf32, wbf, sem):
    del be_ref
    _stream_expert_weights(ridx_ref, run_e_ref, meta_ref, w_hbm, wf32, wbf, sem)

    def compute_rows(n):
        y = jnp.dot(act_ref[0:n, :], wbf[...], preferred_element_type=F32) + b_ref[0]
        half = y.shape[1] // 2
        y_ref[0:n, :] = _pack_bf16_pair(y[:, 0:half], y[:, half:2 * half])

    _for_real_rows(rows_ref, meta_ref, y_ref, compute_rows)


def _moe_down(sched, act, w_down, b_down):
    p, f = act.shape
    d = w_down.shape[2]
    nblk = p // MOE_BLOCK
    blk = _last_real_block
    vmem = f * d * 4 + f * d * 2 + 2 * MOE_BLOCK * f * 2 + 5 * MOE_BLOCK * d * 4
    return pl.pallas_call(
        _down_kernel,
        grid_spec=pltpu.PrefetchScalarGridSpec(
            num_scalar_prefetch=5, grid=(1, nblk),
            in_specs=[pl.BlockSpec((MOE_BLOCK, f), lambda j, s, be, ri, re, mt, rw: (blk(s, mt), 0)),
                      pl.BlockSpec(memory_space=pl.ANY),
                      pl.BlockSpec((1, 1, d), lambda j, s, be, ri, re, mt, rw: (be[blk(s, mt)], 0, 0))],
            out_specs=pl.BlockSpec((MOE_BLOCK, d // 2), lambda j, s, be, ri, re, mt, rw: (s, 0)),
            scratch_shapes=_expert_stream_scratch(f, d)),
        out_shape=jax.ShapeDtypeStruct((p, d // 2), I32),
        compiler_params=pltpu.CompilerParams(dimension_semantics=("arbitrary", "arbitrary"),
                                             vmem_limit_bytes=_vmem_limit(vmem)),
        name="moe_down",
    )(*sched, act, w_down, b_down)


def _combine_kernel(dest_ref, ys_ref, x1_ref, gate_ref, mods_ref, o_ref, buf, sem, *, tb, d, nt, n_tiles):
    i = pl.program_id(0)
    slot = lax.rem(i, 2)

    def row_copy(tile, sl, r, k):
        return pltpu.make_async_copy(ys_ref.at[pl.ds(dest_ref[(tile * tb + r) * TOP_K + k], 1)],
                                     buf.at[sl, k, pl.ds(r, 1)], sem.at[sl])

    def gather(tile, sl):
        def body(j, _):
            r0 = pl.multiple_of(j * SUBLANES, SUBLANES)
            for u in range(SUBLANES):
                for k in range(TOP_K):
                    row_copy(tile, sl, r0 + u, k).start(priority=k % 2)
            return 0
        lax.fori_loop(0, tb // SUBLANES, body, 0)

    @pl.when(i == 0)
    def _():
        gather(0, 0)

    for next_slot in range(2):
        @pl.when((i + 1 < n_tiles) & (slot != next_slot))
        def _():
            gather(i + 1, next_slot)

    for k in range(TOP_K):
        pltpu.make_async_copy(ys_ref.at[pl.ds(0, tb)], buf.at[slot, k], sem.at[slot]).wait()
    half = d // 2
    gate = gate_ref[...]
    y_lo = jnp.zeros((tb, half), F32)
    y_hi = jnp.zeros((tb, half), F32)
    for k in range(TOP_K):
        w = buf[slot, k]
        gk = gate[:, k:k + 1]
        y_lo = y_lo + lax.bitcast_convert_type(lax.shift_left(w, 16), F32) * gk
        y_hi = y_hi + lax.bitcast_convert_type(w & jnp.int32(-65536), F32) * gk
    gt2 = mods_ref[pl.ds(i // nt, 1), 5 * d:6 * d]
    o_ref[:, 0:half] = x1_ref[:, 0:half] + gt2[:, 0:half] * y_lo
    o_ref[:, half:d] = x1_ref[:, half:d] + gt2[:, half:d] * y_hi


def _combine(dest_flat, ys, x1, gate, mods, *, tb, seq):
    t, d = x1.shape
    n_tiles = t // tb
    nt = seq // tb
    vmem = 2 * TOP_K * tb * d * 2 + 4 * tb * d * 4 + 6 * tb * d * 4
    return pl.pallas_call(
        functools.partial(_combine_kernel, tb=tb, d=d, nt=nt, n_tiles=n_tiles),
        grid_spec=pltpu.PrefetchScalarGridSpec(
            num_scalar_prefetch=1, grid=(n_tiles,),
            in_specs=[pl.BlockSpec(memory_space=pl.ANY),
                      pl.BlockSpec((tb, d), lambda i, dr: (i, 0)),
                      pl.BlockSpec((tb, TOP_K), lambda i, dr: (i, 0)),
                      pl.BlockSpec(mods.shape, lambda i, dr: (0, 0))],
            out_specs=pl.BlockSpec((tb, d), lambda i, dr: (i, 0)),
            scratch_shapes=[pltpu.VMEM((2, TOP_K, tb, d // 2), I32), pltpu.SemaphoreType.DMA((2,))]),
        out_shape=jax.ShapeDtypeStruct((t, d), F32),
        compiler_params=pltpu.CompilerParams(dimension_semantics=("arbitrary",),
                                             vmem_limit_bytes=_vmem_limit(vmem)),
        name="combine",
    )(dest_flat, ys, x1, gate, mods)


def _rope_table(s):
    rows = s // GRID_W
    row = np.repeat(np.arange(rows), GRID_W).astype(np.float32)
    col = np.tile(np.arange(GRID_W), rows).astype(np.float32)
    inv = (np.float32(ROPE_THETA) ** (-np.arange(ROPE_FREQS, dtype=np.float32) / np.float32(ROPE_FREQS))).astype(np.float32)
    ang_r = row[:, None] * inv
    ang_c = col[:, None] * inv
    z = np.zeros_like(ang_r)
    cosf = np.concatenate([np.cos(ang_r)] * 2 + [np.cos(ang_c)] * 2, axis=1)
    sneg = np.concatenate([-np.sin(ang_r), z, -np.sin(ang_c), z], axis=1)
    spos = np.concatenate([z, np.sin(ang_r), z, np.sin(ang_c)], axis=1)
    return jnp.asarray(np.concatenate([cosf, sneg, spos], axis=1), dtype=F32)


def kernel(x, c, ctx, c_ctx, w_mod, b_mod, g_norm1, w_in, g_q, g_k, conv_w, conv_b, w_gate_a, b_gate_a,
           w_gate_x, b_gate_x, lru_lambda, g_att_out, g_rec_out, w_out, g_norm2, w_router, b_router,
           w_gate_up, b_gate_up, w_down, b_down):
    b, s, d = x.shape
    cl = ctx.shape[1]
    t = b * s
    assert w_mod.shape[0] == 1, "single-layer kernel"
    assert b + 1 <= SUBLANES and d - ATT_WIDTH == REC_BLOCKS * LANES
    assert s % (SCAN_SEGMENTS * SUBLANES) == 0 and cl % (SCAN_SEGMENTS * SUBLANES) == 0

    ctx_row = b
    c8 = jnp.zeros((SUBLANES, d), F32).at[:b].set(c).at[ctx_row].set(c_ctx)
    mods = _mod(c8, w_mod[0], b_mod[0])

    w_in_bf = w_in[0].astype(BF16)
    tm = min(512, s)
    q, k, v, xr, yr = _inproj(x, mods, g_norm1[0], w_in_bf, _rope_table(s), g_q[0], g_k[0],
                              latent=True, ctx_row=ctx_row, tm=tm)
    kc, vc, xrc = _inproj(ctx, mods, g_norm1[0], w_in_bf, None, g_q[0], g_k[0],
                          latent=False, ctx_row=ctx_row, tm=cl)

    att = _attention(q, kc, k, vc, v, tq=min(256, s))

    w_gates = jnp.concatenate([w_gate_a[0, 0], w_gate_x[0, 0], w_gate_a[0, 1], w_gate_x[0, 1]], axis=-1).astype(BF16)
    rw = d - ATT_WIDTH
    bias = lambda bb: bb.reshape(REC_BLOCKS, LANES)
    b_gates = jnp.concatenate([bias(b_gate_a[0, 0]), bias(b_gate_x[0, 0]), bias(b_gate_a[0, 1]), bias(b_gate_x[0, 1])],
                              axis=-1).reshape(1, 4 * rw)
    rec = _rglru(xr, xrc, yr, conv_w[0], conv_b[0].reshape(1, rw), w_gates, b_gates, lru_lambda[0])

    x1, h2p, top_idx, gate, mask = _merge(att, rec, x, mods, g_att_out[0], g_rec_out[0], w_out[0].astype(BF16),
                                          g_norm2[0], w_router[0], b_router[0], tm=tm)

    rank, counts = _rank(mask, top_idx, tb=MOE_BLOCK)
    counts = counts[0]
    padded = (counts + MOE_BLOCK - 1) // MOE_BLOCK * MOE_BLOCK
    pad_ends = jnp.cumsum(padded)
    pad_starts = pad_ends - padded
    dest = (pad_starts[top_idx] + rank).reshape(t * TOP_K)
    n_blocks = (t * TOP_K + N_EXPERTS * (MOE_BLOCK - 1) + MOE_BLOCK - 1) // MOE_BLOCK
    block_start = jnp.arange(n_blocks, dtype=I32) * MOE_BLOCK
    block_e = jnp.minimum(jnp.sum((pad_ends[None, :] <= block_start[:, None]).astype(I32), axis=1), N_EXPERTS - 1)
    n_valid = (pad_ends[N_EXPERTS - 1] // MOE_BLOCK).reshape(1).astype(I32)
    fill = jnp.concatenate([pad_starts + counts, pad_ends, n_valid]).astype(I32)

    xs = _dispatch(dest, fill, h2p, tb=MOE_BLOCK, n_blocks=n_blocks)

    used = padded > 0
    run_of_expert = jnp.cumsum(used.astype(I32)) - 1
    experts = jnp.arange(N_EXPERTS, dtype=I32)
    run_e = jnp.sum(jnp.where(used[None, :] & (run_of_expert[None, :] == experts[:, None]), experts[None, :], 0), axis=1)
    meta = jnp.concatenate([n_valid, jnp.sum(used.astype(I32)).reshape(1)])
    block_rows = jnp.clip((pad_starts + counts)[block_e] - block_start, 0, MOE_BLOCK).astype(I32)
    sched = (block_e, run_of_expert[block_e], run_e.astype(I32), meta, block_rows)

    f2 = w_gate_up.shape[3]
    act = _moe_gu(sched, xs, w_gate_up[0], b_gate_up[0].reshape(N_EXPERTS, 1, f2), tn=2048)
    ys = _moe_down(sched, act, w_down[0], b_down[0].reshape(N_EXPERTS, 1, d))

    return _combine(dest, ys, x1.reshape(t, d), gate, mods, tb=256, seq=s).reshape(b, s, d)
```

```python
import functools
import math

import jax
import jax.numpy as jnp
import numpy as np
from jax import lax
from jax.experimental import pallas as pl
from jax.experimental.pallas import tpu as pltpu

F32 = jnp.float32
BF16 = jnp.bfloat16
I32 = jnp.int32

EPS = 1e-6
GRID_W = 64
HEAD_DIM = 128
N_Q_HEADS = 8
N_KV_HEADS = 2
GROUP = N_Q_HEADS // N_KV_HEADS
ATT_WIDTH = N_Q_HEADS * HEAD_DIM
KV_WIDTH = N_KV_HEADS * HEAD_DIM
ROPE_THETA = 10000.0
ROPE_FREQS = HEAD_DIM // 4
REC_BLOCKS = 8
CONV_W = 4
CONV_LEFT = 2
LRU_C = 8.0
N_EXPERTS = 32
TOP_K = 4
SWIGLU_LIMIT = 7.0
SWIGLU_ALPHA = 1.702
MOE_BLOCK = 512

V7X_VMEM_BYTES = 64 * 1024 * 1024
SUBLANES = 8
LANES = 128
SCAN_SEGMENTS = SUBLANES
CONV_PAD = SUBLANES

HIGHEST = lax.Precision.HIGHEST


def _vmem_limit(nbytes):
    return int(min(V7X_VMEM_BYTES - 4 * 1024 * 1024, max(nbytes, 16 * 1024 * 1024)))


def _rms(x, g):
    return x * lax.rsqrt(jnp.mean(x * x, axis=-1, keepdims=True) + EPS) * g


def _mod_kernel(c_ref, w_ref, b_ref, o_ref):
    c = c_ref[...]
    a = c * jax.nn.sigmoid(c)
    o_ref[...] = jnp.dot(a, w_ref[...], preferred_element_type=F32, precision=HIGHEST) + b_ref[...]


def _mod(c8, w_mod, b_mod):
    d, n = w_mod.shape
    tn = 2048
    return pl.pallas_call(
        _mod_kernel,
        grid=(n // tn,),
        in_specs=[pl.BlockSpec((SUBLANES, d), lambda j: (0, 0)),
                  pl.BlockSpec((d, tn), lambda j: (0, j)),
                  pl.BlockSpec((1, tn), lambda j: (0, j))],
        out_specs=pl.BlockSpec((SUBLANES, tn), lambda j: (0, j)),
        out_shape=jax.ShapeDtypeStruct((SUBLANES, n), F32),
        compiler_params=pltpu.CompilerParams(dimension_semantics=("arbitrary",),
                                             vmem_limit_bytes=_vmem_limit(3 * d * tn * 4)),
        name="mod",
    )(c8, w_mod, b_mod.reshape(1, n))


def _qk_norm_rope(y, g, rope):
    yn = _rms(y, g)
    if rope is None:
        return yn
    cosf, sneg, spos = rope
    return yn * cosf + pltpu.roll(yn, HEAD_DIM - ROPE_FREQS, 1) * sneg + pltpu.roll(yn, ROPE_FREQS, 1) * spos


def _inproj_kernel(*refs, d, latent, ctx_row):
    if latent:
        (x_ref, mods_ref, g1_ref, w_ref, rope_ref, gq_ref, gk_ref,
         q_ref, k_ref, v_ref, xr_ref, yr_ref) = refs
        row = pl.program_id(0)
    else:
        x_ref, mods_ref, g1_ref, w_ref, gk_ref, k_ref, v_ref, xr_ref = refs
        row = ctx_row
    sh = mods_ref[pl.ds(row, 1), 0:d]
    sc = mods_ref[pl.ds(row, 1), d:2 * d]
    h = _rms(x_ref[0], g1_ref[...]) * (1.0 + sc) + sh
    hb = h.astype(BF16)

    def proj(lo, hi):
        return jnp.dot(hb, w_ref[:, lo:hi], preferred_element_type=F32)

    o_k = ATT_WIDTH
    o_v = o_k + KV_WIDTH
    o_xr = o_v + KV_WIDTH
    rec_w = d - ATT_WIDTH
    o_yr = o_xr + rec_w
    rope = None
    if latent:
        rp = rope_ref[...]
        rope = (rp[:, 0:HEAD_DIM], rp[:, HEAD_DIM:2 * HEAD_DIM], rp[:, 2 * HEAD_DIM:3 * HEAD_DIM])
        q = proj(0, ATT_WIDTH)
        for hd in range(N_Q_HEADS):
            sl = slice(hd * HEAD_DIM, (hd + 1) * HEAD_DIM)
            q_ref[0, :, sl] = (_qk_norm_rope(q[:, sl], gq_ref[...], rope) * (HEAD_DIM ** -0.5)).astype(BF16)
    k = proj(o_k, o_v)
    for hd in range(N_KV_HEADS):
        sl = slice(hd * HEAD_DIM, (hd + 1) * HEAD_DIM)
        k_ref[0, :, sl] = _qk_norm_rope(k[:, sl], gk_ref[...], rope).astype(BF16)
    v = proj(o_v, o_xr).astype(BF16)
    for hd in range(N_KV_HEADS):
        v_ref[0, :, 2 * hd * HEAD_DIM:(2 * hd + 1) * HEAD_DIM] = v[:, hd * HEAD_DIM:(hd + 1) * HEAD_DIM]
        v_ref[0, :, (2 * hd + 1) * HEAD_DIM:(2 * hd + 2) * HEAD_DIM] = jnp.ones((v.shape[0], HEAD_DIM), BF16)
    xr_ref[0] = proj(o_xr, o_yr)
    if latent:
        yr_ref[0] = proj(o_yr, o_yr + rec_w)


def _inproj(x, mods, g1, w_in_bf, rope_tab, g_q, g_k, *, latent, ctx_row, tm):
    b, s, d = x.shape
    n = w_in_bf.shape[1]
    rec_w = d - ATT_WIDTH
    grid = (b, s // tm)
    row_spec = lambda w: pl.BlockSpec((1, tm, w), lambda bi, i: (bi, i, 0))
    full2 = lambda a: pl.BlockSpec(a.shape, lambda bi, i: (0, 0))
    in_specs = [row_spec(d), full2(mods), pl.BlockSpec((1, d), lambda bi, i: (0, 0)),
                pl.BlockSpec((d, n), lambda bi, i: (0, 0), pipeline_mode=pl.Buffered(1))]
    args = [x, mods, g1.reshape(1, d), w_in_bf]
    out_specs, out_shape = [], []
    if latent:
        in_specs += [pl.BlockSpec((tm, 3 * HEAD_DIM), lambda bi, i: (i, 0)),
                     pl.BlockSpec((1, HEAD_DIM), lambda bi, i: (0, 0))]
        args += [rope_tab, g_q.reshape(1, HEAD_DIM)]
        out_specs.append(row_spec(ATT_WIDTH))
        out_shape.append(jax.ShapeDtypeStruct((b, s, ATT_WIDTH), BF16))
    in_specs.append(pl.BlockSpec((1, HEAD_DIM), lambda bi, i: (0, 0)))
    args.append(g_k.reshape(1, HEAD_DIM))
    out_specs += [row_spec(KV_WIDTH), row_spec(2 * KV_WIDTH), row_spec(rec_w)]
    out_shape += [jax.ShapeDtypeStruct((b, s, KV_WIDTH), BF16), jax.ShapeDtypeStruct((b, s, 2 * KV_WIDTH), BF16),
                  jax.ShapeDtypeStruct((b, s, rec_w), F32)]
    if latent:
        out_specs.append(row_spec(rec_w))
        out_shape.append(jax.ShapeDtypeStruct((b, s, rec_w), F32))
    vmem = d * n * 2 + 2 * tm * d * 4 + 2 * tm * n * 4 + 3 * tm * d * 4 + 2 * tm * n * 4
    return pl.pallas_call(
        functools.partial(_inproj_kernel, d=d, latent=latent, ctx_row=ctx_row),
        grid=grid, in_specs=in_specs, out_specs=out_specs, out_shape=out_shape,
        compiler_params=pltpu.CompilerParams(dimension_semantics=("arbitrary", "arbitrary"),
                                             vmem_limit_bytes=_vmem_limit(vmem)),
        name="inproj_latent" if latent else "inproj_ctx",
    )(*args)


def _attn_kernel(q_ref, kc_ref, k_ref, vc_ref, v_ref, o_ref, kbuf, vbuf):
    s_lat = k_ref.shape[1]

    @pl.when(pl.program_id(2) == 0)
    def _():
        kbuf[0:s_lat, :] = k_ref[0]
        kbuf[s_lat:, :] = kc_ref[0]
        vbuf[0:s_lat, :] = v_ref[0]
        vbuf[s_lat:, :] = vc_ref[0]

    k = kbuf[...]
    v = vbuf[...]
    for g in range(GROUP):
        sl = slice(g * HEAD_DIM, (g + 1) * HEAD_DIM)
        s = lax.dot_general(q_ref[0, :, sl], k, (((1,), (1,)), ((), ())), preferred_element_type=F32)
        m = jnp.max(s, axis=-1, keepdims=True)
        o = jnp.dot(jnp.exp(s - m).astype(BF16), v, preferred_element_type=F32)
        o_ref[0, :, sl] = (o[:, 0:HEAD_DIM] / o[:, HEAD_DIM:2 * HEAD_DIM]).astype(BF16)


def _attention(q, kc, k, vc, v, *, tq):
    b, s, _ = q.shape
    lk = s + kc.shape[1]
    gw = GROUP * HEAD_DIM
    vmem = 3 * lk * HEAD_DIM * 2 * 3 + 4 * tq * gw * 2 + 4 * tq * lk * 4
    kv = lambda a, w: pl.BlockSpec((1, a.shape[1], w), lambda bi, h, i: (bi, 0, h))
    return pl.pallas_call(
        _attn_kernel,
        grid=(b, N_KV_HEADS, s // tq),
        in_specs=[pl.BlockSpec((1, tq, gw), lambda bi, h, i: (bi, i, h)),
                  kv(kc, HEAD_DIM), kv(k, HEAD_DIM), kv(vc, 2 * HEAD_DIM), kv(v, 2 * HEAD_DIM)],
        out_specs=pl.BlockSpec((1, tq, gw), lambda bi, h, i: (bi, i, h)),
        out_shape=jax.ShapeDtypeStruct((b, s, ATT_WIDTH), BF16),
        scratch_shapes=[pltpu.VMEM((lk, HEAD_DIM), BF16), pltpu.VMEM((lk, 2 * HEAD_DIM), BF16)],
        compiler_params=pltpu.CompilerParams(dimension_semantics=("arbitrary",) * 3,
                                             vmem_limit_bytes=_vmem_limit(vmem)),
        name="attention",
    )(q, kc, k, vc, v)


def _gelu_tanh(x):
    return 0.5 * x * (1.0 + jnp.tanh(math.sqrt(2.0 / math.pi) * (x + 0.044715 * x * x * x)))


def _rglru_kernel(xr_ref, xc_ref, yr_ref, cw_ref, cb_ref, wg_ref, bg_ref, lam_ref, o_ref,
                  xp, xpc, af, bf, ab, bb, caf, cbf, cab, cbb, *, s, c):
    nseg = SCAN_SEGMENTS
    seg = s // nseg
    cseg = c // nseg
    zeros_pad = jnp.zeros((CONV_PAD, LANES), F32)
    xp[0:CONV_PAD, :] = zeros_pad
    xp[CONV_PAD + s:2 * CONV_PAD + s, :] = zeros_pad
    xp[CONV_PAD:CONV_PAD + s, :] = xr_ref[0]
    xpc[0:CONV_PAD, :] = zeros_pad
    xpc[CONV_PAD + c:2 * CONV_PAD + c, :] = zeros_pad
    xpc[CONV_PAD:CONV_PAD + c, :] = xc_ref[0]

    cw = cw_ref[...]
    cb = cb_ref[...]
    wg = wg_ref[0]
    bg = bg_ref[...]
    sp = jax.nn.softplus(-lam_ref[...])

    def coeffs(src, lo, n):
        u = cb
        for j in range(CONV_W):
            u = u + src[CONV_PAD + lo + j - CONV_LEFT:CONV_PAD + lo + j - CONV_LEFT + n, :] * cw[j:j + 1, :]
        g = jnp.dot(u.astype(BF16), wg, preferred_element_type=F32) + bg
        out = []
        for r in range(2):
            ga = g[:, (2 * r) * LANES:(2 * r + 1) * LANES]
            gx = g[:, (2 * r + 1) * LANES:(2 * r + 2) * LANES]
            log_a = (-LRU_C) * jax.nn.sigmoid(ga) * sp[r:r + 1, :]
            a = jnp.exp(log_a)
            mult = jnp.sqrt(-jnp.tanh(log_a) * (1.0 + a * a))
            out.append((a, mult * jax.nn.sigmoid(gx) * u))
        return out

    (a0, b0), (a1, b1) = coeffs(xpc, 0, c)
    for q in range(nseg):
        rows = slice(q * cseg, (q + 1) * cseg)
        dst = pl.ds(q, cseg, stride=nseg)
        caf[dst, :] = a0[rows]
        cbf[dst, :] = b0[rows]
        cab[dst, :] = a1[rows]
        cbb[dst, :] = b1[rows]
    for q in range(nseg):
        (a0, b0), (a1, b1) = coeffs(xp, q * seg, seg)
        dst = pl.ds(q, seg, stride=nseg)
        af[dst, :] = a0
        bf[dst, :] = b0
        ab[dst, :] = a1
        bb[dst, :] = b1

    def scan(a_f, b_f, a_b, b_b, n, store):
        def two_steps(a_ref, b_ref, r1, r2, h, p):
            a1 = a_ref[pl.ds(r1, nseg), :]
            b1 = b_ref[pl.ds(r1, nseg), :]
            a2 = a_ref[pl.ds(r2, nseg), :]
            b2 = b_ref[pl.ds(r2, nseg), :]
            a12 = a2 * a1
            b12 = a2 * b1 + b2
            h1 = a1 * h + b1
            p1 = a1 * p
            h2 = a12 * h + b12
            p2 = a12 * p
            if store:
                a_ref[pl.ds(r1, nseg), :] = p1
                b_ref[pl.ds(r1, nseg), :] = h1
                a_ref[pl.ds(r2, nseg), :] = p2
                b_ref[pl.ds(r2, nseg), :] = h2
            return h2, p2

        def body(j, carry):
            hf, pf, hb, pb = carry
            rf = pl.multiple_of(2 * j * nseg, nseg)
            rb = pl.multiple_of((n - 1 - 2 * j) * nseg, nseg)
            hf, pf = two_steps(a_f, b_f, rf, rf + nseg, hf, pf)
            hb, pb = two_steps(a_b, b_b, rb, rb - nseg, hb, pb)
            return hf, pf, hb, pb
        z = jnp.zeros((nseg, LANES), F32)
        o = jnp.ones((nseg, LANES), F32)
        return lax.fori_loop(0, n // 2, body, (z, o, z, o), unroll=4)

    def chain(h_end, p_end, h0, reverse):
        order = range(nseg - 1, -1, -1) if reverse else range(nseg)
        enter = [None] * nseg
        cur = h0
        for q in order:
            enter[q] = cur
            cur = h_end[q:q + 1, :] + p_end[q:q + 1, :] * cur
        return enter, cur

    zero_row = jnp.zeros((1, LANES), F32)
    hf, pf, hb, pb = scan(caf, cbf, cab, cbb, cseg, False)
    _, h0f = chain(hf, pf, zero_row, False)
    _, h0b = chain(hb, pb, zero_row, True)
    hf, pf, hb, pb = scan(af, bf, ab, bb, seg, True)
    enter_f, _ = chain(hf, pf, h0f, False)
    enter_b, _ = chain(hb, pb, h0b, True)
    for q in range(nseg):
        src = pl.ds(q, seg, stride=nseg)
        h = bf[src, :] + af[src, :] * enter_f[q] + bb[src, :] + ab[src, :] * enter_b[q]
        rows = slice(q * seg, (q + 1) * seg)
        o_ref[0, rows, :] = (h * _gelu_tanh(yr_ref[0, rows, :])).astype(BF16)


def _rglru(xr, xrc, yr, conv_w, conv_b, w_gates, b_gates, lam):
    b, s, w = xr.shape
    c = xrc.shape[1]
    nb = w // LANES
    slab = lambda n: pl.BlockSpec((1, n, LANES), lambda bi, j: (bi, 0, j))
    scr = lambda n: pltpu.VMEM((n, LANES), F32)
    vmem = (3 * 2 + 5) * s * LANES * 4 + 8 * s * LANES * 4
    return pl.pallas_call(
        functools.partial(_rglru_kernel, s=s, c=c),
        grid=(b, nb),
        in_specs=[slab(s), slab(c), slab(s),
                  pl.BlockSpec((CONV_W, LANES), lambda bi, j: (0, j)),
                  pl.BlockSpec((1, LANES), lambda bi, j: (0, j)),
                  pl.BlockSpec((1, LANES, 4 * LANES), lambda bi, j: (j, 0, 0)),
                  pl.BlockSpec((1, 4 * LANES), lambda bi, j: (0, j)),
                  pl.BlockSpec((2, LANES), lambda bi, j: (0, j))],
        out_specs=slab(s),
        out_shape=jax.ShapeDtypeStruct((b, s, w), BF16),
        scratch_shapes=[scr(s + 2 * CONV_PAD), scr(c + 2 * CONV_PAD),
                        scr(s), scr(s), scr(s), scr(s), scr(c), scr(c), scr(c), scr(c)],
        compiler_params=pltpu.CompilerParams(dimension_semantics=("arbitrary", "arbitrary"),
                                             vmem_limit_bytes=_vmem_limit(vmem)),
        name="rglru",
    )(xr, xrc, yr, conv_w, conv_b, w_gates, b_gates, lam)


def _pack_bf16_pair(lo, hi):
    lo_bits = lax.bitcast_convert_type(lo.astype(BF16).astype(F32), I32)
    hi_bits = lax.bitcast_convert_type(hi.astype(BF16).astype(F32), I32)
    return lax.shift_right_logical(lo_bits, 16) | (hi_bits & jnp.int32(-65536))


def _unpack_bf16_pair(w):
    lo = lax.bitcast_convert_type(lax.shift_left(w, 16), F32).astype(BF16)
    hi = lax.bitcast_convert_type(w & jnp.int32(-65536), F32).astype(BF16)
    return lo, hi


def _merge_kernel(att_ref, rec_ref, x_ref, mods_ref, ga_ref, gr_ref, wo_ref, g2_ref, wr_ref, br_ref,
                  x1_ref, h2_ref, idx_ref, gate_ref, mask_ref, *, d):
    row = pl.program_id(0)
    gt1 = mods_ref[pl.ds(row, 1), 2 * d:3 * d]
    sh2 = mods_ref[pl.ds(row, 1), 3 * d:4 * d]
    sc2 = mods_ref[pl.ds(row, 1), 4 * d:5 * d]
    def rows_chain(rows):
        an = _rms(att_ref[0, rows, :].astype(F32), ga_ref[...]).astype(BF16)
        rn = _rms(rec_ref[0, rows, :].astype(F32), gr_ref[...]).astype(BF16)
        mix = (jnp.dot(an, wo_ref[0:ATT_WIDTH, :], preferred_element_type=F32)
               + jnp.dot(rn, wo_ref[ATT_WIDTH:d, :], preferred_element_type=F32))
        x1 = x_ref[0, rows, :] + gt1 * mix
        x1_ref[0, rows, :] = x1
        h2 = _rms(x1, g2_ref[...]) * (1.0 + sc2) + sh2
        half = d // 2
        h2_ref[rows, :] = _pack_bf16_pair(h2[:, 0:half], h2[:, half:d])
        h_hi = h2.astype(BF16)
        h_lo = (h2 - h_hi.astype(F32)).astype(BF16)
        w_split = wr_ref[...]
        part = (jnp.dot(h_hi, w_split, preferred_element_type=F32)
                + jnp.dot(h_lo, w_split, preferred_element_type=F32))
        logits = part[:, 0:N_EXPERTS] + part[:, N_EXPERTS:2 * N_EXPERTS] + br_ref[...]
        n = logits.shape[0]
        lane = lax.broadcasted_iota(I32, (n, N_EXPERTS), 1).astype(F32)
        col = lax.broadcasted_iota(I32, (n, TOP_K), 1)
        idx = jnp.zeros((n, TOP_K), F32)
        ex = jnp.zeros((n, TOP_K), F32)
        mask = jnp.zeros((n, N_EXPERTS), F32)
        rest = logits
        top = None
        for k in range(TOP_K):
            m = jnp.max(rest, axis=-1, keepdims=True)
            first = jnp.min(jnp.where(rest == m, lane, float(N_EXPERTS)), axis=-1, keepdims=True)
            sel = lane == first
            if k == 0:
                top = m
            idx = jnp.where(col == k, first, idx)
            ex = jnp.where(col == k, jnp.exp(m - top), ex)
            mask = jnp.where(sel, 1.0, mask)
            rest = jnp.where(sel, -jnp.inf, rest)
        idx_ref[rows, :] = idx.astype(I32)
        gate_ref[rows, :] = ex / jnp.sum(ex, axis=-1, keepdims=True)
        mask_ref[rows, :] = mask

    tm = x_ref.shape[1]
    n_chains = 2 if tm % (2 * SUBLANES) == 0 else 1
    for ci in range(n_chains):
        rows_chain(slice(ci * (tm // n_chains), (ci + 1) * (tm // n_chains)))


def _merge(att, rec, x, mods, g_att, g_rec, w_out_bf, g2, w_router, b_router, *, tm):
    b, s, d = x.shape
    t = b * s
    nt = s // tm
    rec_w = d - ATT_WIDTH
    row3 = lambda w: pl.BlockSpec((1, tm, w), lambda bi, i: (bi, i, 0))
    tok2 = lambda w: pl.BlockSpec((tm, w), lambda bi, i: (bi * nt + i, 0))
    const = lambda shape, **kw: pl.BlockSpec(shape, lambda bi, i: (0,) * len(shape), **kw)
    vmem = d * d * 2 + 2 * tm * (ATT_WIDTH + rec_w) * 2 + 4 * tm * d * 4 + tm * d * 4 + 8 * tm * d * 4
    return pl.pallas_call(
        functools.partial(_merge_kernel, d=d),
        grid=(b, nt),
        in_specs=[row3(ATT_WIDTH), row3(rec_w), row3(d), const(mods.shape),
                  const((1, ATT_WIDTH)), const((1, rec_w)),
                  const((d, d), pipeline_mode=pl.Buffered(1)), const((1, d)),
                  const((d, 2 * N_EXPERTS)), const((1, N_EXPERTS))],
        out_specs=[row3(d), tok2(d // 2), tok2(TOP_K), tok2(TOP_K), tok2(N_EXPERTS)],
        out_shape=[jax.ShapeDtypeStruct((b, s, d), F32), jax.ShapeDtypeStruct((t, d // 2), I32),
                   jax.ShapeDtypeStruct((t, TOP_K), I32), jax.ShapeDtypeStruct((t, TOP_K), F32),
                   jax.ShapeDtypeStruct((t, N_EXPERTS), F32)],
        compiler_params=pltpu.CompilerParams(dimension_semantics=("arbitrary", "arbitrary"),
                                             vmem_limit_bytes=_vmem_limit(vmem)),
        name="merge",
    )(att, rec, x, mods, g_att.reshape(1, -1), g_rec.reshape(1, -1), w_out_bf, g2.reshape(1, d),
      _split_bf16(w_router), b_router.reshape(1, N_EXPERTS))


def _split_bf16(w):
    hi = w.astype(BF16)
    lo = (w - hi.astype(F32)).astype(BF16)
    return jnp.concatenate([hi, lo], axis=1)


def _rank_kernel(mask_ref, idx_ref, rank_ref, cnt_ref, carry):
    @pl.when(pl.program_id(0) == 0)
    def _():
        carry[...] = jnp.zeros_like(carry)

    m = mask_ref[...]
    tb = m.shape[0]
    r = lax.broadcasted_iota(I32, (tb, tb), 0)
    cidx = lax.broadcasted_iota(I32, (tb, tb), 1)
    tri = jnp.where(cidx < r, 1.0, 0.0).astype(BF16)
    before = jnp.dot(tri, m.astype(BF16), preferred_element_type=F32) + carry[...]
    lane = lax.broadcasted_iota(I32, (tb, N_EXPERTS), 1)
    col = lax.broadcasted_iota(I32, (tb, TOP_K), 1)
    idx = idx_ref[...]
    rank = jnp.zeros((tb, TOP_K), F32)
    for k in range(TOP_K):
        pick = jnp.sum(jnp.where(lane == idx[:, k:k + 1], before, 0.0), axis=-1, keepdims=True)
        rank = jnp.where(col == k, pick, rank)
    rank_ref[...] = rank.astype(I32)
    carry[...] = carry[...] + jnp.sum(m, axis=0, keepdims=True)
    cnt_ref[...] = carry[...].astype(I32)


def _rank(mask, idx, *, tb):
    t = mask.shape[0]
    return pl.pallas_call(
        _rank_kernel,
        grid=(t // tb,),
        in_specs=[pl.BlockSpec((tb, N_EXPERTS), lambda i: (i, 0)), pl.BlockSpec((tb, TOP_K), lambda i: (i, 0))],
        out_specs=[pl.BlockSpec((tb, TOP_K), lambda i: (i, 0)), pl.BlockSpec((1, N_EXPERTS), lambda i: (0, 0))],
        out_shape=[jax.ShapeDtypeStruct((t, TOP_K), I32), jax.ShapeDtypeStruct((1, N_EXPERTS), I32)],
        scratch_shapes=[pltpu.VMEM((1, N_EXPERTS), F32)],
        compiler_params=pltpu.CompilerParams(dimension_semantics=("arbitrary",)),
        name="rank",
    )(mask, idx)


def _dispatch_kernel(dest_ref, fill_ref, h2_ref, xs_ref, zeros, sem, fill_sem, *, tb, n_blocks):
    base = pl.program_id(0) * tb

    @pl.when(pl.program_id(0) == 0)
    def _():
        zeros[...] = jnp.zeros_like(zeros)

        def zero_row(r):
            return pltpu.make_async_copy(zeros.at[pl.ds(0, 1)], xs_ref.at[pl.ds(r, 1)], fill_sem)

        def zero_block(bk):
            rows = pl.ds(pl.multiple_of(bk * MOE_BLOCK, MOE_BLOCK), MOE_BLOCK)
            return pltpu.make_async_copy(zeros, xs_ref.at[rows], fill_sem)

        def zero_rows(start, n):
            return pltpu.make_async_copy(zeros.at[pl.ds(0, n)],
                                         xs_ref.at[pl.ds(pl.multiple_of(start, SUBLANES), n)], fill_sem)

        def for_all_fills(act):
            for e in range(N_EXPERTS):
                lo = fill_ref[e]
                hi = fill_ref[N_EXPERTS + e]
                lo8 = jnp.minimum((lo + SUBLANES - 1) // SUBLANES * SUBLANES, hi)
                lax.fori_loop(lo, lo8, lambda r, c: (act(zero_row(r)), c)[1], 0)
                rest = hi - lo8
                pos = lo8
                size = MOE_BLOCK // 2
                while size >= SUBLANES:
                    take = rest & size

                    @pl.when(take != 0)
                    def _(pos=pos, size=size):
                        act(zero_rows(pos, size))

                    pos = pos + take
                    size //= 2
            lax.fori_loop(fill_ref[2 * N_EXPERTS], n_blocks, lambda bk, c: (act(zero_block(bk)), c)[1], 0)

        for_all_fills(lambda cp: cp.start())
        for_all_fills(lambda cp: cp.wait())

    def row_copy(i, k):
        return pltpu.make_async_copy(h2_ref.at[pl.ds(i, 1)],
                                     xs_ref.at[pl.ds(dest_ref[(base + i) * TOP_K + k], 1)], sem)

    def body(i, _):
        for k in range(TOP_K):
            row_copy(i, k).start(priority=k % 2)
        return 0

    lax.fori_loop(0, tb, body, 0, unroll=4)
    for k in range(TOP_K):
        pltpu.make_async_copy(h2_ref, xs_ref.at[pl.ds(0, tb)], sem).wait()


def _dispatch(dest_flat, fill, h2p, *, tb, n_blocks):
    t, wd = h2p.shape
    return pl.pallas_call(
        functools.partial(_dispatch_kernel, tb=tb, n_blocks=n_blocks),
        grid_spec=pltpu.PrefetchScalarGridSpec(
            num_scalar_prefetch=2, grid=(t // tb,),
            in_specs=[pl.BlockSpec((tb, wd), lambda i, dr, fl: (i, 0))],
            out_specs=pl.BlockSpec(memory_space=pl.ANY),
            scratch_shapes=[pltpu.VMEM((MOE_BLOCK, wd), I32), pltpu.SemaphoreType.DMA(()),
                            pltpu.SemaphoreType.DMA(())]),
        out_shape=jax.ShapeDtypeStruct((n_blocks * MOE_BLOCK, wd), I32),
        compiler_params=pltpu.CompilerParams(dimension_semantics=("arbitrary",), has_side_effects=True),
        name="dispatch",
    )(dest_flat, fill, h2p)


PAIR_CHUNK = 512
MOE_ROW_STEPS = 4


def _last_real_block(s, meta):
    return jnp.maximum(jnp.minimum(s, meta[0] - 1), 0)


def _stream_expert_weights(ridx_ref, run_e_ref, meta_ref, w_hbm, wf32, wbf, sem):
    j = pl.program_id(0)
    s = pl.program_id(1)
    tn = wbf.shape[1]
    n_runs = meta_ref[1]
    r = ridx_ref[s]
    first = (s < meta_ref[0]) & ((s == 0) | (r != ridx_ref[jnp.maximum(s - 1, 0)]))

    def tile_copy(run, sweep):
        cols = pl.ds(pl.multiple_of(sweep * tn, tn), tn)
        return pltpu.make_async_copy(w_hbm.at[run_e_ref[run], :, cols], wf32, sem)

    @pl.when(first)
    def _():
        @pl.when((j == 0) & (r == 0))
        def _():
            tile_copy(0, 0).start()

        tile_copy(r, j).wait()
        wbf[...] = wf32[...].astype(BF16)
        more_runs = r + 1 < n_runs

        @pl.when(more_runs)
        def _():
            tile_copy(r + 1, j).start()

        @pl.when(jnp.logical_not(more_runs) & (j + 1 < pl.num_programs(0)))
        def _():
            tile_copy(0, j + 1).start()


def _expert_stream_scratch(k, tn):
    return [pltpu.VMEM((k, tn), F32), pltpu.VMEM((k, tn), BF16), pltpu.SemaphoreType.DMA(())]


def _for_real_rows(rows_ref, meta_ref, out_ref, compute_rows):
    s = pl.program_id(1)
    real = s < meta_ref[0]
    quarter = MOE_BLOCK // MOE_ROW_STEPS
    used_quarters = (rows_ref[s] + quarter - 1) // quarter

    for q in range(1, MOE_ROW_STEPS + 1):
        @pl.when(real & (used_quarters == q))
        def _(n=q * quarter):
            compute_rows(n)
            if n < MOE_BLOCK:
                out_ref[n:, :] = jnp.zeros((MOE_BLOCK - n, out_ref.shape[1]), out_ref.dtype)

    @pl.when(jnp.logical_not(real))
    def _():
        out_ref[...] = jnp.zeros_like(out_ref)


def _gu_kernel(be_ref, ridx_ref, run_e_ref, meta_ref, rows_ref, xs_ref, w_hbm, b_ref, act_ref, wf32, wbf, sem):
    del be_ref
    _stream_expert_weights(ridx_ref, run_e_ref, meta_ref, w_hbm, wf32, wbf, sem)
    tn = wbf.shape[1]

    def compute_rows(n):
        lo, hi = _unpack_bf16_pair(xs_ref[0:n, :])
        xb = jnp.concatenate([lo, hi], axis=1)
        gu = jnp.dot(xb, wbf[...], preferred_element_type=F32) + b_ref[0]
        g = jnp.minimum(gu, SWIGLU_LIMIT)
        up1 = jnp.clip(gu, -SWIGLU_LIMIT, SWIGLU_LIMIT) + 1.0
        paired = (pltpu.roll(up1, tn - 1, 1) * (g * jax.nn.sigmoid(SWIGLU_ALPHA * g))).astype(BF16)
        r = lax.broadcasted_iota(I32, (PAIR_CHUNK, PAIR_CHUNK // 2), 0)
        c = lax.broadcasted_iota(I32, (PAIR_CHUNK, PAIR_CHUNK // 2), 1)
        sel = jnp.where(r == 2 * c, 1.0, 0.0).astype(BF16)
        for ch in range(tn // PAIR_CHUNK):
            act_ref[0:n, ch * (PAIR_CHUNK // 2):(ch + 1) * (PAIR_CHUNK // 2)] = jnp.dot(
                paired[:, ch * PAIR_CHUNK:(ch + 1) * PAIR_CHUNK], sel, preferred_element_type=F32).astype(BF16)

    _for_real_rows(rows_ref, meta_ref, act_ref, compute_rows)


def _moe_gu(sched, xs, w_gu, b_gu, *, tn):
    p, half = xs.shape
    d = 2 * half
    f2 = w_gu.shape[2]
    nj = f2 // tn
    nblk = p // MOE_BLOCK
    blk = _last_real_block
    vmem = d * tn * 4 + d * tn * 2 + 2 * MOE_BLOCK * half * 4 + 2 * MOE_BLOCK * tn + 8 * MOE_BLOCK * tn * 4
    return pl.pallas_call(
        _gu_kernel,
        grid_spec=pltpu.PrefetchScalarGridSpec(
            num_scalar_prefetch=5, grid=(nj, nblk),
            in_specs=[pl.BlockSpec((MOE_BLOCK, half), lambda j, s, be, ri, re, mt, rw: (blk(s, mt), 0)),
                      pl.BlockSpec(memory_space=pl.ANY),
                      pl.BlockSpec((1, 1, tn), lambda j, s, be, ri, re, mt, rw: (be[blk(s, mt)], 0, j))],
            out_specs=pl.BlockSpec((MOE_BLOCK, tn // 2), lambda j, s, be, ri, re, mt, rw: (s, j)),
            scratch_shapes=_expert_stream_scratch(d, tn)),
        out_shape=jax.ShapeDtypeStruct((p, f2 // 2), BF16),
        compiler_params=pltpu.CompilerParams(dimension_semantics=("arbitrary", "arbitrary"),
                                             vmem_limit_bytes=_vmem_limit(vmem)),
        name="moe_gu",
    )(*sched, xs, w_gu, b_gu)


def _down_kernel(be_ref, ridx_ref, run_e_ref, meta_ref, rows_ref, act_ref, w_hbm, b_ref, y_ref, wf32, wbf, sem):
    del be_ref
    _stream_expert_weights(ridx_ref, run_e_ref, meta_ref, w_hbm, wf32, wbf, sem)

    def compute_rows(n):
        y = jnp.dot(act_ref[0:n, :], wbf[...], preferred_element_type=F32) + b_ref[0]
        half = y.shape[1] // 2
        y_ref[0:n, :] = _pack_bf16_pair(y[:, 0:half], y[:, half:2 * half])

    _for_real_rows(rows_ref, meta_ref, y_ref, compute_rows)


def _moe_down(sched, act, w_down, b_down):
    p, f = act.shape
    d = w_down.shape[2]
    nblk = p // MOE_BLOCK
    blk = _last_real_block
    vmem = f * d * 4 + f * d * 2 + 2 * MOE_BLOCK * f * 2 + 5 * MOE_BLOCK * d * 4
    return pl.pallas_call(
        _down_kernel,
        grid_spec=pltpu.PrefetchScalarGridSpec(
            num_scalar_prefetch=5, grid=(1, nblk),
            in_specs=[pl.BlockSpec((MOE_BLOCK, f), lambda j, s, be, ri, re, mt, rw: (blk(s, mt), 0)),
                      pl.BlockSpec(memory_space=pl.ANY),
                      pl.BlockSpec((1, 1, d), lambda j, s, be, ri, re, mt, rw: (be[blk(s, mt)], 0, 0))],
            out_specs=pl.BlockSpec((MOE_BLOCK, d // 2), lambda j, s, be, ri, re, mt, rw: (s, 0)),
            scratch_shapes=_expert_stream_scratch(f, d)),
        out_shape=jax.ShapeDtypeStruct((p, d // 2), I32),
        compiler_params=pltpu.CompilerParams(dimension_semantics=("arbitrary", "arbitrary"),
                                             vmem_limit_bytes=_vmem_limit(vmem)),
        name="moe_down",
    )(*sched, act, w_down, b_down)


def _combine_kernel(dest_ref, ys_ref, x1_ref, gate_ref, mods_ref, o_ref, buf, sem, *, tb, d, nt, n_tiles):
    i = pl.program_id(0)
    slot = lax.rem(i, 2)

    def row_copy(tile, sl, r, k):
        return pltpu.make_async_copy(ys_ref.at[pl.ds(dest_ref[(tile * tb + r) * TOP_K + k], 1)],
                                     buf.at[sl, k, pl.ds(r, 1)], sem.at[sl])

    def gather(tile, sl):
        def body(j, _):
            r0 = pl.multiple_of(j * SUBLANES, SUBLANES)
            for u in range(SUBLANES):
                for k in range(TOP_K):
                    row_copy(tile, sl, r0 + u, k).start(priority=k % 2)
            return 0
        lax.fori_loop(0, tb // SUBLANES, body, 0)

    @pl.when(i == 0)
    def _():
        gather(0, 0)

    for next_slot in range(2):
        @pl.when((i + 1 < n_tiles) & (slot != next_slot))
        def _():
            gather(i + 1, next_slot)

    for k in range(TOP_K):
        pltpu.make_async_copy(ys_ref.at[pl.ds(0, tb)], buf.at[slot, k], sem.at[slot]).wait()
    half = d // 2
    gate = gate_ref[...]
    y_lo = jnp.zeros((tb, half), F32)
    y_hi = jnp.zeros((tb, half), F32)
    for k in range(TOP_K):
        w = buf[slot, k]
        gk = gate[:, k:k + 1]
        y_lo = y_lo + lax.bitcast_convert_type(lax.shift_left(w, 16), F32) * gk
        y_hi = y_hi + lax.bitcast_convert_type(w & jnp.int32(-65536), F32) * gk
    gt2 = mods_ref[pl.ds(i // nt, 1), 5 * d:6 * d]
    o_ref[:, 0:half] = x1_ref[:, 0:half] + gt2[:, 0:half] * y_lo
    o_ref[:, half:d] = x1_ref[:, half:d] + gt2[:, half:d] * y_hi


def _combine(dest_flat, ys, x1, gate, mods, *, tb, seq):
    t, d = x1.shape
    n_tiles = t // tb
    nt = seq // tb
    vmem = 2 * TOP_K * tb * d * 2 + 4 * tb * d * 4 + 6 * tb * d * 4
    return pl.pallas_call(
        functools.partial(_combine_kernel, tb=tb, d=d, nt=nt, n_tiles=n_tiles),
        grid_spec=pltpu.PrefetchScalarGridSpec(
            num_scalar_prefetch=1, grid=(n_tiles,),
            in_specs=[pl.BlockSpec(memory_space=pl.ANY),
                      pl.BlockSpec((tb, d), lambda i, dr: (i, 0)),
                      pl.BlockSpec((tb, TOP_K), lambda i, dr: (i, 0)),
                      pl.BlockSpec(mods.shape, lambda i, dr: (0, 0))],
            out_specs=pl.BlockSpec((tb, d), lambda i, dr: (i, 0)),
            scratch_shapes=[pltpu.VMEM((2, TOP_K, tb, d // 2), I32), pltpu.SemaphoreType.DMA((2,))]),
        out_shape=jax.ShapeDtypeStruct((t, d), F32),
        compiler_params=pltpu.CompilerParams(dimension_semantics=("arbitrary",),
                                             vmem_limit_bytes=_vmem_limit(vmem)),
        name="combine",
    )(dest_flat, ys, x1, gate, mods)


def _rope_table(s):
    rows = s // GRID_W
    row = np.repeat(np.arange(rows), GRID_W).astype(np.float32)
    col = np.tile(np.arange(GRID_W), rows).astype(np.float32)
    inv = (np.float32(ROPE_THETA) ** (-np.arange(ROPE_FREQS, dtype=np.float32) / np.float32(ROPE_FREQS))).astype(np.float32)
    ang_r = row[:, None] * inv
    ang_c = col[:, None] * inv
    z = np.zeros_like(ang_r)
    cosf = np.concatenate([np.cos(ang_r)] * 2 + [np.cos(ang_c)] * 2, axis=1)
    sneg = np.concatenate([-np.sin(ang_r), z, -np.sin(ang_c), z], axis=1)
    spos = np.concatenate([z, np.sin(ang_r), z, np.sin(ang_c)], axis=1)
    return jnp.asarray(np.concatenate([cosf, sneg, spos], axis=1), dtype=F32)


def kernel(x, c, ctx, c_ctx, w_mod, b_mod, g_norm1, w_in, g_q, g_k, conv_w, conv_b, w_gate_a, b_gate_a,
           w_gate_x, b_gate_x, lru_lambda, g_att_out, g_rec_out, w_out, g_norm2, w_router, b_router,
           w_gate_up, b_gate_up, w_down, b_down):
    b, s, d = x.shape
    cl = ctx.shape[1]
    t = b * s
    assert w_mod.shape[0] == 1, "single-layer kernel"
    assert b + 1 <= SUBLANES and d - ATT_WIDTH == REC_BLOCKS * LANES
    assert s % (SCAN_SEGMENTS * SUBLANES) == 0 and cl % (SCAN_SEGMENTS * SUBLANES) == 0

    ctx_row = b
    c8 = jnp.zeros((SUBLANES, d), F32).at[:b].set(c).at[ctx_row].set(c_ctx)
    mods = _mod(c8, w_mod[0], b_mod[0])

    w_in_bf = w_in[0].astype(BF16)
    tm = min(512, s)
    q, k, v, xr, yr = _inproj(x, mods, g_norm1[0], w_in_bf, _rope_table(s), g_q[0], g_k[0],
                              latent=True, ctx_row=ctx_row, tm=tm)
    kc, vc, xrc = _inproj(ctx, mods, g_norm1[0], w_in_bf, None, g_q[0], g_k[0],
                          latent=False, ctx_row=ctx_row, tm=cl)

    att = _attention(q, kc, k, vc, v, tq=min(256, s))

    w_gates = jnp.concatenate([w_gate_a[0, 0], w_gate_x[0, 0], w_gate_a[0, 1], w_gate_x[0, 1]], axis=-1).astype(BF16)
    rw = d - ATT_WIDTH
    bias = lambda bb: bb.reshape(REC_BLOCKS, LANES)
    b_gates = jnp.concatenate([bias(b_gate_a[0, 0]), bias(b_gate_x[0, 0]), bias(b_gate_a[0, 1]), bias(b_gate_x[0, 1])],
                              axis=-1).reshape(1, 4 * rw)
    rec = _rglru(xr, xrc, yr, conv_w[0], conv_b[0].reshape(1, rw), w_gates, b_gates, lru_lambda[0])

    x1, h2p, top_idx, gate, mask = _merge(att, rec, x, mods, g_att_out[0], g_rec_out[0], w_out[0].astype(BF16),
                                          g_norm2[0], w_router[0], b_router[0], tm=tm)

    rank, counts = _rank(mask, top_idx, tb=MOE_BLOCK)
    counts = counts[0]
    padded = (counts + MOE_BLOCK - 1) // MOE_BLOCK * MOE_BLOCK
    pad_ends = jnp.cumsum(padded)
    pad_starts = pad_ends - padded
    dest = (pad_starts[top_idx] + rank).reshape(t * TOP_K)
    n_blocks = (t * TOP_K + N_EXPERTS * (MOE_BLOCK - 1) + MOE_BLOCK - 1) // MOE_BLOCK
    block_start = jnp.arange(n_blocks, dtype=I32) * MOE_BLOCK
    block_e = jnp.minimum(jnp.sum((pad_ends[None, :] <= block_start[:, None]).astype(I32), axis=1), N_EXPERTS - 1)
    n_valid = (pad_ends[N_EXPERTS - 1] // MOE_BLOCK).reshape(1).astype(I32)
    fill = jnp.concatenate([pad_starts + counts, pad_ends, n_valid]).astype(I32)

    xs = _dispatch(dest, fill, h2p, tb=MOE_BLOCK, n_blocks=n_blocks)

    used = padded > 0
    run_of_expert = jnp.cumsum(used.astype(I32)) - 1
    experts = jnp.arange(N_EXPERTS, dtype=I32)
    run_e = jnp.sum(jnp.where(used[None, :] & (run_of_expert[None, :] == experts[:, None]), experts[None, :], 0), axis=1)
    meta = jnp.concatenate([n_valid, jnp.sum(used.astype(I32)).reshape(1)])
    block_rows = jnp.clip((pad_starts + counts)[block_e] - block_start, 0, MOE_BLOCK).astype(I32)
    sched = (block_e, run_of_expert[block_e], run_e.astype(I32), meta, block_rows)

    f2 = w_gate_up.shape[3]
    act = _moe_gu(sched, xs, w_gate_up[0], b_gate_up[0].reshape(N_EXPERTS, 1, f2), tn=2048)
    ys = _moe_down(sched, act, w_down[0], b_down[0].reshape(N_EXPERTS, 1, d))

    return _combine(dest, ys, x1.reshape(t, d), gate, mods, tb=256, seq=s).reshape(b, s, d)
```

```python
import functools
import math

import jax
import jax.numpy as jnp
import numpy as np
from jax import lax
from jax.experimental import pallas as pl
from jax.experimental.pallas import tpu as pltpu

F32 = jnp.float32
BF16 = jnp.bfloat16
I32 = jnp.int32

EPS = 1e-6
GRID_W = 64
HEAD_DIM = 128
N_Q_HEADS = 8
N_KV_HEADS = 2
GROUP = N_Q_HEADS // N_KV_HEADS
ATT_WIDTH = N_Q_HEADS * HEAD_DIM
KV_WIDTH = N_KV_HEADS * HEAD_DIM
ROPE_THETA = 10000.0
ROPE_FREQS = HEAD_DIM // 4
REC_BLOCKS = 8
CONV_W = 4
CONV_LEFT = 2
LRU_C = 8.0
N_EXPERTS = 32
TOP_K = 4
SWIGLU_LIMIT = 7.0
SWIGLU_ALPHA = 1.702
MOE_BLOCK = 512

V7X_VMEM_BYTES = 64 * 1024 * 1024
SUBLANES = 8
LANES = 128
SCAN_SEGMENTS = SUBLANES
CONV_PAD = SUBLANES

HIGHEST = lax.Precision.HIGHEST


def _vmem_limit(nbytes):
    return int(min(V7X_VMEM_BYTES - 4 * 1024 * 1024, max(nbytes, 16 * 1024 * 1024)))


def _rms(x, g):
    return x * lax.rsqrt(jnp.mean(x * x, axis=-1, keepdims=True) + EPS) * g


def _mod_kernel(c_ref, w_ref, b_ref, o_ref):
    c = c_ref[...]
    a = c * jax.nn.sigmoid(c)
    o_ref[...] = jnp.dot(a, w_ref[...], preferred_element_type=F32, precision=HIGHEST) + b_ref[...]


def _mod(c8, w_mod, b_mod):
    d, n = w_mod.shape
    tn = 2048
    return pl.pallas_call(
        _mod_kernel,
        grid=(n // tn,),
        in_specs=[pl.BlockSpec((SUBLANES, d), lambda j: (0, 0)),
                  pl.BlockSpec((d, tn), lambda j: (0, j)),
                  pl.BlockSpec((1, tn), lambda j: (0, j))],
        out_specs=pl.BlockSpec((SUBLANES, tn), lambda j: (0, j)),
        out_shape=jax.ShapeDtypeStruct((SUBLANES, n), F32),
        compiler_params=pltpu.CompilerParams(dimension_semantics=("arbitrary",),
                                             vmem_limit_bytes=_vmem_limit(3 * d * tn * 4)),
        name="mod",
    )(c8, w_mod, b_mod.reshape(1, n))


def _qk_norm_rope(y, g, rope):
    yn = _rms(y, g)
    if rope is None:
        return yn
    cosf, sneg, spos = rope
    return yn * cosf + pltpu.roll(yn, HEAD_DIM - ROPE_FREQS, 1) * sneg + pltpu.roll(yn, ROPE_FREQS, 1) * spos


def _inproj_kernel(*refs, d, latent, ctx_row):
    if latent:
        (x_ref, mods_ref, g1_ref, w_ref, rope_ref, gq_ref, gk_ref,
         q_ref, k_ref, v_ref, xr_ref, yr_ref) = refs
        row = pl.program_id(0)
    else:
        x_ref, mods_ref, g1_ref, w_ref, gk_ref, k_ref, v_ref, xr_ref = refs
        row = ctx_row
    sh = mods_ref[pl.ds(row, 1), 0:d]
    sc = mods_ref[pl.ds(row, 1), d:2 * d]
    h = _rms(x_ref[0], g1_ref[...]) * (1.0 + sc) + sh
    hb = h.astype(BF16)

    def proj(lo, hi):
        return jnp.dot(hb, w_ref[:, lo:hi], preferred_element_type=F32)

    o_k = ATT_WIDTH
    o_v = o_k + KV_WIDTH
    o_xr = o_v + KV_WIDTH
    rec_w = d - ATT_WIDTH
    o_yr = o_xr + rec_w
    rope = None
    if latent:
        rp = rope_ref[...]
        rope = (rp[:, 0:HEAD_DIM], rp[:, HEAD_DIM:2 * HEAD_DIM], rp[:, 2 * HEAD_DIM:3 * HEAD_DIM])
        q = proj(0, ATT_WIDTH)
        for hd in range(N_Q_HEADS):
            sl = slice(hd * HEAD_DIM, (hd + 1) * HEAD_DIM)
            q_ref[0, :, sl] = (_qk_norm_rope(q[:, sl], gq_ref[...], rope) * (HEAD_DIM ** -0.5)).astype(BF16)
    k = proj(o_k, o_v)
    for hd in range(N_KV_HEADS):
        sl = slice(hd * HEAD_DIM, (hd + 1) * HEAD_DIM)
        k_ref[0, :, sl] = _qk_norm_rope(k[:, sl], gk_ref[...], rope).astype(BF16)
    v = proj(o_v, o_xr).astype(BF16)
    for hd in range(N_KV_HEADS):
        v_ref[0, :, 2 * hd * HEAD_DIM:(2 * hd + 1) * HEAD_DIM] = v[:, hd * HEAD_DIM:(hd + 1) * HEAD_DIM]
        v_ref[0, :, (2 * hd + 1) * HEAD_DIM:(2 * hd + 2) * HEAD_DIM] = jnp.ones((v.shape[0], HEAD_DIM), BF16)
    xr_ref[0] = proj(o_xr, o_yr)
    if latent:
        yr_ref[0] = proj(o_yr, o_yr + rec_w)


def _inproj(x, mods, g1, w_in_bf, rope_tab, g_q, g_k, *, latent, ctx_row, tm):
    b, s, d = x.shape
    n = w_in_bf.shape[1]
    rec_w = d - ATT_WIDTH
    grid = (b, s // tm)
    row_spec = lambda w: pl.BlockSpec((1, tm, w), lambda bi, i: (bi, i, 0))
    full2 = lambda a: pl.BlockSpec(a.shape, lambda bi, i: (0, 0))
    in_specs = [row_spec(d), full2(mods), pl.BlockSpec((1, d), lambda bi, i: (0, 0)),
                pl.BlockSpec((d, n), lambda bi, i: (0, 0), pipeline_mode=pl.Buffered(1))]
    args = [x, mods, g1.reshape(1, d), w_in_bf]
    out_specs, out_shape = [], []
    if latent:
        in_specs += [pl.BlockSpec((tm, 3 * HEAD_DIM), lambda bi, i: (i, 0)),
                     pl.BlockSpec((1, HEAD_DIM), lambda bi, i: (0, 0))]
        args += [rope_tab, g_q.reshape(1, HEAD_DIM)]
        out_specs.append(row_spec(ATT_WIDTH))
        out_shape.append(jax.ShapeDtypeStruct((b, s, ATT_WIDTH), BF16))
    in_specs.append(pl.BlockSpec((1, HEAD_DIM), lambda bi, i: (0, 0)))
    args.append(g_k.reshape(1, HEAD_DIM))
    out_specs += [row_spec(KV_WIDTH), row_spec(2 * KV_WIDTH), row_spec(rec_w)]
    out_shape += [jax.ShapeDtypeStruct((b, s, KV_WIDTH), BF16), jax.ShapeDtypeStruct((b, s, 2 * KV_WIDTH), BF16),
                  jax.ShapeDtypeStruct((b, s, rec_w), F32)]
    if latent:
        out_specs.append(row_spec(rec_w))
        out_shape.append(jax.ShapeDtypeStruct((b, s, rec_w), F32))
    vmem = d * n * 2 + 2 * tm * d * 4 + 2 * tm * n * 4 + 3 * tm * d * 4 + 2 * tm * n * 4
    return pl.pallas_call(
        functools.partial(_inproj_kernel, d=d, latent=latent, ctx_row=ctx_row),
        grid=grid, in_specs=in_specs, out_specs=out_specs, out_shape=out_shape,
        compiler_params=pltpu.CompilerParams(dimension_semantics=("arbitrary", "arbitrary"),
                                             vmem_limit_bytes=_vmem_limit(vmem)),
        name="inproj_latent" if latent else "inproj_ctx",
    )(*args)


def _attn_kernel(q_ref, kc_ref, k_ref, vc_ref, v_ref, o_ref, kbuf, vbuf):
    s_lat = k_ref.shape[1]

    @pl.when(pl.program_id(2) == 0)
    def _():
        kbuf[0:s_lat, :] = k_ref[0]
        kbuf[s_lat:, :] = kc_ref[0]
        vbuf[0:s_lat, :] = v_ref[0]
        vbuf[s_lat:, :] = vc_ref[0]

    k = kbuf[...]
    v = vbuf[...]
    for g in range(GROUP):
        sl = slice(g * HEAD_DIM, (g + 1) * HEAD_DIM)
        s = lax.dot_general(q_ref[0, :, sl], k, (((1,), (1,)), ((), ())), preferred_element_type=F32)
        m = jnp.max(s, axis=-1, keepdims=True)
        o = jnp.dot(jnp.exp(s - m).astype(BF16), v, preferred_element_type=F32)
        o_ref[0, :, sl] = (o[:, 0:HEAD_DIM] / o[:, HEAD_DIM:2 * HEAD_DIM]).astype(BF16)


def _attention(q, kc, k, vc, v, *, tq):
    b, s, _ = q.shape
    lk = s + kc.shape[1]
    gw = GROUP * HEAD_DIM
    vmem = 3 * lk * HEAD_DIM * 2 * 3 + 4 * tq * gw * 2 + 4 * tq * lk * 4
    kv = lambda a, w: pl.BlockSpec((1, a.shape[1], w), lambda bi, h, i: (bi, 0, h))
    return pl.pallas_call(
        _attn_kernel,
        grid=(b, N_KV_HEADS, s // tq),
        in_specs=[pl.BlockSpec((1, tq, gw), lambda bi, h, i: (bi, i, h)),
                  kv(kc, HEAD_DIM), kv(k, HEAD_DIM), kv(vc, 2 * HEAD_DIM), kv(v, 2 * HEAD_DIM)],
        out_specs=pl.BlockSpec((1, tq, gw), lambda bi, h, i: (bi, i, h)),
        out_shape=jax.ShapeDtypeStruct((b, s, ATT_WIDTH), BF16),
        scratch_shapes=[pltpu.VMEM((lk, HEAD_DIM), BF16), pltpu.VMEM((lk, 2 * HEAD_DIM), BF16)],
        compiler_params=pltpu.CompilerParams(dimension_semantics=("arbitrary",) * 3,
                                             vmem_limit_bytes=_vmem_limit(vmem)),
        name="attention",
    )(q, kc, k, vc, v)


def _gelu_tanh(x):
    return 0.5 * x * (1.0 + jnp.tanh(math.sqrt(2.0 / math.pi) * (x + 0.044715 * x * x * x)))


def _rglru_kernel(xr_ref, xc_ref, yr_ref, cw_ref, cb_ref, wg_ref, bg_ref, lam_ref, o_ref,
                  xp, xpc, af, bf, ab, bb, caf, cbf, cab, cbb, *, s, c):
    nseg = SCAN_SEGMENTS
    seg = s // nseg
    cseg = c // nseg
    zeros_pad = jnp.zeros((CONV_PAD, LANES), F32)
    xp[0:CONV_PAD, :] = zeros_pad
    xp[CONV_PAD + s:2 * CONV_PAD + s, :] = zeros_pad
    xp[CONV_PAD:CONV_PAD + s, :] = xr_ref[0]
    xpc[0:CONV_PAD, :] = zeros_pad
    xpc[CONV_PAD + c:2 * CONV_PAD + c, :] = zeros_pad
    xpc[CONV_PAD:CONV_PAD + c, :] = xc_ref[0]

    cw = cw_ref[...]
    cb = cb_ref[...]
    wg = wg_ref[0]
    bg = bg_ref[...]
    sp = jax.nn.softplus(-lam_ref[...])

    def coeffs(src, lo, n):
        u = cb
        for j in range(CONV_W):
            u = u + src[CONV_PAD + lo + j - CONV_LEFT:CONV_PAD + lo + j - CONV_LEFT + n, :] * cw[j:j + 1, :]
        g = jnp.dot(u.astype(BF16), wg, preferred_element_type=F32) + bg
        out = []
        for r in range(2):
            ga = g[:, (2 * r) * LANES:(2 * r + 1) * LANES]
            gx = g[:, (2 * r + 1) * LANES:(2 * r + 2) * LANES]
            log_a = (-LRU_C) * jax.nn.sigmoid(ga) * sp[r:r + 1, :]
            a = jnp.exp(log_a)
            mult = jnp.sqrt(-jnp.tanh(log_a) * (1.0 + a * a))
            out.append((a, mult * jax.nn.sigmoid(gx) * u))
        return out

    (a0, b0), (a1, b1) = coeffs(xpc, 0, c)
    for q in range(nseg):
        rows = slice(q * cseg, (q + 1) * cseg)
        dst = pl.ds(q, cseg, stride=nseg)
        caf[dst, :] = a0[rows]
        cbf[dst, :] = b0[rows]
        cab[dst, :] = a1[rows]
        cbb[dst, :] = b1[rows]
    for q in range(nseg):
        (a0, b0), (a1, b1) = coeffs(xp, q * seg, seg)
        dst = pl.ds(q, seg, stride=nseg)
        af[dst, :] = a0
        bf[dst, :] = b0
        ab[dst, :] = a1
        bb[dst, :] = b1

    def scan(a_f, b_f, a_b, b_b, n, store):
        def two_steps(a_ref, b_ref, r1, r2, h, p):
            a1 = a_ref[pl.ds(r1, nseg), :]
            b1 = b_ref[pl.ds(r1, nseg), :]
            a2 = a_ref[pl.ds(r2, nseg), :]
            b2 = b_ref[pl.ds(r2, nseg), :]
            a12 = a2 * a1
            b12 = a2 * b1 + b2
            h1 = a1 * h + b1
            p1 = a1 * p
            h2 = a12 * h + b12
            p2 = a12 * p
            if store:
                a_ref[pl.ds(r1, nseg), :] = p1
                b_ref[pl.ds(r1, nseg), :] = h1
                a_ref[pl.ds(r2, nseg), :] = p2
                b_ref[pl.ds(r2, nseg), :] = h2
            return h2, p2

        def body(j, carry):
            hf, pf, hb, pb = carry
            rf = pl.multiple_of(2 * j * nseg, nseg)
            rb = pl.multiple_of((n - 1 - 2 * j) * nseg, nseg)
            hf, pf = two_steps(a_f, b_f, rf, rf + nseg, hf, pf)
            hb, pb = two_steps(a_b, b_b, rb, rb - nseg, hb, pb)
            return hf, pf, hb, pb
        z = jnp.zeros((nseg, LANES), F32)
        o = jnp.ones((nseg, LANES), F32)
        return lax.fori_loop(0, n // 2, body, (z, o, z, o), unroll=4)

    def chain(h_end, p_end, h0, reverse):
        order = range(nseg - 1, -1, -1) if reverse else range(nseg)
        enter = [None] * nseg
        cur = h0
        for q in order:
            enter[q] = cur
            cur = h_end[q:q + 1, :] + p_end[q:q + 1, :] * cur
        return enter, cur

    zero_row = jnp.zeros((1, LANES), F32)
    hf, pf, hb, pb = scan(caf, cbf, cab, cbb, cseg, False)
    _, h0f = chain(hf, pf, zero_row, False)
    _, h0b = chain(hb, pb, zero_row, True)
    hf, pf, hb, pb = scan(af, bf, ab, bb, seg, True)
    enter_f, _ = chain(hf, pf, h0f, False)
    enter_b, _ = chain(hb, pb, h0b, True)
    for q in range(nseg):
        src = pl.ds(q, seg, stride=nseg)
        h = bf[src, :] + af[src, :] * enter_f[q] + bb[src, :] + ab[src, :] * enter_b[q]
        rows = slice(q * seg, (q + 1) * seg)
        o_ref[0, rows, :] = (h * _gelu_tanh(yr_ref[0, rows, :])).astype(BF16)


def _rglru(xr, xrc, yr, conv_w, conv_b, w_gates, b_gates, lam):
    b, s, w = xr.shape
    c = xrc.shape[1]
    nb = w // LANES
    slab = lambda n: pl.BlockSpec((1, n, LANES), lambda bi, j: (bi, 0, j))
    scr = lambda n: pltpu.VMEM((n, LANES), F32)
    vmem = (3 * 2 + 5) * s * LANES * 4 + 8 * s * LANES * 4
    return pl.pallas_call(
        functools.partial(_rglru_kernel, s=s, c=c),
        grid=(b, nb),
        in_specs=[slab(s), slab(c), slab(s),
                  pl.BlockSpec((CONV_W, LANES), lambda bi, j: (0, j)),
                  pl.BlockSpec((1, LANES), lambda bi, j: (0, j)),
                  pl.BlockSpec((1, LANES, 4 * LANES), lambda bi, j: (j, 0, 0)),
                  pl.BlockSpec((1, 4 * LANES), lambda bi, j: (0, j)),
                  pl.BlockSpec((2, LANES), lambda bi, j: (0, j))],
        out_specs=slab(s),
        out_shape=jax.ShapeDtypeStruct((b, s, w), BF16),
        scratch_shapes=[scr(s + 2 * CONV_PAD), scr(c + 2 * CONV_PAD),
                        scr(s), scr(s), scr(s), scr(s), scr(c), scr(c), scr(c), scr(c)],
        compiler_params=pltpu.CompilerParams(dimension_semantics=("arbitrary", "arbitrary"),
                                             vmem_limit_bytes=_vmem_limit(vmem)),
        name="rglru",
    )(xr, xrc, yr, conv_w, conv_b, w_gates, b_gates, lam)


def _pack_bf16_pair(lo, hi):
    lo_bits = lax.bitcast_convert_type(lo.astype(BF16).astype(F32), I32)
    hi_bits = lax.bitcast_convert_type(hi.astype(BF16).astype(F32), I32)
    return lax.shift_right_logical(lo_bits, 16) | (hi_bits & jnp.int32(-65536))


def _unpack_bf16_pair(w):
    lo = lax.bitcast_convert_type(lax.shift_left(w, 16), F32).astype(BF16)
    hi = lax.bitcast_convert_type(w & jnp.int32(-65536), F32).astype(BF16)
    return lo, hi


def _merge_kernel(att_ref, rec_ref, x_ref, mods_ref, ga_ref, gr_ref, wo_ref, g2_ref, wr_ref, br_ref,
                  x1_ref, h2_ref, idx_ref, gate_ref, mask_ref, cnt_ref, *, d):
    row = pl.program_id(0)

    @pl.when((pl.program_id(0) == 0) & (pl.program_id(1) == 0))
    def _():
        cnt_ref[...] = jnp.zeros_like(cnt_ref)
    gt1 = mods_ref[pl.ds(row, 1), 2 * d:3 * d]
    sh2 = mods_ref[pl.ds(row, 1), 3 * d:4 * d]
    sc2 = mods_ref[pl.ds(row, 1), 4 * d:5 * d]
    def rows_chain(rows):
        an = _rms(att_ref[0, rows, :].astype(F32), ga_ref[...]).astype(BF16)
        rn = _rms(rec_ref[0, rows, :].astype(F32), gr_ref[...]).astype(BF16)
        mix = (jnp.dot(an, wo_ref[0:ATT_WIDTH, :], preferred_element_type=F32)
               + jnp.dot(rn, wo_ref[ATT_WIDTH:d, :], preferred_element_type=F32))
        x1 = x_ref[0, rows, :] + gt1 * mix
        x1_ref[0, rows, :] = x1
        h2 = _rms(x1, g2_ref[...]) * (1.0 + sc2) + sh2
        half = d // 2
        h2_ref[rows, :] = _pack_bf16_pair(h2[:, 0:half], h2[:, half:d])
        h_hi = h2.astype(BF16)
        h_lo = (h2 - h_hi.astype(F32)).astype(BF16)
        w_split = wr_ref[...]
        part = (jnp.dot(h_hi, w_split, preferred_element_type=F32)
                + jnp.dot(h_lo, w_split, preferred_element_type=F32))
        logits = part[:, 0:N_EXPERTS] + part[:, N_EXPERTS:2 * N_EXPERTS] + br_ref[...]
        n = logits.shape[0]
        lane = lax.broadcasted_iota(I32, (n, N_EXPERTS), 1).astype(F32)
        col = lax.broadcasted_iota(I32, (n, TOP_K), 1)
        idx = jnp.zeros((n, TOP_K), F32)
        ex = jnp.zeros((n, TOP_K), F32)
        mask = jnp.zeros((n, N_EXPERTS), F32)
        rest = logits
        top = None
        for k in range(TOP_K):
            m = jnp.max(rest, axis=-1, keepdims=True)
            first = jnp.min(jnp.where(rest == m, lane, float(N_EXPERTS)), axis=-1, keepdims=True)
            sel = lane == first
            if k == 0:
                top = m
            idx = jnp.where(col == k, first, idx)
            ex = jnp.where(col == k, jnp.exp(m - top), ex)
            mask = jnp.where(sel, 1.0, mask)
            rest = jnp.where(sel, -jnp.inf, rest)
        idx_ref[rows, :] = idx.astype(I32)
        gate_ref[rows, :] = ex / jnp.sum(ex, axis=-1, keepdims=True)
        mask_ref[rows, :] = mask
        return jnp.sum(mask, axis=0, keepdims=True)

    tm = x_ref.shape[1]
    n_chains = 2 if tm % (2 * SUBLANES) == 0 else 1
    routed = [rows_chain(slice(ci * (tm // n_chains), (ci + 1) * (tm // n_chains))) for ci in range(n_chains)]
    cnt_ref[...] = cnt_ref[...] + sum(routed)


def _merge(att, rec, x, mods, g_att, g_rec, w_out_bf, g2, w_router, b_router, *, tm):
    b, s, d = x.shape
    t = b * s
    nt = s // tm
    rec_w = d - ATT_WIDTH
    row3 = lambda w: pl.BlockSpec((1, tm, w), lambda bi, i: (bi, i, 0))
    tok2 = lambda w: pl.BlockSpec((tm, w), lambda bi, i: (bi * nt + i, 0))
    const = lambda shape, **kw: pl.BlockSpec(shape, lambda bi, i: (0,) * len(shape), **kw)
    vmem = d * d * 2 + 2 * tm * (ATT_WIDTH + rec_w) * 2 + 4 * tm * d * 4 + tm * d * 4 + 8 * tm * d * 4
    return pl.pallas_call(
        functools.partial(_merge_kernel, d=d),
        grid=(b, nt),
        in_specs=[row3(ATT_WIDTH), row3(rec_w), row3(d), const(mods.shape),
                  const((1, ATT_WIDTH)), const((1, rec_w)),
                  const((d, d), pipeline_mode=pl.Buffered(1)), const((1, d)),
                  const((d, 2 * N_EXPERTS)), const((1, N_EXPERTS))],
        out_specs=[row3(d), tok2(d // 2), tok2(TOP_K), tok2(TOP_K), tok2(N_EXPERTS), const((1, N_EXPERTS))],
        out_shape=[jax.ShapeDtypeStruct((b, s, d), F32), jax.ShapeDtypeStruct((t, d // 2), I32),
                   jax.ShapeDtypeStruct((t, TOP_K), I32), jax.ShapeDtypeStruct((t, TOP_K), F32),
                   jax.ShapeDtypeStruct((t, N_EXPERTS), F32), jax.ShapeDtypeStruct((1, N_EXPERTS), F32)],
        compiler_params=pltpu.CompilerParams(dimension_semantics=("arbitrary", "arbitrary"),
                                             vmem_limit_bytes=_vmem_limit(vmem)),
        name="merge",
    )(att, rec, x, mods, g_att.reshape(1, -1), g_rec.reshape(1, -1), w_out_bf, g2.reshape(1, d),
      _split_bf16(w_router), b_router.reshape(1, N_EXPERTS))


def _split_bf16(w):
    hi = w.astype(BF16)
    lo = (w - hi.astype(F32)).astype(BF16)
    return jnp.concatenate([hi, lo], axis=1)


def _rank_kernel(mask_ref, idx_ref, start_ref, slot_ref, carry):
    @pl.when(pl.program_id(0) == 0)
    def _():
        carry[...] = jnp.zeros_like(carry)

    m = mask_ref[...]
    tb = m.shape[0]
    r = lax.broadcasted_iota(I32, (tb, tb), 0)
    cidx = lax.broadcasted_iota(I32, (tb, tb), 1)
    tri = jnp.where(cidx < r, 1.0, 0.0).astype(BF16)
    slot_e = jnp.dot(tri, m.astype(BF16), preferred_element_type=F32) + (carry[...] + start_ref[...])
    lane = lax.broadcasted_iota(I32, (tb, N_EXPERTS), 1)
    col = lax.broadcasted_iota(I32, (tb, TOP_K), 1)
    idx = idx_ref[...]
    slot = jnp.zeros((tb, TOP_K), F32)
    for k in range(TOP_K):
        pick = jnp.sum(jnp.where(lane == idx[:, k:k + 1], slot_e, 0.0), axis=-1, keepdims=True)
        slot = jnp.where(col == k, pick, slot)
    slot_ref[...] = slot.astype(I32)
    carry[...] = carry[...] + jnp.sum(m, axis=0, keepdims=True)


def _rank(mask, idx, starts, *, tb):
    t = mask.shape[0]
    return pl.pallas_call(
        _rank_kernel,
        grid=(t // tb,),
        in_specs=[pl.BlockSpec((tb, N_EXPERTS), lambda i: (i, 0)), pl.BlockSpec((tb, TOP_K), lambda i: (i, 0)),
                  pl.BlockSpec((1, N_EXPERTS), lambda i: (0, 0))],
        out_specs=pl.BlockSpec((tb, TOP_K), lambda i: (i, 0)),
        out_shape=jax.ShapeDtypeStruct((t, TOP_K), I32),
        scratch_shapes=[pltpu.VMEM((1, N_EXPERTS), F32)],
        compiler_params=pltpu.CompilerParams(dimension_semantics=("arbitrary",)),
        name="rank",
    )(mask, idx, starts.astype(F32).reshape(1, N_EXPERTS))


def _dispatch_kernel(dest_ref, fill_ref, h2_ref, xs_ref, zeros, sem, fill_sem, *, tb, n_blocks):
    base = pl.program_id(0) * tb

    @pl.when(pl.program_id(0) == 0)
    def _():
        zeros[...] = jnp.zeros_like(zeros)

        def zero_row(r):
            return pltpu.make_async_copy(zeros.at[pl.ds(0, 1)], xs_ref.at[pl.ds(r, 1)], fill_sem)

        def zero_block(bk):
            rows = pl.ds(pl.multiple_of(bk * MOE_BLOCK, MOE_BLOCK), MOE_BLOCK)
            return pltpu.make_async_copy(zeros, xs_ref.at[rows], fill_sem)

        def zero_rows(start, n):
            return pltpu.make_async_copy(zeros.at[pl.ds(0, n)],
                                         xs_ref.at[pl.ds(pl.multiple_of(start, SUBLANES), n)], fill_sem)

        def for_all_fills(act):
            for e in range(N_EXPERTS):
                lo = fill_ref[e]
                hi = fill_ref[N_EXPERTS + e]
                lo8 = jnp.minimum((lo + SUBLANES - 1) // SUBLANES * SUBLANES, hi)
                lax.fori_loop(lo, lo8, lambda r, c: (act(zero_row(r)), c)[1], 0)
                rest = hi - lo8
                pos = lo8
                size = MOE_BLOCK // 2
                while size >= SUBLANES:
                    take = rest & size

                    @pl.when(take != 0)
                    def _(pos=pos, size=size):
                        act(zero_rows(pos, size))

                    pos = pos + take
                    size //= 2
            lax.fori_loop(fill_ref[2 * N_EXPERTS], n_blocks, lambda bk, c: (act(zero_block(bk)), c)[1], 0)

        for_all_fills(lambda cp: cp.start())
        for_all_fills(lambda cp: cp.wait())

    def row_copy(i, k):
        return pltpu.make_async_copy(h2_ref.at[pl.ds(i, 1)],
                                     xs_ref.at[pl.ds(dest_ref[(base + i) * TOP_K + k], 1)], sem)

    def body(i, _):
        for k in range(TOP_K):
            row_copy(i, k).start(priority=k % 2)
        return 0

    lax.fori_loop(0, tb, body, 0, unroll=4)
    for k in range(TOP_K):
        pltpu.make_async_copy(h2_ref, xs_ref.at[pl.ds(0, tb)], sem).wait()


def _dispatch(dest_flat, fill, h2p, *, tb, n_blocks):
    t, wd = h2p.shape
    return pl.pallas_call(
        functools.partial(_dispatch_kernel, tb=tb, n_blocks=n_blocks),
        grid_spec=pltpu.PrefetchScalarGridSpec(
            num_scalar_prefetch=2, grid=(t // tb,),
            in_specs=[pl.BlockSpec((tb, wd), lambda i, dr, fl: (i, 0))],
            out_specs=pl.BlockSpec(memory_space=pl.ANY),
            scratch_shapes=[pltpu.VMEM((MOE_BLOCK, wd), I32), pltpu.SemaphoreType.DMA(()),
                            pltpu.SemaphoreType.DMA(())]),
        out_shape=jax.ShapeDtypeStruct((n_blocks * MOE_BLOCK, wd), I32),
        compiler_params=pltpu.CompilerParams(dimension_semantics=("arbitrary",), has_side_effects=True),
        name="dispatch",
    )(dest_flat, fill, h2p)


PAIR_CHUNK = 512
MOE_ROW_STEPS = 4


def _last_real_block(s, meta):
    return jnp.maximum(jnp.minimum(s, meta[0] - 1), 0)


def _stream_expert_weights(ridx_ref, run_e_ref, meta_ref, w_hbm, wf32, wbf, sem):
    j = pl.program_id(0)
    s = pl.program_id(1)
    tn = wbf.shape[1]
    n_runs = meta_ref[1]
    r = ridx_ref[s]
    first = (s < meta_ref[0]) & ((s == 0) | (r != ridx_ref[jnp.maximum(s - 1, 0)]))

    def tile_copy(run, sweep):
        cols = pl.ds(pl.multiple_of(sweep * tn, tn), tn)
        return pltpu.make_async_copy(w_hbm.at[run_e_ref[run], :, cols], wf32, sem)

    @pl.when(first)
    def _():
        @pl.when((j == 0) & (r == 0))
        def _():
            tile_copy(0, 0).start()

        tile_copy(r, j).wait()
        wbf[...] = wf32[...].astype(BF16)
        more_runs = r + 1 < n_runs

        @pl.when(more_runs)
        def _():
            tile_copy(r + 1, j).start()

        @pl.when(jnp.logical_not(more_runs) & (j + 1 < pl.num_programs(0)))
        def _():
            tile_copy(0, j + 1).start()


def _expert_stream_scratch(k, tn):
    return [pltpu.VMEM((k, tn), F32), pltpu.VMEM((k, tn), BF16), pltpu.SemaphoreType.DMA(())]


def _for_real_rows(rows_ref, meta_ref, out_ref, compute_rows):
    s = pl.program_id(1)
    real = s < meta_ref[0]
    quarter = MOE_BLOCK // MOE_ROW_STEPS
    used_quarters = (rows_ref[s] + quarter - 1) // quarter

    for q in range(1, MOE_ROW_STEPS + 1):
        @pl.when(real & (used_quarters == q))
        def _(n=q * quarter):
            compute_rows(n)
            if n < MOE_BLOCK:
                out_ref[n:, :] = jnp.zeros((MOE_BLOCK - n, out_ref.shape[1]), out_ref.dtype)

    @pl.when(jnp.logical_not(real))
    def _():
        out_ref[...] = jnp.zeros_like(out_ref)


def _gu_kernel(be_ref, ridx_ref, run_e_ref, meta_ref, rows_ref, xs_ref, w_hbm, b_ref, act_ref, wf32, wbf, sem):
    del be_ref
    _stream_expert_weights(ridx_ref, run_e_ref, meta_ref, w_hbm, wf32, wbf, sem)
    tn = wbf.shape[1]

    def compute_rows(n):
        lo, hi = _unpack_bf16_pair(xs_ref[0:n, :])
        xb = jnp.concatenate([lo, hi], axis=1)
        gu = jnp.dot(xb, wbf[...], preferred_element_type=F32) + b_ref[0]
        g = jnp.minimum(gu, SWIGLU_LIMIT)
        up1 = jnp.clip(gu, -SWIGLU_LIMIT, SWIGLU_LIMIT) + 1.0
        paired = (pltpu.roll(up1, tn - 1, 1) * (g * jax.nn.sigmoid(SWIGLU_ALPHA * g))).astype(BF16)
        r = lax.broadcasted_iota(I32, (PAIR_CHUNK, PAIR_CHUNK // 2), 0)
        c = lax.broadcasted_iota(I32, (PAIR_CHUNK, PAIR_CHUNK // 2), 1)
        sel = jnp.where(r == 2 * c, 1.0, 0.0).astype(BF16)
        for ch in range(tn // PAIR_CHUNK):
            act_ref[0:n, ch * (PAIR_CHUNK // 2):(ch + 1) * (PAIR_CHUNK // 2)] = jnp.dot(
                paired[:, ch * PAIR_CHUNK:(ch + 1) * PAIR_CHUNK], sel, preferred_element_type=F32).astype(BF16)

    _for_real_rows(rows_ref, meta_ref, act_ref, compute_rows)


def _moe_gu(sched, xs, w_gu, b_gu, *, tn):
    p, half = xs.shape
    d = 2 * half
    f2 = w_gu.shape[2]
    nj = f2 // tn
    nblk = p // MOE_BLOCK
    blk = _last_real_block
    vmem = d * tn * 4 + d * tn * 2 + 2 * MOE_BLOCK * half * 4 + 2 * MOE_BLOCK * tn + 8 * MOE_BLOCK * tn * 4
    return pl.pallas_call(
        _gu_kernel,
        grid_spec=pltpu.PrefetchScalarGridSpec(
            num_scalar_prefetch=5, grid=(nj, nblk),
            in_specs=[pl.BlockSpec((MOE_BLOCK, half), lambda j, s, be, ri, re, mt, rw: (blk(s, mt), 0)),
                      pl.BlockSpec(memory_space=pl.ANY),
                      pl.BlockSpec((1, 1, tn), lambda j, s, be, ri, re, mt, rw: (be[blk(s, mt)], 0, j))],
            out_specs=pl.BlockSpec((MOE_BLOCK, tn // 2), lambda j, s, be, ri, re, mt, rw: (s, j)),
            scratch_shapes=_expert_stream_scratch(d, tn)),
        out_shape=jax.ShapeDtypeStruct((p, f2 // 2), BF16),
        compiler_params=pltpu.CompilerParams(dimension_semantics=("arbitrary", "arbitrary"),
                                             vmem_limit_bytes=_vmem_limit(vmem)),
        name="moe_gu",
    )(*sched, xs, w_gu, b_gu)


def _down_kernel(be_ref, ridx_ref, run_e_ref, meta_ref, rows_ref, act_ref, w_hbm, b_ref, y_ref, wf32, wbf, sem):
    del be_ref
    _stream_expert_weights(ridx_ref, run_e_ref, meta_ref, w_hbm, wf32, wbf, sem)

    def compute_rows(n):
        y = jnp.dot(act_ref[0:n, :], wbf[...], preferred_element_type=F32) + b_ref[0]
        half = y.shape[1] // 2
        y_ref[0:n, :] = _pack_bf16_pair(y[:, 0:half], y[:, half:2 * half])

    _for_real_rows(rows_ref, meta_ref, y_ref, compute_rows)


def _moe_down(sched, act, w_down, b_down):
    p, f = act.shape
    d = w_down.shape[2]
    nblk = p // MOE_BLOCK
    blk = _last_real_block
    vmem = f * d * 4 + f * d * 2 + 2 * MOE_BLOCK * f * 2 + 5 * MOE_BLOCK * d * 4
    return pl.pallas_call(
        _down_kernel,
        grid_spec=pltpu.PrefetchScalarGridSpec(
            num_scalar_prefetch=5, grid=(1, nblk),
            in_specs=[pl.BlockSpec((MOE_BLOCK, f), lambda j, s, be, ri, re, mt, rw: (blk(s, mt), 0)),
                      pl.BlockSpec(memory_space=pl.ANY),
                      pl.BlockSpec((1, 1, d), lambda j, s, be, ri, re, mt, rw: (be[blk(s, mt)], 0, 0))],
            out_specs=pl.BlockSpec((MOE_BLOCK, d // 2), lambda j, s, be, ri, re, mt, rw: (s, 0)),
            scratch_shapes=_expert_stream_scratch(f, d)),
        out_shape=jax.ShapeDtypeStruct((p, d // 2), I32),
        compiler_params=pltpu.CompilerParams(dimension_semantics=("arbitrary", "arbitrary"),
                                             vmem_limit_bytes=_vmem_limit(vmem)),
        name="moe_down",
    )(*sched, act, w_down, b_down)


def _combine_kernel(dest_ref, ys_ref, x1_ref, gate_ref, mods_ref, o_ref, buf, sem, *, tb, d, nt, n_tiles):
    i = pl.program_id(0)
    slot = lax.rem(i, 2)

    def row_copy(tile, sl, r, k):
        return pltpu.make_async_copy(ys_ref.at[pl.ds(dest_ref[(tile * tb + r) * TOP_K + k], 1)],
                                     buf.at[sl, k, pl.ds(r, 1)], sem.at[sl])

    def gather(tile, sl):
        def body(j, _):
            r0 = pl.multiple_of(j * SUBLANES, SUBLANES)
            for u in range(SUBLANES):
                for k in range(TOP_K):
                    row_copy(tile, sl, r0 + u, k).start(priority=k % 2)
            return 0
        lax.fori_loop(0, tb // SUBLANES, body, 0)

    @pl.when(i == 0)
    def _():
        gather(0, 0)

    for next_slot in range(2):
        @pl.when((i + 1 < n_tiles) & (slot != next_slot))
        def _():
            gather(i + 1, next_slot)

    for k in range(TOP_K):
        pltpu.make_async_copy(ys_ref.at[pl.ds(0, tb)], buf.at[slot, k], sem.at[slot]).wait()
    half = d // 2
    gate = gate_ref[...]
    y_lo = jnp.zeros((tb, half), F32)
    y_hi = jnp.zeros((tb, half), F32)
    for k in range(TOP_K):
        w = buf[slot, k]
        gk = gate[:, k:k + 1]
        y_lo = y_lo + lax.bitcast_convert_type(lax.shift_left(w, 16), F32) * gk
        y_hi = y_hi + lax.bitcast_convert_type(w & jnp.int32(-65536), F32) * gk
    gt2 = mods_ref[pl.ds(i // nt, 1), 5 * d:6 * d]
    o_ref[:, 0:half] = x1_ref[:, 0:half] + gt2[:, 0:half] * y_lo
    o_ref[:, half:d] = x1_ref[:, half:d] + gt2[:, half:d] * y_hi


def _combine(dest_flat, ys, x1, gate, mods, *, tb, seq):
    t, d = x1.shape
    n_tiles = t // tb
    nt = seq // tb
    vmem = 2 * TOP_K * tb * d * 2 + 4 * tb * d * 4 + 6 * tb * d * 4
    return pl.pallas_call(
        functools.partial(_combine_kernel, tb=tb, d=d, nt=nt, n_tiles=n_tiles),
        grid_spec=pltpu.PrefetchScalarGridSpec(
            num_scalar_prefetch=1, grid=(n_tiles,),
            in_specs=[pl.BlockSpec(memory_space=pl.ANY),
                      pl.BlockSpec((tb, d), lambda i, dr: (i, 0)),
                      pl.BlockSpec((tb, TOP_K), lambda i, dr: (i, 0)),
                      pl.BlockSpec(mods.shape, lambda i, dr: (0, 0))],
            out_specs=pl.BlockSpec((tb, d), lambda i, dr: (i, 0)),
            scratch_shapes=[pltpu.VMEM((2, TOP_K, tb, d // 2), I32), pltpu.SemaphoreType.DMA((2,))]),
        out_shape=jax.ShapeDtypeStruct((t, d), F32),
        compiler_params=pltpu.CompilerParams(dimension_semantics=("arbitrary",),
                                             vmem_limit_bytes=_vmem_limit(vmem)),
        name="combine",
    )(dest_flat, ys, x1, gate, mods)


def _rope_table(s):
    rows = s // GRID_W
    row = np.repeat(np.arange(rows), GRID_W).astype(np.float32)
    col = np.tile(np.arange(GRID_W), rows).astype(np.float32)
    inv = (np.float32(ROPE_THETA) ** (-np.arange(ROPE_FREQS, dtype=np.float32) / np.float32(ROPE_FREQS))).astype(np.float32)
    ang_r = row[:, None] * inv
    ang_c = col[:, None] * inv
    z = np.zeros_like(ang_r)
    cosf = np.concatenate([np.cos(ang_r)] * 2 + [np.cos(ang_c)] * 2, axis=1)
    sneg = np.concatenate([-np.sin(ang_r), z, -np.sin(ang_c), z], axis=1)
    spos = np.concatenate([z, np.sin(ang_r), z, np.sin(ang_c)], axis=1)
    return jnp.asarray(np.concatenate([cosf, sneg, spos], axis=1), dtype=F32)


def kernel(x, c, ctx, c_ctx, w_mod, b_mod, g_norm1, w_in, g_q, g_k, conv_w, conv_b, w_gate_a, b_gate_a,
           w_gate_x, b_gate_x, lru_lambda, g_att_out, g_rec_out, w_out, g_norm2, w_router, b_router,
           w_gate_up, b_gate_up, w_down, b_down):
    b, s, d = x.shape
    cl = ctx.shape[1]
    t = b * s
    assert w_mod.shape[0] == 1, "single-layer kernel"
    assert b + 1 <= SUBLANES and d - ATT_WIDTH == REC_BLOCKS * LANES
    assert s % (SCAN_SEGMENTS * SUBLANES) == 0 and cl % (SCAN_SEGMENTS * SUBLANES) == 0

    ctx_row = b
    c8 = jnp.zeros((SUBLANES, d), F32).at[:b].set(c).at[ctx_row].set(c_ctx)
    mods = _mod(c8, w_mod[0], b_mod[0])

    w_in_bf = w_in[0].astype(BF16)
    tm = min(512, s)
    q, k, v, xr, yr = _inproj(x, mods, g_norm1[0], w_in_bf, _rope_table(s), g_q[0], g_k[0],
                              latent=True, ctx_row=ctx_row, tm=tm)
    kc, vc, xrc = _inproj(ctx, mods, g_norm1[0], w_in_bf, None, g_q[0], g_k[0],
                          latent=False, ctx_row=ctx_row, tm=cl)

    att = _attention(q, kc, k, vc, v, tq=min(256, s))

    w_gates = jnp.concatenate([w_gate_a[0, 0], w_gate_x[0, 0], w_gate_a[0, 1], w_gate_x[0, 1]], axis=-1).astype(BF16)
    rw = d - ATT_WIDTH
    bias = lambda bb: bb.reshape(REC_BLOCKS, LANES)
    b_gates = jnp.concatenate([bias(b_gate_a[0, 0]), bias(b_gate_x[0, 0]), bias(b_gate_a[0, 1]), bias(b_gate_x[0, 1])],
                              axis=-1).reshape(1, 4 * rw)
    rec = _rglru(xr, xrc, yr, conv_w[0], conv_b[0].reshape(1, rw), w_gates, b_gates, lru_lambda[0])

    x1, h2p, top_idx, gate, mask, routed = _merge(att, rec, x, mods, g_att_out[0], g_rec_out[0],
                                                  w_out[0].astype(BF16), g_norm2[0], w_router[0], b_router[0], tm=tm)

    counts = routed[0].astype(I32)
    padded = (counts + MOE_BLOCK - 1) // MOE_BLOCK * MOE_BLOCK
    pad_ends = jnp.cumsum(padded)
    pad_starts = pad_ends - padded
    dest = _rank(mask, top_idx, pad_starts, tb=MOE_BLOCK).reshape(t * TOP_K)
    n_blocks = (t * TOP_K + N_EXPERTS * (MOE_BLOCK - 1) + MOE_BLOCK - 1) // MOE_BLOCK
    block_start = jnp.arange(n_blocks, dtype=I32) * MOE_BLOCK
    block_e = jnp.minimum(jnp.sum((pad_ends[None, :] <= block_start[:, None]).astype(I32), axis=1), N_EXPERTS - 1)
    n_valid = (pad_ends[N_EXPERTS - 1] // MOE_BLOCK).reshape(1).astype(I32)
    fill = jnp.concatenate([pad_starts + counts, pad_ends, n_valid]).astype(I32)

    xs = _dispatch(dest, fill, h2p, tb=MOE_BLOCK, n_blocks=n_blocks)

    used = padded > 0
    run_of_expert = jnp.cumsum(used.astype(I32)) - 1
    experts = jnp.arange(N_EXPERTS, dtype=I32)
    run_e = jnp.sum(jnp.where(used[None, :] & (run_of_expert[None, :] == experts[:, None]), experts[None, :], 0), axis=1)
    meta = jnp.concatenate([n_valid, jnp.sum(used.astype(I32)).reshape(1)])
    block_rows = jnp.clip((pad_starts + counts)[block_e] - block_start, 0, MOE_BLOCK).astype(I32)
    sched = (block_e, run_of_expert[block_e], run_e.astype(I32), meta, block_rows)

    f2 = w_gate_up.shape[3]
    act = _moe_gu(sched, xs, w_gate_up[0], b_gate_up[0].reshape(N_EXPERTS, 1, f2), tn=2048)
    ys = _moe_down(sched, act, w_down[0], b_down[0].reshape(N_EXPERTS, 1, d))

    return _combine(dest, ys, x1.reshape(t, d), gate, mods, tb=256, seq=s).reshape(b, s, d)
```

```python
import functools
import math

import jax
import jax.numpy as jnp
import numpy as np
from jax import lax
from jax.experimental import pallas as pl
from jax.experimental.pallas import tpu as pltpu

F32 = jnp.float32
BF16 = jnp.bfloat16
I32 = jnp.int32

EPS = 1e-6
GRID_W = 64
HEAD_DIM = 128
N_Q_HEADS = 8
N_KV_HEADS = 2
GROUP = N_Q_HEADS // N_KV_HEADS
ATT_WIDTH = N_Q_HEADS * HEAD_DIM
KV_WIDTH = N_KV_HEADS * HEAD_DIM
ROPE_THETA = 10000.0
ROPE_FREQS = HEAD_DIM // 4
REC_BLOCKS = 8
CONV_W = 4
CONV_LEFT = 2
LRU_C = 8.0
N_EXPERTS = 32
TOP_K = 4
SWIGLU_LIMIT = 7.0
SWIGLU_ALPHA = 1.702
MOE_BLOCK = 512

V7X_VMEM_BYTES = 64 * 1024 * 1024
SUBLANES = 8
LANES = 128
SCAN_SEGMENTS = SUBLANES
CONV_PAD = SUBLANES

HIGHEST = lax.Precision.HIGHEST


def _vmem_limit(nbytes):
    return int(min(V7X_VMEM_BYTES - 4 * 1024 * 1024, max(nbytes, 16 * 1024 * 1024)))


def _rms(x, g):
    return x * lax.rsqrt(jnp.mean(x * x, axis=-1, keepdims=True) + EPS) * g


def _mod_kernel(c_ref, w_ref, b_ref, o_ref):
    c = c_ref[...]
    a = c * jax.nn.sigmoid(c)
    w = w_ref[...]
    a_hi = a.astype(BF16)
    a_lo = (a - a_hi.astype(F32)).astype(BF16)
    w_hi = w.astype(BF16)
    w_lo = (w - w_hi.astype(F32)).astype(BF16)
    dot = functools.partial(jnp.dot, preferred_element_type=F32)
    o_ref[...] = dot(a_hi, w_hi) + dot(a_hi, w_lo) + dot(a_lo, w_hi) + dot(a_lo, w_lo) + b_ref[...]


def _mod(c8, w_mod, b_mod):
    d, n = w_mod.shape
    tn = 1024
    return pl.pallas_call(
        _mod_kernel,
        grid=(n // tn,),
        in_specs=[pl.BlockSpec((SUBLANES, d), lambda j: (0, 0)),
                  pl.BlockSpec((d, tn), lambda j: (0, j)),
                  pl.BlockSpec((1, tn), lambda j: (0, j))],
        out_specs=pl.BlockSpec((SUBLANES, tn), lambda j: (0, j)),
        out_shape=jax.ShapeDtypeStruct((SUBLANES, n), F32),
        compiler_params=pltpu.CompilerParams(dimension_semantics=("arbitrary",),
                                             vmem_limit_bytes=_vmem_limit(6 * d * tn * 4)),
        name="mod",
    )(c8, w_mod, b_mod.reshape(1, n))


def _qk_norm_rope(y, g, rope):
    yn = _rms(y, g)
    if rope is None:
        return yn
    cosf, sneg, spos = rope
    return yn * cosf + pltpu.roll(yn, HEAD_DIM - ROPE_FREQS, 1) * sneg + pltpu.roll(yn, ROPE_FREQS, 1) * spos


def _inproj_kernel(*refs, d, latent, ctx_row):
    if latent:
        (x_ref, mods_ref, g1_ref, w_ref, rope_ref, gq_ref, gk_ref,
         q_ref, k_ref, v_ref, xr_ref, yr_ref) = refs
        row = pl.program_id(0)
    else:
        x_ref, mods_ref, g1_ref, w_ref, gk_ref, k_ref, v_ref, xr_ref = refs
        row = ctx_row
    sh = mods_ref[pl.ds(row, 1), 0:d]
    sc = mods_ref[pl.ds(row, 1), d:2 * d]
    h = _rms(x_ref[0], g1_ref[...]) * (1.0 + sc) + sh
    hb = h.astype(BF16)

    def proj(lo, hi):
        return jnp.dot(hb, w_ref[:, lo:hi], preferred_element_type=F32)

    o_k = ATT_WIDTH
    o_v = o_k + KV_WIDTH
    o_xr = o_v + KV_WIDTH
    rec_w = d - ATT_WIDTH
    o_yr = o_xr + rec_w
    rope = None
    if latent:
        rp = rope_ref[...]
        rope = (rp[:, 0:HEAD_DIM], rp[:, HEAD_DIM:2 * HEAD_DIM], rp[:, 2 * HEAD_DIM:3 * HEAD_DIM])
        q = proj(0, ATT_WIDTH)
        for hd in range(N_Q_HEADS):
            sl = slice(hd * HEAD_DIM, (hd + 1) * HEAD_DIM)
            q_ref[0, :, sl] = (_qk_norm_rope(q[:, sl], gq_ref[...], rope) * (HEAD_DIM ** -0.5)).astype(BF16)
    k = proj(o_k, o_v)
    for hd in range(N_KV_HEADS):
        sl = slice(hd * HEAD_DIM, (hd + 1) * HEAD_DIM)
        k_ref[0, :, sl] = _qk_norm_rope(k[:, sl], gk_ref[...], rope).astype(BF16)
    v = proj(o_v, o_xr).astype(BF16)
    for hd in range(N_KV_HEADS):
        v_ref[0, :, 2 * hd * HEAD_DIM:(2 * hd + 1) * HEAD_DIM] = v[:, hd * HEAD_DIM:(hd + 1) * HEAD_DIM]
        v_ref[0, :, (2 * hd + 1) * HEAD_DIM:(2 * hd + 2) * HEAD_DIM] = jnp.ones((v.shape[0], HEAD_DIM), BF16)
    xr_ref[0] = proj(o_xr, o_yr)
    if latent:
        yr_ref[0] = proj(o_yr, o_yr + rec_w)


def _inproj(x, mods, g1, w_in_bf, rope_tab, g_q, g_k, *, latent, ctx_row, tm):
    b, s, d = x.shape
    n = w_in_bf.shape[1]
    rec_w = d - ATT_WIDTH
    grid = (b, s // tm)
    row_spec = lambda w: pl.BlockSpec((1, tm, w), lambda bi, i: (bi, i, 0))
    full2 = lambda a: pl.BlockSpec(a.shape, lambda bi, i: (0, 0))
    in_specs = [row_spec(d), full2(mods), pl.BlockSpec((1, d), lambda bi, i: (0, 0)),
                pl.BlockSpec((d, n), lambda bi, i: (0, 0), pipeline_mode=pl.Buffered(1))]
    args = [x, mods, g1.reshape(1, d), w_in_bf]
    out_specs, out_shape = [], []
    if latent:
        in_specs += [pl.BlockSpec((tm, 3 * HEAD_DIM), lambda bi, i: (i, 0)),
                     pl.BlockSpec((1, HEAD_DIM), lambda bi, i: (0, 0))]
        args += [rope_tab, g_q.reshape(1, HEAD_DIM)]
        out_specs.append(row_spec(ATT_WIDTH))
        out_shape.append(jax.ShapeDtypeStruct((b, s, ATT_WIDTH), BF16))
    in_specs.append(pl.BlockSpec((1, HEAD_DIM), lambda bi, i: (0, 0)))
    args.append(g_k.reshape(1, HEAD_DIM))
    out_specs += [row_spec(KV_WIDTH), row_spec(2 * KV_WIDTH), row_spec(rec_w)]
    out_shape += [jax.ShapeDtypeStruct((b, s, KV_WIDTH), BF16), jax.ShapeDtypeStruct((b, s, 2 * KV_WIDTH), BF16),
                  jax.ShapeDtypeStruct((b, s, rec_w), F32)]
    if latent:
        out_specs.append(row_spec(rec_w))
        out_shape.append(jax.ShapeDtypeStruct((b, s, rec_w), F32))
    vmem = d * n * 2 + 2 * tm * d * 4 + 2 * tm * n * 4 + 3 * tm * d * 4 + 2 * tm * n * 4
    return pl.pallas_call(
        functools.partial(_inproj_kernel, d=d, latent=latent, ctx_row=ctx_row),
        grid=grid, in_specs=in_specs, out_specs=out_specs, out_shape=out_shape,
        compiler_params=pltpu.CompilerParams(dimension_semantics=("arbitrary", "arbitrary"),
                                             vmem_limit_bytes=_vmem_limit(vmem)),
        name="inproj_latent" if latent else "inproj_ctx",
    )(*args)


def _attn_kernel(q_ref, kc_ref, k_ref, vc_ref, v_ref, o_ref, kbuf, vbuf):
    s_lat = k_ref.shape[1]

    @pl.when(pl.program_id(2) == 0)
    def _():
        kbuf[0:s_lat, :] = k_ref[0]
        kbuf[s_lat:, :] = kc_ref[0]
        vbuf[0:s_lat, :] = v_ref[0]
        vbuf[s_lat:, :] = vc_ref[0]

    k = kbuf[...]
    v = vbuf[...]
    for g in range(GROUP):
        sl = slice(g * HEAD_DIM, (g + 1) * HEAD_DIM)
        s = lax.dot_general(q_ref[0, :, sl], k, (((1,), (1,)), ((), ())), preferred_element_type=F32)
        m = jnp.max(s, axis=-1, keepdims=True)
        o = jnp.dot(jnp.exp(s - m).astype(BF16), v, preferred_element_type=F32)
        o_ref[0, :, sl] = (o[:, 0:HEAD_DIM] / o[:, HEAD_DIM:2 * HEAD_DIM]).astype(BF16)


def _attention(q, kc, k, vc, v, *, tq):
    b, s, _ = q.shape
    lk = s + kc.shape[1]
    gw = GROUP * HEAD_DIM
    vmem = 3 * lk * HEAD_DIM * 2 * 3 + 4 * tq * gw * 2 + 4 * tq * lk * 4
    kv = lambda a, w: pl.BlockSpec((1, a.shape[1], w), lambda bi, h, i: (bi, 0, h))
    return pl.pallas_call(
        _attn_kernel,
        grid=(b, N_KV_HEADS, s // tq),
        in_specs=[pl.BlockSpec((1, tq, gw), lambda bi, h, i: (bi, i, h)),
                  kv(kc, HEAD_DIM), kv(k, HEAD_DIM), kv(vc, 2 * HEAD_DIM), kv(v, 2 * HEAD_DIM)],
        out_specs=pl.BlockSpec((1, tq, gw), lambda bi, h, i: (bi, i, h)),
        out_shape=jax.ShapeDtypeStruct((b, s, ATT_WIDTH), BF16),
        scratch_shapes=[pltpu.VMEM((lk, HEAD_DIM), BF16), pltpu.VMEM((lk, 2 * HEAD_DIM), BF16)],
        compiler_params=pltpu.CompilerParams(dimension_semantics=("arbitrary",) * 3,
                                             vmem_limit_bytes=_vmem_limit(vmem)),
        name="attention",
    )(q, kc, k, vc, v)


def _gelu_tanh(x):
    return 0.5 * x * (1.0 + jnp.tanh(math.sqrt(2.0 / math.pi) * (x + 0.044715 * x * x * x)))


def _rglru_kernel(xr_ref, xc_ref, yr_ref, cw_ref, cb_ref, wg_ref, bg_ref, lam_ref, o_ref,
                  xp, xpc, af, bf, ab, bb, caf, cbf, cab, cbb, *, s, c):
    nseg = SCAN_SEGMENTS
    seg = s // nseg
    cseg = c // nseg
    zeros_pad = jnp.zeros((CONV_PAD, LANES), F32)
    xp[0:CONV_PAD, :] = zeros_pad
    xp[CONV_PAD + s:2 * CONV_PAD + s, :] = zeros_pad
    xp[CONV_PAD:CONV_PAD + s, :] = xr_ref[0]
    xpc[0:CONV_PAD, :] = zeros_pad
    xpc[CONV_PAD + c:2 * CONV_PAD + c, :] = zeros_pad
    xpc[CONV_PAD:CONV_PAD + c, :] = xc_ref[0]

    cw = cw_ref[...]
    cb = cb_ref[...]
    wg = wg_ref[0]
    bg = bg_ref[...]
    sp = jax.nn.softplus(-lam_ref[...])

    def coeffs(src, lo, n):
        u = cb
        for j in range(CONV_W):
            u = u + src[CONV_PAD + lo + j - CONV_LEFT:CONV_PAD + lo + j - CONV_LEFT + n, :] * cw[j:j + 1, :]
        g = jnp.dot(u.astype(BF16), wg, preferred_element_type=F32) + bg
        out = []
        for r in range(2):
            ga = g[:, (2 * r) * LANES:(2 * r + 1) * LANES]
            gx = g[:, (2 * r + 1) * LANES:(2 * r + 2) * LANES]
            log_a = (-LRU_C) * jax.nn.sigmoid(ga) * sp[r:r + 1, :]
            a = jnp.exp(log_a)
            mult = jnp.sqrt(-jnp.tanh(log_a) * (1.0 + a * a))
            out.append((a, mult * jax.nn.sigmoid(gx) * u))
        return out

    (a0, b0), (a1, b1) = coeffs(xpc, 0, c)
    for q in range(nseg):
        rows = slice(q * cseg, (q + 1) * cseg)
        dst = pl.ds(q, cseg, stride=nseg)
        caf[dst, :] = a0[rows]
        cbf[dst, :] = b0[rows]
        cab[dst, :] = a1[rows]
        cbb[dst, :] = b1[rows]
    for q in range(nseg):
        (a0, b0), (a1, b1) = coeffs(xp, q * seg, seg)
        dst = pl.ds(q, seg, stride=nseg)
        af[dst, :] = a0
        bf[dst, :] = b0
        ab[dst, :] = a1
        bb[dst, :] = b1

    def scan(a_f, b_f, a_b, b_b, n, store):
        def two_steps(a_ref, b_ref, r1, r2, h, p):
            a1 = a_ref[pl.ds(r1, nseg), :]
            b1 = b_ref[pl.ds(r1, nseg), :]
            a2 = a_ref[pl.ds(r2, nseg), :]
            b2 = b_ref[pl.ds(r2, nseg), :]
            a12 = a2 * a1
            b12 = a2 * b1 + b2
            h1 = a1 * h + b1
            p1 = a1 * p
            h2 = a12 * h + b12
            p2 = a12 * p
            if store:
                a_ref[pl.ds(r1, nseg), :] = p1
                b_ref[pl.ds(r1, nseg), :] = h1
                a_ref[pl.ds(r2, nseg), :] = p2
                b_ref[pl.ds(r2, nseg), :] = h2
            return h2, p2

        def body(j, carry):
            hf, pf, hb, pb = carry
            rf = pl.multiple_of(2 * j * nseg, nseg)
            rb = pl.multiple_of((n - 1 - 2 * j) * nseg, nseg)
            hf, pf = two_steps(a_f, b_f, rf, rf + nseg, hf, pf)
            hb, pb = two_steps(a_b, b_b, rb, rb - nseg, hb, pb)
            return hf, pf, hb, pb
        z = jnp.zeros((nseg, LANES), F32)
        o = jnp.ones((nseg, LANES), F32)
        return lax.fori_loop(0, n // 2, body, (z, o, z, o), unroll=4)

    def chain(h_end, p_end, h0, reverse):
        order = range(nseg - 1, -1, -1) if reverse else range(nseg)
        enter = [None] * nseg
        cur = h0
        for q in order:
            enter[q] = cur
            cur = h_end[q:q + 1, :] + p_end[q:q + 1, :] * cur
        return enter, cur

    zero_row = jnp.zeros((1, LANES), F32)
    hf, pf, hb, pb = scan(caf, cbf, cab, cbb, cseg, False)
    _, h0f = chain(hf, pf, zero_row, False)
    _, h0b = chain(hb, pb, zero_row, True)
    hf, pf, hb, pb = scan(af, bf, ab, bb, seg, True)
    enter_f, _ = chain(hf, pf, h0f, False)
    enter_b, _ = chain(hb, pb, h0b, True)
    for q in range(nseg):
        src = pl.ds(q, seg, stride=nseg)
        h = bf[src, :] + af[src, :] * enter_f[q] + bb[src, :] + ab[src, :] * enter_b[q]
        rows = slice(q * seg, (q + 1) * seg)
        o_ref[0, rows, :] = (h * _gelu_tanh(yr_ref[0, rows, :])).astype(BF16)


def _rglru(xr, xrc, yr, conv_w, conv_b, w_gates, b_gates, lam):
    b, s, w = xr.shape
    c = xrc.shape[1]
    nb = w // LANES
    slab = lambda n: pl.BlockSpec((1, n, LANES), lambda bi, j: (bi, 0, j))
    scr = lambda n: pltpu.VMEM((n, LANES), F32)
    vmem = (3 * 2 + 5) * s * LANES * 4 + 8 * s * LANES * 4
    return pl.pallas_call(
        functools.partial(_rglru_kernel, s=s, c=c),
        grid=(b, nb),
        in_specs=[slab(s), slab(c), slab(s),
                  pl.BlockSpec((CONV_W, LANES), lambda bi, j: (0, j)),
                  pl.BlockSpec((1, LANES), lambda bi, j: (0, j)),
                  pl.BlockSpec((1, LANES, 4 * LANES), lambda bi, j: (j, 0, 0)),
                  pl.BlockSpec((1, 4 * LANES), lambda bi, j: (0, j)),
                  pl.BlockSpec((2, LANES), lambda bi, j: (0, j))],
        out_specs=slab(s),
        out_shape=jax.ShapeDtypeStruct((b, s, w), BF16),
        scratch_shapes=[scr(s + 2 * CONV_PAD), scr(c + 2 * CONV_PAD),
                        scr(s), scr(s), scr(s), scr(s), scr(c), scr(c), scr(c), scr(c)],
        compiler_params=pltpu.CompilerParams(dimension_semantics=("arbitrary", "arbitrary"),
                                             vmem_limit_bytes=_vmem_limit(vmem)),
        name="rglru",
    )(xr, xrc, yr, conv_w, conv_b, w_gates, b_gates, lam)


def _pack_bf16_pair(lo, hi):
    lo_bits = lax.bitcast_convert_type(lo.astype(BF16).astype(F32), I32)
    hi_bits = lax.bitcast_convert_type(hi.astype(BF16).astype(F32), I32)
    return lax.shift_right_logical(lo_bits, 16) | (hi_bits & jnp.int32(-65536))


def _unpack_bf16_pair(w):
    lo = lax.bitcast_convert_type(lax.shift_left(w, 16), F32).astype(BF16)
    hi = lax.bitcast_convert_type(w & jnp.int32(-65536), F32).astype(BF16)
    return lo, hi


def _merge_kernel(att_ref, rec_ref, x_ref, mods_ref, ga_ref, gr_ref, wo_ref, g2_ref, wr_ref, br_ref,
                  x1_ref, h2_ref, idx_ref, gate_ref, mask_ref, cnt_ref, *, d):
    row = pl.program_id(0)

    @pl.when((pl.program_id(0) == 0) & (pl.program_id(1) == 0))
    def _():
        cnt_ref[...] = jnp.zeros_like(cnt_ref)
    gt1 = mods_ref[pl.ds(row, 1), 2 * d:3 * d]
    sh2 = mods_ref[pl.ds(row, 1), 3 * d:4 * d]
    sc2 = mods_ref[pl.ds(row, 1), 4 * d:5 * d]
    def rows_chain(rows):
        an = _rms(att_ref[0, rows, :].astype(F32), ga_ref[...]).astype(BF16)
        rn = _rms(rec_ref[0, rows, :].astype(F32), gr_ref[...]).astype(BF16)
        mix = (jnp.dot(an, wo_ref[0:ATT_WIDTH, :], preferred_element_type=F32)
               + jnp.dot(rn, wo_ref[ATT_WIDTH:d, :], preferred_element_type=F32))
        x1 = x_ref[0, rows, :] + gt1 * mix
        x1_ref[0, rows, :] = x1
        h2 = _rms(x1, g2_ref[...]) * (1.0 + sc2) + sh2
        half = d // 2
        h2_ref[rows, :] = _pack_bf16_pair(h2[:, 0:half], h2[:, half:d])
        h_hi = h2.astype(BF16)
        h_lo = (h2 - h_hi.astype(F32)).astype(BF16)
        w_split = wr_ref[...]
        part = (jnp.dot(h_hi, w_split, preferred_element_type=F32)
                + jnp.dot(h_lo, w_split, preferred_element_type=F32))
        logits = part[:, 0:N_EXPERTS] + part[:, N_EXPERTS:2 * N_EXPERTS] + br_ref[...]
        n = logits.shape[0]
        lane = lax.broadcasted_iota(I32, (n, N_EXPERTS), 1).astype(F32)
        col = lax.broadcasted_iota(I32, (n, TOP_K), 1)
        idx = jnp.zeros((n, TOP_K), F32)
        ex = jnp.zeros((n, TOP_K), F32)
        mask = jnp.zeros((n, N_EXPERTS), F32)
        rest = logits
        top = None
        for k in range(TOP_K):
            m = jnp.max(rest, axis=-1, keepdims=True)
            first = jnp.min(jnp.where(rest == m, lane, float(N_EXPERTS)), axis=-1, keepdims=True)
            sel = lane == first
            if k == 0:
                top = m
            idx = jnp.where(col == k, first, idx)
            ex = jnp.where(col == k, jnp.exp(m - top), ex)
            mask = jnp.where(sel, 1.0, mask)
            rest = jnp.where(sel, -jnp.inf, rest)
        idx_ref[rows, :] = idx.astype(I32)
        gate_ref[rows, :] = ex / jnp.sum(ex, axis=-1, keepdims=True)
        mask_ref[rows, :] = mask
        return jnp.sum(mask, axis=0, keepdims=True)

    tm = x_ref.shape[1]
    n_chains = 2 if tm % (2 * SUBLANES) == 0 else 1
    routed = [rows_chain(slice(ci * (tm // n_chains), (ci + 1) * (tm // n_chains))) for ci in range(n_chains)]
    cnt_ref[...] = cnt_ref[...] + sum(routed)


def _merge(att, rec, x, mods, g_att, g_rec, w_out_bf, g2, w_router, b_router, *, tm):
    b, s, d = x.shape
    t = b * s
    nt = s // tm
    rec_w = d - ATT_WIDTH
    row3 = lambda w: pl.BlockSpec((1, tm, w), lambda bi, i: (bi, i, 0))
    tok2 = lambda w: pl.BlockSpec((tm, w), lambda bi, i: (bi * nt + i, 0))
    const = lambda shape, **kw: pl.BlockSpec(shape, lambda bi, i: (0,) * len(shape), **kw)
    vmem = d * d * 2 + 2 * tm * (ATT_WIDTH + rec_w) * 2 + 4 * tm * d * 4 + tm * d * 4 + 8 * tm * d * 4
    return pl.pallas_call(
        functools.partial(_merge_kernel, d=d),
        grid=(b, nt),
        in_specs=[row3(ATT_WIDTH), row3(rec_w), row3(d), const(mods.shape),
                  const((1, ATT_WIDTH)), const((1, rec_w)),
                  const((d, d), pipeline_mode=pl.Buffered(1)), const((1, d)),
                  const((d, 2 * N_EXPERTS)), const((1, N_EXPERTS))],
        out_specs=[row3(d), tok2(d // 2), tok2(TOP_K), tok2(TOP_K), tok2(N_EXPERTS), const((1, N_EXPERTS))],
        out_shape=[jax.ShapeDtypeStruct((b, s, d), F32), jax.ShapeDtypeStruct((t, d // 2), I32),
                   jax.ShapeDtypeStruct((t, TOP_K), I32), jax.ShapeDtypeStruct((t, TOP_K), F32),
                   jax.ShapeDtypeStruct((t, N_EXPERTS), F32), jax.ShapeDtypeStruct((1, N_EXPERTS), F32)],
        compiler_params=pltpu.CompilerParams(dimension_semantics=("arbitrary", "arbitrary"),
                                             vmem_limit_bytes=_vmem_limit(vmem)),
        name="merge",
    )(att, rec, x, mods, g_att.reshape(1, -1), g_rec.reshape(1, -1), w_out_bf, g2.reshape(1, d),
      _split_bf16(w_router), b_router.reshape(1, N_EXPERTS))


def _split_bf16(w):
    hi = w.astype(BF16)
    lo = (w - hi.astype(F32)).astype(BF16)
    return jnp.concatenate([hi, lo], axis=1)


def _rank_kernel(mask_ref, idx_ref, start_ref, slot_ref, carry):
    @pl.when(pl.program_id(0) == 0)
    def _():
        carry[...] = jnp.zeros_like(carry)

    m = mask_ref[...]
    tb = m.shape[0]
    r = lax.broadcasted_iota(I32, (tb, tb), 0)
    cidx = lax.broadcasted_iota(I32, (tb, tb), 1)
    tri = jnp.where(cidx < r, 1.0, 0.0).astype(BF16)
    slot_e = jnp.dot(tri, m.astype(BF16), preferred_element_type=F32) + (carry[...] + start_ref[...])
    lane = lax.broadcasted_iota(I32, (tb, N_EXPERTS), 1)
    col = lax.broadcasted_iota(I32, (tb, TOP_K), 1)
    idx = idx_ref[...]
    slot = jnp.zeros((tb, TOP_K), F32)
    for k in range(TOP_K):
        pick = jnp.sum(jnp.where(lane == idx[:, k:k + 1], slot_e, 0.0), axis=-1, keepdims=True)
        slot = jnp.where(col == k, pick, slot)
    slot_ref[...] = slot.astype(I32)
    carry[...] = carry[...] + jnp.sum(m, axis=0, keepdims=True)


def _rank(mask, idx, starts, *, tb):
    t = mask.shape[0]
    return pl.pallas_call(
        _rank_kernel,
        grid=(t // tb,),
        in_specs=[pl.BlockSpec((tb, N_EXPERTS), lambda i: (i, 0)), pl.BlockSpec((tb, TOP_K), lambda i: (i, 0)),
                  pl.BlockSpec((1, N_EXPERTS), lambda i: (0, 0))],
        out_specs=pl.BlockSpec((tb, TOP_K), lambda i: (i, 0)),
        out_shape=jax.ShapeDtypeStruct((t, TOP_K), I32),
        scratch_shapes=[pltpu.VMEM((1, N_EXPERTS), F32)],
        compiler_params=pltpu.CompilerParams(dimension_semantics=("arbitrary",)),
        name="rank",
    )(mask, idx, starts.astype(F32).reshape(1, N_EXPERTS))


def _dispatch_kernel(dest_ref, fill_ref, h2_ref, xs_ref, zeros, sem, fill_sem, *, tb, n_blocks):
    base = pl.program_id(0) * tb

    @pl.when(pl.program_id(0) == 0)
    def _():
        zeros[...] = jnp.zeros_like(zeros)

        def zero_row(r):
            return pltpu.make_async_copy(zeros.at[pl.ds(0, 1)], xs_ref.at[pl.ds(r, 1)], fill_sem)

        def zero_block(bk):
            rows = pl.ds(pl.multiple_of(bk * MOE_BLOCK, MOE_BLOCK), MOE_BLOCK)
            return pltpu.make_async_copy(zeros, xs_ref.at[rows], fill_sem)

        def zero_rows(start, n):
            return pltpu.make_async_copy(zeros.at[pl.ds(0, n)],
                                         xs_ref.at[pl.ds(pl.multiple_of(start, SUBLANES), n)], fill_sem)

        def for_all_fills(act):
            for e in range(N_EXPERTS):
                lo = fill_ref[e]
                hi = fill_ref[N_EXPERTS + e]
                lo8 = jnp.minimum((lo + SUBLANES - 1) // SUBLANES * SUBLANES, hi)
                lax.fori_loop(lo, lo8, lambda r, c: (act(zero_row(r)), c)[1], 0)
                rest = hi - lo8
                pos = lo8
                size = MOE_BLOCK // 2
                while size >= SUBLANES:
                    take = rest & size

                    @pl.when(take != 0)
                    def _(pos=pos, size=size):
                        act(zero_rows(pos, size))

                    pos = pos + take
                    size //= 2
            lax.fori_loop(fill_ref[2 * N_EXPERTS], n_blocks, lambda bk, c: (act(zero_block(bk)), c)[1], 0)

        for_all_fills(lambda cp: cp.start())
        for_all_fills(lambda cp: cp.wait())

    def row_copy(i, k):
        return pltpu.make_async_copy(h2_ref.at[pl.ds(i, 1)],
                                     xs_ref.at[pl.ds(dest_ref[(base + i) * TOP_K + k], 1)], sem)

    def body(i, _):
        for k in range(TOP_K):
            row_copy(i, k).start(priority=k % 2)
        return 0

    lax.fori_loop(0, tb, body, 0, unroll=4)
    for k in range(TOP_K):
        pltpu.make_async_copy(h2_ref, xs_ref.at[pl.ds(0, tb)], sem).wait()


def _dispatch(dest_flat, fill, h2p, *, tb, n_blocks):
    t, wd = h2p.shape
    return pl.pallas_call(
        functools.partial(_dispatch_kernel, tb=tb, n_blocks=n_blocks),
        grid_spec=pltpu.PrefetchScalarGridSpec(
            num_scalar_prefetch=2, grid=(t // tb,),
            in_specs=[pl.BlockSpec((tb, wd), lambda i, dr, fl: (i, 0))],
            out_specs=pl.BlockSpec(memory_space=pl.ANY),
            scratch_shapes=[pltpu.VMEM((MOE_BLOCK, wd), I32), pltpu.SemaphoreType.DMA(()),
                            pltpu.SemaphoreType.DMA(())]),
        out_shape=jax.ShapeDtypeStruct((n_blocks * MOE_BLOCK, wd), I32),
        compiler_params=pltpu.CompilerParams(dimension_semantics=("arbitrary",), has_side_effects=True),
        name="dispatch",
    )(dest_flat, fill, h2p)


PAIR_CHUNK = 512
MOE_ROW_STEPS = 4


def _last_real_block(s, meta):
    return jnp.maximum(jnp.minimum(s, meta[0] - 1), 0)


def _stream_expert_weights(ridx_ref, run_e_ref, meta_ref, w_hbm, wf32, wbf, sem):
    j = pl.program_id(0)
    s = pl.program_id(1)
    tn = wbf.shape[1]
    n_runs = meta_ref[1]
    r = ridx_ref[s]
    first = (s < meta_ref[0]) & ((s == 0) | (r != ridx_ref[jnp.maximum(s - 1, 0)]))

    def tile_copy(run, sweep):
        cols = pl.ds(pl.multiple_of(sweep * tn, tn), tn)
        return pltpu.make_async_copy(w_hbm.at[run_e_ref[run], :, cols], wf32, sem)

    @pl.when(first)
    def _():
        @pl.when((j == 0) & (r == 0))
        def _():
            tile_copy(0, 0).start()

        tile_copy(r, j).wait()
        wbf[...] = wf32[...].astype(BF16)
        more_runs = r + 1 < n_runs

        @pl.when(more_runs)
        def _():
            tile_copy(r + 1, j).start()

        @pl.when(jnp.logical_not(more_runs) & (j + 1 < pl.num_programs(0)))
        def _():
            tile_copy(0, j + 1).start()


def _expert_stream_scratch(k, tn):
    return [pltpu.VMEM((k, tn), F32), pltpu.VMEM((k, tn), BF16), pltpu.SemaphoreType.DMA(())]


def _for_real_rows(rows_ref, meta_ref, out_ref, compute_rows):
    s = pl.program_id(1)
    real = s < meta_ref[0]
    quarter = MOE_BLOCK // MOE_ROW_STEPS
    used_quarters = (rows_ref[s] + quarter - 1) // quarter

    for q in range(1, MOE_ROW_STEPS + 1):
        @pl.when(real & (used_quarters == q))
        def _(n=q * quarter):
            compute_rows(n)
            if n < MOE_BLOCK:
                out_ref[n:, :] = jnp.zeros((MOE_BLOCK - n, out_ref.shape[1]), out_ref.dtype)

    @pl.when(jnp.logical_not(real))
    def _():
        out_ref[...] = jnp.zeros_like(out_ref)


def _gu_kernel(be_ref, ridx_ref, run_e_ref, meta_ref, rows_ref, xs_ref, w_hbm, b_ref, act_ref, wf32, wbf, sem):
    del be_ref
    _stream_expert_weights(ridx_ref, run_e_ref, meta_ref, w_hbm, wf32, wbf, sem)
    tn = wbf.shape[1]

    def compute_rows(n):
        lo, hi = _unpack_bf16_pair(xs_ref[0:n, :])
        xb = jnp.concatenate([lo, hi], axis=1)
        gu = jnp.dot(xb, wbf[...], preferred_element_type=F32) + b_ref[0]
        g = jnp.minimum(gu, SWIGLU_LIMIT)
        up1 = jnp.clip(gu, -SWIGLU_LIMIT, SWIGLU_LIMIT) + 1.0
        paired = (pltpu.roll(up1, tn - 1, 1) * (g * jax.nn.sigmoid(SWIGLU_ALPHA * g))).astype(BF16)
        r = lax.broadcasted_iota(I32, (PAIR_CHUNK, PAIR_CHUNK // 2), 0)
        c = lax.broadcasted_iota(I32, (PAIR_CHUNK, PAIR_CHUNK // 2), 1)
        sel = jnp.where(r == 2 * c, 1.0, 0.0).astype(BF16)
        for ch in range(tn // PAIR_CHUNK):
            act_ref[0:n, ch * (PAIR_CHUNK // 2):(ch + 1) * (PAIR_CHUNK // 2)] = jnp.dot(
                paired[:, ch * PAIR_CHUNK:(ch + 1) * PAIR_CHUNK], sel, preferred_element_type=F32).astype(BF16)

    _for_real_rows(rows_ref, meta_ref, act_ref, compute_rows)


def _moe_gu(sched, xs, w_gu, b_gu, *, tn):
    p, half = xs.shape
    d = 2 * half
    f2 = w_gu.shape[2]
    nj = f2 // tn
    nblk = p // MOE_BLOCK
    blk = _last_real_block
    vmem = d * tn * 4 + d * tn * 2 + 2 * MOE_BLOCK * half * 4 + 2 * MOE_BLOCK * tn + 8 * MOE_BLOCK * tn * 4
    return pl.pallas_call(
        _gu_kernel,
        grid_spec=pltpu.PrefetchScalarGridSpec(
            num_scalar_prefetch=5, grid=(nj, nblk),
            in_specs=[pl.BlockSpec((MOE_BLOCK, half), lambda j, s, be, ri, re, mt, rw: (blk(s, mt), 0)),
                      pl.BlockSpec(memory_space=pl.ANY),
                      pl.BlockSpec((1, 1, tn), lambda j, s, be, ri, re, mt, rw: (be[blk(s, mt)], 0, j))],
            out_specs=pl.BlockSpec((MOE_BLOCK, tn // 2), lambda j, s, be, ri, re, mt, rw: (s, j)),
            scratch_shapes=_expert_stream_scratch(d, tn)),
        out_shape=jax.ShapeDtypeStruct((p, f2 // 2), BF16),
        compiler_params=pltpu.CompilerParams(dimension_semantics=("arbitrary", "arbitrary"),
                                             vmem_limit_bytes=_vmem_limit(vmem)),
        name="moe_gu",
    )(*sched, xs, w_gu, b_gu)


def _down_kernel(be_ref, ridx_ref, run_e_ref, meta_ref, rows_ref, act_ref, w_hbm, b_ref, y_ref, wf32, wbf, sem):
    del be_ref
    _stream_expert_weights(ridx_ref, run_e_ref, meta_ref, w_hbm, wf32, wbf, sem)

    def compute_rows(n):
        y = jnp.dot(act_ref[0:n, :], wbf[...], preferred_element_type=F32) + b_ref[0]
        half = y.shape[1] // 2
        y_ref[0:n, :] = _pack_bf16_pair(y[:, 0:half], y[:, half:2 * half])

    _for_real_rows(rows_ref, meta_ref, y_ref, compute_rows)


def _moe_down(sched, act, w_down, b_down):
    p, f = act.shape
    d = w_down.shape[2]
    nblk = p // MOE_BLOCK
    blk = _last_real_block
    vmem = f * d * 4 + f * d * 2 + 2 * MOE_BLOCK * f * 2 + 5 * MOE_BLOCK * d * 4
    return pl.pallas_call(
        _down_kernel,
        grid_spec=pltpu.PrefetchScalarGridSpec(
            num_scalar_prefetch=5, grid=(1, nblk),
            in_specs=[pl.BlockSpec((MOE_BLOCK, f), lambda j, s, be, ri, re, mt, rw: (blk(s, mt), 0)),
                      pl.BlockSpec(memory_space=pl.ANY),
                      pl.BlockSpec((1, 1, d), lambda j, s, be, ri, re, mt, rw: (be[blk(s, mt)], 0, 0))],
            out_specs=pl.BlockSpec((MOE_BLOCK, d // 2), lambda j, s, be, ri, re, mt, rw: (s, 0)),
            scratch_shapes=_expert_stream_scratch(f, d)),
        out_shape=jax.ShapeDtypeStruct((p, d // 2), I32),
        compiler_params=pltpu.CompilerParams(dimension_semantics=("arbitrary", "arbitrary"),
                                             vmem_limit_bytes=_vmem_limit(vmem)),
        name="moe_down",
    )(*sched, act, w_down, b_down)


def _combine_kernel(dest_ref, ys_ref, x1_ref, gate_ref, mods_ref, o_ref, buf, sem, *, tb, d, nt, n_tiles):
    i = pl.program_id(0)
    slot = lax.rem(i, 2)

    def row_copy(tile, sl, r, k):
        return pltpu.make_async_copy(ys_ref.at[pl.ds(dest_ref[(tile * tb + r) * TOP_K + k], 1)],
                                     buf.at[sl, k, pl.ds(r, 1)], sem.at[sl])

    def gather(tile, sl):
        def body(j, _):
            r0 = pl.multiple_of(j * SUBLANES, SUBLANES)
            for u in range(SUBLANES):
                for k in range(TOP_K):
                    row_copy(tile, sl, r0 + u, k).start(priority=k % 2)
            return 0
        lax.fori_loop(0, tb // SUBLANES, body, 0)

    @pl.when(i == 0)
    def _():
        gather(0, 0)

    for next_slot in range(2):
        @pl.when((i + 1 < n_tiles) & (slot != next_slot))
        def _():
            gather(i + 1, next_slot)

    for k in range(TOP_K):
        pltpu.make_async_copy(ys_ref.at[pl.ds(0, tb)], buf.at[slot, k], sem.at[slot]).wait()
    half = d // 2
    gate = gate_ref[...]
    y_lo = jnp.zeros((tb, half), F32)
    y_hi = jnp.zeros((tb, half), F32)
    for k in range(TOP_K):
        w = buf[slot, k]
        gk = gate[:, k:k + 1]
        y_lo = y_lo + lax.bitcast_convert_type(lax.shift_left(w, 16), F32) * gk
        y_hi = y_hi + lax.bitcast_convert_type(w & jnp.int32(-65536), F32) * gk
    gt2 = mods_ref[pl.ds(i // nt, 1), 5 * d:6 * d]
    o_ref[:, 0:half] = x1_ref[:, 0:half] + gt2[:, 0:half] * y_lo
    o_ref[:, half:d] = x1_ref[:, half:d] + gt2[:, half:d] * y_hi


def _combine(dest_flat, ys, x1, gate, mods, *, tb, seq):
    t, d = x1.shape
    n_tiles = t // tb
    nt = seq // tb
    vmem = 2 * TOP_K * tb * d * 2 + 4 * tb * d * 4 + 6 * tb * d * 4
    return pl.pallas_call(
        functools.partial(_combine_kernel, tb=tb, d=d, nt=nt, n_tiles=n_tiles),
        grid_spec=pltpu.PrefetchScalarGridSpec(
            num_scalar_prefetch=1, grid=(n_tiles,),
            in_specs=[pl.BlockSpec(memory_space=pl.ANY),
                      pl.BlockSpec((tb, d), lambda i, dr: (i, 0)),
                      pl.BlockSpec((tb, TOP_K), lambda i, dr: (i, 0)),
                      pl.BlockSpec(mods.shape, lambda i, dr: (0, 0))],
            out_specs=pl.BlockSpec((tb, d), lambda i, dr: (i, 0)),
            scratch_shapes=[pltpu.VMEM((2, TOP_K, tb, d // 2), I32), pltpu.SemaphoreType.DMA((2,))]),
        out_shape=jax.ShapeDtypeStruct((t, d), F32),
        compiler_params=pltpu.CompilerParams(dimension_semantics=("arbitrary",),
                                             vmem_limit_bytes=_vmem_limit(vmem)),
        name="combine",
    )(dest_flat, ys, x1, gate, mods)


def _rope_table(s):
    rows = s // GRID_W
    row = np.repeat(np.arange(rows), GRID_W).astype(np.float32)
    col = np.tile(np.arange(GRID_W), rows).astype(np.float32)
    inv = (np.float32(ROPE_THETA) ** (-np.arange(ROPE_FREQS, dtype=np.float32) / np.float32(ROPE_FREQS))).astype(np.float32)
    ang_r = row[:, None] * inv
    ang_c = col[:, None] * inv
    z = np.zeros_like(ang_r)
    cosf = np.concatenate([np.cos(ang_r)] * 2 + [np.cos(ang_c)] * 2, axis=1)
    sneg = np.concatenate([-np.sin(ang_r), z, -np.sin(ang_c), z], axis=1)
    spos = np.concatenate([z, np.sin(ang_r), z, np.sin(ang_c)], axis=1)
    return jnp.asarray(np.concatenate([cosf, sneg, spos], axis=1), dtype=F32)


def kernel(x, c, ctx, c_ctx, w_mod, b_mod, g_norm1, w_in, g_q, g_k, conv_w, conv_b, w_gate_a, b_gate_a,
           w_gate_x, b_gate_x, lru_lambda, g_att_out, g_rec_out, w_out, g_norm2, w_router, b_router,
           w_gate_up, b_gate_up, w_down, b_down):
    b, s, d = x.shape
    cl = ctx.shape[1]
    t = b * s
    assert w_mod.shape[0] == 1, "single-layer kernel"
    assert b + 1 <= SUBLANES and d - ATT_WIDTH == REC_BLOCKS * LANES
    assert s % (SCAN_SEGMENTS * SUBLANES) == 0 and cl % (SCAN_SEGMENTS * SUBLANES) == 0

    ctx_row = b
    c8 = jnp.zeros((SUBLANES, d), F32).at[:b].set(c).at[ctx_row].set(c_ctx)
    mods = _mod(c8, w_mod[0], b_mod[0])

    w_in_bf = w_in[0].astype(BF16)
    tm = min(512, s)
    q, k, v, xr, yr = _inproj(x, mods, g_norm1[0], w_in_bf, _rope_table(s), g_q[0], g_k[0],
                              latent=True, ctx_row=ctx_row, tm=tm)
    kc, vc, xrc = _inproj(ctx, mods, g_norm1[0], w_in_bf, None, g_q[0], g_k[0],
                          latent=False, ctx_row=ctx_row, tm=cl)

    att = _attention(q, kc, k, vc, v, tq=min(256, s))

    w_gates = jnp.concatenate([w_gate_a[0, 0], w_gate_x[0, 0], w_gate_a[0, 1], w_gate_x[0, 1]], axis=-1).astype(BF16)
    rw = d - ATT_WIDTH
    bias = lambda bb: bb.reshape(REC_BLOCKS, LANES)
    b_gates = jnp.concatenate([bias(b_gate_a[0, 0]), bias(b_gate_x[0, 0]), bias(b_gate_a[0, 1]), bias(b_gate_x[0, 1])],
                              axis=-1).reshape(1, 4 * rw)
    rec = _rglru(xr, xrc, yr, conv_w[0], conv_b[0].reshape(1, rw), w_gates, b_gates, lru_lambda[0])

    x1, h2p, top_idx, gate, mask, routed = _merge(att, rec, x, mods, g_att_out[0], g_rec_out[0],
                                                  w_out[0].astype(BF16), g_norm2[0], w_router[0], b_router[0], tm=tm)

    counts = routed[0].astype(I32)
    padded = (counts + MOE_BLOCK - 1) // MOE_BLOCK * MOE_BLOCK
    pad_ends = jnp.cumsum(padded)
    pad_starts = pad_ends - padded
    dest = _rank(mask, top_idx, pad_starts, tb=MOE_BLOCK).reshape(t * TOP_K)
    n_blocks = (t * TOP_K + N_EXPERTS * (MOE_BLOCK - 1) + MOE_BLOCK - 1) // MOE_BLOCK
    block_start = jnp.arange(n_blocks, dtype=I32) * MOE_BLOCK
    block_e = jnp.minimum(jnp.sum((pad_ends[None, :] <= block_start[:, None]).astype(I32), axis=1), N_EXPERTS - 1)
    n_valid = (pad_ends[N_EXPERTS - 1] // MOE_BLOCK).reshape(1).astype(I32)
    fill = jnp.concatenate([pad_starts + counts, pad_ends, n_valid]).astype(I32)

    xs = _dispatch(dest, fill, h2p, tb=MOE_BLOCK, n_blocks=n_blocks)

    used = padded > 0
    run_of_expert = jnp.cumsum(used.astype(I32)) - 1
    experts = jnp.arange(N_EXPERTS, dtype=I32)
    run_e = jnp.sum(jnp.where(used[None, :] & (run_of_expert[None, :] == experts[:, None]), experts[None, :], 0), axis=1)
    meta = jnp.concatenate([n_valid, jnp.sum(used.astype(I32)).reshape(1)])
    block_rows = jnp.clip((pad_starts + counts)[block_e] - block_start, 0, MOE_BLOCK).astype(I32)
    sched = (block_e, run_of_expert[block_e], run_e.astype(I32), meta, block_rows)

    f2 = w_gate_up.shape[3]
    act = _moe_gu(sched, xs, w_gate_up[0], b_gate_up[0].reshape(N_EXPERTS, 1, f2), tn=2048)
    ys = _moe_down(sched, act, w_down[0], b_down[0].reshape(N_EXPERTS, 1, d))

    return _combine(dest, ys, x1.reshape(t, d), gate, mods, tb=256, seq=s).reshape(b, s, d)
```
